```python
import jax
import jax.numpy as jnp
from jax import lax
import numpy as np

D_MODEL = 2048
BATCH = 8
SEQ = 8192
DEPTH = 4

GRID_W = 64
CTX_LEN = 256
EPS = 1e-6

NA_HEADS = 16
NA_HEAD_DIM = 128
NA_WIDTH = NA_HEADS * NA_HEAD_DIM
NA_KH = 8
NA_KW = 16

SSD_HEAD_DIM = 64
SSD_WIDTH = D_MODEL
SSD_HEADS = SSD_WIDTH // SSD_HEAD_DIM
SSD_GROUPS = 8
SSD_HPG = SSD_HEADS // SSD_GROUPS
SSD_STATE = 128
SSD_CONV = 5
SSD_CHUNK = 128
SSD_XBC = SSD_WIDTH + 2 * SSD_GROUPS * SSD_STATE

SC_WIDTH = D_MODEL
SC_CONV = 3

MIX_WIDTH = NA_WIDTH + SSD_WIDTH
COL_Q = 0
COL_GATE_A = COL_Q + NA_WIDTH
COL_Z = COL_GATE_A + NA_WIDTH
COL_K = COL_Z + SSD_WIDTH
COL_V = COL_K + NA_WIDTH
COL_XBC = COL_V + NA_WIDTH
COL_DT = COL_XBC + SSD_XBC
EVEN_IN = COL_DT + 2 * SSD_HEADS
ODD_IN = 4 * SC_WIDTH
N_EVEN = (DEPTH + 1) // 2
N_ODD = DEPTH // 2

kernel_name = 'hybrid_natten_ssd_shortconv_dit'


def rms_norm(x, g):
    xf = x.astype(jnp.float32)
    y = xf * lax.rsqrt(jnp.mean(xf * xf, axis=-1, keepdims=True) + EPS)
    return (y * g.astype(jnp.float32)).astype(x.dtype)


def adaln(cond, w, b):
    m = jax.nn.silu(cond) @ w + b
    return jnp.split(m, 3, axis=-1)


def modulate(h, shift, scale):
    return h * (1 + scale) + shift


def cols(p, start, width, base):
    return p[..., start - base:start - base + width]


def heads(t, n, d):
    return t.reshape(*t.shape[:2], n, d)


def dw_conv(x, w, b=None):
    k = w.shape[0]
    y = lax.conv_general_dilated(x, w[:, None, :], window_strides=(1,), padding=[(k // 2, k // 2)],
                                 dimension_numbers=('NWC', 'WIO', 'NWC'), feature_group_count=x.shape[-1])
    return y if b is None else y + b


def neighbourhood_attention(q, k, v, k_ctx, v_ctx, rpb):
    b, seq, nh, dh = q.shape
    rows = seq // GRID_W
    kh = min(NA_KH, rows)
    scale = dh ** -0.5
    qg = q.reshape(b, rows, GRID_W, nh, dh)
    kg = k.reshape(b, rows, GRID_W, nh, dh)
    vg = v.reshape(b, rows, GRID_W, nh, dh)
    col = jnp.arange(GRID_W)
    col_start = jnp.clip(col - NA_KW // 2, 0, GRID_W - NA_KW)
    in_win = (col[None, :] >= col_start[:, None]) & (col[None, :] < col_start[:, None] + NA_KW)
    mask = jnp.broadcast_to(in_win[:, None, :], (GRID_W, kh, GRID_W)).reshape(GRID_W, kh * GRID_W)
    dc_idx = jnp.clip(col[None, :] - col[:, None], -(NA_KW - 1), NA_KW - 1) + NA_KW - 1

    def row_block(r):
        rs = jnp.clip(r - kh // 2, 0, rows - kh)
        qr = lax.dynamic_index_in_dim(qg, r, axis=1, keepdims=False)
        kb = lax.dynamic_slice_in_dim(kg, rs, kh, axis=1).reshape(b, kh * GRID_W, nh, dh)
        vb = lax.dynamic_slice_in_dim(vg, rs, kh, axis=1).reshape(b, kh * GRID_W, nh, dh)
        dr_idx = rs + jnp.arange(kh) - r + NA_KH - 1
        bias = rpb[:, dr_idx[None, :, None], dc_idx[:, None, :]].reshape(nh, GRID_W, kh * GRID_W)
        s_lat = jnp.einsum('bqhd,bkhd->bhqk', qr, kb).astype(jnp.float32) * scale + bias.astype(jnp.float32)
        s_lat = jnp.where(mask, s_lat, -jnp.inf)
        s_ctx = jnp.einsum('bqhd,bkhd->bhqk', qr, k_ctx).astype(jnp.float32) * scale
        pr = jax.nn.softmax(jnp.concatenate([s_lat, s_ctx], axis=-1), axis=-1).astype(v.dtype)
        return (jnp.einsum('bhqk,bkhd->bqhd', pr[..., :kh * GRID_W], vb)
                + jnp.einsum('bhqk,bkhd->bqhd', pr[..., kh * GRID_W:], v_ctx))

    out = lax.map(row_block, jnp.arange(rows))
    return jnp.moveaxis(out, 0, 1).reshape(b, seq, nh, dh)


def context_attention(q, k, v):
    s = jnp.einsum('bqhd,bkhd->bhqk', q, k).astype(jnp.float32) * q.shape[-1] ** -0.5
    pr = jax.nn.softmax(s, axis=-1).astype(v.dtype)
    return jnp.einsum('bhqk,bkhd->bqhd', pr, v)


def ssd_scan(x, dt, a, bm, cm, h0):
    b, seq = x.shape[:2]
    nc = seq // SSD_CHUNK

    def chunks(t):
        return jnp.moveaxis(t.reshape(b, nc, SSD_CHUNK, *t.shape[2:]), 1, 0)

    tri = jnp.tril(jnp.ones((SSD_CHUNK, SSD_CHUNK), bool))[None, :, :, None, None]

    def step(h, inp):
        xc, dtc, bc, cc = inp
        cs = jnp.cumsum(dtc * a, axis=1)
        seg = cs[:, :, None] - cs[:, None, :]
        decay = jnp.exp(jnp.where(tri, seg, -jnp.inf))
        cb = jnp.einsum('bign,bjgn->bijg', cc, bc)
        w = cb[..., None] * decay * dtc[:, None]
        y = jnp.einsum('bijgr,bjgrp->bigrp', w, xc)
        y = y + jnp.einsum('bign,bgrpn->bigrp', cc, h) * jnp.exp(cs)[..., None]
        to_end = jnp.exp(cs[:, -1:] - cs) * dtc
        h = h * jnp.exp(cs[:, -1])[..., None, None] + jnp.einsum('bjgn,bjgr,bjgrp->bgrpn', bc, to_end, xc)
        return h, y

    h, ys = lax.scan(step, h0, (chunks(x), chunks(dt), chunks(bm), chunks(cm)))
    return jnp.moveaxis(ys, 0, 1).reshape(x.shape), h


def ssd_final_state(x, dt, a, bm):
    cs = jnp.cumsum(dt * a, axis=1)
    to_end = jnp.exp(cs[:, -1:] - cs) * dt
    return jnp.einsum('bjgn,bjgr,bjgrp->bgrpn', bm, to_end, x)


def _flip(t, rev):
    return jnp.flip(t, axis=1) if rev else t


def bidirectional_ssd(lat, con, a, want_ctx):
    xs, dt, bm, cm = lat
    xs_c, dt_c, bm_c, cm_c = con
    b = xs.shape[0]
    y = jnp.zeros_like(xs)
    yc = jnp.zeros_like(xs_c) if want_ctx else None
    for d in range(2):
        rev = d == 1
        if want_ctx:
            h0 = jnp.zeros((b, SSD_GROUPS, SSD_HPG, SSD_HEAD_DIM, SSD_STATE), jnp.float32)
            yc_d, hc = ssd_scan(_flip(xs_c, rev), _flip(dt_c[:, :, d], rev), a[d], _flip(bm_c, rev), _flip(cm_c, rev), h0)
            yc = yc + _flip(yc_d, rev)
        else:
            hc = ssd_final_state(_flip(xs_c, rev), _flip(dt_c[:, :, d], rev), a[d], _flip(bm_c, rev))
        y_d, _ = ssd_scan(_flip(xs, rev), _flip(dt[:, :, d], rev), a[d], _flip(bm, rev), _flip(cm, rev), hc)
        y = y + _flip(y_d, rev)
    return y, yc


def ssd_inputs(p, base, conv_w, conv_b, dt_bias):
    b, seq = p.shape[:2]
    gn = SSD_GROUPS * SSD_STATE
    xbc = jax.nn.silu(dw_conv(cols(p, COL_XBC, SSD_XBC, base), conv_w, conv_b)).astype(jnp.float32)
    xs = xbc[..., :SSD_WIDTH].reshape(b, seq, SSD_GROUPS, SSD_HPG, SSD_HEAD_DIM)
    bm = xbc[..., SSD_WIDTH:SSD_WIDTH + gn].reshape(b, seq, SSD_GROUPS, SSD_STATE)
    cm = xbc[..., SSD_WIDTH + gn:].reshape(b, seq, SSD_GROUPS, SSD_STATE)
    dt_raw = cols(p, COL_DT, 2 * SSD_HEADS, base).astype(jnp.float32).reshape(b, seq, 2, SSD_GROUPS, SSD_HPG)
    dt = jax.nn.softplus(dt_raw + dt_bias.astype(jnp.float32).reshape(2, SSD_GROUPS, SSD_HPG))
    return xs, dt, bm, cm


def gated_group_rmsnorm(y, z, g):
    b, seq = y.shape[:2]
    yz = (y.reshape(b, seq, SSD_WIDTH) * jax.nn.silu(z.astype(jnp.float32))).reshape(b, seq, SSD_GROUPS, -1)
    yz = yz * lax.rsqrt(jnp.mean(yz * yz, axis=-1, keepdims=True) + EPS)
    return (yz.reshape(b, seq, SSD_WIDTH) * g.astype(jnp.float32)).astype(z.dtype)


def na_ssd_mixer(h, hc, w_in, conv_w, conv_b, a_log, dt_bias, d_skip, ssm_norm_g,
                 q_norm_g, k_norm_g, rpb, w_out, update_ctx):
    b, seq, _ = h.shape
    p = h @ w_in
    base = 0 if update_ctx else COL_K
    pc = hc @ w_in[:, base:]
    q = rms_norm(heads(cols(p, COL_Q, NA_WIDTH, 0), NA_HEADS, NA_HEAD_DIM), q_norm_g)
    k = rms_norm(heads(cols(p, COL_K, NA_WIDTH, 0), NA_HEADS, NA_HEAD_DIM), k_norm_g)
    v = heads(cols(p, COL_V, NA_WIDTH, 0), NA_HEADS, NA_HEAD_DIM)
    kc = rms_norm(heads(cols(pc, COL_K, NA_WIDTH, base), NA_HEADS, NA_HEAD_DIM), k_norm_g)
    vc = heads(cols(pc, COL_V, NA_WIDTH, base), NA_HEADS, NA_HEAD_DIM)
    ya = neighbourhood_attention(q, k, v, kc, vc, rpb).reshape(b, seq, NA_WIDTH)
    ya = ya * jax.nn.silu(cols(p, COL_GATE_A, NA_WIDTH, 0))
    a = -jnp.exp(a_log.astype(jnp.float32)).reshape(2, SSD_GROUPS, SSD_HPG)
    d = d_skip.astype(jnp.float32).reshape(SSD_GROUPS, SSD_HPG)[..., None]
    lat = ssd_inputs(p, 0, conv_w, conv_b, dt_bias)
    con = ssd_inputs(pc, base, conv_w, conv_b, dt_bias)
    y_ssd, yc_ssd = bidirectional_ssd(lat, con, a, update_ctx)
    yb = gated_group_rmsnorm(y_ssd + d * lat[0], cols(p, COL_Z, SSD_WIDTH, 0), ssm_norm_g)
    out = jnp.concatenate([ya, yb], axis=-1) @ w_out
    out_c = None
    if update_ctx:
        n_ctx = hc.shape[1]
        qc = rms_norm(heads(cols(pc, COL_Q, NA_WIDTH, 0), NA_HEADS, NA_HEAD_DIM), q_norm_g)
        yac = context_attention(qc, kc, vc).reshape(b, n_ctx, NA_WIDTH) * jax.nn.silu(cols(pc, COL_GATE_A, NA_WIDTH, 0))
        ybc = gated_group_rmsnorm(yc_ssd + d * con[0], cols(pc, COL_Z, SSD_WIDTH, 0), ssm_norm_g)
        out_c = jnp.concatenate([yac, ybc], axis=-1) @ w_out
    return out, out_c


def short_conv_mixer(h, w_in, conv_w, w_out):
    p = h @ w_in
    bg, cg, hv, g = jnp.split(p, 4, axis=-1)
    y = bg * dw_conv(cg * hv, conv_w)
    return (jax.nn.silu(g) * y) @ w_out


def _fwd_setup_inputs(seed: int = 0) -> dict:
    key = jax.random.key(seed)
    ks = jax.random.split(key, 21)
    nrm = jax.random.normal
    dt0 = jnp.exp(jax.random.uniform(ks[11], (N_EVEN, 2, SSD_HEADS), minval=float(np.log(1e-3)), maxval=float(np.log(1e-1))))
    return {
        'x': nrm(ks[0], (BATCH, SEQ, D_MODEL), jnp.float32),
        'c': nrm(ks[1], (BATCH, D_MODEL), jnp.float32),
        'ctx': nrm(ks[2], (BATCH, CTX_LEN, D_MODEL), jnp.float32),
        'c_ctx': nrm(ks[3], (D_MODEL,), jnp.float32),
        'ada_w': nrm(ks[4], (DEPTH, D_MODEL, 3 * D_MODEL), jnp.float32) * (0.5 * D_MODEL ** -0.5),
        'ada_b': nrm(ks[5], (DEPTH, 3 * D_MODEL), jnp.float32) * 0.02,
        'norm_g': 1.0 + 0.05 * nrm(ks[6], (DEPTH, D_MODEL), jnp.float32),
        'na_ssd_w_in': nrm(ks[7], (N_EVEN, D_MODEL, EVEN_IN), jnp.float32) * D_MODEL ** -0.5,
        'ssd_conv_w': nrm(ks[8], (N_EVEN, SSD_CONV, SSD_XBC), jnp.float32) * SSD_CONV ** -0.5,
        'ssd_conv_b': nrm(ks[9], (N_EVEN, SSD_XBC), jnp.float32) * 0.02,
        'ssd_a_log': jnp.log(jax.random.uniform(ks[10], (N_EVEN, 2, SSD_HEADS), minval=1.0, maxval=16.0)),
        'ssd_dt_bias': dt0 + jnp.log(-jnp.expm1(-dt0)),
        'ssd_d': 1.0 + 0.1 * nrm(ks[12], (N_EVEN, SSD_HEADS), jnp.float32),
        'ssd_norm_g': 1.0 + 0.05 * nrm(ks[13], (N_EVEN, SSD_WIDTH), jnp.float32),
        'q_norm_g': 1.0 + 0.05 * nrm(ks[14], (N_EVEN, NA_HEAD_DIM), jnp.float32),
        'k_norm_g': 1.0 + 0.05 * nrm(ks[15], (N_EVEN, NA_HEAD_DIM), jnp.float32),
        'na_rpb': nrm(ks[16], (N_EVEN, NA_HEADS, 2 * NA_KH - 1, 2 * NA_KW - 1), jnp.float32) * 0.1,
        'na_ssd_w_out': nrm(ks[17], (N_EVEN, MIX_WIDTH, D_MODEL), jnp.float32) * MIX_WIDTH ** -0.5,
        'sc_w_in': nrm(ks[18], (N_ODD, D_MODEL, ODD_IN), jnp.float32) * D_MODEL ** -0.5,
        'sc_conv_w': nrm(ks[19], (N_ODD, SC_CONV, SC_WIDTH), jnp.float32) * SC_CONV ** -0.5,
        'sc_w_out': nrm(ks[20], (N_ODD, SC_WIDTH, D_MODEL), jnp.float32) * SC_WIDTH ** -0.5,
    }


def _fwd_reference(x, c, ctx, c_ctx, ada_w, ada_b, norm_g, na_ssd_w_in, ssd_conv_w, ssd_conv_b,
              ssd_a_log, ssd_dt_bias, ssd_d, ssd_norm_g, q_norm_g, k_norm_g, na_rpb,
              na_ssd_w_out, sc_w_in, sc_conv_w, sc_w_out):
    for i in range(DEPTH):
        update_ctx = any(j % 2 == 0 for j in range(i + 1, DEPTH))
        needs_ctx = (i % 2 == 0) or update_ctx
        shift, scale, gate = adaln(c, ada_w[i], ada_b[i])
        h = modulate(rms_norm(x, norm_g[i]), shift[:, None], scale[:, None])
        if needs_ctx:
            shift_c, scale_c, gate_c = adaln(c_ctx, ada_w[i], ada_b[i])
            hc = modulate(rms_norm(ctx, norm_g[i]), shift_c, scale_c)
        if i % 2 == 0:
            e = i // 2
            y, yc = na_ssd_mixer(h, hc, na_ssd_w_in[e], ssd_conv_w[e], ssd_conv_b[e], ssd_a_log[e],
                                 ssd_dt_bias[e], ssd_d[e], ssd_norm_g[e], q_norm_g[e], k_norm_g[e],
                                 na_rpb[e], na_ssd_w_out[e], update_ctx)
        else:
            o = i // 2
            y = short_conv_mixer(h, sc_w_in[o], sc_conv_w[o], sc_w_out[o])
            yc = short_conv_mixer(hc, sc_w_in[o], sc_conv_w[o], sc_w_out[o]) if update_ctx else None
        x = x + gate[:, None] * y
        if update_ctx:
            ctx = ctx + gate_c * yc
    return x


import jax as _jax
import jax.numpy as _jnp

TWIN_FORMAT = 'train_step'
FWD_PARAMS = ['x', 'c', 'ctx', 'c_ctx', 'ada_w', 'ada_b', 'norm_g', 'na_ssd_w_in', 'ssd_conv_w', 'ssd_conv_b', 'ssd_a_log', 'ssd_dt_bias', 'ssd_d', 'ssd_norm_g', 'q_norm_g', 'k_norm_g', 'na_rpb', 'na_ssd_w_out', 'sc_w_in', 'sc_conv_w', 'sc_w_out']
TWIN_WEIGHTS = ['c_ctx', 'ada_w', 'ada_b', 'norm_g', 'na_ssd_w_in', 'ssd_conv_w', 'ssd_conv_b', 'ssd_a_log', 'ssd_dt_bias', 'ssd_d', 'ssd_norm_g', 'q_norm_g', 'k_norm_g', 'na_rpb', 'na_ssd_w_out', 'sc_w_in', 'sc_conv_w', 'sc_w_out']
TWIN_DIFF_INPUT = 'x'
TWIN_INPUTS = ['x', 'c', 'ctx', 'c_ctx', 'ada_w', 'ada_b', 'norm_g', 'na_ssd_w_in', 'ssd_conv_w', 'ssd_conv_b', 'ssd_a_log', 'ssd_dt_bias', 'ssd_d', 'ssd_norm_g', 'q_norm_g', 'k_norm_g', 'na_rpb', 'na_ssd_w_out', 'sc_w_in', 'sc_conv_w', 'sc_w_out', 'loss_target', 'm_c_ctx', 'm_ada_w', 'm_ada_b', 'm_norm_g', 'm_na_ssd_w_in', 'm_ssd_conv_w', 'm_ssd_conv_b', 'm_ssd_a_log', 'm_ssd_dt_bias', 'm_ssd_d', 'm_ssd_norm_g', 'm_q_norm_g', 'm_k_norm_g', 'm_na_rpb', 'm_na_ssd_w_out', 'm_sc_w_in', 'm_sc_conv_w', 'm_sc_w_out', 'v_c_ctx', 'v_ada_w', 'v_ada_b', 'v_norm_g', 'v_na_ssd_w_in', 'v_ssd_conv_w', 'v_ssd_conv_b', 'v_ssd_a_log', 'v_ssd_dt_bias', 'v_ssd_d', 'v_ssd_norm_g', 'v_q_norm_g', 'v_k_norm_g', 'v_na_rpb', 'v_na_ssd_w_out', 'v_sc_w_in', 'v_sc_conv_w', 'v_sc_w_out']
TWIN_OUTPUTS = ['loss', 'grad_x', 'grad_c_ctx', 'grad_ada_w', 'grad_ada_b', 'grad_norm_g', 'grad_na_ssd_w_in', 'grad_ssd_conv_w', 'grad_ssd_conv_b', 'grad_ssd_a_log', 'grad_ssd_dt_bias', 'grad_ssd_d', 'grad_ssd_norm_g', 'grad_q_norm_g', 'grad_k_norm_g', 'grad_na_rpb', 'grad_na_ssd_w_out', 'grad_sc_w_in', 'grad_sc_conv_w', 'grad_sc_w_out', 'delta_c_ctx', 'delta_ada_w', 'delta_ada_b', 'delta_norm_g', 'delta_na_ssd_w_in', 'delta_ssd_conv_w', 'delta_ssd_conv_b', 'delta_ssd_a_log', 'delta_ssd_dt_bias', 'delta_ssd_d', 'delta_ssd_norm_g', 'delta_q_norm_g', 'delta_k_norm_g', 'delta_na_rpb', 'delta_na_ssd_w_out', 'delta_sc_w_in', 'delta_sc_conv_w', 'delta_sc_w_out', 'new_m_c_ctx', 'new_m_ada_w', 'new_m_ada_b', 'new_m_norm_g', 'new_m_na_ssd_w_in', 'new_m_ssd_conv_w', 'new_m_ssd_conv_b', 'new_m_ssd_a_log', 'new_m_ssd_dt_bias', 'new_m_ssd_d', 'new_m_ssd_norm_g', 'new_m_q_norm_g', 'new_m_k_norm_g', 'new_m_na_rpb', 'new_m_na_ssd_w_out', 'new_m_sc_w_in', 'new_m_sc_conv_w', 'new_m_sc_w_out', 'new_v_c_ctx', 'new_v_ada_w', 'new_v_ada_b', 'new_v_norm_g', 'new_v_na_ssd_w_in', 'new_v_ssd_conv_w', 'new_v_ssd_conv_b', 'new_v_ssd_a_log', 'new_v_ssd_dt_bias', 'new_v_ssd_d', 'new_v_ssd_norm_g', 'new_v_q_norm_g', 'new_v_k_norm_g', 'new_v_na_rpb', 'new_v_na_ssd_w_out', 'new_v_sc_w_in', 'new_v_sc_conv_w', 'new_v_sc_w_out']
TWIN_LEAF_KINDS = {'loss': 'loss', 'grad_x': 'grad_x', 'grad_c_ctx': 'grad_w', 'grad_ada_w': 'grad_w', 'grad_ada_b': 'grad_w', 'grad_norm_g': 'grad_w', 'grad_na_ssd_w_in': 'grad_w', 'grad_ssd_conv_w': 'grad_w', 'grad_ssd_conv_b': 'grad_w', 'grad_ssd_a_log': 'grad_w', 'grad_ssd_dt_bias': 'grad_w', 'grad_ssd_d': 'grad_w', 'grad_ssd_norm_g': 'grad_w', 'grad_q_norm_g': 'grad_w', 'grad_k_norm_g': 'grad_w', 'grad_na_rpb': 'grad_w', 'grad_na_ssd_w_out': 'grad_w', 'grad_sc_w_in': 'grad_w', 'grad_sc_conv_w': 'grad_w', 'grad_sc_w_out': 'grad_w', 'delta_c_ctx': 'delta_w', 'delta_ada_w': 'delta_w', 'delta_ada_b': 'delta_w', 'delta_norm_g': 'delta_w', 'delta_na_ssd_w_in': 'delta_w', 'delta_ssd_conv_w': 'delta_w', 'delta_ssd_conv_b': 'delta_w', 'delta_ssd_a_log': 'delta_w', 'delta_ssd_dt_bias': 'delta_w', 'delta_ssd_d': 'delta_w', 'delta_ssd_norm_g': 'delta_w', 'delta_q_norm_g': 'delta_w', 'delta_k_norm_g': 'delta_w', 'delta_na_rpb': 'delta_w', 'delta_na_ssd_w_out': 'delta_w', 'delta_sc_w_in': 'delta_w', 'delta_sc_conv_w': 'delta_w', 'delta_sc_w_out': 'delta_w', 'new_m_c_ctx': 'new_m', 'new_m_ada_w': 'new_m', 'new_m_ada_b': 'new_m', 'new_m_norm_g': 'new_m', 'new_m_na_ssd_w_in': 'new_m', 'new_m_ssd_conv_w': 'new_m', 'new_m_ssd_conv_b': 'new_m', 'new_m_ssd_a_log': 'new_m', 'new_m_ssd_dt_bias': 'new_m', 'new_m_ssd_d': 'new_m', 'new_m_ssd_norm_g': 'new_m', 'new_m_q_norm_g': 'new_m', 'new_m_k_norm_g': 'new_m', 'new_m_na_rpb': 'new_m', 'new_m_na_ssd_w_out': 'new_m', 'new_m_sc_w_in': 'new_m', 'new_m_sc_conv_w': 'new_m', 'new_m_sc_w_out': 'new_m', 'new_v_c_ctx': 'new_v', 'new_v_ada_w': 'new_v', 'new_v_ada_b': 'new_v', 'new_v_norm_g': 'new_v', 'new_v_na_ssd_w_in': 'new_v', 'new_v_ssd_conv_w': 'new_v', 'new_v_ssd_conv_b': 'new_v', 'new_v_ssd_a_log': 'new_v', 'new_v_ssd_dt_bias': 'new_v', 'new_v_ssd_d': 'new_v', 'new_v_ssd_norm_g': 'new_v', 'new_v_q_norm_g': 'new_v', 'new_v_k_norm_g': 'new_v', 'new_v_na_rpb': 'new_v', 'new_v_na_ssd_w_out': 'new_v', 'new_v_sc_w_in': 'new_v', 'new_v_sc_conv_w': 'new_v', 'new_v_sc_w_out': 'new_v'}


def _forward(args):
    return _fwd_reference(*[args[k] for k in FWD_PARAMS])


def _output_shape():
    def fwd():
        inp = _fwd_setup_inputs(0)
        return _fwd_reference(*[inp[k] for k in FWD_PARAMS])
    out = _jax.eval_shape(fwd)
    return out.shape, out.dtype

N_MICROBATCH = 1
ADAM_LR = 0.001
ADAM_B1 = 0.9
ADAM_B2 = 0.999
ADAM_EPS = 1e-08
ADAM_WD = 0.01
ADAM_STEP = 10
PER_EXAMPLE_BATCH_AXIS = {'x': 0, 'c': 0, 'ctx': 0, 'loss_target': 0}
SHARED_INPUTS = []
_WEIGHT_DTYPES = {'c_ctx': _jnp.float32, 'ada_w': _jnp.float32, 'ada_b': _jnp.float32, 'norm_g': _jnp.float32, 'na_ssd_w_in': _jnp.float32, 'ssd_conv_w': _jnp.float32, 'ssd_conv_b': _jnp.float32, 'ssd_a_log': _jnp.float32, 'ssd_dt_bias': _jnp.float32, 'ssd_d': _jnp.float32, 'ssd_norm_g': _jnp.float32, 'q_norm_g': _jnp.float32, 'k_norm_g': _jnp.float32, 'na_rpb': _jnp.float32, 'na_ssd_w_out': _jnp.float32, 'sc_w_in': _jnp.float32, 'sc_conv_w': _jnp.float32, 'sc_w_out': _jnp.float32}
MOMENT_SCALE = {'c_ctx': 6.852433e-02, 'ada_w': 1.099755e+00, 'ada_b': 3.344568e+00, 'norm_g': 5.527698e+00, 'na_ssd_w_in': 4.513037e-02, 'ssd_conv_w': 9.181016e-02, 'ssd_conv_b': 2.166326e-01, 'ssd_a_log': 3.649254e-01, 'ssd_dt_bias': 1.098311e-01, 'ssd_d': 2.677666e-01, 'ssd_norm_g': 2.156477e+00, 'q_norm_g': 6.472561e-02, 'k_norm_g': 6.612205e-02, 'na_rpb': 1.383217e-03, 'na_ssd_w_out': 1.414901e-01, 'sc_w_in': 1.345797e-01, 'sc_conv_w': 1.105357e+00, 'sc_w_out': 6.640610e-02}


def _to_microbatches(a, axis):
    t = _jnp.moveaxis(a, axis, 0)
    t = t.reshape((N_MICROBATCH, t.shape[0] // N_MICROBATCH) + t.shape[1:])
    return _jnp.moveaxis(t, 1, axis + 1)


def setup_inputs(seed: int = 0) -> dict:
    inp = _fwd_setup_inputs(seed)
    key = _jax.random.fold_in(_jax.random.key(seed), 7919)
    shape, _ = _output_shape()
    out = dict(inp)
    out["loss_target"] = _jax.random.normal(_jax.random.fold_in(key, 0), shape, _jnp.float32)
    for i, name in enumerate(TWIN_WEIGHTS):
        w = inp[name].astype(_jnp.float32)
        if MOMENT_SCALE is None:
            s = _jnp.sqrt(_jnp.mean(_jnp.square(w)) + 1e-30)
        else:
            s = MOMENT_SCALE[name]
        km, kv = _jax.random.split(_jax.random.fold_in(key, i + 1))
        out[name] = w
        out["m_" + name] = s * _jax.random.normal(km, w.shape, _jnp.float32)
        out["v_" + name] = (s * s) * _jax.random.uniform(kv, w.shape, _jnp.float32, 0.5, 1.5)
    if N_MICROBATCH > 1:
        for name, axis in PER_EXAMPLE_BATCH_AXIS.items():
            out[name] = _to_microbatches(out[name], axis)
    return {'x': out['x'], 'c': out['c'], 'ctx': out['ctx'], 'c_ctx': out['c_ctx'], 'ada_w': out['ada_w'], 'ada_b': out['ada_b'], 'norm_g': out['norm_g'], 'na_ssd_w_in': out['na_ssd_w_in'], 'ssd_conv_w': out['ssd_conv_w'], 'ssd_conv_b': out['ssd_conv_b'], 'ssd_a_log': out['ssd_a_log'], 'ssd_dt_bias': out['ssd_dt_bias'], 'ssd_d': out['ssd_d'], 'ssd_norm_g': out['ssd_norm_g'], 'q_norm_g': out['q_norm_g'], 'k_norm_g': out['k_norm_g'], 'na_rpb': out['na_rpb'], 'na_ssd_w_out': out['na_ssd_w_out'], 'sc_w_in': out['sc_w_in'], 'sc_conv_w': out['sc_conv_w'], 'sc_w_out': out['sc_w_out'], 'loss_target': out['loss_target'], 'm_c_ctx': out['m_c_ctx'], 'm_ada_w': out['m_ada_w'], 'm_ada_b': out['m_ada_b'], 'm_norm_g': out['m_norm_g'], 'm_na_ssd_w_in': out['m_na_ssd_w_in'], 'm_ssd_conv_w': out['m_ssd_conv_w'], 'm_ssd_conv_b': out['m_ssd_conv_b'], 'm_ssd_a_log': out['m_ssd_a_log'], 'm_ssd_dt_bias': out['m_ssd_dt_bias'], 'm_ssd_d': out['m_ssd_d'], 'm_ssd_norm_g': out['m_ssd_norm_g'], 'm_q_norm_g': out['m_q_norm_g'], 'm_k_norm_g': out['m_k_norm_g'], 'm_na_rpb': out['m_na_rpb'], 'm_na_ssd_w_out': out['m_na_ssd_w_out'], 'm_sc_w_in': out['m_sc_w_in'], 'm_sc_conv_w': out['m_sc_conv_w'], 'm_sc_w_out': out['m_sc_w_out'], 'v_c_ctx': out['v_c_ctx'], 'v_ada_w': out['v_ada_w'], 'v_ada_b': out['v_ada_b'], 'v_norm_g': out['v_norm_g'], 'v_na_ssd_w_in': out['v_na_ssd_w_in'], 'v_ssd_conv_w': out['v_ssd_conv_w'], 'v_ssd_conv_b': out['v_ssd_conv_b'], 'v_ssd_a_log': out['v_ssd_a_log'], 'v_ssd_dt_bias': out['v_ssd_dt_bias'], 'v_ssd_d': out['v_ssd_d'], 'v_ssd_norm_g': out['v_ssd_norm_g'], 'v_q_norm_g': out['v_q_norm_g'], 'v_k_norm_g': out['v_k_norm_g'], 'v_na_rpb': out['v_na_rpb'], 'v_na_ssd_w_out': out['v_na_ssd_w_out'], 'v_sc_w_in': out['v_sc_w_in'], 'v_sc_conv_w': out['v_sc_conv_w'], 'v_sc_w_out': out['v_sc_w_out']}


def _loss(weights, diff, rest, loss_target):
    with _jax.named_scope("forward"):
        args = {**rest, TWIN_DIFF_INPUT: diff, **{k: w.astype(_WEIGHT_DTYPES[k]) for k, w in weights.items()}}
        y = _forward(args)
    with _jax.named_scope("loss_head"):
        err = _jnp.square(y.astype(_jnp.float32) - loss_target)
        return 0.5 * _jnp.sum(_jnp.mean(err, axis=-1)) if err.ndim else 0.5 * err


def _adamw(w, g, m, v):
    m = ADAM_B1 * m + (1.0 - ADAM_B1) * g
    v = ADAM_B2 * v + (1.0 - ADAM_B2) * _jnp.square(g)
    m_hat = m / (1.0 - ADAM_B1 ** ADAM_STEP)
    v_hat = v / (1.0 - ADAM_B2 ** ADAM_STEP)
    delta = -ADAM_LR * (m_hat / (_jnp.sqrt(v_hat) + ADAM_EPS) + ADAM_WD * w)
    return delta, m, v


def reference(x, c, ctx, c_ctx, ada_w, ada_b, norm_g, na_ssd_w_in, ssd_conv_w, ssd_conv_b, ssd_a_log, ssd_dt_bias, ssd_d, ssd_norm_g, q_norm_g, k_norm_g, na_rpb, na_ssd_w_out, sc_w_in, sc_conv_w, sc_w_out, loss_target, m_c_ctx, m_ada_w, m_ada_b, m_norm_g, m_na_ssd_w_in, m_ssd_conv_w, m_ssd_conv_b, m_ssd_a_log, m_ssd_dt_bias, m_ssd_d, m_ssd_norm_g, m_q_norm_g, m_k_norm_g, m_na_rpb, m_na_ssd_w_out, m_sc_w_in, m_sc_conv_w, m_sc_w_out, v_c_ctx, v_ada_w, v_ada_b, v_norm_g, v_na_ssd_w_in, v_ssd_conv_w, v_ssd_conv_b, v_ssd_a_log, v_ssd_dt_bias, v_ssd_d, v_ssd_norm_g, v_q_norm_g, v_k_norm_g, v_na_rpb, v_na_ssd_w_out, v_sc_w_in, v_sc_conv_w, v_sc_w_out):
    given = dict(x=x, c=c, ctx=ctx, c_ctx=c_ctx, ada_w=ada_w, ada_b=ada_b, norm_g=norm_g, na_ssd_w_in=na_ssd_w_in, ssd_conv_w=ssd_conv_w, ssd_conv_b=ssd_conv_b, ssd_a_log=ssd_a_log, ssd_dt_bias=ssd_dt_bias, ssd_d=ssd_d, ssd_norm_g=ssd_norm_g, q_norm_g=q_norm_g, k_norm_g=k_norm_g, na_rpb=na_rpb, na_ssd_w_out=na_ssd_w_out, sc_w_in=sc_w_in, sc_conv_w=sc_conv_w, sc_w_out=sc_w_out, loss_target=loss_target, m_c_ctx=m_c_ctx, m_ada_w=m_ada_w, m_ada_b=m_ada_b, m_norm_g=m_norm_g, m_na_ssd_w_in=m_na_ssd_w_in, m_ssd_conv_w=m_ssd_conv_w, m_ssd_conv_b=m_ssd_conv_b, m_ssd_a_log=m_ssd_a_log, m_ssd_dt_bias=m_ssd_dt_bias, m_ssd_d=m_ssd_d, m_ssd_norm_g=m_ssd_norm_g, m_q_norm_g=m_q_norm_g, m_k_norm_g=m_k_norm_g, m_na_rpb=m_na_rpb, m_na_ssd_w_out=m_na_ssd_w_out, m_sc_w_in=m_sc_w_in, m_sc_conv_w=m_sc_conv_w, m_sc_w_out=m_sc_w_out, v_c_ctx=v_c_ctx, v_ada_w=v_ada_w, v_ada_b=v_ada_b, v_norm_g=v_norm_g, v_na_ssd_w_in=v_na_ssd_w_in, v_ssd_conv_w=v_ssd_conv_w, v_ssd_conv_b=v_ssd_conv_b, v_ssd_a_log=v_ssd_a_log, v_ssd_dt_bias=v_ssd_dt_bias, v_ssd_d=v_ssd_d, v_ssd_norm_g=v_ssd_norm_g, v_q_norm_g=v_q_norm_g, v_k_norm_g=v_k_norm_g, v_na_rpb=v_na_rpb, v_na_ssd_w_out=v_na_ssd_w_out, v_sc_w_in=v_sc_w_in, v_sc_conv_w=v_sc_conv_w, v_sc_w_out=v_sc_w_out)
    weights = {n: given[n] for n in TWIN_WEIGHTS}
    shared = {n: given[n] for n in SHARED_INPUTS}
    per_example = {n: given[n] for n in ['x', 'c', 'ctx']}
    grad_fn = _jax.value_and_grad(_loss, argnums=(0, 1))

    def one_microbatch(ex, loss_target):
        ex = dict(ex)
        diff = ex.pop(TWIN_DIFF_INPUT)
        return grad_fn(weights, diff, {**shared, **ex}, loss_target)

    if N_MICROBATCH == 1:
        loss, (grad_w, grad_x) = one_microbatch(per_example, given["loss_target"])
    else:
        def body(carry, xs):
            loss_sum, grad_sum = carry
            l_k, (gw_k, gx_k) = one_microbatch(xs[0], xs[1])
            with _jax.named_scope("update"):
                return (loss_sum + l_k, _jax.tree.map(_jnp.add, grad_sum, gw_k)), gx_k

        init = (_jnp.zeros((), _jnp.float32), _jax.tree.map(_jnp.zeros_like, weights))
        (loss, grad_w), grad_x = _jax.lax.scan(body, init, (per_example, given["loss_target"]))
    with _jax.named_scope("update"):
        delta_w, new_m, new_v = {}, {}, {}
        for n in TWIN_WEIGHTS:
            delta_w[n], new_m[n], new_v[n] = _adamw(weights[n], grad_w[n], given["m_" + n], given["v_" + n])
    return (loss, grad_x, *[grad_w[n] for n in TWIN_WEIGHTS], *[delta_w[n] for n in TWIN_WEIGHTS],
            *[new_m[n] for n in TWIN_WEIGHTS], *[new_v[n] for n in TWIN_WEIGHTS])
```

```python
import functools
import math
from types import SimpleNamespace

import jax
import jax.numpy as jnp
from jax import lax
from jax.experimental import pallas as pl
from jax.experimental.pallas import tpu as pltpu

F32 = jnp.float32
BF16 = jnp.bfloat16
U32 = jnp.uint32
HIGHEST = lax.Precision.HIGHEST
MESH = pl.DeviceIdType.MESH
ANY = pl.BlockSpec(memory_space=pl.ANY)
VMEM = pl.BlockSpec(memory_space=pltpu.VMEM)

NDEV = 8
DEPTH = 4
GRID_W = 64
EPS = 1e-6
NA_HEADS = 16
NA_HEAD_DIM = 128
NA_KH = 8
NA_KW = 16
SSD_HEAD_DIM = 64
SSD_GROUPS = 8
SSD_STATE = 128
SSD_CONV = 5
SSD_CHUNK = 128
SC_CONV = 3
ADAM_LR = 0.001
ADAM_B1 = 0.9
ADAM_B2 = 0.999
ADAM_EPS = 1e-08
ADAM_WD = 0.01
ADAM_STEP = 10
NEG = -1e30
VMEM_LIMIT = 56 * 1024 * 1024
PAD_TO = 512


def _pos():
    return lax.axis_index("x"), lax.axis_index("y"), lax.axis_index("c")


def _my_index():
    x, y, c = _pos()
    return 4 * x + 2 * y + c


def _pick(n, prefs):
    for p in prefs:
        if n % p == 0:
            return p
    return n


def _cparams(**kw):
    return pltpu.CompilerParams(vmem_limit_bytes=VMEM_LIMIT, **kw)


def _small_allgather(v, name):
    rows, lanes = v.shape

    def body(x_ref, out_ref, sum_ref, send_sems, recv_sems):
        x, y, c = _pos()
        me = 4 * x + 2 * y + c
        out_ref[me] = x_ref[...]
        copies = []
        for k in range(1, NDEV):
            peer = (1 - x if k & 4 else x, 1 - y if k & 2 else y, 1 - c if k & 1 else c)
            cp = pltpu.make_async_remote_copy(src_ref=x_ref, dst_ref=out_ref.at[me], send_sem=send_sems.at[k - 1],
                                              recv_sem=recv_sems.at[k - 1], device_id=peer, device_id_type=MESH)
            cp.start()
            copies.append(cp)
        for cp in copies:
            cp.wait()
        acc = out_ref[0]
        for j in range(1, NDEV):
            acc = acc + out_ref[j]
        sum_ref[...] = acc

    return pl.pallas_call(
        body, name=name,
        out_shape=(jax.ShapeDtypeStruct((NDEV, rows, lanes), v.dtype), jax.ShapeDtypeStruct((rows, lanes), v.dtype)),
        in_specs=[VMEM], out_specs=(VMEM, VMEM),
        scratch_shapes=[pltpu.SemaphoreType.DMA((NDEV - 1,)), pltpu.SemaphoreType.DMA((NDEV - 1,))],
        compiler_params=_cparams(),
    )(v)


def _window(ref, axis, idx, length):
    sl = [slice(None)] * len(ref.shape)
    sl[axis] = pl.ds(pl.multiple_of(idx * length, min(length & -length, 1024)), length)
    return ref.at[tuple(sl)]


def _big_allgather(shards, axes, name):
    n = len(shards)
    out_shapes = []
    for s, ax in zip(shards, axes):
        shp = list(s.shape)
        shp[ax] *= NDEV
        out_shapes.append(jax.ShapeDtypeStruct(tuple(shp), s.dtype))

    def body(*refs):
        xs, outs = refs[:n], refs[n:2 * n]
        send_sems, recv_sems, local_sems = refs[2 * n:]
        x, y, c = _pos()
        me, sib = (x, y, c), (x, y, 1 - c)
        chips = [(1 - x, y), (x, 1 - y), (1 - x, 1 - y)]

        def win(a, px, py, pc):
            return _window(outs[a], axes[a], 4 * px + 2 * py + pc, shards[a].shape[axes[a]])

        def copy(a, k, block, to, src=None):
            return pltpu.make_async_remote_copy(src_ref=win(a, *block) if src is None else src, dst_ref=win(a, *block),
                                                send_sem=send_sems.at[a * 7 + k], recv_sem=recv_sems.at[a * 7 + k],
                                                device_id=to, device_id_type=MESH)

        mine = [pltpu.make_async_copy(xs[a], win(a, *me), local_sems.at[a]) for a in range(n)]
        for cp in mine:
            cp.start()
        first = []
        for a in range(n):
            first.append(copy(a, 0, me, sib, src=xs[a]))
            first += [copy(a, 1 + j, me, (*chip, c), src=xs[a]) for j, chip in enumerate(chips)]
        for cp in first:
            cp.start()
        passed = []
        for j, chip in enumerate(chips):
            for a in range(n):
                copy(a, 1 + j, (*chip, c), me).wait_recv()
                cp = copy(a, 4 + j, (*chip, c), sib)
                cp.start()
                passed.append(cp)
        for a in range(n):
            copy(a, 0, sib, me).wait_recv()
            for j, chip in enumerate(chips):
                copy(a, 4 + j, (*chip, 1 - c), me).wait_recv()
        for cp in first + passed:
            cp.wait_send()
        for cp in mine:
            cp.wait()

    return pl.pallas_call(
        body, name=name, out_shape=tuple(out_shapes), in_specs=[ANY] * n, out_specs=tuple([ANY] * n),
        scratch_shapes=[pltpu.SemaphoreType.DMA((7 * n,)), pltpu.SemaphoreType.DMA((7 * n,)), pltpu.SemaphoreType.DMA((n,))],
        compiler_params=_cparams(),
    )(*shards)


def _rs_stage_a(grads, axes, lens, name):
    n = len(grads)
    out_shapes = []
    for g, ax, ln in zip(grads, axes, lens):
        shp = list(g.shape)
        shp[ax] = ln
        out_shapes.append(jax.ShapeDtypeStruct((4, *shp), g.dtype))

    def body(*refs):
        gs, outs = refs[:n], refs[n:2 * n]
        send_sems, recv_sems = refs[2 * n:]
        x, y, c = _pos()
        sib = (x, y, 1 - c)
        copies = []
        for a in range(n):
            for k in range(4):
                src = _window(gs[a], axes[a], 2 * k + (1 - c), lens[a])
                cp = pltpu.make_async_remote_copy(src_ref=src, dst_ref=outs[a].at[k], send_sem=send_sems.at[a * 4 + k],
                                                  recv_sem=recv_sems.at[a * 4 + k], device_id=sib, device_id_type=MESH)
                cp.start()
                copies.append(cp)
        for cp in copies:
            cp.wait()

    return pl.pallas_call(
        body, name=name, out_shape=tuple(out_shapes), in_specs=[ANY] * n, out_specs=tuple([ANY] * n),
        scratch_shapes=[pltpu.SemaphoreType.DMA((4 * n,)), pltpu.SemaphoreType.DMA((4 * n,))],
        compiler_params=_cparams(),
    )(*grads)


def _rs_stage_b(parts, name):
    n = len(parts)
    out_shapes = [jax.ShapeDtypeStruct((3, *p.shape[1:]), p.dtype) for p in parts]

    def body(*refs):
        ps, outs = refs[:n], refs[n:2 * n]
        send_sems, recv_sems = refs[2 * n:]
        x, y, c = _pos()
        chips = [(1 - x, y), (x, 1 - y), (1 - x, 1 - y)]
        copies = []
        for a in range(n):
            for j, (px, py) in enumerate(chips):
                cp = pltpu.make_async_remote_copy(src_ref=ps[a].at[2 * px + py], dst_ref=outs[a].at[j],
                                                  send_sem=send_sems.at[a * 3 + j], recv_sem=recv_sems.at[a * 3 + j],
                                                  device_id=(px, py, c), device_id_type=MESH)
                cp.start()
                copies.append(cp)
        for cp in copies:
            cp.wait()

    return pl.pallas_call(
        body, name=name, out_shape=tuple(out_shapes), in_specs=[ANY] * n, out_specs=tuple([ANY] * n),
        scratch_shapes=[pltpu.SemaphoreType.DMA((3 * n,)), pltpu.SemaphoreType.DMA((3 * n,))],
        compiler_params=_cparams(),
    )(*parts)


def _matmul(a, b, *, ta=False, tb=False, tm, tn, tk, name):
    m, kdim = (a.shape[1], a.shape[0]) if ta else a.shape
    n = b.shape[0] if tb else b.shape[1]
    assert (b.shape[1] if tb else b.shape[0]) == kdim, (a.shape, b.shape, ta, tb)
    assert m % tm == 0 and n % tn == 0 and kdim % tk == 0, (m, n, kdim, tm, tn, tk)
    nk = kdim // tk
    dn = (((0 if ta else 1,), (1 if tb else 0,)), ((), ()))

    def body(a_ref, b_ref, o_ref, acc_ref):
        k = pl.program_id(2)
        part = lax.dot_general(a_ref[...].astype(BF16), b_ref[...].astype(BF16), dn, preferred_element_type=F32)
        if nk == 1:
            o_ref[...] = part
        else:
            @pl.when(k == 0)
            def _():
                acc_ref[...] = part

            @pl.when(k > 0)
            def _():
                acc_ref[...] += part

            @pl.when(k == nk - 1)
            def _():
                o_ref[...] = acc_ref[...]

    a_spec = pl.BlockSpec((tk, tm), lambda i, j, k: (k, i)) if ta else pl.BlockSpec((tm, tk), lambda i, j, k: (i, k))
    b_spec = pl.BlockSpec((tn, tk), lambda i, j, k: (j, k)) if tb else pl.BlockSpec((tk, tn), lambda i, j, k: (k, j))
    acc_shape = (tm, tn) if nk > 1 else (8, 128)
    return pl.pallas_call(
        body, name=name, grid=(m // tm, n // tn, nk), out_shape=jax.ShapeDtypeStruct((m, n), F32),
        in_specs=[a_spec, b_spec], out_specs=pl.BlockSpec((tm, tn), lambda i, j, k: (i, j)),
        scratch_shapes=[pltpu.VMEM(acc_shape, F32)], compiler_params=_cparams(),
    )(a, b)


def _linear(a, w, gslot, *, w_is_nk, tiles, name):
    tm, tn, tk = tiles

    @jax.custom_vjp
    def lin(a, w, gslot):
        return _matmul(a, w, tb=w_is_nk, tm=tm, tn=tn, tk=tk, name=name + "_fwd")

    def fwd(a, w, gslot):
        return lin(a, w, gslot), (a, w)

    def bwd(res, g):
        a, w = res
        gb = g.astype(BF16)
        t, kdim = a.shape
        n = g.shape[1]
        tt = _pick(t, (1408, 768, 512, 256, 128))
        tkk = _pick(kdim, (2048, 1024, 512, 256, 128))
        tnn = _pick(n, (512, 256, 128))
        da = _matmul(gb, w, tb=not w_is_nk, tm=tt, tn=tkk, tk=tnn, name=name + "_bwd_a")
        tok = _pick(t, (768, 512, 256, 128))
        if w_is_nk:
            dw = _matmul(gb, a, ta=True, tm=tnn, tn=tkk, tk=tok, name=name + "_bwd_w")
        else:
            tkw = _pick(kdim, (512, 256, 128))
            tnw = _pick(n, (2048, 1024, 512, 256, 128))
            dw = _matmul(a, gb, ta=True, tm=tkw, tn=tnw, tk=tok, name=name + "_bwd_w")
        return da.astype(a.dtype), jnp.zeros_like(w), dw

    lin.defvjp(fwd, bwd)
    return lin(a, w, gslot)


def _pack_transposed(w, name):
    nl, kdim, r = w.shape
    half = kdim // 2
    tc = _pick(half, (256, 128))

    def body(lo_ref, hi_ref, o_ref):
        lo = pltpu.bitcast(lo_ref[0].astype(BF16).astype(F32).T, U32) >> 16
        hi = pltpu.bitcast(hi_ref[0].astype(BF16).astype(F32).T, U32) & jnp.uint32(0xFFFF0000)
        o_ref[0] = pltpu.bitcast(hi | lo, F32)

    nb = half // tc
    return pl.pallas_call(
        body, name=name, grid=(nl, nb), out_shape=jax.ShapeDtypeStruct((nl, r, half), F32),
        in_specs=[pl.BlockSpec((1, tc, r), lambda l, t: (l, t, 0)), pl.BlockSpec((1, tc, r), lambda l, t: (l, t + nb, 0))],
        out_specs=pl.BlockSpec((1, r, tc), lambda l, t: (l, 0, t)), compiler_params=_cparams(),
    )(w, w)


def _unpack(packed, n_pad, name):
    nl, n, half = packed.shape
    tr = math.gcd(math.gcd(n, n_pad - n), 64) if n_pad > n else _pick(n, (64, 32, 16))
    nin = n // tr

    def body(p_ref, o_ref):
        t = pl.program_id(1)

        @pl.when(t < nin)
        def _():
            u = pltpu.bitcast(p_ref[0], U32)
            o_ref[0, :, :half] = pltpu.bitcast(u << 16, F32).astype(BF16)
            o_ref[0, :, half:] = pltpu.bitcast(u & jnp.uint32(0xFFFF0000), F32).astype(BF16)

        @pl.when(t >= nin)
        def _():
            o_ref[...] = jnp.zeros_like(o_ref)

    return pl.pallas_call(
        body, name=name, grid=(nl, n_pad // tr), out_shape=jax.ShapeDtypeStruct((nl, n_pad, 2 * half), BF16),
        in_specs=[pl.BlockSpec((1, tr, half), lambda l, t: (l, jnp.minimum(t, nin - 1), 0))],
        out_specs=pl.BlockSpec((1, tr, 2 * half), lambda l, t: (l, t, 0)), compiler_params=_cparams(),
    )(packed)


def _cast_bf16(w, name):
    nl, r, c = w.shape
    tr = _pick(r, (512, 256, 128, 64, 32, 16))

    def body(w_ref, o_ref):
        o_ref[...] = w_ref[...].astype(BF16)

    return pl.pallas_call(
        body, name=name, grid=(nl, r // tr), out_shape=jax.ShapeDtypeStruct(w.shape, BF16),
        in_specs=[pl.BlockSpec((1, tr, c), lambda l, t: (l, t, 0))], out_specs=pl.BlockSpec((1, tr, c), lambda l, t: (l, t, 0)),
        compiler_params=_cparams(),
    )(w)


def _na_scores(q_ref, k_ref, kc_ref, b_ref, r, rows):
    rs = jnp.clip(r - NA_KH // 2, 0, rows - NA_KH)
    off = rs - r + NA_KH - 1
    start = pl.multiple_of(rs * GRID_W, GRID_W)
    scale = NA_HEAD_DIM ** -0.5
    nt = (((1,), (1,)), ((), ()))
    q = q_ref[...]
    kw = k_ref[pl.ds(start, NA_KH * GRID_W), :]
    s1 = lax.dot_general(q, kw, nt, preferred_element_type=F32) * scale + b_ref[0, off]
    s2 = lax.dot_general(q, kc_ref[...], nt, preferred_element_type=F32) * scale
    m = jnp.maximum(jnp.max(s1, axis=-1, keepdims=True), jnp.max(s2, axis=-1, keepdims=True))
    e1 = jnp.exp(s1 - m)
    e2 = jnp.exp(s2 - m)
    inv = 1.0 / (jnp.sum(e1, axis=-1, keepdims=True) + jnp.sum(e2, axis=-1, keepdims=True))
    return q, kw, start, off, e1 * inv, e2 * inv


def _na_specs(seq, nctx):
    dh = NA_HEAD_DIM
    win = NA_KH * GRID_W
    return [
        pl.BlockSpec((GRID_W, dh), lambda h, r: (r, h)),
        pl.BlockSpec((seq, dh), lambda h, r: (0, h)),
        pl.BlockSpec((seq, dh), lambda h, r: (0, h)),
        pl.BlockSpec((nctx, dh), lambda h, r: (0, h)),
        pl.BlockSpec((nctx, dh), lambda h, r: (0, h)),
        pl.BlockSpec((1, NA_KH, GRID_W, win), lambda h, r: (h, 0, 0, 0)),
    ]


def _na_forward(name, q, k, v, kc, vc, biasw):
    seq, width = q.shape
    nctx = kc.shape[0]
    rows = seq // GRID_W

    def body(q_ref, k_ref, v_ref, kc_ref, vc_ref, b_ref, o_ref):
        r = pl.program_id(1)
        _, _, start, _, p1, p2 = _na_scores(q_ref, k_ref, kc_ref, b_ref, r, rows)
        vw = v_ref[pl.ds(start, NA_KH * GRID_W), :]
        o_ref[...] = (jnp.dot(p1.astype(BF16), vw, preferred_element_type=F32)
                      + jnp.dot(p2.astype(BF16), vc_ref[...], preferred_element_type=F32))

    return pl.pallas_call(
        body, name=name, grid=(NA_HEADS, rows), out_shape=jax.ShapeDtypeStruct((seq, width), F32),
        in_specs=_na_specs(seq, nctx), out_specs=pl.BlockSpec((GRID_W, NA_HEAD_DIM), lambda h, r: (r, h)),
        compiler_params=_cparams(),
    )(q, k, v, kc, vc, biasw)


def _na_backward(name, q, k, v, kc, vc, biasw, do):
    seq, width = q.shape
    nctx = kc.shape[0]
    rows = seq // GRID_W
    win = NA_KH * GRID_W
    dh = NA_HEAD_DIM
    scale = dh ** -0.5
    nt = (((1,), (1,)), ((), ()))
    tn = (((0,), (0,)), ((), ()))

    def body(q_ref, k_ref, v_ref, kc_ref, vc_ref, b_ref, do_ref, dq_ref, dk_ref, dv_ref, dkc_ref, dvc_ref, db_ref):
        r = pl.program_id(1)

        @pl.when(r == 0)
        def _():
            dk_ref[...] = jnp.zeros_like(dk_ref)
            dv_ref[...] = jnp.zeros_like(dv_ref)
            dkc_ref[...] = jnp.zeros_like(dkc_ref)
            dvc_ref[...] = jnp.zeros_like(dvc_ref)
            db_ref[...] = jnp.zeros_like(db_ref)

        q, kw, start, off, p1, p2 = _na_scores(q_ref, k_ref, kc_ref, b_ref, r, rows)
        sl = pl.ds(start, win)
        vw = v_ref[sl, :]
        vcb = vc_ref[...]
        kcb = kc_ref[...]
        dob = do_ref[...].astype(BF16)
        p1b, p2b = p1.astype(BF16), p2.astype(BF16)
        dv_ref[sl, :] += lax.dot_general(p1b, dob, tn, preferred_element_type=F32)
        dvc_ref[...] += lax.dot_general(p2b, dob, tn, preferred_element_type=F32)
        dp1 = lax.dot_general(dob, vw, nt, preferred_element_type=F32)
        dp2 = lax.dot_general(dob, vcb, nt, preferred_element_type=F32)
        delta = jnp.sum(dp1 * p1, axis=-1, keepdims=True) + jnp.sum(dp2 * p2, axis=-1, keepdims=True)
        ds1 = p1 * (dp1 - delta)
        ds2 = p2 * (dp2 - delta)
        db_ref[0, off] += ds1
        ds1b = (ds1 * scale).astype(BF16)
        ds2b = (ds2 * scale).astype(BF16)
        dq_ref[...] = (jnp.dot(ds1b, kw, preferred_element_type=F32) + jnp.dot(ds2b, kcb, preferred_element_type=F32))
        dk_ref[sl, :] += lax.dot_general(ds1b, q, tn, preferred_element_type=F32)
        dkc_ref[...] += lax.dot_general(ds2b, q, tn, preferred_element_type=F32)

    row_spec = pl.BlockSpec((GRID_W, dh), lambda h, r: (r, h))
    seq_spec = pl.BlockSpec((seq, dh), lambda h, r: (0, h))
    ctx_spec = pl.BlockSpec((nctx, dh), lambda h, r: (0, h))
    b_spec = pl.BlockSpec((1, NA_KH, GRID_W, win), lambda h, r: (h, 0, 0, 0))
    return pl.pallas_call(
        body, name=name, grid=(NA_HEADS, rows),
        out_shape=(jax.ShapeDtypeStruct((seq, width), F32), jax.ShapeDtypeStruct((seq, width), F32),
                   jax.ShapeDtypeStruct((seq, width), F32), jax.ShapeDtypeStruct((nctx, width), F32),
                   jax.ShapeDtypeStruct((nctx, width), F32), jax.ShapeDtypeStruct(biasw.shape, F32)),
        in_specs=_na_specs(seq, nctx) + [row_spec],
        out_specs=(row_spec, seq_spec, seq_spec, ctx_spec, ctx_spec, b_spec),
        compiler_params=_cparams(),
    )(q, k, v, kc, vc, biasw, do)


def _na_attention(e, q, k, v, kc, vc, biasw):
    @jax.custom_vjp
    def attn(q, k, v, kc, vc, biasw):
        return _na_forward(f"na_fwd{e}", q.astype(BF16), k.astype(BF16), v.astype(BF16), kc.astype(BF16), vc.astype(BF16), biasw)

    def fwd(q, k, v, kc, vc, biasw):
        res = (q.astype(BF16), k.astype(BF16), v.astype(BF16), kc.astype(BF16), vc.astype(BF16), biasw)
        return _na_forward(f"na_fwd{e}", *res), res

    def bwd(res, do):
        return _na_backward(f"na_bwd{e}", *res, do)

    attn.defvjp(fwd, bwd)
    return attn(q, k, v, kc, vc, biasw)


def _bias_windows(rpb):
    col = jnp.arange(GRID_W)
    dc = jnp.clip(col[None, :] - col[:, None], -(NA_KW - 1), NA_KW - 1) + NA_KW - 1
    onehot = (dc[None] == jnp.arange(2 * NA_KW - 1)[:, None, None]).astype(F32)
    tq = jnp.einsum("hrd,dqk->hrqk", rpb, onehot, precision=HIGHEST)
    col_start = jnp.clip(col - NA_KW // 2, 0, GRID_W - NA_KW)
    in_win = (col[None, :] >= col_start[:, None]) & (col[None, :] < col_start[:, None] + NA_KW)
    wins = jnp.stack([tq[:, off:off + NA_KH] for off in range(NA_KH)], axis=1)
    wins = jnp.where(in_win[None, None, None], wins, NEG)
    return wins.transpose(0, 1, 3, 2, 4).reshape(rpb.shape[0], NA_KH, GRID_W, NA_KH * GRID_W)


def _ssd_step(h, x, bm, cm, dt, dt_t, dta, dta_t, *, direction, hpg, pdim):
    ln = x.shape[0]
    row = lax.broadcasted_iota(jnp.int32, (ln, ln), 0)
    colm = lax.broadcasted_iota(jnp.int32, (ln, ln), 1)
    valid = (colm - row) * (1 - 2 * direction) <= 0
    tri = valid.astype(F32)
    cs_col = jnp.dot(tri, dta, precision=HIGHEST, preferred_element_type=F32)
    cs_row = lax.dot_general(dta_t, tri, (((1,), (1,)), ((), ())), precision=HIGHEST, preferred_element_type=F32)
    tot = jnp.sum(dta, axis=0, keepdims=True)
    cb16, bb16 = cm.astype(BF16), bm.astype(BF16)
    cbm = lax.dot_general(cb16, bb16, (((1,), (1,)), ((), ())), preferred_element_type=F32)
    lane_head = lax.broadcasted_iota(jnp.int32, (1, hpg * pdim), 1) // pdim
    sub_head = lax.broadcasted_iota(jnp.int32, (hpg * pdim, 1), 0) // pdim
    y = jnp.zeros((ln, hpg * pdim), F32)
    es = jnp.zeros((ln, hpg * pdim), F32)
    we = jnp.zeros((ln, hpg * pdim), F32)
    dend = jnp.zeros((hpg * pdim, 1), F32)
    for r in range(hpg):
        cc = cs_col[:, r:r + 1]
        cr = cs_row[r:r + 1, :]
        decay = jnp.exp(jnp.where(valid, cc - cr, NEG))
        wm = (cbm * decay * dt_t[r:r + 1, :]).astype(BF16)
        mask = (lane_head == r).astype(F32)
        y = y + jnp.dot(wm, (x * mask).astype(BF16), preferred_element_type=F32)
        es = es + mask * jnp.exp(cc)
        we = we + mask * (jnp.exp(tot[:, r:r + 1] - cc) * dt[:, r:r + 1])
        dend = dend + (sub_head == r).astype(F32) * jnp.exp(tot[:, r:r + 1])
    y = y + es * lax.dot_general(cb16, h.astype(BF16), (((1,), (1,)), ((), ())), preferred_element_type=F32)
    h_new = h * dend + lax.dot_general((x * we).astype(BF16), bb16, (((0,), (0,)), ((), ())), preferred_element_type=F32)
    return y, h_new


def _ssd_chunk_of(d, s, ncc, nc):
    return jnp.where(d == 0, s, jnp.where(s < ncc, ncc - 1 - s, nc - 1 - s + ncc))


def _ssd_specs(cfg, step_of):
    hp, n, ln, hpg = cfg.hpg * SSD_HEAD_DIM, SSD_STATE, SSD_CHUNK, cfg.hpg
    ncc, nc = cfg.nctx // ln, cfg.t // ln

    def ch(d, s):
        return _ssd_chunk_of(d, step_of(s), ncc, nc)

    return dict(
        x=pl.BlockSpec((ln, hp), lambda d, g, s: (ch(d, s), g)),
        bc=pl.BlockSpec((ln, n), lambda d, g, s: (ch(d, s), g)),
        dt=pl.BlockSpec((1, 1, ln, hpg), lambda d, g, s: (d, g, ch(d, s), 0)),
        dt_t=pl.BlockSpec((1, 1, 8, ln), lambda d, g, s: (d, g, 0, ch(d, s))),
        y=pl.BlockSpec((1, ln, hp), lambda d, g, s: (d, ch(d, s), g)),
        bc2=pl.BlockSpec((1, ln, n), lambda d, g, s: (d, ch(d, s), g)),
        h=pl.BlockSpec((1, 1, 1, hp, n), lambda d, g, s: (d, g, step_of(s), 0, 0)),
    )


def _ssd_forward(cfg, name, xs, bm, cm, dt, dt_t, dta, dta_t):
    hp, n, hpg = cfg.hpg * SSD_HEAD_DIM, SSD_STATE, cfg.hpg
    nc = cfg.t // SSD_CHUNK
    sp = _ssd_specs(cfg, lambda s: s)

    def body(x_ref, b_ref, c_ref, dt_ref, dtt_ref, dta_ref, dtat_ref, y_ref, hs_ref, h_ref):
        d, s = pl.program_id(0), pl.program_id(2)

        @pl.when(s == 0)
        def _():
            h_ref[...] = jnp.zeros_like(h_ref)

        h = h_ref[...]
        hs_ref[0, 0, 0] = h
        y, h_new = _ssd_step(h, x_ref[...], b_ref[...], c_ref[...], dt_ref[0, 0], dtt_ref[0, 0], dta_ref[0, 0],
                             dtat_ref[0, 0], direction=d, hpg=hpg, pdim=SSD_HEAD_DIM)
        y_ref[0] = y
        h_ref[...] = h_new

    return pl.pallas_call(
        body, name=name, grid=(2, SSD_GROUPS, nc),
        out_shape=(jax.ShapeDtypeStruct((2, cfg.t, cfg.ssd_width), F32),
                   jax.ShapeDtypeStruct((2, SSD_GROUPS, nc, hp, n), F32)),
        in_specs=[sp["x"], sp["bc"], sp["bc"], sp["dt"], sp["dt_t"], sp["dt"], sp["dt_t"]],
        out_specs=(sp["y"], sp["h"]), scratch_shapes=[pltpu.VMEM((hp, n), F32)], compiler_params=_cparams(),
    )(xs, bm, cm, dt, dt_t, dta, dta_t)


def _ssd_backward(cfg, name, xs, bm, cm, dt, dt_t, dta, dta_t, hsave, dy):
    hp, n, hpg = cfg.hpg * SSD_HEAD_DIM, SSD_STATE, cfg.hpg
    nc = cfg.t // SSD_CHUNK
    sp = _ssd_specs(cfg, lambda s: nc - 1 - s)

    def body(x_ref, b_ref, c_ref, dt_ref, dtt_ref, dta_ref, dtat_ref, hs_ref, dy_ref,
             dx_ref, db_ref, dc_ref, ddt_ref, ddtt_ref, ddta_ref, ddtat_ref, dh_ref):
        d, s = pl.program_id(0), pl.program_id(2)

        @pl.when(s == 0)
        def _():
            dh_ref[...] = jnp.zeros_like(dh_ref)

        step = functools.partial(_ssd_step, direction=d, hpg=hpg, pdim=SSD_HEAD_DIM)
        _, vjp = jax.vjp(step, hs_ref[0, 0, 0], x_ref[...], b_ref[...], c_ref[...], dt_ref[0, 0], dtt_ref[0, 0],
                         dta_ref[0, 0], dtat_ref[0, 0])
        dh, dx, db, dc, ddt, ddtt, ddta, ddtat = vjp((dy_ref[0], dh_ref[...]))
        dh_ref[...] = dh
        dx_ref[0] = dx
        db_ref[0] = db
        dc_ref[0] = dc
        ddt_ref[0, 0] = ddt
        ddtt_ref[0, 0] = ddtt
        ddta_ref[0, 0] = ddta
        ddtat_ref[0, 0] = ddtat

    gn = SSD_GROUPS * n
    return pl.pallas_call(
        body, name=name, grid=(2, SSD_GROUPS, nc),
        out_shape=(jax.ShapeDtypeStruct((2, cfg.t, cfg.ssd_width), F32), jax.ShapeDtypeStruct((2, cfg.t, gn), F32),
                   jax.ShapeDtypeStruct((2, cfg.t, gn), F32), jax.ShapeDtypeStruct(dt.shape, F32),
                   jax.ShapeDtypeStruct(dt_t.shape, F32), jax.ShapeDtypeStruct(dt.shape, F32),
                   jax.ShapeDtypeStruct(dt_t.shape, F32)),
        in_specs=[sp["x"], sp["bc"], sp["bc"], sp["dt"], sp["dt_t"], sp["dt"], sp["dt_t"], sp["h"], sp["y"]],
        out_specs=(sp["y"], sp["bc2"], sp["bc2"], sp["dt"], sp["dt_t"], sp["dt"], sp["dt_t"]),
        scratch_shapes=[pltpu.VMEM((hp, n), F32)], compiler_params=_cparams(),
    )(xs, bm, cm, dt, dt_t, dta, dta_t, hsave, dy)


def _ssd_scan(cfg, e, xs, bm, cm, dt, dt_t, dta, dta_t):
    @jax.custom_vjp
    def scan(xs, bm, cm, dt, dt_t, dta, dta_t):
        return _ssd_forward(cfg, f"ssd_fwd{e}", xs, bm, cm, dt, dt_t, dta, dta_t)[0]

    def fwd(xs, bm, cm, dt, dt_t, dta, dta_t):
        y, hsave = _ssd_forward(cfg, f"ssd_fwd{e}", xs, bm, cm, dt, dt_t, dta, dta_t)
        return y, (xs, bm, cm, dt, dt_t, dta, dta_t, hsave)

    def bwd(res, dy):
        dx, db, dc, ddt, ddtt, ddta, ddtat = _ssd_backward(cfg, f"ssd_bwd{e}", *res, dy)
        return dx[0] + dx[1], db[0] + db[1], dc[0] + dc[1], ddt, ddtt, ddta, ddtat

    scan.defvjp(fwd, bwd)
    return scan(xs, bm, cm, dt, dt_t, dta, dta_t)


def _adam_math(w, g, m, v):
    m2 = ADAM_B1 * m + (1.0 - ADAM_B1) * g
    v2 = ADAM_B2 * v + (1.0 - ADAM_B2) * (g * g)
    m_hat = m2 / (1.0 - ADAM_B1 ** ADAM_STEP)
    v_hat = v2 / (1.0 - ADAM_B2 ** ADAM_STEP)
    delta = -ADAM_LR * (m_hat / (jnp.sqrt(v_hat) + ADAM_EPS) + ADAM_WD * w)
    return delta, m2, v2


def _adam_small(w, g, m, v, name):
    def body(w_ref, g_ref, m_ref, v_ref, d_ref, m2_ref, v2_ref):
        d_ref[...], m2_ref[...], v2_ref[...] = _adam_math(w_ref[...], g_ref[...], m_ref[...], v_ref[...])

    shp = jax.ShapeDtypeStruct(w.shape, F32)
    return pl.pallas_call(body, name=name, out_shape=(shp, shp, shp), in_specs=[VMEM] * 4, out_specs=(VMEM, VMEM, VMEM),
                          compiler_params=_cparams())(w, g, m, v)


def _adam_tiled(w, g, m, v, name):
    nl, r, c = w.shape
    tr = _pick(r, (256, 128, 64, 32, 16, 8))
    spec = pl.BlockSpec((1, tr, c), lambda l, t: (l, t, 0))

    def body(w_ref, g_ref, m_ref, v_ref, d_ref, m2_ref, v2_ref):
        d_ref[...], m2_ref[...], v2_ref[...] = _adam_math(w_ref[...], g_ref[...], m_ref[...], v_ref[...])

    shp = jax.ShapeDtypeStruct(w.shape, F32)
    return pl.pallas_call(body, name=name, grid=(nl, r // tr), out_shape=(shp, shp, shp), in_specs=[spec] * 4,
                          out_specs=(spec, spec, spec), compiler_params=_cparams())(w, g, m, v)


def _adam_sharded(w, m, v, part_a, part_b, *, transposed, name):
    nl, r, c = w.shape
    x, y, _ = _pos()
    chip = (2 * x + y).astype(jnp.int32).reshape(1)
    if transposed:
        tr = _pick(r, (256, 128))
        w_spec = pl.BlockSpec((1, tr, c), lambda l, t, k: (l, t, 0))
        pa_spec = pl.BlockSpec((1, 1, c, tr), lambda l, t, k: (k[0], l, 0, t))
        pb_spec = pl.BlockSpec((3, 1, c, tr), lambda l, t, k: (0, l, 0, t))
    else:
        tr = _pick(r, (256, 128, 64, 32, 16, 8))
        w_spec = pl.BlockSpec((1, tr, c), lambda l, t, k: (l, t, 0))
        pa_spec = pl.BlockSpec((1, 1, tr, c), lambda l, t, k: (k[0], l, t, 0))
        pb_spec = pl.BlockSpec((3, 1, tr, c), lambda l, t, k: (0, l, t, 0))

    def body(k_ref, w_ref, m_ref, v_ref, pa_ref, pb_ref, g_ref, d_ref, m2_ref, v2_ref):
        g = pa_ref[0, 0] + pb_ref[0, 0] + pb_ref[1, 0] + pb_ref[2, 0]
        if transposed:
            g = g.T
        g_ref[0] = g
        d_ref[0], m2_ref[0], v2_ref[0] = _adam_math(w_ref[0], g, m_ref[0], v_ref[0])

    shp = jax.ShapeDtypeStruct(w.shape, F32)
    return pl.pallas_call(
        body, name=name, out_shape=(shp, shp, shp, shp),
        grid_spec=pltpu.PrefetchScalarGridSpec(
            num_scalar_prefetch=1, grid=(nl, r // tr),
            in_specs=[w_spec, w_spec, w_spec, pa_spec, pb_spec], out_specs=(w_spec, w_spec, w_spec, w_spec)),
        compiler_params=_cparams(),
    )(chip, w, m, v, part_a, part_b)


def _pair_sum(g, recv, axis, length, name):
    nl = g.shape[0]
    _, _, c = _pos()
    cc = c.astype(jnp.int32).reshape(1)
    blk_shape = list(g.shape)
    blk_shape[axis] = length
    if axis == 1:
        tr = _pick(length, (600, 512, 360, 256, 200, 128, 64, 40, 32, 16, 8, 2, 1))
        per = length // tr
        cols = g.shape[2]
        g_spec = pl.BlockSpec((1, tr, cols), lambda k, l, t, c_ref: (l, (2 * k + c_ref[0]) * per + t, 0))
        r_spec = pl.BlockSpec((1, 1, tr, cols), lambda k, l, t, c_ref: (k, l, t, 0))
        grid = (4, nl, per)
    else:
        rows = g.shape[1]
        tr = _pick(rows, (512, 256, 128, 64, 32, 16, 8))
        g_spec = pl.BlockSpec((1, tr, length), lambda k, l, t, c_ref: (l, t, 2 * k + c_ref[0]))
        r_spec = pl.BlockSpec((1, 1, tr, length), lambda k, l, t, c_ref: (k, l, t, 0))
        grid = (4, nl, rows // tr)

    def body(c_ref, g_ref, r_ref, o_ref):
        o_ref[0] = g_ref[...] + r_ref[0]

    return pl.pallas_call(
        body, name=name, out_shape=jax.ShapeDtypeStruct((4, *blk_shape), F32),
        grid_spec=pltpu.PrefetchScalarGridSpec(num_scalar_prefetch=1, grid=grid, in_specs=[g_spec, r_spec], out_specs=r_spec),
        compiler_params=_cparams(),
    )(cc, g, recv)


def _flatten(arrs):
    flat = jnp.concatenate([a.reshape(-1).astype(F32) for a in arrs])
    n = flat.shape[0]
    n_pad = -(-n // 1024) * 1024
    return jnp.pad(flat, (0, n_pad - n)).reshape(n_pad // 128, 128)


def _unflatten(buf, shapes):
    flat = buf.reshape(-1)
    out, o = [], 0
    for s in shapes:
        n = math.prod(s)
        out.append(flat[o:o + n].reshape(s))
        o += n
    return out


def _rms(x, g):
    return x * lax.rsqrt(jnp.mean(x * x, axis=-1, keepdims=True) + EPS) * g


def _dw_conv(x, w, b=None):
    k = w.shape[0]
    ln = x.shape[0]
    xp = jnp.pad(x, ((k // 2, k // 2), (0, 0)))
    y = sum(w[i][None, :] * xp[i:i + ln] for i in range(k))
    return y if b is None else y + b


def _conv_two(x, nctx, w, b=None):
    return jnp.concatenate([_dw_conv(x[:nctx], w, b), _dw_conv(x[nctx:], w, b)], axis=0)


def _mod_rows(nctx, seq, ctx_vec, lat_vec):
    return jnp.concatenate([jnp.broadcast_to(ctx_vec, (nctx, ctx_vec.shape[-1])),
                            jnp.broadcast_to(lat_vec, (seq, lat_vec.shape[-1]))], axis=0)


def _even_mixer(cfg, h, e, wd, update_ctx):
    d, nctx, seq, t = cfg.d, cfg.nctx, cfg.s, cfg.t
    naw, sw = cfg.na_width, cfg.ssd_width
    gn = SSD_GROUPS * SSD_STATE
    p = _linear(h.astype(BF16), wd["win_t"][e], wd["g_win_t"][e], w_is_nk=True, tiles=cfg.tiles_in, name=f"in_even{e}")
    col_q, col_gate, col_z = 0, naw, 2 * naw
    col_k = col_z + sw
    col_v = col_k + naw
    col_xbc = col_v + naw
    col_dt = col_xbc + sw + 2 * gn

    def heads_norm(u, g):
        return _rms(u.reshape(u.shape[0], NA_HEADS, NA_HEAD_DIM), g).reshape(u.shape[0], naw)

    qn = heads_norm(p[:, col_q:col_q + naw], wd["q_norm_g"][e])
    kn = heads_norm(p[:, col_k:col_k + naw], wd["k_norm_g"][e])
    vv = p[:, col_v:col_v + naw]
    gate_a = jax.nn.silu(p[:, col_gate:col_gate + naw])
    biasw = _bias_windows(wd["na_rpb"][e])
    ya = _na_attention(e, qn[nctx:], kn[nctx:], vv[nctx:], kn[:nctx], vv[:nctx], biasw)
    if update_ctx:
        qc = qn[:nctx].reshape(nctx, NA_HEADS, NA_HEAD_DIM)
        kc = kn[:nctx].reshape(nctx, NA_HEADS, NA_HEAD_DIM)
        vc = vv[:nctx].reshape(nctx, NA_HEADS, NA_HEAD_DIM)
        sc = jnp.einsum("qhd,khd->hqk", qc, kc).astype(F32) * NA_HEAD_DIM ** -0.5
        yac = jnp.einsum("hqk,khd->qhd", jax.nn.softmax(sc, axis=-1), vc).reshape(nctx, naw)
    else:
        yac = jnp.zeros((nctx, naw), F32)
    ya = jnp.concatenate([yac, ya], axis=0) * gate_a

    xbc = jax.nn.silu(_conv_two(p[:, col_xbc:col_xbc + sw + 2 * gn], nctx, wd["ssd_conv_w"][e], wd["ssd_conv_b"][e]))
    xs, bm, cm = xbc[:, :sw], xbc[:, sw:sw + gn], xbc[:, sw + gn:]
    nh = 2 * SSD_GROUPS * cfg.hpg
    dt = jax.nn.softplus(p[:, col_dt:col_dt + nh] + wd["ssd_dt_bias"][e].reshape(1, nh))
    a = -jnp.exp(wd["ssd_a_log"][e]).reshape(1, nh)
    dta = dt * a

    def arrange(u):
        u4 = u.reshape(t, 2, SSD_GROUPS, cfg.hpg)
        return u4.transpose(1, 2, 0, 3), jnp.pad(u4.transpose(1, 2, 3, 0), ((0, 0), (0, 0), (0, 8 - cfg.hpg), (0, 0)))

    dt4, dt_t = arrange(dt)
    dta4, dta_t = arrange(dta)
    y2 = _ssd_scan(cfg, e, xs, bm, cm, dt4, dt_t, dta4, dta_t)
    dskip = jnp.repeat(wd["ssd_d"][e], SSD_HEAD_DIM)[None, :]
    yz = ((y2[0] + y2[1] + dskip * xs) * jax.nn.silu(p[:, col_z:col_z + sw])).reshape(t, SSD_GROUPS, -1)
    yz = yz * lax.rsqrt(jnp.mean(yz * yz, axis=-1, keepdims=True) + EPS)
    yb = yz.reshape(t, sw) * wd["ssd_norm_g"][e][None, :]
    ycat = jnp.concatenate([ya, yb], axis=-1)
    return _linear(ycat.astype(BF16), wd["wout"][e], wd["g_wout"][e], w_is_nk=False, tiles=cfg.tiles_out_even, name=f"out_even{e}")


def _odd_mixer(cfg, h, o, wd):
    d, nctx = cfg.d, cfg.nctx
    p = _linear(h.astype(BF16), wd["sc_win"][o], wd["g_sc_win"][o], w_is_nk=False, tiles=cfg.tiles_in_odd, name=f"in_odd{o}")
    bg, cg, hv, g = p[:, :d], p[:, d:2 * d], p[:, 2 * d:3 * d], p[:, 3 * d:]
    yy = bg * _conv_two(cg * hv, nctx, wd["sc_conv_w"][o])
    u = jax.nn.silu(g) * yy
    return _linear(u.astype(BF16), wd["sc_wout"][o], wd["g_sc_wout"][o], w_is_nk=False, tiles=cfg.tiles_out_odd, name=f"out_odd{o}")


def _local_loss(cfg, x, ctx, target, mods, mods_c, wd):
    d, nctx, seq = cfg.d, cfg.nctx, cfg.s
    xx = jnp.concatenate([ctx, x], axis=0)
    for i in range(DEPTH):
        update_ctx = any(j % 2 == 0 for j in range(i + 1, DEPTH))
        shift = _mod_rows(nctx, seq, mods_c[i, :d], mods[i, :d])
        scale = _mod_rows(nctx, seq, mods_c[i, d:2 * d], mods[i, d:2 * d])
        gate_c = mods_c[i, 2 * d:] if update_ctx else jnp.zeros((d,), F32)
        gate = _mod_rows(nctx, seq, gate_c, mods[i, 2 * d:])
        h = _rms(xx, wd["norm_g"][i][None, :]) * (1 + scale) + shift
        if i % 2 == 0:
            y = _even_mixer(cfg, h, i // 2, wd, update_ctx)
        else:
            y = _odd_mixer(cfg, h, i // 2, wd)
        xx = xx + gate * y
    err = jnp.square(xx[nctx:] - target)
    return 0.5 * jnp.sum(jnp.mean(err, axis=-1))


SMALL_REPLICATED = ["norm_g", "ssd_conv_b", "ssd_a_log", "ssd_dt_bias", "ssd_d", "ssd_norm_g", "q_norm_g", "k_norm_g", "na_rpb"]
WEIGHT_ORDER = ["c_ctx", "ada_w", "ada_b", "norm_g", "na_ssd_w_in", "ssd_conv_w", "ssd_conv_b", "ssd_a_log", "ssd_dt_bias",
                "ssd_d", "ssd_norm_g", "q_norm_g", "k_norm_g", "na_rpb", "na_ssd_w_out", "sc_w_in", "sc_conv_w", "sc_w_out"]


def kernel(x, c, ctx, c_ctx, ada_w, ada_b, norm_g, na_ssd_w_in, ssd_conv_w, ssd_conv_b, ssd_a_log, ssd_dt_bias, ssd_d, ssd_norm_g, q_norm_g, k_norm_g, na_rpb, na_ssd_w_out, sc_w_in, sc_conv_w, sc_w_out, loss_target, m_c_ctx, m_ada_w, m_ada_b, m_norm_g, m_na_ssd_w_in, m_ssd_conv_w, m_ssd_conv_b, m_ssd_a_log, m_ssd_dt_bias, m_ssd_d, m_ssd_norm_g, m_q_norm_g, m_k_norm_g, m_na_rpb, m_na_ssd_w_out, m_sc_w_in, m_sc_conv_w, m_sc_w_out, v_c_ctx, v_ada_w, v_ada_b, v_norm_g, v_na_ssd_w_in, v_ssd_conv_w, v_ssd_conv_b, v_ssd_a_log, v_ssd_dt_bias, v_ssd_d, v_ssd_norm_g, v_q_norm_g, v_k_norm_g, v_na_rpb, v_na_ssd_w_out, v_sc_w_in, v_sc_conv_w, v_sc_w_out):
    given = dict(locals())
    weights = {n: given[n] for n in WEIGHT_ORDER}
    mom_m = {n: given["m_" + n] for n in WEIGHT_ORDER}
    mom_v = {n: given["v_" + n] for n in WEIGHT_ORDER}

    d = x.shape[-1]
    seq, nctx = x.shape[1], ctx.shape[1]
    n_in_shard = na_ssd_w_in.shape[-1]
    n_in = n_in_shard * NDEV
    n_pad = -(-n_in // PAD_TO) * PAD_TO
    hpg = (d // SSD_HEAD_DIM) // SSD_GROUPS
    t = nctx + seq
    tm = _pick(t, (1408, 768, 512, 256, 128))
    cfg = SimpleNamespace(
        d=d, s=seq, nctx=nctx, t=t, hpg=hpg, na_width=NA_HEADS * NA_HEAD_DIM, ssd_width=d, n_in=n_in, n_pad=n_pad,
        tiles_in=(tm, _pick(n_pad, (512, 256, 128)), d),
        tiles_out_even=(tm, _pick(d, (1024, 512, 256, 128)), _pick(NA_HEADS * NA_HEAD_DIM + d, (1024, 512, 256, 128))),
        tiles_in_odd=(tm, _pick(4 * d, (1024, 512, 256, 128)), d),
        tiles_out_odd=(tm, _pick(d, (1024, 512, 256, 128)), d),
    )
    me = _my_index()
    xl, cl, ctxl, tgt = x[0], c, ctx[0], loss_target[0]

    ncol = ada_w.shape[-1]
    c_rows = -(-d // 128)
    c_all = _small_allgather(jnp.pad(cl.reshape(-1), (0, c_rows * 128 - d)).reshape(c_rows, 128), "gather_c")[0]
    c_all = c_all.reshape(NDEV, -1)[:, :d]
    cond = jnp.concatenate([c_all, c_ctx[None, :], jnp.zeros((16 - NDEV - 1, d), F32)], axis=0)
    s16 = jax.nn.silu(cond)
    ada_b_mine = lax.dynamic_slice_in_dim(ada_b, me * ncol, ncol, axis=1)
    mod_part = jnp.stack([
        _matmul(s16, ada_w[i], tm=16, tn=_pick(ncol, (768, 512, 256, 128)), tk=d, name=f"adaln{i}") + ada_b_mine[i][None, :]
        for i in range(DEPTH)])
    mp_rows = DEPTH * 16 * ncol // 128
    mod_all = _small_allgather(mod_part.reshape(mp_rows, 128), "gather_mod")[0]
    mod_all = mod_all.reshape(NDEV, DEPTH, 16, ncol).transpose(1, 2, 0, 3).reshape(DEPTH, 16, NDEV * ncol)
    mods = lax.dynamic_index_in_dim(mod_all, me, axis=1, keepdims=False)
    mods_c = mod_all[:, NDEV]

    packed = _pack_transposed(na_ssd_w_in, "pack_w_in")
    wout_b = _cast_bf16(na_ssd_w_out, "cast_w_out")
    scwin_b = _cast_bf16(sc_w_in, "cast_sc_w_in")
    scwout_b = _cast_bf16(sc_w_out, "cast_sc_w_out")
    packed_all, wout_all, scwin_all, scwout_all = _big_allgather(
        [packed, wout_b, scwin_b, scwout_b], [1, 1, 2, 1], "gather_weights")
    win_t = _unpack(packed_all, n_pad, "unpack_w_in")
    conv_shapes = [ssd_conv_w.shape, sc_conv_w.shape]
    conv_all = _small_allgather(_flatten([ssd_conv_w, sc_conv_w]), "gather_conv")[0]
    conv_parts = [_unflatten(conv_all[j], conv_shapes) for j in range(NDEV)]
    ssd_conv_full = jnp.concatenate([cp[0] for cp in conv_parts], axis=-1)
    sc_conv_full = jnp.concatenate([cp[1] for cp in conv_parts], axis=-1)

    small = {n: weights[n] for n in SMALL_REPLICATED}
    small["ssd_conv_w"] = ssd_conv_full
    small["sc_conv_w"] = sc_conv_full
    gslots = dict(g_win_t=jnp.zeros(win_t.shape, F32), g_wout=jnp.zeros(wout_all.shape, F32),
                  g_sc_win=jnp.zeros(scwin_all.shape, F32), g_sc_wout=jnp.zeros(scwout_all.shape, F32))
    frozen = dict(win_t=win_t, wout=wout_all, sc_win=scwin_all, sc_wout=scwout_all)

    def loss_fn(xl, mods, mods_c, small, gslots):
        return _local_loss(cfg, xl, ctxl, tgt, mods, mods_c, {**small, **gslots, **frozen})

    loss_local, (g_x, g_mods, g_mods_c, g_small, g_big) = jax.value_and_grad(loss_fn, argnums=(0, 1, 2, 3, 4))(
        xl, mods, mods_c, small, gslots)

    small_names = SMALL_REPLICATED + ["ssd_conv_w", "sc_conv_w"]
    small_shapes = [g_small[n].shape for n in small_names] + [g_mods_c.shape]
    flat_small = _flatten([g_small[n] for n in small_names] + [g_mods_c])
    _, small_sum = _small_allgather(flat_small, "gather_small_grads")
    summed = _unflatten(small_sum, small_shapes)
    g_rep = dict(zip(small_names, summed[:-1]))
    g_mods_c_tot = summed[-1]
    gm_rows = DEPTH * NDEV * ncol // 128
    gm_all = _small_allgather(g_mods.reshape(gm_rows, 128), "gather_mod_grads")[0].reshape(NDEV, DEPTH, NDEV * ncol)
    dm = jnp.concatenate([gm_all.transpose(1, 0, 2), g_mods_c_tot[:, None, :],
                          jnp.zeros((DEPTH, 16 - NDEV - 1, NDEV * ncol), F32)], axis=1)
    grad_ada_b = jnp.sum(dm, axis=1)
    dm_mine = lax.dynamic_slice_in_dim(dm, me * ncol, ncol, axis=2)
    grad_ada_w = jnp.stack([
        _matmul(s16, dm_mine[i], ta=True, tm=_pick(d, (512, 256, 128)), tn=_pick(ncol, (768, 512, 256, 128)), tk=16,
                name=f"adaln_gw{i}") for i in range(DEPTH)])
    ds_part = sum(_matmul(dm_mine[i], ada_w[i], tb=True, tm=16, tn=_pick(d, (2048, 1024, 512, 256, 128)),
                          tk=_pick(ncol, (768, 512, 256, 128)), name=f"adaln_gs{i}") for i in range(DEPTH))[NDEV]
    ds_ctx = _small_allgather(jnp.pad(ds_part, (0, c_rows * 128 - d)).reshape(c_rows, 128), "gather_c_ctx_grad")[1]
    ds_ctx = ds_ctx.reshape(-1)[:d]
    sig = jax.nn.sigmoid(c_ctx)
    grad_c_ctx = ds_ctx * (sig * (1 + c_ctx * (1 - sig)))

    big = [g_big["g_win_t"], g_big["g_wout"], g_big["g_sc_win"], g_big["g_sc_wout"]]
    axes = [1, 1, 2, 1]
    lens = [n_in_shard, na_ssd_w_out.shape[1], sc_w_in.shape[2], sc_w_out.shape[1]]
    recv_a = _rs_stage_a(big, axes, lens, "reduce_scatter_d2d")
    part_a = [_pair_sum(g, r, ax, ln, f"pair_sum{i}") for i, (g, r, ax, ln) in enumerate(zip(big, recv_a, axes, lens))]
    part_b = _rs_stage_b(part_a, "reduce_scatter_ici")

    res = {}
    res["na_ssd_w_in"] = _adam_sharded(na_ssd_w_in, m_na_ssd_w_in, v_na_ssd_w_in, part_a[0], part_b[0], transposed=True, name="adam_w_in")
    res["na_ssd_w_out"] = _adam_sharded(na_ssd_w_out, m_na_ssd_w_out, v_na_ssd_w_out, part_a[1], part_b[1], transposed=False, name="adam_w_out")
    res["sc_w_in"] = _adam_sharded(sc_w_in, m_sc_w_in, v_sc_w_in, part_a[2], part_b[2], transposed=False, name="adam_sc_w_in")
    res["sc_w_out"] = _adam_sharded(sc_w_out, m_sc_w_out, v_sc_w_out, part_a[3], part_b[3], transposed=False, name="adam_sc_w_out")

    grads = dict(g_rep)
    grads["ssd_conv_w"] = lax.dynamic_slice_in_dim(g_rep["ssd_conv_w"], me * ssd_conv_w.shape[-1], ssd_conv_w.shape[-1], axis=2)
    grads["sc_conv_w"] = lax.dynamic_slice_in_dim(g_rep["sc_conv_w"], me * sc_conv_w.shape[-1], sc_conv_w.shape[-1], axis=2)
    grads["c_ctx"] = grad_c_ctx
    res["ada_w"] = (grad_ada_w, *_adam_tiled(ada_w, grad_ada_w, m_ada_w, v_ada_w, "adam_ada_w"))
    grads["ada_b"] = grad_ada_b
    rest = [n for n in WEIGHT_ORDER if n not in res]
    shapes = [weights[n].shape for n in rest]
    d_flat, m_flat, v_flat = _adam_small(_flatten([weights[n] for n in rest]), _flatten([grads[n] for n in rest]),
                                         _flatten([mom_m[n] for n in rest]), _flatten([mom_v[n] for n in rest]), "adam_small")
    for n, dd, mm, vv in zip(rest, _unflatten(d_flat, shapes), _unflatten(m_flat, shapes), _unflatten(v_flat, shapes)):
        res[n] = (grads[n], dd, mm, vv)

    loss = lax.psum(loss_local, ("x", "y", "c"))
    return (loss, g_x[None], *[res[n][0] for n in WEIGHT_ORDER], *[res[n][1] for n in WEIGHT_ORDER],
            *[res[n][2] for n in WEIGHT_ORDER], *[res[n][3] for n in WEIGHT_ORDER])
```

```python
import functools
import math
from types import SimpleNamespace

import jax
import jax.numpy as jnp
from jax import lax
from jax.experimental import pallas as pl
from jax.experimental.pallas import tpu as pltpu

F32 = jnp.float32
BF16 = jnp.bfloat16
U32 = jnp.uint32
HIGHEST = lax.Precision.HIGHEST
MESH = pl.DeviceIdType.MESH
ANY = pl.BlockSpec(memory_space=pl.ANY)
VMEM = pl.BlockSpec(memory_space=pltpu.VMEM)

NDEV = 8
DEPTH = 4
GRID_W = 64
EPS = 1e-6
NA_HEADS = 16
NA_HEAD_DIM = 128
NA_KH = 8
NA_KW = 16
SSD_HEAD_DIM = 64
SSD_GROUPS = 8
SSD_STATE = 128
SSD_CONV = 5
SSD_CHUNK = 128
SC_CONV = 3
ADAM_LR = 0.001
ADAM_B1 = 0.9
ADAM_B2 = 0.999
ADAM_EPS = 1e-08
ADAM_WD = 0.01
ADAM_STEP = 10
NEG = -1e30
VMEM_LIMIT = 56 * 1024 * 1024
PAD_TO = 512


def _pos():
    return lax.axis_index("x"), lax.axis_index("y"), lax.axis_index("c")


def _my_index():
    x, y, c = _pos()
    return 4 * x + 2 * y + c


def _pick(n, prefs):
    for p in prefs:
        if n % p == 0:
            return p
    return n


def _cparams(**kw):
    return pltpu.CompilerParams(vmem_limit_bytes=VMEM_LIMIT, **kw)


def _small_allgather(v, name):
    rows, lanes = v.shape

    def body(x_ref, out_ref, sum_ref, send_sems, recv_sems):
        x, y, c = _pos()
        me = 4 * x + 2 * y + c
        out_ref[me] = x_ref[...]
        copies = []
        for k in range(1, NDEV):
            peer = (1 - x if k & 4 else x, 1 - y if k & 2 else y, 1 - c if k & 1 else c)
            cp = pltpu.make_async_remote_copy(src_ref=x_ref, dst_ref=out_ref.at[me], send_sem=send_sems.at[k - 1],
                                              recv_sem=recv_sems.at[k - 1], device_id=peer, device_id_type=MESH)
            cp.start()
            copies.append(cp)
        for cp in copies:
            cp.wait()
        acc = out_ref[0]
        for j in range(1, NDEV):
            acc = acc + out_ref[j]
        sum_ref[...] = acc

    return pl.pallas_call(
        body, name=name,
        out_shape=(jax.ShapeDtypeStruct((NDEV, rows, lanes), v.dtype), jax.ShapeDtypeStruct((rows, lanes), v.dtype)),
        in_specs=[VMEM], out_specs=(VMEM, VMEM),
        scratch_shapes=[pltpu.SemaphoreType.DMA((NDEV - 1,)), pltpu.SemaphoreType.DMA((NDEV - 1,))],
        compiler_params=_cparams(),
    )(v)


def _window(ref, axis, idx, length):
    sl = [slice(None)] * len(ref.shape)
    sl[axis] = pl.ds(pl.multiple_of(idx * length, min(length & -length, 1024)), length)
    return ref.at[tuple(sl)]


def _big_allgather(shards, axes, name):
    n = len(shards)
    out_shapes = []
    for s, ax in zip(shards, axes):
        shp = list(s.shape)
        shp[ax] *= NDEV
        out_shapes.append(jax.ShapeDtypeStruct(tuple(shp), s.dtype))

    def body(*refs):
        xs, outs = refs[:n], refs[n:2 * n]
        send_sems, recv_sems, local_sems = refs[2 * n:]
        x, y, c = _pos()
        me, sib = (x, y, c), (x, y, 1 - c)
        chips = [(1 - x, y), (x, 1 - y), (1 - x, 1 - y)]

        def win(a, px, py, pc):
            return _window(outs[a], axes[a], 4 * px + 2 * py + pc, shards[a].shape[axes[a]])

        def copy(a, k, block, to, src=None):
            return pltpu.make_async_remote_copy(src_ref=win(a, *block) if src is None else src, dst_ref=win(a, *block),
                                                send_sem=send_sems.at[a * 7 + k], recv_sem=recv_sems.at[a * 7 + k],
                                                device_id=to, device_id_type=MESH)

        mine = [pltpu.make_async_copy(xs[a], win(a, *me), local_sems.at[a]) for a in range(n)]
        for cp in mine:
            cp.start()
        first = []
        for a in range(n):
            first.append(copy(a, 0, me, sib, src=xs[a]))
            first += [copy(a, 1 + j, me, (*chip, c), src=xs[a]) for j, chip in enumerate(chips)]
        for cp in first:
            cp.start()
        passed = []
        for j, chip in enumerate(chips):
            for a in range(n):
                copy(a, 1 + j, (*chip, c), me).wait_recv()
                cp = copy(a, 4 + j, (*chip, c), sib)
                cp.start()
                passed.append(cp)
        for a in range(n):
            copy(a, 0, sib, me).wait_recv()
            for j, chip in enumerate(chips):
                copy(a, 4 + j, (*chip, 1 - c), me).wait_recv()
        for cp in first + passed:
            cp.wait_send()
        for cp in mine:
            cp.wait()

    return pl.pallas_call(
        body, name=name, out_shape=tuple(out_shapes), in_specs=[ANY] * n, out_specs=tuple([ANY] * n),
        scratch_shapes=[pltpu.SemaphoreType.DMA((7 * n,)), pltpu.SemaphoreType.DMA((7 * n,)), pltpu.SemaphoreType.DMA((n,))],
        compiler_params=_cparams(),
    )(*shards)


def _rs_stage_a(sends, name):
    n = len(sends)
    out_shapes = [jax.ShapeDtypeStruct(s.shape, s.dtype) for s in sends]

    def body(*refs):
        ss, outs = refs[:n], refs[n:2 * n]
        send_sems, recv_sems = refs[2 * n:]
        x, y, c = _pos()
        sib = (x, y, 1 - c)
        copies = []
        for a in range(n):
            for k in range(4):
                cp = pltpu.make_async_remote_copy(src_ref=ss[a].at[k], dst_ref=outs[a].at[k], send_sem=send_sems.at[a * 4 + k],
                                                  recv_sem=recv_sems.at[a * 4 + k], device_id=sib, device_id_type=MESH)
                cp.start()
                copies.append(cp)
        for cp in copies:
            cp.wait()

    return pl.pallas_call(
        body, name=name, out_shape=tuple(out_shapes), in_specs=[ANY] * n, out_specs=tuple([ANY] * n),
        scratch_shapes=[pltpu.SemaphoreType.DMA((4 * n,)), pltpu.SemaphoreType.DMA((4 * n,))],
        compiler_params=_cparams(),
    )(*sends)


def _rs_stage_b(parts, name):
    n = len(parts)
    out_shapes = [jax.ShapeDtypeStruct(p.shape, p.dtype) for p in parts]

    def body(*refs):
        ps, outs = refs[:n], refs[n:2 * n]
        send_sems, recv_sems = refs[2 * n:]
        x, y, c = _pos()
        chips = [(1 - x, y), (x, 1 - y), (1 - x, 1 - y)]
        copies = []
        for a in range(n):
            for j, (px, py) in enumerate(chips):
                cp = pltpu.make_async_remote_copy(src_ref=ps[a].at[j], dst_ref=outs[a].at[j],
                                                  send_sem=send_sems.at[a * 3 + j], recv_sem=recv_sems.at[a * 3 + j],
                                                  device_id=(px, py, c), device_id_type=MESH)
                cp.start()
                copies.append(cp)
        for cp in copies:
            cp.wait()

    return pl.pallas_call(
        body, name=name, out_shape=tuple(out_shapes), in_specs=[ANY] * n, out_specs=tuple([ANY] * n),
        scratch_shapes=[pltpu.SemaphoreType.DMA((3 * n,)), pltpu.SemaphoreType.DMA((3 * n,))],
        compiler_params=_cparams(),
    )(*parts)


def _matmul(a, b, *, ta=False, tb=False, tm, tn, tk, name):
    m, kdim = (a.shape[1], a.shape[0]) if ta else a.shape
    n = b.shape[0] if tb else b.shape[1]
    assert (b.shape[1] if tb else b.shape[0]) == kdim, (a.shape, b.shape, ta, tb)
    assert m % tm == 0 and n % tn == 0 and kdim % tk == 0, (m, n, kdim, tm, tn, tk)
    nk = kdim // tk
    dn = (((0 if ta else 1,), (1 if tb else 0,)), ((), ()))

    def body(a_ref, b_ref, o_ref, acc_ref):
        k = pl.program_id(2)
        part = lax.dot_general(a_ref[...].astype(BF16), b_ref[...].astype(BF16), dn, preferred_element_type=F32)
        if nk == 1:
            o_ref[...] = part
        else:
            @pl.when(k == 0)
            def _():
                acc_ref[...] = part

            @pl.when(k > 0)
            def _():
                acc_ref[...] += part

            @pl.when(k == nk - 1)
            def _():
                o_ref[...] = acc_ref[...]

    a_spec = pl.BlockSpec((tk, tm), lambda i, j, k: (k, i)) if ta else pl.BlockSpec((tm, tk), lambda i, j, k: (i, k))
    b_spec = pl.BlockSpec((tn, tk), lambda i, j, k: (j, k)) if tb else pl.BlockSpec((tk, tn), lambda i, j, k: (k, j))
    acc_shape = (tm, tn) if nk > 1 else (8, 128)
    return pl.pallas_call(
        body, name=name, grid=(m // tm, n // tn, nk), out_shape=jax.ShapeDtypeStruct((m, n), F32),
        in_specs=[a_spec, b_spec], out_specs=pl.BlockSpec((tm, tn), lambda i, j, k: (i, j)),
        scratch_shapes=[pltpu.VMEM(acc_shape, F32)], compiler_params=_cparams(),
    )(a, b)


def _linear(a, w, gslot, *, w_is_nk, tiles, name):
    tm, tn, tk = tiles

    @jax.custom_vjp
    def lin(a, w, gslot):
        return _matmul(a, w, tb=w_is_nk, tm=tm, tn=tn, tk=tk, name=name + "_fwd")

    def fwd(a, w, gslot):
        return lin(a, w, gslot), (a, w)

    def bwd(res, g):
        a, w = res
        gb = g.astype(BF16)
        t, kdim = a.shape
        n = g.shape[1]
        tt = _pick(t, (1408, 768, 512, 256, 128))
        tkk = _pick(kdim, (2048, 1024, 512, 256, 128))
        tnn = _pick(n, (512, 256, 128))
        da = _matmul(gb, w, tb=not w_is_nk, tm=tt, tn=tkk, tk=tnn, name=name + "_bwd_a")
        tok = _pick(t, (768, 512, 256, 128))
        if w_is_nk:
            dw = _matmul(gb, a, ta=True, tm=tnn, tn=tkk, tk=tok, name=name + "_bwd_w")
        else:
            tkw = _pick(kdim, (512, 256, 128))
            tnw = _pick(n, (2048, 1024, 512, 256, 128))
            dw = _matmul(a, gb, ta=True, tm=tkw, tn=tnw, tk=tok, name=name + "_bwd_w")
        return da.astype(a.dtype), jnp.zeros_like(w), dw

    lin.defvjp(fwd, bwd)
    return lin(a, w, gslot)


def _pack_transposed(w, name):
    nl, kdim, r = w.shape
    half = kdim // 2
    tc = _pick(half, (256, 128))

    def body(lo_ref, hi_ref, o_ref):
        lo = pltpu.bitcast(lo_ref[0].astype(BF16).astype(F32).T, U32) >> 16
        hi = pltpu.bitcast(hi_ref[0].astype(BF16).astype(F32).T, U32) & jnp.uint32(0xFFFF0000)
        o_ref[0] = pltpu.bitcast(hi | lo, F32)

    nb = half // tc
    return pl.pallas_call(
        body, name=name, grid=(nl, nb), out_shape=jax.ShapeDtypeStruct((nl, r, half), F32),
        in_specs=[pl.BlockSpec((1, tc, r), lambda l, t: (l, t, 0)), pl.BlockSpec((1, tc, r), lambda l, t: (l, t + nb, 0))],
        out_specs=pl.BlockSpec((1, r, tc), lambda l, t: (l, 0, t)), compiler_params=_cparams(),
    )(w, w)


def _unpack(packed, n_pad, name):
    nl, n, half = packed.shape
    tr = math.gcd(math.gcd(n, n_pad - n), 64) if n_pad > n else _pick(n, (64, 32, 16))
    nin = n // tr

    def body(p_ref, o_ref):
        t = pl.program_id(1)

        @pl.when(t < nin)
        def _():
            u = pltpu.bitcast(p_ref[0], U32)
            o_ref[0, :, :half] = pltpu.bitcast(u << 16, F32).astype(BF16)
            o_ref[0, :, half:] = pltpu.bitcast(u & jnp.uint32(0xFFFF0000), F32).astype(BF16)

        @pl.when(t >= nin)
        def _():
            o_ref[...] = jnp.zeros_like(o_ref)

    return pl.pallas_call(
        body, name=name, grid=(nl, n_pad // tr), out_shape=jax.ShapeDtypeStruct((nl, n_pad, 2 * half), BF16),
        in_specs=[pl.BlockSpec((1, tr, half), lambda l, t: (l, jnp.minimum(t, nin - 1), 0))],
        out_specs=pl.BlockSpec((1, tr, 2 * half), lambda l, t: (l, t, 0)), compiler_params=_cparams(),
    )(packed)


def _cast_bf16(w, name):
    nl, r, c = w.shape
    tr = _pick(r, (512, 256, 128, 64, 32, 16))

    def body(w_ref, o_ref):
        o_ref[...] = w_ref[...].astype(BF16)

    return pl.pallas_call(
        body, name=name, grid=(nl, r // tr), out_shape=jax.ShapeDtypeStruct(w.shape, BF16),
        in_specs=[pl.BlockSpec((1, tr, c), lambda l, t: (l, t, 0))], out_specs=pl.BlockSpec((1, tr, c), lambda l, t: (l, t, 0)),
        compiler_params=_cparams(),
    )(w)


NA_ROWS_PER_STEP = 4


def _row_block(i):
    return slice(i * GRID_W, (i + 1) * GRID_W)


def _na_probs(q_ref, k_ref, kc_ref, b_ref, step, rows):
    scale = NA_HEAD_DIM ** -0.5
    nt = (((1,), (1,)), ((), ()))
    qs = q_ref[...]
    kws, starts, offs, s1 = [], [], [], []
    for i in range(NA_ROWS_PER_STEP):
        r = step * NA_ROWS_PER_STEP + i
        rs = jnp.clip(r - NA_KH // 2, 0, rows - NA_KH)
        offs.append(rs - r + NA_KH - 1)
        starts.append(pl.multiple_of(rs * GRID_W, GRID_W))
        kws.append(k_ref[pl.ds(starts[i], NA_KH * GRID_W), :])
        s1.append(lax.dot_general(qs[_row_block(i)], kws[i], nt, preferred_element_type=F32) * scale + b_ref[0, offs[i]])
    s1 = jnp.concatenate(s1, axis=0)
    s2 = lax.dot_general(qs, kc_ref[...], nt, preferred_element_type=F32) * scale
    m = jnp.maximum(jnp.max(s1, axis=-1, keepdims=True), jnp.max(s2, axis=-1, keepdims=True))
    e1 = jnp.exp(s1 - m)
    e2 = jnp.exp(s2 - m)
    inv = 1.0 / (jnp.sum(e1, axis=-1, keepdims=True) + jnp.sum(e2, axis=-1, keepdims=True))
    return qs, kws, starts, offs, e1 * inv, e2 * inv


def _na_specs(seq, nctx):
    dh = NA_HEAD_DIM
    win = NA_KH * GRID_W
    return [
        pl.BlockSpec((NA_ROWS_PER_STEP * GRID_W, dh), lambda h, r: (r, h)),
        pl.BlockSpec((seq, dh), lambda h, r: (0, h)),
        pl.BlockSpec((seq, dh), lambda h, r: (0, h)),
        pl.BlockSpec((nctx, dh), lambda h, r: (0, h)),
        pl.BlockSpec((nctx, dh), lambda h, r: (0, h)),
        pl.BlockSpec((1, NA_KH, GRID_W, win), lambda h, r: (h, 0, 0, 0)),
    ]


def _na_forward(name, q, k, v, kc, vc, biasw):
    seq, width = q.shape
    nctx = kc.shape[0]
    rows = seq // GRID_W
    assert rows % NA_ROWS_PER_STEP == 0, rows

    def body(q_ref, k_ref, v_ref, kc_ref, vc_ref, b_ref, o_ref):
        _, _, starts, _, p1, p2 = _na_probs(q_ref, k_ref, kc_ref, b_ref, pl.program_id(1), rows)
        p1b = p1.astype(BF16)
        o1 = [jnp.dot(p1b[_row_block(i)], v_ref[pl.ds(starts[i], NA_KH * GRID_W), :], preferred_element_type=F32)
              for i in range(NA_ROWS_PER_STEP)]
        o_ref[...] = jnp.concatenate(o1, axis=0) + jnp.dot(p2.astype(BF16), vc_ref[...], preferred_element_type=F32)

    return pl.pallas_call(
        body, name=name, grid=(NA_HEADS, rows // NA_ROWS_PER_STEP), out_shape=jax.ShapeDtypeStruct((seq, width), F32),
        in_specs=_na_specs(seq, nctx),
        out_specs=pl.BlockSpec((NA_ROWS_PER_STEP * GRID_W, NA_HEAD_DIM), lambda h, r: (r, h)),
        compiler_params=_cparams(),
    )(q, k, v, kc, vc, biasw)


def _na_backward(name, q, k, v, kc, vc, biasw, do):
    seq, width = q.shape
    nctx = kc.shape[0]
    rows = seq // GRID_W
    win = NA_KH * GRID_W
    dh = NA_HEAD_DIM
    scale = dh ** -0.5
    nt = (((1,), (1,)), ((), ()))
    tn = (((0,), (0,)), ((), ()))

    def body(q_ref, k_ref, v_ref, kc_ref, vc_ref, b_ref, do_ref, dq_ref, dk_ref, dv_ref, dkc_ref, dvc_ref, db_ref):
        r = pl.program_id(1)

        @pl.when(r == 0)
        def _():
            dk_ref[...] = jnp.zeros_like(dk_ref)
            dv_ref[...] = jnp.zeros_like(dv_ref)
            dkc_ref[...] = jnp.zeros_like(dkc_ref)
            dvc_ref[...] = jnp.zeros_like(dvc_ref)
            db_ref[...] = jnp.zeros_like(db_ref)

        nr = range(NA_ROWS_PER_STEP)
        qs, kws, starts, offs, p1, p2 = _na_probs(q_ref, k_ref, kc_ref, b_ref, r, rows)
        vcb = vc_ref[...]
        kcb = kc_ref[...]
        dob = do_ref[...].astype(BF16)
        p1b, p2b = p1.astype(BF16), p2.astype(BF16)
        dp1 = jnp.concatenate([lax.dot_general(dob[_row_block(i)], v_ref[pl.ds(starts[i], win), :], nt,
                                               preferred_element_type=F32) for i in nr], axis=0)
        dp2 = lax.dot_general(dob, vcb, nt, preferred_element_type=F32)
        delta = jnp.sum(dp1 * p1, axis=-1, keepdims=True) + jnp.sum(dp2 * p2, axis=-1, keepdims=True)
        ds1 = p1 * (dp1 - delta)
        ds2 = p2 * (dp2 - delta)
        ds1b = (ds1 * scale).astype(BF16)
        ds2b = (ds2 * scale).astype(BF16)
        dq1 = [jnp.dot(ds1b[_row_block(i)], kws[i], preferred_element_type=F32) for i in nr]
        dq_ref[...] = jnp.concatenate(dq1, axis=0) + jnp.dot(ds2b, kcb, preferred_element_type=F32)
        dvc_ref[...] += lax.dot_general(p2b, dob, tn, preferred_element_type=F32)
        dkc_ref[...] += lax.dot_general(ds2b, qs, tn, preferred_element_type=F32)
        for i in nr:
            sl = pl.ds(starts[i], win)
            db_ref[0, offs[i]] += ds1[_row_block(i)]
            dv_ref[sl, :] += lax.dot_general(p1b[_row_block(i)], dob[_row_block(i)], tn, preferred_element_type=F32)
            dk_ref[sl, :] += lax.dot_general(ds1b[_row_block(i)], qs[_row_block(i)], tn, preferred_element_type=F32)

    row_spec = pl.BlockSpec((NA_ROWS_PER_STEP * GRID_W, dh), lambda h, r: (r, h))
    seq_spec = pl.BlockSpec((seq, dh), lambda h, r: (0, h))
    ctx_spec = pl.BlockSpec((nctx, dh), lambda h, r: (0, h))
    b_spec = pl.BlockSpec((1, NA_KH, GRID_W, win), lambda h, r: (h, 0, 0, 0))
    return pl.pallas_call(
        body, name=name, grid=(NA_HEADS, rows // NA_ROWS_PER_STEP),
        out_shape=(jax.ShapeDtypeStruct((seq, width), F32), jax.ShapeDtypeStruct((seq, width), F32),
                   jax.ShapeDtypeStruct((seq, width), F32), jax.ShapeDtypeStruct((nctx, width), F32),
                   jax.ShapeDtypeStruct((nctx, width), F32), jax.ShapeDtypeStruct(biasw.shape, F32)),
        in_specs=_na_specs(seq, nctx) + [row_spec],
        out_specs=(row_spec, seq_spec, seq_spec, ctx_spec, ctx_spec, b_spec),
        compiler_params=_cparams(),
    )(q, k, v, kc, vc, biasw, do)


def _na_attention(e, q, k, v, kc, vc, biasw):
    @jax.custom_vjp
    def attn(q, k, v, kc, vc, biasw):
        return _na_forward(f"na_fwd{e}", q.astype(BF16), k.astype(BF16), v.astype(BF16), kc.astype(BF16), vc.astype(BF16), biasw)

    def fwd(q, k, v, kc, vc, biasw):
        res = (q.astype(BF16), k.astype(BF16), v.astype(BF16), kc.astype(BF16), vc.astype(BF16), biasw)
        return _na_forward(f"na_fwd{e}", *res), res

    def bwd(res, do):
        return _na_backward(f"na_bwd{e}", *res, do)

    attn.defvjp(fwd, bwd)
    return attn(q, k, v, kc, vc, biasw)


def _bias_windows(rpb):
    col = jnp.arange(GRID_W)
    dc = jnp.clip(col[None, :] - col[:, None], -(NA_KW - 1), NA_KW - 1) + NA_KW - 1
    onehot = (dc[None] == jnp.arange(2 * NA_KW - 1)[:, None, None]).astype(F32)
    tq = jnp.einsum("hrd,dqk->hrqk", rpb, onehot, precision=HIGHEST)
    col_start = jnp.clip(col - NA_KW // 2, 0, GRID_W - NA_KW)
    in_win = (col[None, :] >= col_start[:, None]) & (col[None, :] < col_start[:, None] + NA_KW)
    wins = jnp.stack([tq[:, off:off + NA_KH] for off in range(NA_KH)], axis=1)
    wins = jnp.where(in_win[None, None, None], wins, NEG)
    return wins.transpose(0, 1, 3, 2, 4).reshape(rpb.shape[0], NA_KH, GRID_W, NA_KH * GRID_W)


def _chunk_cumsum(u, reverse, name):
    t, nh = u.shape
    ln = SSD_CHUNK

    def body(u_ref, o_ref):
        row = lax.broadcasted_iota(jnp.int32, (ln, ln), 0)
        col = lax.broadcasted_iota(jnp.int32, (ln, ln), 1)
        uu = u_ref[...]
        down = jnp.dot((col <= row).astype(F32), uu, precision=HIGHEST, preferred_element_type=F32)
        up = jnp.dot((col >= row).astype(F32), uu, precision=HIGHEST, preferred_element_type=F32)
        first = lax.broadcasted_iota(jnp.int32, (ln, nh), 1) < nh // 2
        o_ref[...] = jnp.where(first, up, down) if reverse else jnp.where(first, down, up)

    spec = pl.BlockSpec((ln, nh), lambda i: (i, 0))
    return pl.pallas_call(body, name=name, grid=(t // ln,), out_shape=jax.ShapeDtypeStruct(u.shape, F32),
                          in_specs=[spec], out_specs=spec, compiler_params=_cparams())(u)


def _ssd_cumsum(e, u):
    @jax.custom_vjp
    def cs(u):
        return _chunk_cumsum(u, False, f"ssd_cumsum{e}")

    cs.defvjp(lambda u: (_chunk_cumsum(u, False, f"ssd_cumsum{e}"), None),
              lambda _, g: (_chunk_cumsum(g, True, f"ssd_cumsum_bwd{e}"),))
    return cs(u)


def _ssd_step(h, x, bm, cm, dt, dt_t, cs, cs_t, *, direction, hpg, pdim):
    ln = x.shape[0]
    hp = hpg * pdim
    nt = (((1,), (1,)), ((), ()))
    row = lax.broadcasted_iota(jnp.int32, (ln, ln), 0)
    colm = lax.broadcasted_iota(jnp.int32, (ln, ln), 1)
    valid = (colm - row) * (1 - 2 * direction) <= 0
    last = (ln - 1) * (1 - direction)
    tot = jnp.sum(jnp.where(lax.broadcasted_iota(jnp.int32, (ln, hpg), 0) == last, cs, 0.0), axis=0, keepdims=True)
    cb16, bb16 = cm.astype(BF16), bm.astype(BF16)
    cbm = lax.dot_general(cb16, bb16, nt, preferred_element_type=F32)
    lane_head = lax.broadcasted_iota(jnp.int32, (1, hp), 1) // pdim
    sub_head = lax.broadcasted_iota(jnp.int32, (hp, 1), 0) // pdim
    y = jnp.zeros((ln, hp), F32)
    es = jnp.zeros((ln, hp), F32)
    we = jnp.zeros((ln, hp), F32)
    dend = jnp.zeros((hp, 1), F32)
    for r in range(hpg):
        cc = cs[:, r:r + 1]
        cr = cs_t[r:r + 1, :]
        decay = jnp.exp(jnp.where(valid, cc - cr, NEG))
        wm = (cbm * decay * dt_t[r:r + 1, :]).astype(BF16)
        mask = lane_head == r
        y = y + jnp.dot(wm, jnp.where(mask, x, 0.0).astype(BF16), preferred_element_type=F32)
        es = es + jnp.where(mask, jnp.exp(cc), 0.0)
        we = we + jnp.where(mask, jnp.exp(tot[:, r:r + 1] - cc) * dt[:, r:r + 1], 0.0)
        dend = dend + jnp.where(sub_head == r, jnp.exp(tot[:, r:r + 1]), 0.0)
    y = y + es * lax.dot_general(cb16, h.astype(BF16), nt, preferred_element_type=F32)
    h_new = h * dend + lax.dot_general((x * we).astype(BF16), bb16, (((0,), (0,)), ((), ())), preferred_element_type=F32)
    return y, h_new


def _ssd_chunk_of(d, s, ncc, nc):
    return jnp.where(d == 0, s, jnp.where(s < ncc, ncc - 1 - s, nc - 1 - s + ncc))


def _ssd_specs(cfg, step_of):
    hp, n, ln, hpg = cfg.hpg * SSD_HEAD_DIM, SSD_STATE, SSD_CHUNK, cfg.hpg
    ncc, nc = cfg.nctx // ln, cfg.t // ln

    def ch(d, s):
        return _ssd_chunk_of(d, step_of(s), ncc, nc)

    return dict(
        x=pl.BlockSpec((ln, hp), lambda d, g, s: (ch(d, s), g)),
        bc=pl.BlockSpec((ln, n), lambda d, g, s: (ch(d, s), g)),
        dt=pl.BlockSpec((1, 1, ln, hpg), lambda d, g, s: (d, g, ch(d, s), 0)),
        dt_t=pl.BlockSpec((1, 1, 8, ln), lambda d, g, s: (d, g, 0, ch(d, s))),
        y=pl.BlockSpec((1, ln, hp), lambda d, g, s: (d, ch(d, s), g)),
        bc2=pl.BlockSpec((1, ln, n), lambda d, g, s: (d, ch(d, s), g)),
        h=pl.BlockSpec((1, 1, 1, hp, n), lambda d, g, s: (d, g, step_of(s), 0, 0)),
    )


def _ssd_forward(cfg, name, xs, bm, cm, dt, dt_t, dta, dta_t):
    hp, n, hpg = cfg.hpg * SSD_HEAD_DIM, SSD_STATE, cfg.hpg
    nc = cfg.t // SSD_CHUNK
    sp = _ssd_specs(cfg, lambda s: s)

    def body(x_ref, b_ref, c_ref, dt_ref, dtt_ref, dta_ref, dtat_ref, y_ref, hs_ref, h_ref):
        d, s = pl.program_id(0), pl.program_id(2)

        @pl.when(s == 0)
        def _():
            h_ref[...] = jnp.zeros_like(h_ref)

        h = h_ref[...]
        hs_ref[0, 0, 0] = h
        y, h_new = _ssd_step(h, x_ref[...], b_ref[...], c_ref[...], dt_ref[0, 0], dtt_ref[0, 0], dta_ref[0, 0],
                             dtat_ref[0, 0], direction=d, hpg=hpg, pdim=SSD_HEAD_DIM)
        y_ref[0] = y
        h_ref[...] = h_new

    return pl.pallas_call(
        body, name=name, grid=(2, SSD_GROUPS, nc),
        out_shape=(jax.ShapeDtypeStruct((2, cfg.t, cfg.ssd_width), F32),
                   jax.ShapeDtypeStruct((2, SSD_GROUPS, nc, hp, n), F32)),
        in_specs=[sp["x"], sp["bc"], sp["bc"], sp["dt"], sp["dt_t"], sp["dt"], sp["dt_t"]],
        out_specs=(sp["y"], sp["h"]), scratch_shapes=[pltpu.VMEM((hp, n), F32)], compiler_params=_cparams(),
    )(xs, bm, cm, dt, dt_t, dta, dta_t)


def _ssd_backward(cfg, name, xs, bm, cm, dt, dt_t, dta, dta_t, hsave, dy):
    hp, n, hpg = cfg.hpg * SSD_HEAD_DIM, SSD_STATE, cfg.hpg
    nc = cfg.t // SSD_CHUNK
    sp = _ssd_specs(cfg, lambda s: nc - 1 - s)

    def body(x_ref, b_ref, c_ref, dt_ref, dtt_ref, dta_ref, dtat_ref, hs_ref, dy_ref,
             dx_ref, db_ref, dc_ref, ddt_ref, ddtt_ref, ddta_ref, ddtat_ref, dh_ref):
        d, s = pl.program_id(0), pl.program_id(2)

        @pl.when(s == 0)
        def _():
            dh_ref[...] = jnp.zeros_like(dh_ref)

        step = functools.partial(_ssd_step, direction=d, hpg=hpg, pdim=SSD_HEAD_DIM)
        _, vjp = jax.vjp(step, hs_ref[0, 0, 0], x_ref[...], b_ref[...], c_ref[...], dt_ref[0, 0], dtt_ref[0, 0],
                         dta_ref[0, 0], dtat_ref[0, 0])
        dh, dx, db, dc, ddt, ddtt, ddta, ddtat = vjp((dy_ref[0], dh_ref[...]))
        dh_ref[...] = dh
        dx_ref[0] = dx
        db_ref[0] = db
        dc_ref[0] = dc
        ddt_ref[0, 0] = ddt
        ddtt_ref[0, 0] = ddtt
        ddta_ref[0, 0] = ddta
        ddtat_ref[0, 0] = ddtat

    gn = SSD_GROUPS * n
    return pl.pallas_call(
        body, name=name, grid=(2, SSD_GROUPS, nc),
        out_shape=(jax.ShapeDtypeStruct((2, cfg.t, cfg.ssd_width), F32), jax.ShapeDtypeStruct((2, cfg.t, gn), F32),
                   jax.ShapeDtypeStruct((2, cfg.t, gn), F32), jax.ShapeDtypeStruct(dt.shape, F32),
                   jax.ShapeDtypeStruct(dt_t.shape, F32), jax.ShapeDtypeStruct(dt.shape, F32),
                   jax.ShapeDtypeStruct(dt_t.shape, F32)),
        in_specs=[sp["x"], sp["bc"], sp["bc"], sp["dt"], sp["dt_t"], sp["dt"], sp["dt_t"], sp["h"], sp["y"]],
        out_specs=(sp["y"], sp["bc2"], sp["bc2"], sp["dt"], sp["dt_t"], sp["dt"], sp["dt_t"]),
        scratch_shapes=[pltpu.VMEM((hp, n), F32)], compiler_params=_cparams(),
    )(xs, bm, cm, dt, dt_t, dta, dta_t, hsave, dy)


def _ssd_scan(cfg, e, xs, bm, cm, dt, dt_t, dta, dta_t):
    @jax.custom_vjp
    def scan(xs, bm, cm, dt, dt_t, dta, dta_t):
        return _ssd_forward(cfg, f"ssd_fwd{e}", xs, bm, cm, dt, dt_t, dta, dta_t)[0]

    def fwd(xs, bm, cm, dt, dt_t, dta, dta_t):
        y, hsave = _ssd_forward(cfg, f"ssd_fwd{e}", xs, bm, cm, dt, dt_t, dta, dta_t)
        return y, (xs, bm, cm, dt, dt_t, dta, dta_t, hsave)

    def bwd(res, dy):
        dx, db, dc, ddt, ddtt, ddta, ddtat = _ssd_backward(cfg, f"ssd_bwd{e}", *res, dy)
        return dx[0] + dx[1], db[0] + db[1], dc[0] + dc[1], ddt, ddtt, ddta, ddtat

    scan.defvjp(fwd, bwd)
    return scan(xs, bm, cm, dt, dt_t, dta, dta_t)


def _adam_math(w, g, m, v):
    m2 = ADAM_B1 * m + (1.0 - ADAM_B1) * g
    v2 = ADAM_B2 * v + (1.0 - ADAM_B2) * (g * g)
    m_hat = m2 / (1.0 - ADAM_B1 ** ADAM_STEP)
    v_hat = v2 / (1.0 - ADAM_B2 ** ADAM_STEP)
    delta = -ADAM_LR * (m_hat / (jnp.sqrt(v_hat) + ADAM_EPS) + ADAM_WD * w)
    return delta, m2, v2


def _adam_small(w, g, m, v, name):
    def body(w_ref, g_ref, m_ref, v_ref, d_ref, m2_ref, v2_ref):
        d_ref[...], m2_ref[...], v2_ref[...] = _adam_math(w_ref[...], g_ref[...], m_ref[...], v_ref[...])

    shp = jax.ShapeDtypeStruct(w.shape, F32)
    return pl.pallas_call(body, name=name, out_shape=(shp, shp, shp), in_specs=[VMEM] * 4, out_specs=(VMEM, VMEM, VMEM),
                          compiler_params=_cparams())(w, g, m, v)


def _adam_tiled(w, g, m, v, name):
    nl, r, c = w.shape
    tr = _pick(r, (256, 128, 64, 32, 16, 8))
    spec = pl.BlockSpec((1, tr, c), lambda l, t: (l, t, 0))

    def body(w_ref, g_ref, m_ref, v_ref, d_ref, m2_ref, v2_ref):
        d_ref[...], m2_ref[...], v2_ref[...] = _adam_math(w_ref[...], g_ref[...], m_ref[...], v_ref[...])

    shp = jax.ShapeDtypeStruct(w.shape, F32)
    return pl.pallas_call(body, name=name, grid=(nl, r // tr), out_shape=(shp, shp, shp), in_specs=[spec] * 4,
                          out_specs=(spec, spec, spec), compiler_params=_cparams())(w, g, m, v)


def _adam_sharded(w, m, v, part_a, part_b, *, transposed, name):
    nl, r, c = w.shape
    tr = _pick(r, (256, 128)) if transposed else _pick(r, (256, 128, 64, 32, 16))
    w_spec = pl.BlockSpec((1, tr, c), lambda l, t: (l, t, 0))
    if transposed:
        pa_spec = pl.BlockSpec((1, 1, c, tr), lambda l, t: (0, l, 0, t))
        pb_spec = pl.BlockSpec((3, 1, c, tr), lambda l, t: (0, l, 0, t))
    else:
        pa_spec = pl.BlockSpec((1, 1, tr, c), lambda l, t: (0, l, t, 0))
        pb_spec = pl.BlockSpec((3, 1, tr, c), lambda l, t: (0, l, t, 0))

    def body(w_ref, m_ref, v_ref, pa_ref, pb_ref, g_ref, d_ref, m2_ref, v2_ref):
        g = pa_ref[0, 0] + pb_ref[0, 0].astype(F32) + pb_ref[1, 0].astype(F32) + pb_ref[2, 0].astype(F32)
        if transposed:
            g = g.T
        g_ref[0] = g
        d_ref[0], m2_ref[0], v2_ref[0] = _adam_math(w_ref[0], g, m_ref[0], v_ref[0])

    shp = jax.ShapeDtypeStruct(w.shape, F32)
    return pl.pallas_call(
        body, name=name, grid=(nl, r // tr), out_shape=(shp, shp, shp, shp),
        in_specs=[w_spec, w_spec, w_spec, pa_spec, pb_spec], out_specs=(w_spec, w_spec, w_spec, w_spec),
        compiler_params=_cparams(),
    )(w, m, v, part_a, part_b)


def _pick_blocks(g, recv, axis, length, g_blocks, r_blocks, out_dtype, name):
    nl = g.shape[0]
    n = g_blocks.shape[0]
    blk_shape = list(g.shape)
    blk_shape[axis] = length
    if axis == 1:
        cols = g.shape[2]
        tc = _pick(cols, (512, 256, 128))
        g_spec = pl.BlockSpec((1, length, tc), lambda i, l, t, gb, rb: (l, gb[i], t))
        r_spec = pl.BlockSpec((1, 1, length, tc), lambda i, l, t, gb, rb: (rb[i], l, 0, t))
        o_spec = pl.BlockSpec((1, 1, length, tc), lambda i, l, t, gb, rb: (i, l, 0, t))
        grid = (n, nl, cols // tc)
    else:
        rows = g.shape[1]
        tr = _pick(rows, (512, 256, 128, 64, 32, 16))
        g_spec = pl.BlockSpec((1, tr, length), lambda i, l, t, gb, rb: (l, t, gb[i]))
        r_spec = pl.BlockSpec((1, 1, tr, length), lambda i, l, t, gb, rb: (rb[i], l, t, 0))
        o_spec = pl.BlockSpec((1, 1, tr, length), lambda i, l, t, gb, rb: (i, l, t, 0))
        grid = (n, nl, rows // tr)

    if recv is None:
        def body(gb_ref, rb_ref, g_ref, o_ref):
            o_ref[0] = g_ref[...].astype(out_dtype)
        in_specs, args = [g_spec], (g,)
    else:
        def body(gb_ref, rb_ref, g_ref, r_ref, o_ref):
            o_ref[0] = (g_ref[...] + r_ref[0].astype(F32)).astype(out_dtype)
        in_specs, args = [g_spec, r_spec], (g, recv)

    return pl.pallas_call(
        body, name=name, out_shape=jax.ShapeDtypeStruct((n, *blk_shape), out_dtype),
        grid_spec=pltpu.PrefetchScalarGridSpec(num_scalar_prefetch=2, grid=grid, in_specs=in_specs, out_specs=o_spec),
        compiler_params=_cparams(),
    )(g_blocks, r_blocks, *args)


def _flatten(arrs):
    flat = jnp.concatenate([a.reshape(-1).astype(F32) for a in arrs])
    n = flat.shape[0]
    n_pad = -(-n // 1024) * 1024
    return jnp.pad(flat, (0, n_pad - n)).reshape(n_pad // 128, 128)


def _unflatten(buf, shapes):
    flat = buf.reshape(-1)
    out, o = [], 0
    for s in shapes:
        n = math.prod(s)
        out.append(flat[o:o + n].reshape(s))
        o += n
    return out


def _rms(x, g):
    return x * lax.rsqrt(jnp.mean(x * x, axis=-1, keepdims=True) + EPS) * g


def _dw_conv(x, w, b=None):
    k = w.shape[0]
    ln = x.shape[0]
    xp = jnp.pad(x, ((k // 2, k // 2), (0, 0)))
    y = sum(w[i][None, :] * xp[i:i + ln] for i in range(k))
    return y if b is None else y + b


def _conv_two(x, nctx, w, b=None):
    return jnp.concatenate([_dw_conv(x[:nctx], w, b), _dw_conv(x[nctx:], w, b)], axis=0)


def _mod_rows(nctx, seq, ctx_vec, lat_vec):
    return jnp.concatenate([jnp.broadcast_to(ctx_vec, (nctx, ctx_vec.shape[-1])),
                            jnp.broadcast_to(lat_vec, (seq, lat_vec.shape[-1]))], axis=0)


def _even_mixer(cfg, h, e, wd, update_ctx):
    d, nctx, seq, t = cfg.d, cfg.nctx, cfg.s, cfg.t
    naw, sw = cfg.na_width, cfg.ssd_width
    gn = SSD_GROUPS * SSD_STATE
    p = _linear(h.astype(BF16), wd["win_t"][e], wd["g_win_t"][e], w_is_nk=True, tiles=cfg.tiles_in, name=f"in_even{e}")
    col_q, col_gate, col_z = 0, naw, 2 * naw
    col_k = col_z + sw
    col_v = col_k + naw
    col_xbc = col_v + naw
    col_dt = col_xbc + sw + 2 * gn

    def heads_norm(u, g):
        return _rms(u.reshape(u.shape[0], NA_HEADS, NA_HEAD_DIM), g).reshape(u.shape[0], naw)

    qn = heads_norm(p[:, col_q:col_q + naw], wd["q_norm_g"][e])
    kn = heads_norm(p[:, col_k:col_k + naw], wd["k_norm_g"][e])
    vv = p[:, col_v:col_v + naw]
    gate_a = jax.nn.silu(p[:, col_gate:col_gate + naw])
    biasw = _bias_windows(wd["na_rpb"][e])
    ya = _na_attention(e, qn[nctx:], kn[nctx:], vv[nctx:], kn[:nctx], vv[:nctx], biasw)
    if update_ctx:
        qc = qn[:nctx].reshape(nctx, NA_HEADS, NA_HEAD_DIM)
        kc = kn[:nctx].reshape(nctx, NA_HEADS, NA_HEAD_DIM)
        vc = vv[:nctx].reshape(nctx, NA_HEADS, NA_HEAD_DIM)
        sc = jnp.einsum("qhd,khd->hqk", qc, kc).astype(F32) * NA_HEAD_DIM ** -0.5
        yac = jnp.einsum("hqk,khd->qhd", jax.nn.softmax(sc, axis=-1), vc).reshape(nctx, naw)
    else:
        yac = jnp.zeros((nctx, naw), F32)
    ya = jnp.concatenate([yac, ya], axis=0) * gate_a

    xbc = jax.nn.silu(_conv_two(p[:, col_xbc:col_xbc + sw + 2 * gn], nctx, wd["ssd_conv_w"][e], wd["ssd_conv_b"][e]))
    xs, bm, cm = xbc[:, :sw], xbc[:, sw:sw + gn], xbc[:, sw + gn:]
    nh = 2 * SSD_GROUPS * cfg.hpg
    dt = jax.nn.softplus(p[:, col_dt:col_dt + nh] + wd["ssd_dt_bias"][e].reshape(1, nh))
    a = -jnp.exp(wd["ssd_a_log"][e]).reshape(1, nh)
    dta = dt * a

    def arrange(u):
        u4 = u.reshape(t, 2, SSD_GROUPS, cfg.hpg)
        return u4.transpose(1, 2, 0, 3), jnp.pad(u4.transpose(1, 2, 3, 0), ((0, 0), (0, 0), (0, 8 - cfg.hpg), (0, 0)))

    dt4, dt_t = arrange(dt)
    cs4, cs_t = arrange(_ssd_cumsum(e, dta))
    y2 = _ssd_scan(cfg, e, xs, bm, cm, dt4, dt_t, cs4, cs_t)
    dskip = jnp.repeat(wd["ssd_d"][e], SSD_HEAD_DIM)[None, :]
    yz = ((y2[0] + y2[1] + dskip * xs) * jax.nn.silu(p[:, col_z:col_z + sw])).reshape(t, SSD_GROUPS, -1)
    yz = yz * lax.rsqrt(jnp.mean(yz * yz, axis=-1, keepdims=True) + EPS)
    yb = yz.reshape(t, sw) * wd["ssd_norm_g"][e][None, :]
    ycat = jnp.concatenate([ya, yb], axis=-1)
    return _linear(ycat.astype(BF16), wd["wout"][e], wd["g_wout"][e], w_is_nk=False, tiles=cfg.tiles_out_even, name=f"out_even{e}")


def _odd_mixer(cfg, h, o, wd):
    d, nctx = cfg.d, cfg.nctx
    p = _linear(h.astype(BF16), wd["sc_win"][o], wd["g_sc_win"][o], w_is_nk=False, tiles=cfg.tiles_in_odd, name=f"in_odd{o}")
    bg, cg, hv, g = p[:, :d], p[:, d:2 * d], p[:, 2 * d:3 * d], p[:, 3 * d:]
    yy = bg * _conv_two(cg * hv, nctx, wd["sc_conv_w"][o])
    u = jax.nn.silu(g) * yy
    return _linear(u.astype(BF16), wd["sc_wout"][o], wd["g_sc_wout"][o], w_is_nk=False, tiles=cfg.tiles_out_odd, name=f"out_odd{o}")


def _local_loss(cfg, x, ctx, target, mods, mods_c, wd):
    d, nctx, seq = cfg.d, cfg.nctx, cfg.s
    xx = jnp.concatenate([ctx, x], axis=0)
    for i in range(DEPTH):
        update_ctx = any(j % 2 == 0 for j in range(i + 1, DEPTH))
        shift = _mod_rows(nctx, seq, mods_c[i, :d], mods[i, :d])
        scale = _mod_rows(nctx, seq, mods_c[i, d:2 * d], mods[i, d:2 * d])
        gate_c = mods_c[i, 2 * d:] if update_ctx else jnp.zeros((d,), F32)
        gate = _mod_rows(nctx, seq, gate_c, mods[i, 2 * d:])
        h = _rms(xx, wd["norm_g"][i][None, :]) * (1 + scale) + shift
        if i % 2 == 0:
            y = _even_mixer(cfg, h, i // 2, wd, update_ctx)
        else:
            y = _odd_mixer(cfg, h, i // 2, wd)
        xx = xx + gate * y
    err = jnp.square(xx[nctx:] - target)
    return 0.5 * jnp.sum(jnp.mean(err, axis=-1))


SMALL_REPLICATED = ["norm_g", "ssd_conv_b", "ssd_a_log", "ssd_dt_bias", "ssd_d", "ssd_norm_g", "q_norm_g", "k_norm_g", "na_rpb"]
WEIGHT_ORDER = ["c_ctx", "ada_w", "ada_b", "norm_g", "na_ssd_w_in", "ssd_conv_w", "ssd_conv_b", "ssd_a_log", "ssd_dt_bias",
                "ssd_d", "ssd_norm_g", "q_norm_g", "k_norm_g", "na_rpb", "na_ssd_w_out", "sc_w_in", "sc_conv_w", "sc_w_out"]


def kernel(x, c, ctx, c_ctx, ada_w, ada_b, norm_g, na_ssd_w_in, ssd_conv_w, ssd_conv_b, ssd_a_log, ssd_dt_bias, ssd_d, ssd_norm_g, q_norm_g, k_norm_g, na_rpb, na_ssd_w_out, sc_w_in, sc_conv_w, sc_w_out, loss_target, m_c_ctx, m_ada_w, m_ada_b, m_norm_g, m_na_ssd_w_in, m_ssd_conv_w, m_ssd_conv_b, m_ssd_a_log, m_ssd_dt_bias, m_ssd_d, m_ssd_norm_g, m_q_norm_g, m_k_norm_g, m_na_rpb, m_na_ssd_w_out, m_sc_w_in, m_sc_conv_w, m_sc_w_out, v_c_ctx, v_ada_w, v_ada_b, v_norm_g, v_na_ssd_w_in, v_ssd_conv_w, v_ssd_conv_b, v_ssd_a_log, v_ssd_dt_bias, v_ssd_d, v_ssd_norm_g, v_q_norm_g, v_k_norm_g, v_na_rpb, v_na_ssd_w_out, v_sc_w_in, v_sc_conv_w, v_sc_w_out):
    given = dict(locals())
    weights = {n: given[n] for n in WEIGHT_ORDER}
    mom_m = {n: given["m_" + n] for n in WEIGHT_ORDER}
    mom_v = {n: given["v_" + n] for n in WEIGHT_ORDER}

    d = x.shape[-1]
    seq, nctx = x.shape[1], ctx.shape[1]
    n_in_shard = na_ssd_w_in.shape[-1]
    n_in = n_in_shard * NDEV
    n_pad = -(-n_in // PAD_TO) * PAD_TO
    hpg = (d // SSD_HEAD_DIM) // SSD_GROUPS
    t = nctx + seq
    tm = _pick(t, (1408, 768, 512, 256, 128))
    cfg = SimpleNamespace(
        d=d, s=seq, nctx=nctx, t=t, hpg=hpg, na_width=NA_HEADS * NA_HEAD_DIM, ssd_width=d, n_in=n_in, n_pad=n_pad,
        tiles_in=(tm, _pick(n_pad, (512, 256, 128)), d),
        tiles_out_even=(tm, _pick(d, (1024, 512, 256, 128)), _pick(NA_HEADS * NA_HEAD_DIM + d, (1024, 512, 256, 128))),
        tiles_in_odd=(tm, _pick(4 * d, (1024, 512, 256, 128)), d),
        tiles_out_odd=(tm, _pick(d, (1024, 512, 256, 128)), d),
    )
    me = _my_index()
    xl, cl, ctxl, tgt = x[0], c, ctx[0], loss_target[0]

    ncol = ada_w.shape[-1]
    c_rows = -(-d // 128)
    c_all = _small_allgather(jnp.pad(cl.reshape(-1), (0, c_rows * 128 - d)).reshape(c_rows, 128), "gather_c")[0]
    c_all = c_all.reshape(NDEV, -1)[:, :d]
    cond = jnp.concatenate([c_all, c_ctx[None, :], jnp.zeros((16 - NDEV - 1, d), F32)], axis=0)
    s16 = jax.nn.silu(cond)
    ada_b_mine = lax.dynamic_slice_in_dim(ada_b, me * ncol, ncol, axis=1)
    mod_part = jnp.stack([
        _matmul(s16, ada_w[i], tm=16, tn=_pick(ncol, (768, 512, 256, 128)), tk=d, name=f"adaln{i}") + ada_b_mine[i][None, :]
        for i in range(DEPTH)])
    mp_rows = DEPTH * 16 * ncol // 128
    mod_all = _small_allgather(mod_part.reshape(mp_rows, 128), "gather_mod")[0]
    mod_all = mod_all.reshape(NDEV, DEPTH, 16, ncol).transpose(1, 2, 0, 3).reshape(DEPTH, 16, NDEV * ncol)
    mods = lax.dynamic_index_in_dim(mod_all, me, axis=1, keepdims=False)
    mods_c = mod_all[:, NDEV]

    packed = _pack_transposed(na_ssd_w_in, "pack_w_in")
    wout_b = _cast_bf16(na_ssd_w_out, "cast_w_out")
    scwin_b = _cast_bf16(sc_w_in, "cast_sc_w_in")
    scwout_b = _cast_bf16(sc_w_out, "cast_sc_w_out")
    packed_all, wout_all, scwin_all, scwout_all = _big_allgather(
        [packed, wout_b, scwin_b, scwout_b], [1, 1, 2, 1], "gather_weights")
    win_t = _unpack(packed_all, n_pad, "unpack_w_in")
    conv_shapes = [ssd_conv_w.shape, sc_conv_w.shape]
    conv_all = _small_allgather(_flatten([ssd_conv_w, sc_conv_w]), "gather_conv")[0]
    conv_parts = [_unflatten(conv_all[j], conv_shapes) for j in range(NDEV)]
    ssd_conv_full = jnp.concatenate([cp[0] for cp in conv_parts], axis=-1)
    sc_conv_full = jnp.concatenate([cp[1] for cp in conv_parts], axis=-1)

    small = {n: weights[n] for n in SMALL_REPLICATED}
    small["ssd_conv_w"] = ssd_conv_full
    small["sc_conv_w"] = sc_conv_full
    gslots = dict(g_win_t=jnp.zeros(win_t.shape, F32), g_wout=jnp.zeros(wout_all.shape, F32),
                  g_sc_win=jnp.zeros(scwin_all.shape, F32), g_sc_wout=jnp.zeros(scwout_all.shape, F32))
    frozen = dict(win_t=win_t, wout=wout_all, sc_win=scwin_all, sc_wout=scwout_all)

    def loss_fn(xl, mods, mods_c, small, gslots):
        return _local_loss(cfg, xl, ctxl, tgt, mods, mods_c, {**small, **gslots, **frozen})

    loss_local, (g_x, g_mods, g_mods_c, g_small, g_big) = jax.value_and_grad(loss_fn, argnums=(0, 1, 2, 3, 4))(
        xl, mods, mods_c, small, gslots)

    small_names = SMALL_REPLICATED + ["ssd_conv_w", "sc_conv_w"]
    small_shapes = [g_small[n].shape for n in small_names] + [g_mods_c.shape]
    flat_small = _flatten([g_small[n] for n in small_names] + [g_mods_c])
    _, small_sum = _small_allgather(flat_small, "gather_small_grads")
    summed = _unflatten(small_sum, small_shapes)
    g_rep = dict(zip(small_names, summed[:-1]))
    g_mods_c_tot = summed[-1]
    gm_rows = DEPTH * NDEV * ncol // 128
    gm_all = _small_allgather(g_mods.reshape(gm_rows, 128), "gather_mod_grads")[0].reshape(NDEV, DEPTH, NDEV * ncol)
    dm = jnp.concatenate([gm_all.transpose(1, 0, 2), g_mods_c_tot[:, None, :],
                          jnp.zeros((DEPTH, 16 - NDEV - 1, NDEV * ncol), F32)], axis=1)
    grad_ada_b = jnp.sum(dm, axis=1)
    dm_mine = lax.dynamic_slice_in_dim(dm, me * ncol, ncol, axis=2)
    grad_ada_w = jnp.stack([
        _matmul(s16, dm_mine[i], ta=True, tm=_pick(d, (512, 256, 128)), tn=_pick(ncol, (768, 512, 256, 128)), tk=16,
                name=f"adaln_gw{i}") for i in range(DEPTH)])
    ds_part = sum(_matmul(dm_mine[i], ada_w[i], tb=True, tm=16, tn=_pick(d, (2048, 1024, 512, 256, 128)),
                          tk=_pick(ncol, (768, 512, 256, 128)), name=f"adaln_gs{i}") for i in range(DEPTH))[NDEV]
    ds_ctx = _small_allgather(jnp.pad(ds_part, (0, c_rows * 128 - d)).reshape(c_rows, 128), "gather_c_ctx_grad")[1]
    ds_ctx = ds_ctx.reshape(-1)[:d]
    sig = jax.nn.sigmoid(c_ctx)
    grad_c_ctx = ds_ctx * (sig * (1 + c_ctx * (1 - sig)))

    big = [g_big["g_win_t"], g_big["g_wout"], g_big["g_sc_win"], g_big["g_sc_wout"]]
    axes = [1, 1, 2, 1]
    lens = [n_in_shard, na_ssd_w_out.shape[1], sc_w_in.shape[2], sc_w_out.shape[1]]
    px, py, pc = _pos()
    kself = 2 * px + py
    kothers = jnp.stack([2 * (1 - px) + py, 2 * px + (1 - py), 2 * (1 - px) + (1 - py)]).astype(jnp.int32)
    k4 = jnp.arange(4, dtype=jnp.int32)
    kme = kself.astype(jnp.int32).reshape(1)
    sends = [_pick_blocks(g, None, ax, ln, 2 * k4 + (1 - pc), k4, BF16, f"rs_cast{i}")
             for i, (g, ax, ln) in enumerate(zip(big, axes, lens))]
    recv_a = _rs_stage_a(sends, "reduce_scatter_d2d")
    part_a = [_pick_blocks(g, r, ax, ln, 2 * kme + pc, kme, F32, f"pair_sum_mine{i}")
              for i, (g, r, ax, ln) in enumerate(zip(big, recv_a, axes, lens))]
    part_s = [_pick_blocks(g, r, ax, ln, 2 * kothers + pc, kothers, BF16, f"pair_sum_send{i}")
              for i, (g, r, ax, ln) in enumerate(zip(big, recv_a, axes, lens))]
    part_b = _rs_stage_b(part_s, "reduce_scatter_ici")

    res = {}
    res["na_ssd_w_in"] = _adam_sharded(na_ssd_w_in, m_na_ssd_w_in, v_na_ssd_w_in, part_a[0], part_b[0], transposed=True, name="adam_w_in")
    res["na_ssd_w_out"] = _adam_sharded(na_ssd_w_out, m_na_ssd_w_out, v_na_ssd_w_out, part_a[1], part_b[1], transposed=False, name="adam_w_out")
    res["sc_w_in"] = _adam_sharded(sc_w_in, m_sc_w_in, v_sc_w_in, part_a[2], part_b[2], transposed=False, name="adam_sc_w_in")
    res["sc_w_out"] = _adam_sharded(sc_w_out, m_sc_w_out, v_sc_w_out, part_a[3], part_b[3], transposed=False, name="adam_sc_w_out")

    grads = dict(g_rep)
    grads["ssd_conv_w"] = lax.dynamic_slice_in_dim(g_rep["ssd_conv_w"], me * ssd_conv_w.shape[-1], ssd_conv_w.shape[-1], axis=2)
    grads["sc_conv_w"] = lax.dynamic_slice_in_dim(g_rep["sc_conv_w"], me * sc_conv_w.shape[-1], sc_conv_w.shape[-1], axis=2)
    grads["c_ctx"] = grad_c_ctx
    res["ada_w"] = (grad_ada_w, *_adam_tiled(ada_w, grad_ada_w, m_ada_w, v_ada_w, "adam_ada_w"))
    grads["ada_b"] = grad_ada_b
    rest = [n for n in WEIGHT_ORDER if n not in res]
    shapes = [weights[n].shape for n in rest]
    d_flat, m_flat, v_flat = _adam_small(_flatten([weights[n] for n in rest]), _flatten([grads[n] for n in rest]),
                                         _flatten([mom_m[n] for n in rest]), _flatten([mom_v[n] for n in rest]), "adam_small")
    for n, dd, mm, vv in zip(rest, _unflatten(d_flat, shapes), _unflatten(m_flat, shapes), _unflatten(v_flat, shapes)):
        res[n] = (grads[n], dd, mm, vv)

    loss = lax.psum(loss_local, ("x", "y", "c"))
    return (loss, g_x[None], *[res[n][0] for n in WEIGHT_ORDER], *[res[n][1] for n in WEIGHT_ORDER],
            *[res[n][2] for n in WEIGHT_ORDER], *[res[n][3] for n in WEIGHT_ORDER])
```

```python
import functools
import math
from types import SimpleNamespace

import jax
import jax.numpy as jnp
from jax import lax
from jax.experimental import pallas as pl
from jax.experimental.pallas import tpu as pltpu

F32 = jnp.float32
BF16 = jnp.bfloat16
U32 = jnp.uint32
HIGHEST = lax.Precision.HIGHEST
MESH = pl.DeviceIdType.MESH
ANY = pl.BlockSpec(memory_space=pl.ANY)
VMEM = pl.BlockSpec(memory_space=pltpu.VMEM)

NDEV = 8
DEPTH = 4
GRID_W = 64
EPS = 1e-6
NA_HEADS = 16
NA_HEAD_DIM = 128
NA_KH = 8
NA_KW = 16
SSD_HEAD_DIM = 64
SSD_GROUPS = 8
SSD_STATE = 128
SSD_CONV = 5
SSD_CHUNK = 128
SC_CONV = 3
ADAM_LR = 0.001
ADAM_B1 = 0.9
ADAM_B2 = 0.999
ADAM_EPS = 1e-08
ADAM_WD = 0.01
ADAM_STEP = 10
NEG = -1e30
VMEM_LIMIT = 56 * 1024 * 1024
PAD_TO = 512


def _pos():
    return lax.axis_index("x"), lax.axis_index("y"), lax.axis_index("c")


def _my_index():
    x, y, c = _pos()
    return 4 * x + 2 * y + c


def _pick(n, prefs):
    for p in prefs:
        if n % p == 0:
            return p
    return n


def _cparams(**kw):
    return pltpu.CompilerParams(vmem_limit_bytes=VMEM_LIMIT, **kw)


def _small_allgather(v, name):
    rows, lanes = v.shape

    def body(x_ref, out_ref, sum_ref, send_sems, recv_sems):
        x, y, c = _pos()
        me = 4 * x + 2 * y + c
        out_ref[me] = x_ref[...]
        copies = []
        for k in range(1, NDEV):
            peer = (1 - x if k & 4 else x, 1 - y if k & 2 else y, 1 - c if k & 1 else c)
            cp = pltpu.make_async_remote_copy(src_ref=x_ref, dst_ref=out_ref.at[me], send_sem=send_sems.at[k - 1],
                                              recv_sem=recv_sems.at[k - 1], device_id=peer, device_id_type=MESH)
            cp.start()
            copies.append(cp)
        for cp in copies:
            cp.wait()
        acc = out_ref[0]
        for j in range(1, NDEV):
            acc = acc + out_ref[j]
        sum_ref[...] = acc

    return pl.pallas_call(
        body, name=name,
        out_shape=(jax.ShapeDtypeStruct((NDEV, rows, lanes), v.dtype), jax.ShapeDtypeStruct((rows, lanes), v.dtype)),
        in_specs=[VMEM], out_specs=(VMEM, VMEM),
        scratch_shapes=[pltpu.SemaphoreType.DMA((NDEV - 1,)), pltpu.SemaphoreType.DMA((NDEV - 1,))],
        compiler_params=_cparams(),
    )(v)


def _window(ref, axis, idx, length):
    sl = [slice(None)] * len(ref.shape)
    sl[axis] = pl.ds(pl.multiple_of(idx * length, min(length & -length, 1024)), length)
    return ref.at[tuple(sl)]


def _big_allgather(shards, axes, name):
    n = len(shards)
    out_shapes = []
    for s, ax in zip(shards, axes):
        shp = list(s.shape)
        shp[ax] *= NDEV
        out_shapes.append(jax.ShapeDtypeStruct(tuple(shp), s.dtype))

    def body(*refs):
        xs, outs = refs[:n], refs[n:2 * n]
        send_sems, recv_sems, local_sems = refs[2 * n:]
        x, y, c = _pos()
        me, sib = (x, y, c), (x, y, 1 - c)
        chips = [(1 - x, y), (x, 1 - y), (1 - x, 1 - y)]

        def win(a, px, py, pc):
            return _window(outs[a], axes[a], 4 * px + 2 * py + pc, shards[a].shape[axes[a]])

        def copy(a, k, block, to, src=None):
            return pltpu.make_async_remote_copy(src_ref=win(a, *block) if src is None else src, dst_ref=win(a, *block),
                                                send_sem=send_sems.at[a * 7 + k], recv_sem=recv_sems.at[a * 7 + k],
                                                device_id=to, device_id_type=MESH)

        mine = [pltpu.make_async_copy(xs[a], win(a, *me), local_sems.at[a]) for a in range(n)]
        for cp in mine:
            cp.start()
        first = []
        for a in range(n):
            first.append(copy(a, 0, me, sib, src=xs[a]))
            first += [copy(a, 1 + j, me, (*chip, c), src=xs[a]) for j, chip in enumerate(chips)]
        for cp in first:
            cp.start()
        passed = []
        for j, chip in enumerate(chips):
            for a in range(n):
                copy(a, 1 + j, (*chip, c), me).wait_recv()
                cp = copy(a, 4 + j, (*chip, c), sib)
                cp.start()
                passed.append(cp)
        for a in range(n):
            copy(a, 0, sib, me).wait_recv()
            for j, chip in enumerate(chips):
                copy(a, 4 + j, (*chip, 1 - c), me).wait_recv()
        for cp in first + passed:
            cp.wait_send()
        for cp in mine:
            cp.wait()

    return pl.pallas_call(
        body, name=name, out_shape=tuple(out_shapes), in_specs=[ANY] * n, out_specs=tuple([ANY] * n),
        scratch_shapes=[pltpu.SemaphoreType.DMA((7 * n,)), pltpu.SemaphoreType.DMA((7 * n,)), pltpu.SemaphoreType.DMA((n,))],
        compiler_params=_cparams(),
    )(*shards)


def _rs_stage_a(sends, name):
    n = len(sends)
    out_shapes = [jax.ShapeDtypeStruct(s.shape, s.dtype) for s in sends]

    def body(*refs):
        ss, outs = refs[:n], refs[n:2 * n]
        send_sems, recv_sems = refs[2 * n:]
        x, y, c = _pos()
        sib = (x, y, 1 - c)
        copies = []
        for a in range(n):
            for k in range(4):
                cp = pltpu.make_async_remote_copy(src_ref=ss[a].at[k], dst_ref=outs[a].at[k], send_sem=send_sems.at[a * 4 + k],
                                                  recv_sem=recv_sems.at[a * 4 + k], device_id=sib, device_id_type=MESH)
                cp.start()
                copies.append(cp)
        for cp in copies:
            cp.wait()

    return pl.pallas_call(
        body, name=name, out_shape=tuple(out_shapes), in_specs=[ANY] * n, out_specs=tuple([ANY] * n),
        scratch_shapes=[pltpu.SemaphoreType.DMA((4 * n,)), pltpu.SemaphoreType.DMA((4 * n,))],
        compiler_params=_cparams(),
    )(*sends)


def _rs_stage_b(parts, name):
    n = len(parts)
    out_shapes = [jax.ShapeDtypeStruct(p.shape, p.dtype) for p in parts]

    def body(*refs):
        ps, outs = refs[:n], refs[n:2 * n]
        send_sems, recv_sems = refs[2 * n:]
        x, y, c = _pos()
        chips = [(1 - x, y), (x, 1 - y), (1 - x, 1 - y)]
        copies = []
        for a in range(n):
            for j, (px, py) in enumerate(chips):
                cp = pltpu.make_async_remote_copy(src_ref=ps[a].at[j], dst_ref=outs[a].at[j],
                                                  send_sem=send_sems.at[a * 3 + j], recv_sem=recv_sems.at[a * 3 + j],
                                                  device_id=(px, py, c), device_id_type=MESH)
                cp.start()
                copies.append(cp)
        for cp in copies:
            cp.wait()

    return pl.pallas_call(
        body, name=name, out_shape=tuple(out_shapes), in_specs=[ANY] * n, out_specs=tuple([ANY] * n),
        scratch_shapes=[pltpu.SemaphoreType.DMA((3 * n,)), pltpu.SemaphoreType.DMA((3 * n,))],
        compiler_params=_cparams(),
    )(*parts)


def _matmul(a, b, *, ta=False, tb=False, tm, tn, tk, name, b_off=(0, 0), b_extent=None):
    m, kdim = (a.shape[1], a.shape[0]) if ta else a.shape
    b_shape = b.shape if b_extent is None else b_extent
    n = b_shape[0] if tb else b_shape[1]
    assert (b_shape[1] if tb else b_shape[0]) == kdim, (a.shape, b_shape, ta, tb)
    assert m % tm == 0 and n % tn == 0 and kdim % tk == 0, (m, n, kdim, tm, tn, tk)
    o0, o1 = b_off
    nk = kdim // tk
    dn = (((0 if ta else 1,), (1 if tb else 0,)), ((), ()))

    def body(a_ref, b_ref, o_ref, acc_ref):
        k = pl.program_id(2)
        part = lax.dot_general(a_ref[...].astype(BF16), b_ref[...].astype(BF16), dn, preferred_element_type=F32)
        if nk == 1:
            o_ref[...] = part
        else:
            @pl.when(k == 0)
            def _():
                acc_ref[...] = part

            @pl.when(k > 0)
            def _():
                acc_ref[...] += part

            @pl.when(k == nk - 1)
            def _():
                o_ref[...] = acc_ref[...]

    a_spec = pl.BlockSpec((tk, tm), lambda i, j, k: (k, i)) if ta else pl.BlockSpec((tm, tk), lambda i, j, k: (i, k))
    b_spec = (pl.BlockSpec((tn, tk), lambda i, j, k: (j + o0, k + o1)) if tb
              else pl.BlockSpec((tk, tn), lambda i, j, k: (k + o0, j + o1)))
    acc_shape = (tm, tn) if nk > 1 else (8, 128)
    return pl.pallas_call(
        body, name=name, grid=(m // tm, n // tn, nk), out_shape=jax.ShapeDtypeStruct((m, n), F32),
        in_specs=[a_spec, b_spec], out_specs=pl.BlockSpec((tm, tn), lambda i, j, k: (i, j)),
        scratch_shapes=[pltpu.VMEM(acc_shape, F32)], compiler_params=_cparams(),
    )(a, b)


def _linear(a, w, gslot, *, w_is_nk, tiles, name):
    tm, tn, tk = tiles

    @jax.custom_vjp
    def lin(a, w, gslot):
        return _matmul(a, w, tb=w_is_nk, tm=tm, tn=tn, tk=tk, name=name + "_fwd")

    def fwd(a, w, gslot):
        return lin(a, w, gslot), (a, w)

    def bwd(res, g):
        a, w = res
        gb = g
        t, kdim = a.shape
        n = g.shape[1]
        tt = _pick(t, (1408, 768, 512, 256, 128))
        tkk = _pick(kdim, (2048, 1024, 512, 256, 128))
        tnn = _pick(n, (512, 256, 128))
        da = _matmul(gb, w, tb=not w_is_nk, tm=tt, tn=tkk, tk=tnn, name=name + "_bwd_a")
        tok = _pick(t, (768, 512, 256, 128))
        if w_is_nk:
            dw = _matmul(gb, a, ta=True, tm=tnn, tn=tkk, tk=tok, name=name + "_bwd_w")
        else:
            tkw = _pick(kdim, (512, 256, 128))
            tnw = _pick(n, (2048, 1024, 512, 256, 128))
            dw = _matmul(a, gb, ta=True, tm=tkw, tn=tnw, tk=tok, name=name + "_bwd_w")
        return da.astype(a.dtype), jnp.zeros_like(w), dw

    lin.defvjp(fwd, bwd)
    return lin(a, w, gslot)


def _pack_transposed(w, name):
    nl, kdim, r = w.shape
    half = kdim // 2
    tc = _pick(half, (256, 128))

    def body(lo_ref, hi_ref, o_ref):
        lo = pltpu.bitcast(lo_ref[0].astype(BF16).astype(F32).T, U32) >> 16
        hi = pltpu.bitcast(hi_ref[0].astype(BF16).astype(F32).T, U32) & jnp.uint32(0xFFFF0000)
        o_ref[0] = pltpu.bitcast(hi | lo, F32)

    nb = half // tc
    return pl.pallas_call(
        body, name=name, grid=(nl, nb), out_shape=jax.ShapeDtypeStruct((nl, r, half), F32),
        in_specs=[pl.BlockSpec((1, tc, r), lambda l, t: (l, t, 0)), pl.BlockSpec((1, tc, r), lambda l, t: (l, t + nb, 0))],
        out_specs=pl.BlockSpec((1, r, tc), lambda l, t: (l, 0, t)), compiler_params=_cparams(),
    )(w, w)


def _unpack(packed, n_pad, name):
    nl, n, half = packed.shape
    tr = math.gcd(math.gcd(n, n_pad - n), 64) if n_pad > n else _pick(n, (64, 32, 16))
    nin = n // tr

    def body(p_ref, o_ref):
        t = pl.program_id(1)

        @pl.when(t < nin)
        def _():
            u = pltpu.bitcast(p_ref[0], U32)
            o_ref[0, :, :half] = pltpu.bitcast(u << 16, F32).astype(BF16)
            o_ref[0, :, half:] = pltpu.bitcast(u & jnp.uint32(0xFFFF0000), F32).astype(BF16)

        @pl.when(t >= nin)
        def _():
            o_ref[...] = jnp.zeros_like(o_ref)

    return pl.pallas_call(
        body, name=name, grid=(nl, n_pad // tr), out_shape=jax.ShapeDtypeStruct((nl, n_pad, 2 * half), BF16),
        in_specs=[pl.BlockSpec((1, tr, half), lambda l, t: (l, jnp.minimum(t, nin - 1), 0))],
        out_specs=pl.BlockSpec((1, tr, 2 * half), lambda l, t: (l, t, 0)), compiler_params=_cparams(),
    )(packed)


def _cast_bf16(w, name):
    nl, r, c = w.shape
    tr = _pick(r, (512, 256, 128, 64, 32, 16))

    def body(w_ref, o_ref):
        o_ref[...] = w_ref[...].astype(BF16)

    return pl.pallas_call(
        body, name=name, grid=(nl, r // tr), out_shape=jax.ShapeDtypeStruct(w.shape, BF16),
        in_specs=[pl.BlockSpec((1, tr, c), lambda l, t: (l, t, 0))], out_specs=pl.BlockSpec((1, tr, c), lambda l, t: (l, t, 0)),
        compiler_params=_cparams(),
    )(w)


NA_ROWS_PER_STEP = 4


def _row_block(i):
    return slice(i * GRID_W, (i + 1) * GRID_W)


def _na_probs(q_ref, k_ref, kc_ref, b_ref, step, rows):
    scale = NA_HEAD_DIM ** -0.5
    nt = (((1,), (1,)), ((), ()))
    qs = q_ref[...]
    kws, starts, offs, s1 = [], [], [], []
    for i in range(NA_ROWS_PER_STEP):
        r = step * NA_ROWS_PER_STEP + i
        rs = jnp.clip(r - NA_KH // 2, 0, rows - NA_KH)
        offs.append(rs - r + NA_KH - 1)
        starts.append(pl.multiple_of(rs * GRID_W, GRID_W))
        kws.append(k_ref[pl.ds(starts[i], NA_KH * GRID_W), :])
        s1.append(lax.dot_general(qs[_row_block(i)], kws[i], nt, preferred_element_type=F32) * scale + b_ref[0, offs[i]])
    s1 = jnp.concatenate(s1, axis=0)
    s2 = lax.dot_general(qs, kc_ref[...], nt, preferred_element_type=F32) * scale
    m = jnp.maximum(jnp.max(s1, axis=-1, keepdims=True), jnp.max(s2, axis=-1, keepdims=True))
    e1 = jnp.exp(s1 - m)
    e2 = jnp.exp(s2 - m)
    inv = 1.0 / (jnp.sum(e1, axis=-1, keepdims=True) + jnp.sum(e2, axis=-1, keepdims=True))
    return qs, kws, starts, offs, e1 * inv, e2 * inv


def _na_specs(seq, nctx):
    dh = NA_HEAD_DIM
    win = NA_KH * GRID_W
    return [
        pl.BlockSpec((NA_ROWS_PER_STEP * GRID_W, dh), lambda h, r: (r, h)),
        pl.BlockSpec((seq, dh), lambda h, r: (0, h)),
        pl.BlockSpec((seq, dh), lambda h, r: (0, h)),
        pl.BlockSpec((nctx, dh), lambda h, r: (0, h)),
        pl.BlockSpec((nctx, dh), lambda h, r: (0, h)),
        pl.BlockSpec((1, NA_KH, GRID_W, win), lambda h, r: (h, 0, 0, 0)),
    ]


def _na_forward(name, q, k, v, kc, vc, biasw):
    seq, width = q.shape
    nctx = kc.shape[0]
    rows = seq // GRID_W
    assert rows % NA_ROWS_PER_STEP == 0, rows

    def body(q_ref, k_ref, v_ref, kc_ref, vc_ref, b_ref, o_ref):
        _, _, starts, _, p1, p2 = _na_probs(q_ref, k_ref, kc_ref, b_ref, pl.program_id(1), rows)
        p1b = p1.astype(BF16)
        o1 = [jnp.dot(p1b[_row_block(i)], v_ref[pl.ds(starts[i], NA_KH * GRID_W), :], preferred_element_type=F32)
              for i in range(NA_ROWS_PER_STEP)]
        o_ref[...] = jnp.concatenate(o1, axis=0) + jnp.dot(p2.astype(BF16), vc_ref[...], preferred_element_type=F32)

    return pl.pallas_call(
        body, name=name, grid=(NA_HEADS, rows // NA_ROWS_PER_STEP), out_shape=jax.ShapeDtypeStruct((seq, width), F32),
        in_specs=_na_specs(seq, nctx),
        out_specs=pl.BlockSpec((NA_ROWS_PER_STEP * GRID_W, NA_HEAD_DIM), lambda h, r: (r, h)),
        compiler_params=_cparams(),
    )(q, k, v, kc, vc, biasw)


def _na_backward(name, q, k, v, kc, vc, biasw, do):
    seq, width = q.shape
    nctx = kc.shape[0]
    rows = seq // GRID_W
    win = NA_KH * GRID_W
    dh = NA_HEAD_DIM
    scale = dh ** -0.5
    nt = (((1,), (1,)), ((), ()))
    tn = (((0,), (0,)), ((), ()))

    def body(q_ref, k_ref, v_ref, kc_ref, vc_ref, b_ref, do_ref, dq_ref, dk_ref, dv_ref, dkc_ref, dvc_ref, db_ref):
        r = pl.program_id(1)

        @pl.when(r == 0)
        def _():
            dk_ref[...] = jnp.zeros_like(dk_ref)
            dv_ref[...] = jnp.zeros_like(dv_ref)
            dkc_ref[...] = jnp.zeros_like(dkc_ref)
            dvc_ref[...] = jnp.zeros_like(dvc_ref)
            db_ref[...] = jnp.zeros_like(db_ref)

        nr = range(NA_ROWS_PER_STEP)
        qs, kws, starts, offs, p1, p2 = _na_probs(q_ref, k_ref, kc_ref, b_ref, r, rows)
        vcb = vc_ref[...]
        kcb = kc_ref[...]
        dob = do_ref[...].astype(BF16)
        p1b, p2b = p1.astype(BF16), p2.astype(BF16)
        dp1 = jnp.concatenate([lax.dot_general(dob[_row_block(i)], v_ref[pl.ds(starts[i], win), :], nt,
                                               preferred_element_type=F32) for i in nr], axis=0)
        dp2 = lax.dot_general(dob, vcb, nt, preferred_element_type=F32)
        delta = jnp.sum(dp1 * p1, axis=-1, keepdims=True) + jnp.sum(dp2 * p2, axis=-1, keepdims=True)
        ds1 = p1 * (dp1 - delta)
        ds2 = p2 * (dp2 - delta)
        ds1b = (ds1 * scale).astype(BF16)
        ds2b = (ds2 * scale).astype(BF16)
        dq1 = [jnp.dot(ds1b[_row_block(i)], kws[i], preferred_element_type=F32) for i in nr]
        dq_ref[...] = jnp.concatenate(dq1, axis=0) + jnp.dot(ds2b, kcb, preferred_element_type=F32)
        dvc_ref[...] += lax.dot_general(p2b, dob, tn, preferred_element_type=F32)
        dkc_ref[...] += lax.dot_general(ds2b, qs, tn, preferred_element_type=F32)
        for i in nr:
            sl = pl.ds(starts[i], win)
            db_ref[0, offs[i]] += ds1[_row_block(i)]
            dv_ref[sl, :] += lax.dot_general(p1b[_row_block(i)], dob[_row_block(i)], tn, preferred_element_type=F32)
            dk_ref[sl, :] += lax.dot_general(ds1b[_row_block(i)], qs[_row_block(i)], tn, preferred_element_type=F32)

    row_spec = pl.BlockSpec((NA_ROWS_PER_STEP * GRID_W, dh), lambda h, r: (r, h))
    seq_spec = pl.BlockSpec((seq, dh), lambda h, r: (0, h))
    ctx_spec = pl.BlockSpec((nctx, dh), lambda h, r: (0, h))
    b_spec = pl.BlockSpec((1, NA_KH, GRID_W, win), lambda h, r: (h, 0, 0, 0))
    return pl.pallas_call(
        body, name=name, grid=(NA_HEADS, rows // NA_ROWS_PER_STEP),
        out_shape=(jax.ShapeDtypeStruct((seq, width), F32), jax.ShapeDtypeStruct((seq, width), F32),
                   jax.ShapeDtypeStruct((seq, width), F32), jax.ShapeDtypeStruct((nctx, width), F32),
                   jax.ShapeDtypeStruct((nctx, width), F32), jax.ShapeDtypeStruct(biasw.shape, F32)),
        in_specs=_na_specs(seq, nctx) + [row_spec],
        out_specs=(row_spec, seq_spec, seq_spec, ctx_spec, ctx_spec, b_spec),
        compiler_params=_cparams(),
    )(q, k, v, kc, vc, biasw, do)


def _na_attention(e, q, k, v, kc, vc, biasw):
    @jax.custom_vjp
    def attn(q, k, v, kc, vc, biasw):
        return _na_forward(f"na_fwd{e}", q.astype(BF16), k.astype(BF16), v.astype(BF16), kc.astype(BF16), vc.astype(BF16), biasw)

    def fwd(q, k, v, kc, vc, biasw):
        res = (q.astype(BF16), k.astype(BF16), v.astype(BF16), kc.astype(BF16), vc.astype(BF16), biasw)
        return _na_forward(f"na_fwd{e}", *res), res

    def bwd(res, do):
        return _na_backward(f"na_bwd{e}", *res, do)

    attn.defvjp(fwd, bwd)
    return attn(q, k, v, kc, vc, biasw)


def _bias_windows(rpb):
    col = jnp.arange(GRID_W)
    dc = jnp.clip(col[None, :] - col[:, None], -(NA_KW - 1), NA_KW - 1) + NA_KW - 1
    onehot = (dc[None] == jnp.arange(2 * NA_KW - 1)[:, None, None]).astype(F32)
    tq = jnp.einsum("hrd,dqk->hrqk", rpb, onehot, precision=HIGHEST)
    col_start = jnp.clip(col - NA_KW // 2, 0, GRID_W - NA_KW)
    in_win = (col[None, :] >= col_start[:, None]) & (col[None, :] < col_start[:, None] + NA_KW)
    wins = jnp.stack([tq[:, off:off + NA_KH] for off in range(NA_KH)], axis=1)
    wins = jnp.where(in_win[None, None, None], wins, NEG)
    return wins.transpose(0, 1, 3, 2, 4).reshape(rpb.shape[0], NA_KH, GRID_W, NA_KH * GRID_W)


def _chunk_cumsum(u, reverse, name):
    t, nh = u.shape
    ln = SSD_CHUNK

    def body(u_ref, o_ref):
        row = lax.broadcasted_iota(jnp.int32, (ln, ln), 0)
        col = lax.broadcasted_iota(jnp.int32, (ln, ln), 1)
        uu = u_ref[...]
        down = jnp.dot((col <= row).astype(F32), uu, precision=HIGHEST, preferred_element_type=F32)
        up = jnp.dot((col >= row).astype(F32), uu, precision=HIGHEST, preferred_element_type=F32)
        first = lax.broadcasted_iota(jnp.int32, (ln, nh), 1) < nh // 2
        o_ref[...] = jnp.where(first, up, down) if reverse else jnp.where(first, down, up)

    spec = pl.BlockSpec((ln, nh), lambda i: (i, 0))
    return pl.pallas_call(body, name=name, grid=(t // ln,), out_shape=jax.ShapeDtypeStruct(u.shape, F32),
                          in_specs=[spec], out_specs=spec, compiler_params=_cparams())(u)


def _ssd_cumsum(e, u):
    @jax.custom_vjp
    def cs(u):
        return _chunk_cumsum(u, False, f"ssd_cumsum{e}")

    cs.defvjp(lambda u: (_chunk_cumsum(u, False, f"ssd_cumsum{e}"), None),
              lambda _, g: (_chunk_cumsum(g, True, f"ssd_cumsum_bwd{e}"),))
    return cs(u)


def _ssd_step(h, x, bm, cm, dt, dt_t, cs, cs_t, *, direction, hpg, pdim):
    ln = x.shape[0]
    hp = hpg * pdim
    nt = (((1,), (1,)), ((), ()))
    row = lax.broadcasted_iota(jnp.int32, (ln, ln), 0)
    colm = lax.broadcasted_iota(jnp.int32, (ln, ln), 1)
    valid = (colm - row) * (1 - 2 * direction) <= 0
    last = (ln - 1) * (1 - direction)
    tot = jnp.sum(jnp.where(lax.broadcasted_iota(jnp.int32, (ln, hpg), 0) == last, cs, 0.0), axis=0, keepdims=True)
    cb16, bb16 = cm.astype(BF16), bm.astype(BF16)
    cbm = lax.dot_general(cb16, bb16, nt, preferred_element_type=F32)
    lane_head = lax.broadcasted_iota(jnp.int32, (1, hp), 1) // pdim
    sub_head = lax.broadcasted_iota(jnp.int32, (hp, 1), 0) // pdim
    y = jnp.zeros((ln, hp), F32)
    es = jnp.zeros((ln, hp), F32)
    we = jnp.zeros((ln, hp), F32)
    dend = jnp.zeros((hp, 1), F32)
    for r in range(hpg):
        cc = cs[:, r:r + 1]
        cr = cs_t[r:r + 1, :]
        decay = jnp.exp(jnp.where(valid, cc - cr, NEG))
        wm = (cbm * decay * dt_t[r:r + 1, :]).astype(BF16)
        mask = lane_head == r
        y = y + jnp.dot(wm, jnp.where(mask, x, 0.0).astype(BF16), preferred_element_type=F32)
        es = es + jnp.where(mask, jnp.exp(cc), 0.0)
        we = we + jnp.where(mask, jnp.exp(tot[:, r:r + 1] - cc) * dt[:, r:r + 1], 0.0)
        dend = dend + jnp.where(sub_head == r, jnp.exp(tot[:, r:r + 1]), 0.0)
    y = y + es * lax.dot_general(cb16, h.astype(BF16), nt, preferred_element_type=F32)
    h_new = h * dend + lax.dot_general((x * we).astype(BF16), bb16, (((0,), (0,)), ((), ())), preferred_element_type=F32)
    return y, h_new


def _ssd_chunk_of(d, s, ncc, nc):
    return jnp.where(d == 0, s, jnp.where(s < ncc, ncc - 1 - s, nc - 1 - s + ncc))


def _ssd_specs(cfg, step_of):
    hp, n, ln, hpg = cfg.hpg * SSD_HEAD_DIM, SSD_STATE, SSD_CHUNK, cfg.hpg
    ncc, nc = cfg.nctx // ln, cfg.t // ln

    def ch(d, s):
        return _ssd_chunk_of(d, step_of(s), ncc, nc)

    return dict(
        x=pl.BlockSpec((ln, hp), lambda d, g, s: (ch(d, s), g)),
        bc=pl.BlockSpec((ln, n), lambda d, g, s: (ch(d, s), g)),
        dt=pl.BlockSpec((1, 1, ln, hpg), lambda d, g, s: (d, g, ch(d, s), 0)),
        dt_t=pl.BlockSpec((1, 1, 8, ln), lambda d, g, s: (d, g, 0, ch(d, s))),
        y=pl.BlockSpec((1, ln, hp), lambda d, g, s: (d, ch(d, s), g)),
        bc2=pl.BlockSpec((1, ln, n), lambda d, g, s: (d, ch(d, s), g)),
        h=pl.BlockSpec((1, 1, 1, hp, n), lambda d, g, s: (d, g, step_of(s), 0, 0)),
    )


def _ssd_forward(cfg, name, xs, bm, cm, dt, dt_t, dta, dta_t):
    hp, n, hpg = cfg.hpg * SSD_HEAD_DIM, SSD_STATE, cfg.hpg
    nc = cfg.t // SSD_CHUNK
    sp = _ssd_specs(cfg, lambda s: s)

    def body(x_ref, b_ref, c_ref, dt_ref, dtt_ref, dta_ref, dtat_ref, y_ref, hs_ref, h_ref):
        d, s = pl.program_id(0), pl.program_id(2)

        @pl.when(s == 0)
        def _():
            h_ref[...] = jnp.zeros_like(h_ref)

        h = h_ref[...]
        hs_ref[0, 0, 0] = h
        y, h_new = _ssd_step(h, x_ref[...], b_ref[...], c_ref[...], dt_ref[0, 0], dtt_ref[0, 0], dta_ref[0, 0],
                             dtat_ref[0, 0], direction=d, hpg=hpg, pdim=SSD_HEAD_DIM)
        y_ref[0] = y
        h_ref[...] = h_new

    return pl.pallas_call(
        body, name=name, grid=(2, SSD_GROUPS, nc),
        out_shape=(jax.ShapeDtypeStruct((2, cfg.t, cfg.ssd_width), F32),
                   jax.ShapeDtypeStruct((2, SSD_GROUPS, nc, hp, n), F32)),
        in_specs=[sp["x"], sp["bc"], sp["bc"], sp["dt"], sp["dt_t"], sp["dt"], sp["dt_t"]],
        out_specs=(sp["y"], sp["h"]), scratch_shapes=[pltpu.VMEM((hp, n), F32)], compiler_params=_cparams(),
    )(xs, bm, cm, dt, dt_t, dta, dta_t)


def _ssd_backward(cfg, name, xs, bm, cm, dt, dt_t, dta, dta_t, hsave, dy):
    hp, n, hpg = cfg.hpg * SSD_HEAD_DIM, SSD_STATE, cfg.hpg
    nc = cfg.t // SSD_CHUNK
    sp = _ssd_specs(cfg, lambda s: nc - 1 - s)

    def body(x_ref, b_ref, c_ref, dt_ref, dtt_ref, dta_ref, dtat_ref, hs_ref, dy_ref,
             dx_ref, db_ref, dc_ref, ddt_ref, ddtt_ref, ddta_ref, ddtat_ref, dh_ref):
        d, s = pl.program_id(0), pl.program_id(2)

        @pl.when(s == 0)
        def _():
            dh_ref[...] = jnp.zeros_like(dh_ref)

        step = functools.partial(_ssd_step, direction=d, hpg=hpg, pdim=SSD_HEAD_DIM)
        _, vjp = jax.vjp(step, hs_ref[0, 0, 0], x_ref[...], b_ref[...], c_ref[...], dt_ref[0, 0], dtt_ref[0, 0],
                         dta_ref[0, 0], dtat_ref[0, 0])
        dh, dx, db, dc, ddt, ddtt, ddta, ddtat = vjp((dy_ref[0], dh_ref[...]))
        dh_ref[...] = dh
        dx_ref[0] = dx
        db_ref[0] = db
        dc_ref[0] = dc
        ddt_ref[0, 0] = ddt
        ddtt_ref[0, 0] = ddtt
        ddta_ref[0, 0] = ddta
        ddtat_ref[0, 0] = ddtat

    gn = SSD_GROUPS * n
    return pl.pallas_call(
        body, name=name, grid=(2, SSD_GROUPS, nc),
        out_shape=(jax.ShapeDtypeStruct((2, cfg.t, cfg.ssd_width), F32), jax.ShapeDtypeStruct((2, cfg.t, gn), F32),
                   jax.ShapeDtypeStruct((2, cfg.t, gn), F32), jax.ShapeDtypeStruct(dt.shape, F32),
                   jax.ShapeDtypeStruct(dt_t.shape, F32), jax.ShapeDtypeStruct(dt.shape, F32),
                   jax.ShapeDtypeStruct(dt_t.shape, F32)),
        in_specs=[sp["x"], sp["bc"], sp["bc"], sp["dt"], sp["dt_t"], sp["dt"], sp["dt_t"], sp["h"], sp["y"]],
        out_specs=(sp["y"], sp["bc2"], sp["bc2"], sp["dt"], sp["dt_t"], sp["dt"], sp["dt_t"]),
        scratch_shapes=[pltpu.VMEM((hp, n), F32)], compiler_params=_cparams(),
    )(xs, bm, cm, dt, dt_t, dta, dta_t, hsave, dy)


def _ssd_scan(cfg, e, xs, bm, cm, dt, dt_t, dta, dta_t):
    @jax.custom_vjp
    def scan(xs, bm, cm, dt, dt_t, dta, dta_t):
        return _ssd_forward(cfg, f"ssd_fwd{e}", xs, bm, cm, dt, dt_t, dta, dta_t)[0]

    def fwd(xs, bm, cm, dt, dt_t, dta, dta_t):
        y, hsave = _ssd_forward(cfg, f"ssd_fwd{e}", xs, bm, cm, dt, dt_t, dta, dta_t)
        return y, (xs, bm, cm, dt, dt_t, dta, dta_t, hsave)

    def bwd(res, dy):
        dx, db, dc, ddt, ddtt, ddta, ddtat = _ssd_backward(cfg, f"ssd_bwd{e}", *res, dy)
        return dx[0] + dx[1], db[0] + db[1], dc[0] + dc[1], ddt, ddtt, ddta, ddtat

    scan.defvjp(fwd, bwd)
    return scan(xs, bm, cm, dt, dt_t, dta, dta_t)


def _adam_math(w, g, m, v):
    m2 = ADAM_B1 * m + (1.0 - ADAM_B1) * g
    v2 = ADAM_B2 * v + (1.0 - ADAM_B2) * (g * g)
    m_hat = m2 / (1.0 - ADAM_B1 ** ADAM_STEP)
    v_hat = v2 / (1.0 - ADAM_B2 ** ADAM_STEP)
    delta = -ADAM_LR * (m_hat / (jnp.sqrt(v_hat) + ADAM_EPS) + ADAM_WD * w)
    return delta, m2, v2


def _adam_small(w, g, m, v, name):
    def body(w_ref, g_ref, m_ref, v_ref, d_ref, m2_ref, v2_ref):
        d_ref[...], m2_ref[...], v2_ref[...] = _adam_math(w_ref[...], g_ref[...], m_ref[...], v_ref[...])

    shp = jax.ShapeDtypeStruct(w.shape, F32)
    return pl.pallas_call(body, name=name, out_shape=(shp, shp, shp), in_specs=[VMEM] * 4, out_specs=(VMEM, VMEM, VMEM),
                          compiler_params=_cparams())(w, g, m, v)


def _adam_tiled(w, g, m, v, name):
    nl, r, c = w.shape
    tr = _pick(r, (256, 128, 64, 32, 16, 8))
    spec = pl.BlockSpec((1, tr, c), lambda l, t: (l, t, 0))

    def body(w_ref, g_ref, m_ref, v_ref, d_ref, m2_ref, v2_ref):
        d_ref[...], m2_ref[...], v2_ref[...] = _adam_math(w_ref[...], g_ref[...], m_ref[...], v_ref[...])

    shp = jax.ShapeDtypeStruct(w.shape, F32)
    return pl.pallas_call(body, name=name, grid=(nl, r // tr), out_shape=(shp, shp, shp), in_specs=[spec] * 4,
                          out_specs=(spec, spec, spec), compiler_params=_cparams())(w, g, m, v)


def _adam_sharded(w, m, v, part_a, part_b, *, transposed, name):
    nl, r, c = w.shape
    tr = _pick(r, (256, 128)) if transposed else _pick(r, (256, 128, 64, 32, 16))
    w_spec = pl.BlockSpec((1, tr, c), lambda l, t: (l, t, 0))
    if transposed:
        pa_spec = pl.BlockSpec((1, 1, c, tr), lambda l, t: (0, l, 0, t))
        pb_spec = pl.BlockSpec((3, 1, c, tr), lambda l, t: (0, l, 0, t))
    else:
        pa_spec = pl.BlockSpec((1, 1, tr, c), lambda l, t: (0, l, t, 0))
        pb_spec = pl.BlockSpec((3, 1, tr, c), lambda l, t: (0, l, t, 0))

    def body(w_ref, m_ref, v_ref, pa_ref, pb_ref, g_ref, d_ref, m2_ref, v2_ref):
        g = pa_ref[0, 0] + pb_ref[0, 0].astype(F32) + pb_ref[1, 0].astype(F32) + pb_ref[2, 0].astype(F32)
        if transposed:
            g = g.T
        g_ref[0] = g
        d_ref[0], m2_ref[0], v2_ref[0] = _adam_math(w_ref[0], g, m_ref[0], v_ref[0])

    shp = jax.ShapeDtypeStruct(w.shape, F32)
    return pl.pallas_call(
        body, name=name, grid=(nl, r // tr), out_shape=(shp, shp, shp, shp),
        in_specs=[w_spec, w_spec, w_spec, pa_spec, pb_spec], out_specs=(w_spec, w_spec, w_spec, w_spec),
        compiler_params=_cparams(),
    )(w, m, v, part_a, part_b)


def _pick_blocks(g, recv, axis, length, g_blocks, r_blocks, out_dtype, name):
    nl = g.shape[0]
    n = g_blocks.shape[0]
    blk_shape = list(g.shape)
    blk_shape[axis] = length
    if axis == 1:
        cols = g.shape[2]
        tc = _pick(cols, (512, 256, 128))
        g_spec = pl.BlockSpec((1, length, tc), lambda i, l, t, gb, rb: (l, gb[i], t))
        r_spec = pl.BlockSpec((1, 1, length, tc), lambda i, l, t, gb, rb: (rb[i], l, 0, t))
        o_spec = pl.BlockSpec((1, 1, length, tc), lambda i, l, t, gb, rb: (i, l, 0, t))
        grid = (n, nl, cols // tc)
    else:
        rows = g.shape[1]
        tr = _pick(rows, (512, 256, 128, 64, 32, 16))
        g_spec = pl.BlockSpec((1, tr, length), lambda i, l, t, gb, rb: (l, t, gb[i]))
        r_spec = pl.BlockSpec((1, 1, tr, length), lambda i, l, t, gb, rb: (rb[i], l, t, 0))
        o_spec = pl.BlockSpec((1, 1, tr, length), lambda i, l, t, gb, rb: (i, l, t, 0))
        grid = (n, nl, rows // tr)

    if recv is None:
        def body(gb_ref, rb_ref, g_ref, o_ref):
            o_ref[0] = g_ref[...].astype(out_dtype)
        in_specs, args = [g_spec], (g,)
    else:
        def body(gb_ref, rb_ref, g_ref, r_ref, o_ref):
            o_ref[0] = (g_ref[...] + r_ref[0].astype(F32)).astype(out_dtype)
        in_specs, args = [g_spec, r_spec], (g, recv)

    return pl.pallas_call(
        body, name=name, out_shape=jax.ShapeDtypeStruct((n, *blk_shape), out_dtype),
        grid_spec=pltpu.PrefetchScalarGridSpec(num_scalar_prefetch=2, grid=grid, in_specs=in_specs, out_specs=o_spec),
        compiler_params=_cparams(),
    )(g_blocks, r_blocks, *args)


def _flatten(arrs):
    flat = jnp.concatenate([a.reshape(-1).astype(F32) for a in arrs])
    n = flat.shape[0]
    n_pad = -(-n // 1024) * 1024
    return jnp.pad(flat, (0, n_pad - n)).reshape(n_pad // 128, 128)


def _unflatten(buf, shapes):
    flat = buf.reshape(-1)
    out, o = [], 0
    for s in shapes:
        n = math.prod(s)
        out.append(flat[o:o + n].reshape(s))
        o += n
    return out


def _rowwise(name, fn, rows, seg, shared, out_cols, out_dtypes, tile, nct):
    t = rows[0].shape[0]
    nr, ns, nsh, no = len(rows), len(seg), len(shared), len(out_cols)
    n_in = nr + ns + nsh

    def row_spec(c):
        return pl.BlockSpec((tile, c), lambda i: (i, 0))

    def seg_spec(c):
        return pl.BlockSpec((1, 1, c), lambda i: (jnp.where(i < nct, 0, 1), 0, 0))

    def whole_spec(shape):
        return pl.BlockSpec(shape, lambda i: (0, 0))

    in_specs = ([row_spec(r.shape[1]) for r in rows] + [seg_spec(s.shape[1]) for s in seg]
                + [whole_spec(s.shape) for s in shared])
    out_shapes = tuple(jax.ShapeDtypeStruct((t, c), dt) for c, dt in zip(out_cols, out_dtypes))
    out_specs = tuple(row_spec(c) for c in out_cols)

    def load(refs):
        return [r[0] if nr <= j < nr + ns else r[...] for j, r in enumerate(refs[:n_in])]

    def lift(args):
        return [a[:, None, :] if nr <= j < nr + ns else a for j, a in enumerate(args)]

    def forward(*args):
        def body(*refs):
            outs = fn(*load(refs))
            for o_ref, o in zip(refs[n_in:], outs):
                o_ref[...] = o

        return pl.pallas_call(body, name=name + "_fwd", grid=(t // tile,), out_shape=out_shapes, in_specs=in_specs,
                              out_specs=out_specs, compiler_params=_cparams())(*lift(args))

    def backward(args, cts):
        def body(*refs):
            i = pl.program_id(0)
            ct = tuple(r[...] for r in refs[n_in:n_in + no])
            d_refs = refs[n_in + no:]
            _, vjp = jax.vjp(fn, *load(refs))
            grads = vjp(ct)
            for ref, g in zip(d_refs[:nr], grads[:nr]):
                ref[...] = g
            first_seg = jnp.logical_or(i == 0, i == nct)
            for j in range(nr, n_in):
                ref, g = d_refs[j], grads[j]
                first = first_seg if j < nr + ns else i == 0
                g = g[None] if j < nr + ns else g

                @pl.when(first)
                def _(ref=ref, g=g):
                    ref[...] = g

                @pl.when(jnp.logical_not(first))
                def _(ref=ref, g=g):
                    ref[...] += g

        largs = lift(args)
        d_shapes = tuple(jax.ShapeDtypeStruct(a.shape, F32) for a in largs)
        ct_specs = [row_spec(c) for c in out_cols]
        outs = pl.pallas_call(body, name=name + "_bwd", grid=(t // tile,), out_shape=d_shapes,
                              in_specs=in_specs + ct_specs, out_specs=tuple(in_specs), compiler_params=_cparams())(*largs, *cts)
        return [o[:, 0, :] if nr <= j < nr + ns else o for j, o in enumerate(outs)]

    @jax.custom_vjp
    def prim(*args):
        return tuple(forward(*args))

    prim.defvjp(lambda *args: (tuple(forward(*args)), args), lambda args, cts: tuple(backward(args, cts)))
    return prim(*rows, *seg, *shared)


def _silu(x):
    return x * (1.0 / (1.0 + jnp.exp(-x)))


def _softplus(x):
    return jnp.maximum(x, 0.0) + jnp.log(1.0 + jnp.exp(-jnp.abs(x)))


def _rms_rows(x):
    return x * lax.rsqrt(jnp.mean(x * x, axis=-1, keepdims=True) + EPS)


def _pre0_tile(x, scale, shift, g):
    return ((_rms_rows(x) * g * (1 + scale) + shift).astype(BF16),)


def _pre_tile(x, y_prev, gate, scale, shift, g):
    xn = x + gate * y_prev
    return xn, (_rms_rows(xn) * g * (1 + scale) + shift).astype(BF16)


def _loss_tile(x, y_prev, target, gate, weight):
    err = (x + gate * y_prev - target) * weight
    return (0.5 * jnp.mean(err * err, axis=-1, keepdims=True),)


def _mid_even_tile(pa, pd, qg, kg, dt_bias, a, *, naw, sw, nh):
    def heads_norm(u, g):
        return jnp.concatenate([_rms_rows(u[:, j:j + NA_HEAD_DIM]) * g for j in range(0, naw, NA_HEAD_DIM)], axis=1)

    q, gate, z = pa[:, :naw], pa[:, naw:2 * naw], pa[:, 2 * naw:2 * naw + sw]
    k, v = pa[:, 2 * naw + sw:3 * naw + sw], pa[:, 3 * naw + sw:]
    dt = _softplus(pd[:, :nh] + dt_bias)
    return heads_norm(q, qg), heads_norm(k, kg), v, _silu(gate), _silu(z), dt, dt * a


def _post_even_tile(ya, sg, y0, y1, xs, sz, dskip, g, *, sw):
    yz = (y0 + y1 + dskip * xs) * sz
    gw = sw // SSD_GROUPS
    yb = jnp.concatenate([_rms_rows(yz[:, j:j + gw]) for j in range(0, sw, gw)], axis=1) * g
    return (jnp.concatenate([ya * sg, yb], axis=1).astype(BF16),)


def _in_proj_split(a, w, gslot, widths, *, w_is_nk, tm, tn, name):
    t, kdim = a.shape
    starts = [sum(widths[:i]) for i in range(len(widths))]
    assert all(s % tn == 0 and wd % tn == 0 for s, wd in zip(starts, widths)), (starts, widths, tn)

    def piece(i, blocks):
        off = (starts[i] // blocks, 0) if w_is_nk else (0, starts[i] // blocks)
        ext = (widths[i], kdim) if w_is_nk else (kdim, widths[i])
        return off, ext

    def forward(a, w):
        outs = []
        for i in range(len(widths)):
            off, ext = piece(i, tn)
            outs.append(_matmul(a, w, tb=w_is_nk, tm=tm, tn=tn, tk=kdim, b_off=off, b_extent=ext, name=f"{name}_fwd{i}"))
        return tuple(outs)

    @jax.custom_vjp
    def proj(a, w, gslot):
        return forward(a, w)

    def bwd(res, gs):
        a, w = res
        tt = _pick(t, (1408, 768, 512, 256, 128))
        tkk = _pick(kdim, (2048, 1024, 512, 256, 128))
        tok = _pick(t, (768, 512, 256, 128))
        da, dws = None, []
        for i, g in enumerate(gs):
            off, ext = piece(i, tn)
            part = _matmul(g, w, tb=not w_is_nk, tm=tt, tn=tkk, tk=tn, b_off=off, b_extent=ext, name=f"{name}_bwd_a{i}")
            da = part if da is None else da + part
            if w_is_nk:
                dws.append(_matmul(g, a, ta=True, tm=tn, tn=tkk, tk=tok, name=f"{name}_bwd_w{i}"))
            else:
                dws.append(_matmul(a, g, ta=True, tm=_pick(kdim, (512, 256, 128)), tn=_pick(widths[i], (2048, 1024, 512, 256, 128)),
                                   tk=tok, name=f"{name}_bwd_w{i}"))
        dw = jnp.concatenate(dws, axis=0 if w_is_nk else 1)
        if dw.shape != w.shape:
            dw = jnp.pad(dw, [(0, w.shape[0] - dw.shape[0]), (0, w.shape[1] - dw.shape[1])])
        return da.astype(a.dtype), jnp.zeros_like(w), dw

    proj.defvjp(lambda a, w, gslot: (forward(a, w), (a, w)), bwd)
    return proj(a, w, gslot)


def _rms(x, g):
    return x * lax.rsqrt(jnp.mean(x * x, axis=-1, keepdims=True) + EPS) * g


def _dw_conv(x, w, b=None):
    k = w.shape[0]
    ln = x.shape[0]
    xp = jnp.pad(x, ((k // 2, k // 2), (0, 0)))
    y = sum(w[i][None, :] * xp[i:i + ln] for i in range(k))
    return y if b is None else y + b


def _conv_two(x, nctx, w, b=None):
    return jnp.concatenate([_dw_conv(x[:nctx], w, b), _dw_conv(x[nctx:], w, b)], axis=0)


def _mod_rows(nctx, seq, ctx_vec, lat_vec):
    return jnp.concatenate([jnp.broadcast_to(ctx_vec, (nctx, ctx_vec.shape[-1])),
                            jnp.broadcast_to(lat_vec, (seq, lat_vec.shape[-1]))], axis=0)


def _even_mixer(cfg, h, e, wd, update_ctx):
    d, nctx, seq, t = cfg.d, cfg.nctx, cfg.s, cfg.t
    naw, sw = cfg.na_width, cfg.ssd_width
    gn = SSD_GROUPS * SSD_STATE
    nh = 2 * SSD_GROUPS * cfg.hpg
    wa, wx = 4 * naw + sw, sw + 2 * gn
    pa, px, pd = _in_proj_split(h, wd["win_t"][e], wd["g_win_t"][e], (wa, wx, cfg.n_pad - wa - wx), w_is_nk=True,
                                tm=cfg.tiles_in[0], tn=cfg.tiles_in[1], name=f"in_even{e}")
    tile = cfg.tile_tok
    nct = nctx // tile
    a_neg = -jnp.exp(wd["ssd_a_log"][e]).reshape(1, nh)
    qn, kn, vv, sg, sz, dt, dta = _rowwise(
        f"mid_even{e}", functools.partial(_mid_even_tile, naw=naw, sw=sw, nh=nh), [pa, pd], [],
        [wd["q_norm_g"][e][None, :], wd["k_norm_g"][e][None, :], wd["ssd_dt_bias"][e].reshape(1, nh), a_neg],
        [naw, naw, naw, naw, sw, nh, nh], [F32] * 7, tile, nct)
    biasw = _bias_windows(wd["na_rpb"][e])
    ya = _na_attention(e, qn[nctx:], kn[nctx:], vv[nctx:], kn[:nctx], vv[:nctx], biasw)
    if update_ctx:
        qc = qn[:nctx].reshape(nctx, NA_HEADS, NA_HEAD_DIM)
        kc = kn[:nctx].reshape(nctx, NA_HEADS, NA_HEAD_DIM)
        vc = vv[:nctx].reshape(nctx, NA_HEADS, NA_HEAD_DIM)
        sc = jnp.einsum("qhd,khd->hqk", qc, kc).astype(F32) * NA_HEAD_DIM ** -0.5
        yac = jnp.einsum("hqk,khd->qhd", jax.nn.softmax(sc, axis=-1), vc).reshape(nctx, naw)
    else:
        yac = jnp.zeros((nctx, naw), F32)
    ya = jnp.concatenate([yac, ya], axis=0)

    xbc = jax.nn.silu(_conv_two(px, nctx, wd["ssd_conv_w"][e], wd["ssd_conv_b"][e]))
    xs, bm, cm = xbc[:, :sw], xbc[:, sw:sw + gn], xbc[:, sw + gn:]

    def arrange(u):
        u4 = u.reshape(t, 2, SSD_GROUPS, cfg.hpg)
        return u4.transpose(1, 2, 0, 3), jnp.pad(u4.transpose(1, 2, 3, 0), ((0, 0), (0, 0), (0, 8 - cfg.hpg), (0, 0)))

    dt4, dt_t = arrange(dt)
    cs4, cs_t = arrange(_ssd_cumsum(e, dta))
    y2 = _ssd_scan(cfg, e, xs, bm, cm, dt4, dt_t, cs4, cs_t)
    dskip = jnp.repeat(wd["ssd_d"][e], SSD_HEAD_DIM)[None, :]
    (ycat,) = _rowwise(f"post_even{e}", functools.partial(_post_even_tile, sw=sw), [ya, sg, y2[0], y2[1], xs, sz], [],
                       [dskip, wd["ssd_norm_g"][e][None, :]], [naw + sw], [BF16], tile, nct)
    return _linear(ycat, wd["wout"][e], wd["g_wout"][e], w_is_nk=False, tiles=cfg.tiles_out_even, name=f"out_even{e}")


def _odd_mixer(cfg, h, o, wd):
    d, nctx = cfg.d, cfg.nctx
    p = _linear(h, wd["sc_win"][o], wd["g_sc_win"][o], w_is_nk=False, tiles=cfg.tiles_in_odd, name=f"in_odd{o}")
    bg, cg, hv, g = p[:, :d], p[:, d:2 * d], p[:, 2 * d:3 * d], p[:, 3 * d:]
    yy = bg * _conv_two(cg * hv, nctx, wd["sc_conv_w"][o])
    u = jax.nn.silu(g) * yy
    return _linear(u.astype(BF16), wd["sc_wout"][o], wd["g_sc_wout"][o], w_is_nk=False, tiles=cfg.tiles_out_odd, name=f"out_odd{o}")


def _local_loss(cfg, x, ctx, target, mods, mods_c, wd):
    d, nctx, seq = cfg.d, cfg.nctx, cfg.s
    tile = cfg.tile_res
    nct = nctx // tile
    xx = jnp.concatenate([ctx, x], axis=0)
    y_prev = gate_prev = None
    for i in range(DEPTH):
        update_ctx = any(j % 2 == 0 for j in range(i + 1, DEPTH))
        shift = jnp.stack([mods_c[i, :d], mods[i, :d]])
        scale = jnp.stack([mods_c[i, d:2 * d], mods[i, d:2 * d]])
        g = wd["norm_g"][i][None, :]
        if y_prev is None:
            (h,) = _rowwise(f"pre{i}", _pre0_tile, [xx], [scale, shift], [g], [d], [BF16], tile, nct)
        else:
            xx, h = _rowwise(f"pre{i}", _pre_tile, [xx, y_prev], [gate_prev, scale, shift], [g], [d, d], [F32, BF16], tile, nct)
        y_prev = _even_mixer(cfg, h, i // 2, wd, update_ctx) if i % 2 == 0 else _odd_mixer(cfg, h, i // 2, wd)
        gate_c = mods_c[i, 2 * d:] if update_ctx else jnp.zeros((d,), F32)
        gate_prev = jnp.stack([gate_c, mods[i, 2 * d:]])
    target_rows = jnp.concatenate([jnp.zeros((nctx, d), F32), target], axis=0)
    weight = jnp.stack([jnp.zeros((d,), F32), jnp.ones((d,), F32)])
    (row_loss,) = _rowwise("loss", _loss_tile, [xx, y_prev, target_rows], [gate_prev, weight], [], [1], [F32], tile, nct)
    return jnp.sum(row_loss)


SMALL_REPLICATED = ["norm_g", "ssd_conv_b", "ssd_a_log", "ssd_dt_bias", "ssd_d", "ssd_norm_g", "q_norm_g", "k_norm_g", "na_rpb"]
WEIGHT_ORDER = ["c_ctx", "ada_w", "ada_b", "norm_g", "na_ssd_w_in", "ssd_conv_w", "ssd_conv_b", "ssd_a_log", "ssd_dt_bias",
                "ssd_d", "ssd_norm_g", "q_norm_g", "k_norm_g", "na_rpb", "na_ssd_w_out", "sc_w_in", "sc_conv_w", "sc_w_out"]


def kernel(x, c, ctx, c_ctx, ada_w, ada_b, norm_g, na_ssd_w_in, ssd_conv_w, ssd_conv_b, ssd_a_log, ssd_dt_bias, ssd_d, ssd_norm_g, q_norm_g, k_norm_g, na_rpb, na_ssd_w_out, sc_w_in, sc_conv_w, sc_w_out, loss_target, m_c_ctx, m_ada_w, m_ada_b, m_norm_g, m_na_ssd_w_in, m_ssd_conv_w, m_ssd_conv_b, m_ssd_a_log, m_ssd_dt_bias, m_ssd_d, m_ssd_norm_g, m_q_norm_g, m_k_norm_g, m_na_rpb, m_na_ssd_w_out, m_sc_w_in, m_sc_conv_w, m_sc_w_out, v_c_ctx, v_ada_w, v_ada_b, v_norm_g, v_na_ssd_w_in, v_ssd_conv_w, v_ssd_conv_b, v_ssd_a_log, v_ssd_dt_bias, v_ssd_d, v_ssd_norm_g, v_q_norm_g, v_k_norm_g, v_na_rpb, v_na_ssd_w_out, v_sc_w_in, v_sc_conv_w, v_sc_w_out):
    given = dict(locals())
    weights = {n: given[n] for n in WEIGHT_ORDER}
    mom_m = {n: given["m_" + n] for n in WEIGHT_ORDER}
    mom_v = {n: given["v_" + n] for n in WEIGHT_ORDER}

    d = x.shape[-1]
    seq, nctx = x.shape[1], ctx.shape[1]
    n_in_shard = na_ssd_w_in.shape[-1]
    n_in = n_in_shard * NDEV
    n_pad = -(-n_in // PAD_TO) * PAD_TO
    hpg = (d // SSD_HEAD_DIM) // SSD_GROUPS
    t = nctx + seq
    tm = _pick(t, (1408, 768, 512, 256, 128))
    cfg = SimpleNamespace(
        d=d, s=seq, nctx=nctx, t=t, hpg=hpg, na_width=NA_HEADS * NA_HEAD_DIM, ssd_width=d, n_in=n_in, n_pad=n_pad,
        tiles_in=(tm, _pick(n_pad, (512, 256, 128)), d),
        tiles_out_even=(tm, _pick(d, (1024, 512, 256, 128)), _pick(NA_HEADS * NA_HEAD_DIM + d, (1024, 512, 256, 128))),
        tiles_in_odd=(tm, _pick(4 * d, (1024, 512, 256, 128)), d),
        tiles_out_odd=(tm, _pick(d, (1024, 512, 256, 128)), d),
        tile_tok=128, tile_res=256,
    )
    me = _my_index()
    xl, cl, ctxl, tgt = x[0], c, ctx[0], loss_target[0]

    ncol = ada_w.shape[-1]
    c_rows = -(-d // 128)
    c_all = _small_allgather(jnp.pad(cl.reshape(-1), (0, c_rows * 128 - d)).reshape(c_rows, 128), "gather_c")[0]
    c_all = c_all.reshape(NDEV, -1)[:, :d]
    cond = jnp.concatenate([c_all, c_ctx[None, :], jnp.zeros((16 - NDEV - 1, d), F32)], axis=0)
    s16 = jax.nn.silu(cond)
    ada_b_mine = lax.dynamic_slice_in_dim(ada_b, me * ncol, ncol, axis=1)
    mod_part = jnp.stack([
        _matmul(s16, ada_w[i], tm=16, tn=_pick(ncol, (768, 512, 256, 128)), tk=d, name=f"adaln{i}") + ada_b_mine[i][None, :]
        for i in range(DEPTH)])
    mp_rows = DEPTH * 16 * ncol // 128
    mod_all = _small_allgather(mod_part.reshape(mp_rows, 128), "gather_mod")[0]
    mod_all = mod_all.reshape(NDEV, DEPTH, 16, ncol).transpose(1, 2, 0, 3).reshape(DEPTH, 16, NDEV * ncol)
    mods = lax.dynamic_index_in_dim(mod_all, me, axis=1, keepdims=False)
    mods_c = mod_all[:, NDEV]

    packed = _pack_transposed(na_ssd_w_in, "pack_w_in")
    wout_b = _cast_bf16(na_ssd_w_out, "cast_w_out")
    scwin_b = _cast_bf16(sc_w_in, "cast_sc_w_in")
    scwout_b = _cast_bf16(sc_w_out, "cast_sc_w_out")
    packed_all, wout_all, scwin_all, scwout_all = _big_allgather(
        [packed, wout_b, scwin_b, scwout_b], [1, 1, 2, 1], "gather_weights")
    win_t = _unpack(packed_all, n_pad, "unpack_w_in")
    conv_shapes = [ssd_conv_w.shape, sc_conv_w.shape]
    conv_all = _small_allgather(_flatten([ssd_conv_w, sc_conv_w]), "gather_conv")[0]
    conv_parts = [_unflatten(conv_all[j], conv_shapes) for j in range(NDEV)]
    ssd_conv_full = jnp.concatenate([cp[0] for cp in conv_parts], axis=-1)
    sc_conv_full = jnp.concatenate([cp[1] for cp in conv_parts], axis=-1)

    small = {n: weights[n] for n in SMALL_REPLICATED}
    small["ssd_conv_w"] = ssd_conv_full
    small["sc_conv_w"] = sc_conv_full
    gslots = dict(g_win_t=jnp.zeros(win_t.shape, F32), g_wout=jnp.zeros(wout_all.shape, F32),
                  g_sc_win=jnp.zeros(scwin_all.shape, F32), g_sc_wout=jnp.zeros(scwout_all.shape, F32))
    frozen = dict(win_t=win_t, wout=wout_all, sc_win=scwin_all, sc_wout=scwout_all)

    def loss_fn(xl, mods, mods_c, small, gslots):
        return _local_loss(cfg, xl, ctxl, tgt, mods, mods_c, {**small, **gslots, **frozen})

    loss_local, (g_x, g_mods, g_mods_c, g_small, g_big) = jax.value_and_grad(loss_fn, argnums=(0, 1, 2, 3, 4))(
        xl, mods, mods_c, small, gslots)

    small_names = SMALL_REPLICATED + ["ssd_conv_w", "sc_conv_w"]
    small_shapes = [g_small[n].shape for n in small_names] + [g_mods_c.shape]
    flat_small = _flatten([g_small[n] for n in small_names] + [g_mods_c])
    _, small_sum = _small_allgather(flat_small, "gather_small_grads")
    summed = _unflatten(small_sum, small_shapes)
    g_rep = dict(zip(small_names, summed[:-1]))
    g_mods_c_tot = summed[-1]
    gm_rows = DEPTH * NDEV * ncol // 128
    gm_all = _small_allgather(g_mods.reshape(gm_rows, 128), "gather_mod_grads")[0].reshape(NDEV, DEPTH, NDEV * ncol)
    dm = jnp.concatenate([gm_all.transpose(1, 0, 2), g_mods_c_tot[:, None, :],
                          jnp.zeros((DEPTH, 16 - NDEV - 1, NDEV * ncol), F32)], axis=1)
    grad_ada_b = jnp.sum(dm, axis=1)
    dm_mine = lax.dynamic_slice_in_dim(dm, me * ncol, ncol, axis=2)
    grad_ada_w = jnp.stack([
        _matmul(s16, dm_mine[i], ta=True, tm=_pick(d, (512, 256, 128)), tn=_pick(ncol, (768, 512, 256, 128)), tk=16,
                name=f"adaln_gw{i}") for i in range(DEPTH)])
    ds_part = sum(_matmul(dm_mine[i], ada_w[i], tb=True, tm=16, tn=_pick(d, (2048, 1024, 512, 256, 128)),
                          tk=_pick(ncol, (768, 512, 256, 128)), name=f"adaln_gs{i}") for i in range(DEPTH))[NDEV]
    ds_ctx = _small_allgather(jnp.pad(ds_part, (0, c_rows * 128 - d)).reshape(c_rows, 128), "gather_c_ctx_grad")[1]
    ds_ctx = ds_ctx.reshape(-1)[:d]
    sig = jax.nn.sigmoid(c_ctx)
    grad_c_ctx = ds_ctx * (sig * (1 + c_ctx * (1 - sig)))

    big = [g_big["g_win_t"], g_big["g_wout"], g_big["g_sc_win"], g_big["g_sc_wout"]]
    axes = [1, 1, 2, 1]
    lens = [n_in_shard, na_ssd_w_out.shape[1], sc_w_in.shape[2], sc_w_out.shape[1]]
    px, py, pc = _pos()
    kself = 2 * px + py
    kothers = jnp.stack([2 * (1 - px) + py, 2 * px + (1 - py), 2 * (1 - px) + (1 - py)]).astype(jnp.int32)
    k4 = jnp.arange(4, dtype=jnp.int32)
    kme = kself.astype(jnp.int32).reshape(1)
    sends = [_pick_blocks(g, None, ax, ln, 2 * k4 + (1 - pc), k4, BF16, f"rs_cast{i}")
             for i, (g, ax, ln) in enumerate(zip(big, axes, lens))]
    recv_a = _rs_stage_a(sends, "reduce_scatter_d2d")
    part_a = [_pick_blocks(g, r, ax, ln, 2 * kme + pc, kme, F32, f"pair_sum_mine{i}")
              for i, (g, r, ax, ln) in enumerate(zip(big, recv_a, axes, lens))]
    part_s = [_pick_blocks(g, r, ax, ln, 2 * kothers + pc, kothers, BF16, f"pair_sum_send{i}")
              for i, (g, r, ax, ln) in enumerate(zip(big, recv_a, axes, lens))]
    part_b = _rs_stage_b(part_s, "reduce_scatter_ici")

    res = {}
    res["na_ssd_w_in"] = _adam_sharded(na_ssd_w_in, m_na_ssd_w_in, v_na_ssd_w_in, part_a[0], part_b[0], transposed=True, name="adam_w_in")
    res["na_ssd_w_out"] = _adam_sharded(na_ssd_w_out, m_na_ssd_w_out, v_na_ssd_w_out, part_a[1], part_b[1], transposed=False, name="adam_w_out")
    res["sc_w_in"] = _adam_sharded(sc_w_in, m_sc_w_in, v_sc_w_in, part_a[2], part_b[2], transposed=False, name="adam_sc_w_in")
    res["sc_w_out"] = _adam_sharded(sc_w_out, m_sc_w_out, v_sc_w_out, part_a[3], part_b[3], transposed=False, name="adam_sc_w_out")

    grads = dict(g_rep)
    grads["ssd_conv_w"] = lax.dynamic_slice_in_dim(g_rep["ssd_conv_w"], me * ssd_conv_w.shape[-1], ssd_conv_w.shape[-1], axis=2)
    grads["sc_conv_w"] = lax.dynamic_slice_in_dim(g_rep["sc_conv_w"], me * sc_conv_w.shape[-1], sc_conv_w.shape[-1], axis=2)
    grads["c_ctx"] = grad_c_ctx
    res["ada_w"] = (grad_ada_w, *_adam_tiled(ada_w, grad_ada_w, m_ada_w, v_ada_w, "adam_ada_w"))
    grads["ada_b"] = grad_ada_b
    rest = [n for n in WEIGHT_ORDER if n not in res]
    shapes = [weights[n].shape for n in rest]
    d_flat, m_flat, v_flat = _adam_small(_flatten([weights[n] for n in rest]), _flatten([grads[n] for n in rest]),
                                         _flatten([mom_m[n] for n in rest]), _flatten([mom_v[n] for n in rest]), "adam_small")
    for n, dd, mm, vv in zip(rest, _unflatten(d_flat, shapes), _unflatten(m_flat, shapes), _unflatten(v_flat, shapes)):
        res[n] = (grads[n], dd, mm, vv)

    loss = lax.psum(loss_local, ("x", "y", "c"))
    return (loss, g_x[None], *[res[n][0] for n in WEIGHT_ORDER], *[res[n][1] for n in WEIGHT_ORDER],
            *[res[n][2] for n in WEIGHT_ORDER], *[res[n][3] for n in WEIGHT_ORDER])
```

```python
import functools
import math
from types import SimpleNamespace

import jax
import jax.numpy as jnp
from jax import lax
from jax.experimental import pallas as pl
from jax.experimental.pallas import tpu as pltpu

F32 = jnp.float32
BF16 = jnp.bfloat16
U32 = jnp.uint32
HIGHEST = lax.Precision.HIGHEST
MESH = pl.DeviceIdType.MESH
ANY = pl.BlockSpec(memory_space=pl.ANY)
VMEM = pl.BlockSpec(memory_space=pltpu.VMEM)

NDEV = 8
DEPTH = 4
GRID_W = 64
EPS = 1e-6
NA_HEADS = 16
NA_HEAD_DIM = 128
NA_KH = 8
NA_KW = 16
SSD_HEAD_DIM = 64
SSD_GROUPS = 8
SSD_STATE = 128
SSD_CONV = 5
SSD_CHUNK = 128
SC_CONV = 3
ADAM_LR = 0.001
ADAM_B1 = 0.9
ADAM_B2 = 0.999
ADAM_EPS = 1e-08
ADAM_WD = 0.01
ADAM_STEP = 10
NEG = -1e30
VMEM_LIMIT = 56 * 1024 * 1024
PAD_TO = 512


def _pos():
    return lax.axis_index("x"), lax.axis_index("y"), lax.axis_index("c")


def _my_index():
    x, y, c = _pos()
    return 4 * x + 2 * y + c


def _pick(n, prefs):
    for p in prefs:
        if n % p == 0:
            return p
    return n


def _cparams(**kw):
    return pltpu.CompilerParams(vmem_limit_bytes=VMEM_LIMIT, **kw)


def _small_allgather(v, name):
    rows, lanes = v.shape

    def body(x_ref, out_ref, sum_ref, send_sems, recv_sems):
        x, y, c = _pos()
        me = 4 * x + 2 * y + c
        out_ref[me] = x_ref[...]
        copies = []
        for k in range(1, NDEV):
            peer = (1 - x if k & 4 else x, 1 - y if k & 2 else y, 1 - c if k & 1 else c)
            cp = pltpu.make_async_remote_copy(src_ref=x_ref, dst_ref=out_ref.at[me], send_sem=send_sems.at[k - 1],
                                              recv_sem=recv_sems.at[k - 1], device_id=peer, device_id_type=MESH)
            cp.start()
            copies.append(cp)
        for cp in copies:
            cp.wait()
        acc = out_ref[0]
        for j in range(1, NDEV):
            acc = acc + out_ref[j]
        sum_ref[...] = acc

    return pl.pallas_call(
        body, name=name,
        out_shape=(jax.ShapeDtypeStruct((NDEV, rows, lanes), v.dtype), jax.ShapeDtypeStruct((rows, lanes), v.dtype)),
        in_specs=[VMEM], out_specs=(VMEM, VMEM),
        scratch_shapes=[pltpu.SemaphoreType.DMA((NDEV - 1,)), pltpu.SemaphoreType.DMA((NDEV - 1,))],
        compiler_params=_cparams(),
    )(v)


def _window(ref, axis, idx, length):
    sl = [slice(None)] * len(ref.shape)
    sl[axis] = pl.ds(pl.multiple_of(idx * length, min(length & -length, 1024)), length)
    return ref.at[tuple(sl)]


def _big_allgather(shards, axes, name):
    n = len(shards)
    out_shapes = []
    for s, ax in zip(shards, axes):
        shp = list(s.shape)
        shp[ax] *= NDEV
        out_shapes.append(jax.ShapeDtypeStruct(tuple(shp), s.dtype))

    def body(*refs):
        xs, outs = refs[:n], refs[n:2 * n]
        send_sems, recv_sems, local_sems = refs[2 * n:]
        x, y, c = _pos()
        me, sib = (x, y, c), (x, y, 1 - c)
        chips = [(1 - x, y), (x, 1 - y), (1 - x, 1 - y)]

        def win(a, px, py, pc):
            return _window(outs[a], axes[a], 4 * px + 2 * py + pc, shards[a].shape[axes[a]])

        def copy(a, k, block, to, src=None):
            return pltpu.make_async_remote_copy(src_ref=win(a, *block) if src is None else src, dst_ref=win(a, *block),
                                                send_sem=send_sems.at[a * 7 + k], recv_sem=recv_sems.at[a * 7 + k],
                                                device_id=to, device_id_type=MESH)

        mine = [pltpu.make_async_copy(xs[a], win(a, *me), local_sems.at[a]) for a in range(n)]
        for cp in mine:
            cp.start()
        first = []
        for a in range(n):
            first.append(copy(a, 0, me, sib, src=xs[a]))
            first += [copy(a, 1 + j, me, (*chip, c), src=xs[a]) for j, chip in enumerate(chips)]
        for cp in first:
            cp.start()
        passed = []
        for j, chip in enumerate(chips):
            for a in range(n):
                copy(a, 1 + j, (*chip, c), me).wait_recv()
                cp = copy(a, 4 + j, (*chip, c), sib)
                cp.start()
                passed.append(cp)
        for a in range(n):
            copy(a, 0, sib, me).wait_recv()
            for j, chip in enumerate(chips):
                copy(a, 4 + j, (*chip, 1 - c), me).wait_recv()
        for cp in first + passed:
            cp.wait_send()
        for cp in mine:
            cp.wait()

    return pl.pallas_call(
        body, name=name, out_shape=tuple(out_shapes), in_specs=[ANY] * n, out_specs=tuple([ANY] * n),
        scratch_shapes=[pltpu.SemaphoreType.DMA((7 * n,)), pltpu.SemaphoreType.DMA((7 * n,)), pltpu.SemaphoreType.DMA((n,))],
        compiler_params=_cparams(),
    )(*shards)


def _rs_stage_a(sends, name):
    n = len(sends)
    out_shapes = [jax.ShapeDtypeStruct(s.shape, s.dtype) for s in sends]

    def body(*refs):
        ss, outs = refs[:n], refs[n:2 * n]
        send_sems, recv_sems = refs[2 * n:]
        x, y, c = _pos()
        sib = (x, y, 1 - c)
        copies = []
        for a in range(n):
            for k in range(4):
                cp = pltpu.make_async_remote_copy(src_ref=ss[a].at[k], dst_ref=outs[a].at[k], send_sem=send_sems.at[a * 4 + k],
                                                  recv_sem=recv_sems.at[a * 4 + k], device_id=sib, device_id_type=MESH)
                cp.start()
                copies.append(cp)
        for cp in copies:
            cp.wait()

    return pl.pallas_call(
        body, name=name, out_shape=tuple(out_shapes), in_specs=[ANY] * n, out_specs=tuple([ANY] * n),
        scratch_shapes=[pltpu.SemaphoreType.DMA((4 * n,)), pltpu.SemaphoreType.DMA((4 * n,))],
        compiler_params=_cparams(),
    )(*sends)


def _rs_stage_b(parts, name):
    n = len(parts)
    out_shapes = [jax.ShapeDtypeStruct(p.shape, p.dtype) for p in parts]

    def body(*refs):
        ps, outs = refs[:n], refs[n:2 * n]
        send_sems, recv_sems = refs[2 * n:]
        x, y, c = _pos()
        chips = [(1 - x, y), (x, 1 - y), (1 - x, 1 - y)]
        copies = []
        for a in range(n):
            for j, (px, py) in enumerate(chips):
                cp = pltpu.make_async_remote_copy(src_ref=ps[a].at[j], dst_ref=outs[a].at[j],
                                                  send_sem=send_sems.at[a * 3 + j], recv_sem=recv_sems.at[a * 3 + j],
                                                  device_id=(px, py, c), device_id_type=MESH)
                cp.start()
                copies.append(cp)
        for cp in copies:
            cp.wait()

    return pl.pallas_call(
        body, name=name, out_shape=tuple(out_shapes), in_specs=[ANY] * n, out_specs=tuple([ANY] * n),
        scratch_shapes=[pltpu.SemaphoreType.DMA((3 * n,)), pltpu.SemaphoreType.DMA((3 * n,))],
        compiler_params=_cparams(),
    )(*parts)


def _matmul(a, b, *, ta=False, tb=False, tm, tn, tk, name, b_off=(0, 0), b_extent=None):
    m, kdim = (a.shape[1], a.shape[0]) if ta else a.shape
    b_shape = b.shape if b_extent is None else b_extent
    n = b_shape[0] if tb else b_shape[1]
    assert (b_shape[1] if tb else b_shape[0]) == kdim, (a.shape, b_shape, ta, tb)
    assert m % tm == 0 and n % tn == 0 and kdim % tk == 0, (m, n, kdim, tm, tn, tk)
    o0, o1 = b_off
    nk = kdim // tk
    dn = (((0 if ta else 1,), (1 if tb else 0,)), ((), ()))

    def body(a_ref, b_ref, o_ref, acc_ref):
        k = pl.program_id(2)
        part = lax.dot_general(a_ref[...].astype(BF16), b_ref[...].astype(BF16), dn, preferred_element_type=F32)
        if nk == 1:
            o_ref[...] = part
        else:
            @pl.when(k == 0)
            def _():
                acc_ref[...] = part

            @pl.when(k > 0)
            def _():
                acc_ref[...] += part

            @pl.when(k == nk - 1)
            def _():
                o_ref[...] = acc_ref[...]

    a_spec = pl.BlockSpec((tk, tm), lambda i, j, k: (k, i)) if ta else pl.BlockSpec((tm, tk), lambda i, j, k: (i, k))
    b_spec = (pl.BlockSpec((tn, tk), lambda i, j, k: (j + o0, k + o1)) if tb
              else pl.BlockSpec((tk, tn), lambda i, j, k: (k + o0, j + o1)))
    acc_shape = (tm, tn) if nk > 1 else (8, 128)
    return pl.pallas_call(
        body, name=name, grid=(m // tm, n // tn, nk), out_shape=jax.ShapeDtypeStruct((m, n), F32),
        in_specs=[a_spec, b_spec], out_specs=pl.BlockSpec((tm, tn), lambda i, j, k: (i, j)),
        scratch_shapes=[pltpu.VMEM(acc_shape, F32)], compiler_params=_cparams(),
    )(a, b)


def _linear(a, w, gslot, *, w_is_nk, tiles, name):
    tm, tn, tk = tiles

    @jax.custom_vjp
    def lin(a, w, gslot):
        return _matmul(a, w, tb=w_is_nk, tm=tm, tn=tn, tk=tk, name=name + "_fwd")

    def fwd(a, w, gslot):
        return lin(a, w, gslot), (a, w)

    def bwd(res, g):
        a, w = res
        gb = g
        t, kdim = a.shape
        n = g.shape[1]
        tt = _pick(t, (768, 512, 256, 128))
        tkk = _pick(kdim, (2048, 1024, 512, 256, 128))
        tnn = _pick(n, (1024, 512, 256, 128))
        da = _matmul(gb, w, tb=not w_is_nk, tm=tt, tn=tkk, tk=tnn, name=name + "_bwd_a")
        tok = _pick(t, (1408, 768, 512, 256, 128))
        if w_is_nk:
            dw = _matmul(gb, a, ta=True, tm=tnn, tn=tkk, tk=tok, name=name + "_bwd_w")
        else:
            tkw = _pick(kdim, (1024, 512, 256, 128))
            tnw = _pick(n, (1024, 512, 256, 128))
            dw = _matmul(a, gb, ta=True, tm=tkw, tn=tnw, tk=tok, name=name + "_bwd_w")
        return da.astype(a.dtype), jnp.zeros_like(w), dw

    lin.defvjp(fwd, bwd)
    return lin(a, w, gslot)


def _pack_transposed(w, name):
    nl, kdim, r = w.shape
    half = kdim // 2
    tc = _pick(half, (256, 128))

    def body(lo_ref, hi_ref, o_ref):
        lo = pltpu.bitcast(lo_ref[0].astype(BF16).astype(F32).T, U32) >> 16
        hi = pltpu.bitcast(hi_ref[0].astype(BF16).astype(F32).T, U32) & jnp.uint32(0xFFFF0000)
        o_ref[0] = pltpu.bitcast(hi | lo, F32)

    nb = half // tc
    return pl.pallas_call(
        body, name=name, grid=(nl, nb), out_shape=jax.ShapeDtypeStruct((nl, r, half), F32),
        in_specs=[pl.BlockSpec((1, tc, r), lambda l, t: (l, t, 0)), pl.BlockSpec((1, tc, r), lambda l, t: (l, t + nb, 0))],
        out_specs=pl.BlockSpec((1, r, tc), lambda l, t: (l, 0, t)), compiler_params=_cparams(),
    )(w, w)


def _unpack(packed, n_pad, name):
    nl, n, half = packed.shape
    tr = math.gcd(math.gcd(n, n_pad - n), 64) if n_pad > n else _pick(n, (64, 32, 16))
    nin = n // tr

    def body(p_ref, o_ref):
        t = pl.program_id(1)

        @pl.when(t < nin)
        def _():
            u = pltpu.bitcast(p_ref[0], U32)
            o_ref[0, :, :half] = pltpu.bitcast(u << 16, F32).astype(BF16)
            o_ref[0, :, half:] = pltpu.bitcast(u & jnp.uint32(0xFFFF0000), F32).astype(BF16)

        @pl.when(t >= nin)
        def _():
            o_ref[...] = jnp.zeros_like(o_ref)

    return pl.pallas_call(
        body, name=name, grid=(nl, n_pad // tr), out_shape=jax.ShapeDtypeStruct((nl, n_pad, 2 * half), BF16),
        in_specs=[pl.BlockSpec((1, tr, half), lambda l, t: (l, jnp.minimum(t, nin - 1), 0))],
        out_specs=pl.BlockSpec((1, tr, 2 * half), lambda l, t: (l, t, 0)), compiler_params=_cparams(),
    )(packed)


def _cast_bf16(w, name):
    nl, r, c = w.shape
    tr = _pick(r, (512, 256, 128, 64, 32, 16))

    def body(w_ref, o_ref):
        o_ref[...] = w_ref[...].astype(BF16)

    return pl.pallas_call(
        body, name=name, grid=(nl, r // tr), out_shape=jax.ShapeDtypeStruct(w.shape, BF16),
        in_specs=[pl.BlockSpec((1, tr, c), lambda l, t: (l, t, 0))], out_specs=pl.BlockSpec((1, tr, c), lambda l, t: (l, t, 0)),
        compiler_params=_cparams(),
    )(w)


NA_ROWS_PER_STEP = 4


def _row_block(i):
    return slice(i * GRID_W, (i + 1) * GRID_W)


def _na_probs(q_ref, k_ref, kc_ref, b_ref, step, rows):
    scale = NA_HEAD_DIM ** -0.5
    nt = (((1,), (1,)), ((), ()))
    qs = q_ref[...]
    kws, starts, offs, s1 = [], [], [], []
    for i in range(NA_ROWS_PER_STEP):
        r = step * NA_ROWS_PER_STEP + i
        rs = jnp.clip(r - NA_KH // 2, 0, rows - NA_KH)
        offs.append(rs - r + NA_KH - 1)
        starts.append(pl.multiple_of(rs * GRID_W, GRID_W))
        kws.append(k_ref[pl.ds(starts[i], NA_KH * GRID_W), :])
        s1.append(lax.dot_general(qs[_row_block(i)], kws[i], nt, preferred_element_type=F32) * scale + b_ref[0, offs[i]])
    s1 = jnp.concatenate(s1, axis=0)
    s2 = lax.dot_general(qs, kc_ref[...], nt, preferred_element_type=F32) * scale
    m = jnp.maximum(jnp.max(s1, axis=-1, keepdims=True), jnp.max(s2, axis=-1, keepdims=True))
    e1 = jnp.exp(s1 - m)
    e2 = jnp.exp(s2 - m)
    inv = 1.0 / (jnp.sum(e1, axis=-1, keepdims=True) + jnp.sum(e2, axis=-1, keepdims=True))
    return qs, kws, starts, offs, e1 * inv, e2 * inv


def _na_specs(seq, nctx):
    dh = NA_HEAD_DIM
    win = NA_KH * GRID_W
    return [
        pl.BlockSpec((NA_ROWS_PER_STEP * GRID_W, dh), lambda h, r: (r, h)),
        pl.BlockSpec((seq, dh), lambda h, r: (0, h)),
        pl.BlockSpec((seq, dh), lambda h, r: (0, h)),
        pl.BlockSpec((nctx, dh), lambda h, r: (0, h)),
        pl.BlockSpec((nctx, dh), lambda h, r: (0, h)),
        pl.BlockSpec((1, NA_KH, GRID_W, win), lambda h, r: (h, 0, 0, 0)),
    ]


def _na_forward(name, q, k, v, kc, vc, biasw):
    seq, width = q.shape
    nctx = kc.shape[0]
    rows = seq // GRID_W
    assert rows % NA_ROWS_PER_STEP == 0, rows

    def body(q_ref, k_ref, v_ref, kc_ref, vc_ref, b_ref, o_ref):
        _, _, starts, _, p1, p2 = _na_probs(q_ref, k_ref, kc_ref, b_ref, pl.program_id(1), rows)
        p1b = p1.astype(BF16)
        o1 = [jnp.dot(p1b[_row_block(i)], v_ref[pl.ds(starts[i], NA_KH * GRID_W), :], preferred_element_type=F32)
              for i in range(NA_ROWS_PER_STEP)]
        o_ref[...] = jnp.concatenate(o1, axis=0) + jnp.dot(p2.astype(BF16), vc_ref[...], preferred_element_type=F32)

    return pl.pallas_call(
        body, name=name, grid=(NA_HEADS, rows // NA_ROWS_PER_STEP), out_shape=jax.ShapeDtypeStruct((seq, width), F32),
        in_specs=_na_specs(seq, nctx),
        out_specs=pl.BlockSpec((NA_ROWS_PER_STEP * GRID_W, NA_HEAD_DIM), lambda h, r: (r, h)),
        compiler_params=_cparams(),
    )(q, k, v, kc, vc, biasw)


def _na_backward(name, q, k, v, kc, vc, biasw, do):
    seq, width = q.shape
    nctx = kc.shape[0]
    rows = seq // GRID_W
    win = NA_KH * GRID_W
    dh = NA_HEAD_DIM
    scale = dh ** -0.5
    nt = (((1,), (1,)), ((), ()))
    tn = (((0,), (0,)), ((), ()))

    def body(q_ref, k_ref, v_ref, kc_ref, vc_ref, b_ref, do_ref, dq_ref, dk_ref, dv_ref, dkc_ref, dvc_ref, db_ref):
        r = pl.program_id(1)

        @pl.when(r == 0)
        def _():
            dk_ref[...] = jnp.zeros_like(dk_ref)
            dv_ref[...] = jnp.zeros_like(dv_ref)
            dkc_ref[...] = jnp.zeros_like(dkc_ref)
            dvc_ref[...] = jnp.zeros_like(dvc_ref)
            db_ref[...] = jnp.zeros_like(db_ref)

        nr = range(NA_ROWS_PER_STEP)
        qs, kws, starts, offs, p1, p2 = _na_probs(q_ref, k_ref, kc_ref, b_ref, r, rows)
        vcb = vc_ref[...]
        kcb = kc_ref[...]
        dob = do_ref[...].astype(BF16)
        p1b, p2b = p1.astype(BF16), p2.astype(BF16)
        dp1 = jnp.concatenate([lax.dot_general(dob[_row_block(i)], v_ref[pl.ds(starts[i], win), :], nt,
                                               preferred_element_type=F32) for i in nr], axis=0)
        dp2 = lax.dot_general(dob, vcb, nt, preferred_element_type=F32)
        delta = jnp.sum(dp1 * p1, axis=-1, keepdims=True) + jnp.sum(dp2 * p2, axis=-1, keepdims=True)
        ds1 = p1 * (dp1 - delta)
        ds2 = p2 * (dp2 - delta)
        ds1b = (ds1 * scale).astype(BF16)
        ds2b = (ds2 * scale).astype(BF16)
        dq1 = [jnp.dot(ds1b[_row_block(i)], kws[i], preferred_element_type=F32) for i in nr]
        dq_ref[...] = jnp.concatenate(dq1, axis=0) + jnp.dot(ds2b, kcb, preferred_element_type=F32)
        dvc_ref[...] += lax.dot_general(p2b, dob, tn, preferred_element_type=F32)
        dkc_ref[...] += lax.dot_general(ds2b, qs, tn, preferred_element_type=F32)
        for i in nr:
            sl = pl.ds(starts[i], win)
            db_ref[0, offs[i]] += ds1[_row_block(i)]
            dv_ref[sl, :] += lax.dot_general(p1b[_row_block(i)], dob[_row_block(i)], tn, preferred_element_type=F32)
            dk_ref[sl, :] += lax.dot_general(ds1b[_row_block(i)], qs[_row_block(i)], tn, preferred_element_type=F32)

    row_spec = pl.BlockSpec((NA_ROWS_PER_STEP * GRID_W, dh), lambda h, r: (r, h))
    seq_spec = pl.BlockSpec((seq, dh), lambda h, r: (0, h))
    ctx_spec = pl.BlockSpec((nctx, dh), lambda h, r: (0, h))
    b_spec = pl.BlockSpec((1, NA_KH, GRID_W, win), lambda h, r: (h, 0, 0, 0))
    return pl.pallas_call(
        body, name=name, grid=(NA_HEADS, rows // NA_ROWS_PER_STEP),
        out_shape=(jax.ShapeDtypeStruct((seq, width), F32), jax.ShapeDtypeStruct((seq, width), F32),
                   jax.ShapeDtypeStruct((seq, width), F32), jax.ShapeDtypeStruct((nctx, width), F32),
                   jax.ShapeDtypeStruct((nctx, width), F32), jax.ShapeDtypeStruct(biasw.shape, F32)),
        in_specs=_na_specs(seq, nctx) + [row_spec],
        out_specs=(row_spec, seq_spec, seq_spec, ctx_spec, ctx_spec, b_spec),
        compiler_params=_cparams(),
    )(q, k, v, kc, vc, biasw, do)


def _na_attention(e, q, k, v, kc, vc, biasw):
    @jax.custom_vjp
    def attn(q, k, v, kc, vc, biasw):
        return _na_forward(f"na_fwd{e}", q.astype(BF16), k.astype(BF16), v.astype(BF16), kc.astype(BF16), vc.astype(BF16), biasw)

    def fwd(q, k, v, kc, vc, biasw):
        res = (q.astype(BF16), k.astype(BF16), v.astype(BF16), kc.astype(BF16), vc.astype(BF16), biasw)
        return _na_forward(f"na_fwd{e}", *res), res

    def bwd(res, do):
        return _na_backward(f"na_bwd{e}", *res, do)

    attn.defvjp(fwd, bwd)
    return attn(q, k, v, kc, vc, biasw)


def _bias_windows(rpb):
    col = jnp.arange(GRID_W)
    dc = jnp.clip(col[None, :] - col[:, None], -(NA_KW - 1), NA_KW - 1) + NA_KW - 1
    onehot = (dc[None] == jnp.arange(2 * NA_KW - 1)[:, None, None]).astype(F32)
    tq = jnp.einsum("hrd,dqk->hrqk", rpb, onehot, precision=HIGHEST)
    col_start = jnp.clip(col - NA_KW // 2, 0, GRID_W - NA_KW)
    in_win = (col[None, :] >= col_start[:, None]) & (col[None, :] < col_start[:, None] + NA_KW)
    wins = jnp.stack([tq[:, off:off + NA_KH] for off in range(NA_KH)], axis=1)
    wins = jnp.where(in_win[None, None, None], wins, NEG)
    return wins.transpose(0, 1, 3, 2, 4).reshape(rpb.shape[0], NA_KH, GRID_W, NA_KH * GRID_W)


def _chunk_cumsum(u, reverse, name):
    t, nh = u.shape
    ln = SSD_CHUNK

    def body(u_ref, o_ref):
        row = lax.broadcasted_iota(jnp.int32, (ln, ln), 0)
        col = lax.broadcasted_iota(jnp.int32, (ln, ln), 1)
        uu = u_ref[...]
        down = jnp.dot((col <= row).astype(F32), uu, precision=HIGHEST, preferred_element_type=F32)
        up = jnp.dot((col >= row).astype(F32), uu, precision=HIGHEST, preferred_element_type=F32)
        first = lax.broadcasted_iota(jnp.int32, (ln, nh), 1) < nh // 2
        o_ref[...] = jnp.where(first, up, down) if reverse else jnp.where(first, down, up)

    spec = pl.BlockSpec((ln, nh), lambda i: (i, 0))
    return pl.pallas_call(body, name=name, grid=(t // ln,), out_shape=jax.ShapeDtypeStruct(u.shape, F32),
                          in_specs=[spec], out_specs=spec, compiler_params=_cparams())(u)


def _ssd_cumsum(e, u):
    @jax.custom_vjp
    def cs(u):
        return _chunk_cumsum(u, False, f"ssd_cumsum{e}")

    cs.defvjp(lambda u: (_chunk_cumsum(u, False, f"ssd_cumsum{e}"), None),
              lambda _, g: (_chunk_cumsum(g, True, f"ssd_cumsum_bwd{e}"),))
    return cs(u)


def _mxu_dots():
    c_nn, c_nt, c_tn = (((1,), (0,)), ((), ())), (((1,), (1,)), ((), ())), (((0,), (0,)), ((), ()))

    def dot(a, b, dn):
        return lax.dot_general(a.astype(BF16), b.astype(BF16), dn, preferred_element_type=F32)

    def make(dn, dn_da, a_first, dn_db, b_first):
        @jax.custom_vjp
        def f(a, b):
            return dot(a, b, dn)

        def bwd(res, g):
            a, b = res
            da = dot(g, b, dn_da) if a_first else dot(b, g, dn_da)
            db = dot(g, a, dn_db) if b_first else dot(a, g, dn_db)
            return da, db

        f.defvjp(lambda a, b: (dot(a, b, dn), (a, b)), bwd)
        return f

    nn = make(c_nn, c_nt, True, c_tn, False)
    nt = make(c_nt, c_nn, True, c_tn, True)
    tn = make(c_tn, c_nt, False, c_nn, False)
    return nn, nt, tn


def _ssd_step(h, x, bm, cm, dt, dt_t, cs, cs_t, *, direction, hpg, pdim):
    ln = x.shape[0]
    hp = hpg * pdim
    nn, nt, tn = _mxu_dots()
    row = lax.broadcasted_iota(jnp.int32, (ln, ln), 0)
    colm = lax.broadcasted_iota(jnp.int32, (ln, ln), 1)
    valid = (colm - row) * (1 - 2 * direction) <= 0
    last = (ln - 1) * (1 - direction)
    tot = jnp.sum(jnp.where(lax.broadcasted_iota(jnp.int32, (ln, hpg), 0) == last, cs, 0.0), axis=0, keepdims=True)
    cbm = nt(cm, bm)
    lane_head = lax.broadcasted_iota(jnp.int32, (1, hp), 1) // pdim
    sub_head = lax.broadcasted_iota(jnp.int32, (hp, 1), 0) // pdim
    y = jnp.zeros((ln, hp), F32)
    es = jnp.zeros((ln, hp), F32)
    we = jnp.zeros((ln, hp), F32)
    dend = jnp.zeros((hp, 1), F32)
    for r in range(hpg):
        cc = cs[:, r:r + 1]
        cr = cs_t[r:r + 1, :]
        decay = jnp.exp(jnp.where(valid, cc - cr, NEG))
        mask = lane_head == r
        y = y + nn(cbm * decay * dt_t[r:r + 1, :], jnp.where(mask, x, 0.0))
        es = es + jnp.where(mask, jnp.exp(cc), 0.0)
        we = we + jnp.where(mask, jnp.exp(tot[:, r:r + 1] - cc) * dt[:, r:r + 1], 0.0)
        dend = dend + jnp.where(sub_head == r, jnp.exp(tot[:, r:r + 1]), 0.0)
    y = y + es * nt(cm, h)
    h_new = h * dend + tn(x * we, bm)
    return y, h_new


def _ssd_chunk_of(d, s, ncc, nc):
    return jnp.where(d == 0, s, jnp.where(s < ncc, ncc - 1 - s, nc - 1 - s + ncc))


def _ssd_specs(cfg, step_of):
    hp, n, ln, hpg = cfg.hpg * SSD_HEAD_DIM, SSD_STATE, SSD_CHUNK, cfg.hpg
    ncc, nc = cfg.nctx // ln, cfg.t // ln

    def ch(d, s):
        return _ssd_chunk_of(d, step_of(s), ncc, nc)

    return dict(
        x=pl.BlockSpec((ln, hp), lambda d, g, s: (ch(d, s), g)),
        bc=pl.BlockSpec((ln, n), lambda d, g, s: (ch(d, s), g)),
        dt=pl.BlockSpec((1, 1, ln, hpg), lambda d, g, s: (d, g, ch(d, s), 0)),
        dt_t=pl.BlockSpec((1, 1, 8, ln), lambda d, g, s: (d, g, 0, ch(d, s))),
        y=pl.BlockSpec((1, ln, hp), lambda d, g, s: (d, ch(d, s), g)),
        bc2=pl.BlockSpec((1, ln, n), lambda d, g, s: (d, ch(d, s), g)),
        h=pl.BlockSpec((1, 1, 1, hp, n), lambda d, g, s: (d, g, step_of(s), 0, 0)),
    )


def _ssd_forward(cfg, name, xs, bm, cm, dt, dt_t, dta, dta_t):
    hp, n, hpg = cfg.hpg * SSD_HEAD_DIM, SSD_STATE, cfg.hpg
    nc = cfg.t // SSD_CHUNK
    sp = _ssd_specs(cfg, lambda s: s)

    def body(x_ref, b_ref, c_ref, dt_ref, dtt_ref, dta_ref, dtat_ref, y_ref, hs_ref, h_ref):
        d, s = pl.program_id(0), pl.program_id(2)

        @pl.when(s == 0)
        def _():
            h_ref[...] = jnp.zeros_like(h_ref)

        h = h_ref[...]
        hs_ref[0, 0, 0] = h
        y, h_new = _ssd_step(h, x_ref[...], b_ref[...], c_ref[...], dt_ref[0, 0], dtt_ref[0, 0], dta_ref[0, 0],
                             dtat_ref[0, 0], direction=d, hpg=hpg, pdim=SSD_HEAD_DIM)
        y_ref[0] = y
        h_ref[...] = h_new

    return pl.pallas_call(
        body, name=name, grid=(2, SSD_GROUPS, nc),
        out_shape=(jax.ShapeDtypeStruct((2, cfg.t, cfg.ssd_width), F32),
                   jax.ShapeDtypeStruct((2, SSD_GROUPS, nc, hp, n), F32)),
        in_specs=[sp["x"], sp["bc"], sp["bc"], sp["dt"], sp["dt_t"], sp["dt"], sp["dt_t"]],
        out_specs=(sp["y"], sp["h"]), scratch_shapes=[pltpu.VMEM((hp, n), F32)], compiler_params=_cparams(),
    )(xs, bm, cm, dt, dt_t, dta, dta_t)


def _ssd_backward(cfg, name, xs, bm, cm, dt, dt_t, dta, dta_t, hsave, dy):
    hp, n, hpg = cfg.hpg * SSD_HEAD_DIM, SSD_STATE, cfg.hpg
    nc = cfg.t // SSD_CHUNK
    sp = _ssd_specs(cfg, lambda s: nc - 1 - s)

    def body(x_ref, b_ref, c_ref, dt_ref, dtt_ref, dta_ref, dtat_ref, hs_ref, dy_ref,
             dx_ref, db_ref, dc_ref, ddt_ref, ddtt_ref, ddta_ref, ddtat_ref, dh_ref):
        d, s = pl.program_id(0), pl.program_id(2)

        @pl.when(s == 0)
        def _():
            dh_ref[...] = jnp.zeros_like(dh_ref)

        step = functools.partial(_ssd_step, direction=d, hpg=hpg, pdim=SSD_HEAD_DIM)
        _, vjp = jax.vjp(step, hs_ref[0, 0, 0], x_ref[...], b_ref[...], c_ref[...], dt_ref[0, 0], dtt_ref[0, 0],
                         dta_ref[0, 0], dtat_ref[0, 0])
        dh, dx, db, dc, ddt, ddtt, ddta, ddtat = vjp((dy_ref[0], dh_ref[...]))
        dh_ref[...] = dh
        dx_ref[0] = dx
        db_ref[0] = db
        dc_ref[0] = dc
        ddt_ref[0, 0] = ddt
        ddtt_ref[0, 0] = ddtt
        ddta_ref[0, 0] = ddta
        ddtat_ref[0, 0] = ddtat

    gn = SSD_GROUPS * n
    return pl.pallas_call(
        body, name=name, grid=(2, SSD_GROUPS, nc),
        out_shape=(jax.ShapeDtypeStruct((2, cfg.t, cfg.ssd_width), F32), jax.ShapeDtypeStruct((2, cfg.t, gn), F32),
                   jax.ShapeDtypeStruct((2, cfg.t, gn), F32), jax.ShapeDtypeStruct(dt.shape, F32),
                   jax.ShapeDtypeStruct(dt_t.shape, F32), jax.ShapeDtypeStruct(dt.shape, F32),
                   jax.ShapeDtypeStruct(dt_t.shape, F32)),
        in_specs=[sp["x"], sp["bc"], sp["bc"], sp["dt"], sp["dt_t"], sp["dt"], sp["dt_t"], sp["h"], sp["y"]],
        out_specs=(sp["y"], sp["bc2"], sp["bc2"], sp["dt"], sp["dt_t"], sp["dt"], sp["dt_t"]),
        scratch_shapes=[pltpu.VMEM((hp, n), F32)], compiler_params=_cparams(),
    )(xs, bm, cm, dt, dt_t, dta, dta_t, hsave, dy)


def _ssd_scan(cfg, e, xs, bm, cm, dt, dt_t, dta, dta_t):
    @jax.custom_vjp
    def scan(xs, bm, cm, dt, dt_t, dta, dta_t):
        return _ssd_forward(cfg, f"ssd_fwd{e}", xs, bm, cm, dt, dt_t, dta, dta_t)[0]

    def fwd(xs, bm, cm, dt, dt_t, dta, dta_t):
        y, hsave = _ssd_forward(cfg, f"ssd_fwd{e}", xs, bm, cm, dt, dt_t, dta, dta_t)
        return y, (xs, bm, cm, dt, dt_t, dta, dta_t, hsave)

    def bwd(res, dy):
        dx, db, dc, ddt, ddtt, ddta, ddtat = _ssd_backward(cfg, f"ssd_bwd{e}", *res, dy)
        return dx[0] + dx[1], db[0] + db[1], dc[0] + dc[1], ddt, ddtt, ddta, ddtat

    scan.defvjp(fwd, bwd)
    return scan(xs, bm, cm, dt, dt_t, dta, dta_t)


def _adam_math(w, g, m, v):
    m2 = ADAM_B1 * m + (1.0 - ADAM_B1) * g
    v2 = ADAM_B2 * v + (1.0 - ADAM_B2) * (g * g)
    m_hat = m2 / (1.0 - ADAM_B1 ** ADAM_STEP)
    v_hat = v2 / (1.0 - ADAM_B2 ** ADAM_STEP)
    delta = -ADAM_LR * (m_hat / (jnp.sqrt(v_hat) + ADAM_EPS) + ADAM_WD * w)
    return delta, m2, v2


def _adam_small(w, g, m, v, name):
    def body(w_ref, g_ref, m_ref, v_ref, d_ref, m2_ref, v2_ref):
        d_ref[...], m2_ref[...], v2_ref[...] = _adam_math(w_ref[...], g_ref[...], m_ref[...], v_ref[...])

    shp = jax.ShapeDtypeStruct(w.shape, F32)
    return pl.pallas_call(body, name=name, out_shape=(shp, shp, shp), in_specs=[VMEM] * 4, out_specs=(VMEM, VMEM, VMEM),
                          compiler_params=_cparams())(w, g, m, v)


def _adam_tiled(w, g, m, v, name):
    nl, r, c = w.shape
    tr = _pick(r, (256, 128, 64, 32, 16, 8))
    spec = pl.BlockSpec((1, tr, c), lambda l, t: (l, t, 0))

    def body(w_ref, g_ref, m_ref, v_ref, d_ref, m2_ref, v2_ref):
        d_ref[...], m2_ref[...], v2_ref[...] = _adam_math(w_ref[...], g_ref[...], m_ref[...], v_ref[...])

    shp = jax.ShapeDtypeStruct(w.shape, F32)
    return pl.pallas_call(body, name=name, grid=(nl, r // tr), out_shape=(shp, shp, shp), in_specs=[spec] * 4,
                          out_specs=(spec, spec, spec), compiler_params=_cparams())(w, g, m, v)


def _adam_sharded(w, m, v, part_a, part_b, *, transposed, name):
    nl, r, c = w.shape
    tr = _pick(r, (256, 128)) if transposed else _pick(r, (256, 128, 64, 32, 16))
    w_spec = pl.BlockSpec((1, tr, c), lambda l, t: (l, t, 0))
    if transposed:
        pa_spec = pl.BlockSpec((1, 1, c, tr), lambda l, t: (0, l, 0, t))
        pb_spec = pl.BlockSpec((3, 1, c, tr), lambda l, t: (0, l, 0, t))
    else:
        pa_spec = pl.BlockSpec((1, 1, tr, c), lambda l, t: (0, l, t, 0))
        pb_spec = pl.BlockSpec((3, 1, tr, c), lambda l, t: (0, l, t, 0))

    def body(w_ref, m_ref, v_ref, pa_ref, pb_ref, g_ref, d_ref, m2_ref, v2_ref):
        g = pa_ref[0, 0] + pb_ref[0, 0].astype(F32) + pb_ref[1, 0].astype(F32) + pb_ref[2, 0].astype(F32)
        if transposed:
            g = g.T
        g_ref[0] = g
        d_ref[0], m2_ref[0], v2_ref[0] = _adam_math(w_ref[0], g, m_ref[0], v_ref[0])

    shp = jax.ShapeDtypeStruct(w.shape, F32)
    return pl.pallas_call(
        body, name=name, grid=(nl, r // tr), out_shape=(shp, shp, shp, shp),
        in_specs=[w_spec, w_spec, w_spec, pa_spec, pb_spec], out_specs=(w_spec, w_spec, w_spec, w_spec),
        compiler_params=_cparams(),
    )(w, m, v, part_a, part_b)


def _pick_blocks(g, recv, axis, length, g_blocks, r_blocks, out_dtype, name):
    nl = g.shape[0]
    n = g_blocks.shape[0]
    blk_shape = list(g.shape)
    blk_shape[axis] = length
    if axis == 1:
        cols = g.shape[2]
        tc = _pick(cols, (512, 256, 128))
        g_spec = pl.BlockSpec((1, length, tc), lambda i, l, t, gb, rb: (l, gb[i], t))
        r_spec = pl.BlockSpec((1, 1, length, tc), lambda i, l, t, gb, rb: (rb[i], l, 0, t))
        o_spec = pl.BlockSpec((1, 1, length, tc), lambda i, l, t, gb, rb: (i, l, 0, t))
        grid = (n, nl, cols // tc)
    else:
        rows = g.shape[1]
        tr = _pick(rows, (512, 256, 128, 64, 32, 16))
        g_spec = pl.BlockSpec((1, tr, length), lambda i, l, t, gb, rb: (l, t, gb[i]))
        r_spec = pl.BlockSpec((1, 1, tr, length), lambda i, l, t, gb, rb: (rb[i], l, t, 0))
        o_spec = pl.BlockSpec((1, 1, tr, length), lambda i, l, t, gb, rb: (i, l, t, 0))
        grid = (n, nl, rows // tr)

    if recv is None:
        def body(gb_ref, rb_ref, g_ref, o_ref):
            o_ref[0] = g_ref[...].astype(out_dtype)
        in_specs, args = [g_spec], (g,)
    else:
        def body(gb_ref, rb_ref, g_ref, r_ref, o_ref):
            o_ref[0] = (g_ref[...] + r_ref[0].astype(F32)).astype(out_dtype)
        in_specs, args = [g_spec, r_spec], (g, recv)

    return pl.pallas_call(
        body, name=name, out_shape=jax.ShapeDtypeStruct((n, *blk_shape), out_dtype),
        grid_spec=pltpu.PrefetchScalarGridSpec(num_scalar_prefetch=2, grid=grid, in_specs=in_specs, out_specs=o_spec),
        compiler_params=_cparams(),
    )(g_blocks, r_blocks, *args)


def _flatten(arrs):
    flat = jnp.concatenate([a.reshape(-1).astype(F32) for a in arrs])
    n = flat.shape[0]
    n_pad = -(-n // 1024) * 1024
    return jnp.pad(flat, (0, n_pad - n)).reshape(n_pad // 128, 128)


def _unflatten(buf, shapes):
    flat = buf.reshape(-1)
    out, o = [], 0
    for s in shapes:
        n = math.prod(s)
        out.append(flat[o:o + n].reshape(s))
        o += n
    return out


def _rowwise(name, fn, rows, seg, shared, out_cols, out_dtypes, tile, nct):
    t = rows[0].shape[0]
    nr, ns, nsh, no = len(rows), len(seg), len(shared), len(out_cols)
    n_in = nr + ns + nsh

    def row_spec(c):
        return pl.BlockSpec((tile, c), lambda i: (i, 0))

    def seg_spec(c):
        return pl.BlockSpec((1, 1, c), lambda i: (jnp.where(i < nct, 0, 1), 0, 0))

    def whole_spec(shape):
        return pl.BlockSpec(shape, lambda i: (0, 0))

    in_specs = ([row_spec(r.shape[1]) for r in rows] + [seg_spec(s.shape[1]) for s in seg]
                + [whole_spec(s.shape) for s in shared])
    out_shapes = tuple(jax.ShapeDtypeStruct((t, c), dt) for c, dt in zip(out_cols, out_dtypes))
    out_specs = tuple(row_spec(c) for c in out_cols)

    def load(refs):
        return [r[0] if nr <= j < nr + ns else r[...] for j, r in enumerate(refs[:n_in])]

    def lift(args):
        return [a[:, None, :] if nr <= j < nr + ns else a for j, a in enumerate(args)]

    def forward(*args):
        def body(*refs):
            outs = fn(*load(refs))
            for o_ref, o in zip(refs[n_in:], outs):
                o_ref[...] = o

        return pl.pallas_call(body, name=name + "_fwd", grid=(t // tile,), out_shape=out_shapes, in_specs=in_specs,
                              out_specs=out_specs, compiler_params=_cparams())(*lift(args))

    def backward(args, cts):
        def body(*refs):
            i = pl.program_id(0)
            ct = tuple(r[...] for r in refs[n_in:n_in + no])
            d_refs = refs[n_in + no:]
            _, vjp = jax.vjp(fn, *load(refs))
            grads = vjp(ct)
            for ref, g in zip(d_refs[:nr], grads[:nr]):
                ref[...] = g
            first_seg = jnp.logical_or(i == 0, i == nct)
            for j in range(nr, n_in):
                ref, g = d_refs[j], grads[j]
                first = first_seg if j < nr + ns else i == 0
                g = g[None] if j < nr + ns else g

                @pl.when(first)
                def _(ref=ref, g=g):
                    ref[...] = g

                @pl.when(jnp.logical_not(first))
                def _(ref=ref, g=g):
                    ref[...] += g

        largs = lift(args)
        d_shapes = tuple(jax.ShapeDtypeStruct(a.shape, F32) for a in largs)
        ct_specs = [row_spec(c) for c in out_cols]
        outs = pl.pallas_call(body, name=name + "_bwd", grid=(t // tile,), out_shape=d_shapes,
                              in_specs=in_specs + ct_specs, out_specs=tuple(in_specs), compiler_params=_cparams())(*largs, *cts)
        return [o[:, 0, :] if nr <= j < nr + ns else o for j, o in enumerate(outs)]

    @jax.custom_vjp
    def prim(*args):
        return tuple(forward(*args))

    prim.defvjp(lambda *args: (tuple(forward(*args)), args), lambda args, cts: tuple(backward(args, cts)))
    return prim(*rows, *seg, *shared)


def _silu(x):
    return x * (1.0 / (1.0 + jnp.exp(-x)))


def _softplus(x):
    return jnp.maximum(x, 0.0) + jnp.log(1.0 + jnp.exp(-jnp.abs(x)))


def _dwconv(name, x, w, b, act, nctx):
    t, c = x.shape
    kk = w.shape[0]
    half = kk // 2
    tile = 128
    tc = _pick(c, (1024, 512, 256, 128))
    nt, nct, hb = t // tile, nctx // tile, tile // 8
    cur = pl.BlockSpec((tile, tc), lambda j, i: (i, j))
    prev = pl.BlockSpec((8, tc), lambda j, i: (jnp.maximum(i * hb - 1, 0), j))
    nxt = pl.BlockSpec((8, tc), lambda j, i: (jnp.minimum((i + 1) * hb, t // 8 - 1), j))
    w_spec = pl.BlockSpec((kk, tc), lambda j, i: (0, j))
    b_spec = pl.BlockSpec((1, tc), lambda j, i: (0, j))
    grid = (c // tc, nt)

    def extended(cur_ref, prev_ref, next_ref, i):
        has_prev = jnp.logical_and(i != 0, i != nct)
        has_next = jnp.logical_and(i != nct - 1, i != nt - 1)
        return jnp.concatenate([jnp.where(has_prev, prev_ref[...], 0.0), cur_ref[...],
                                jnp.where(has_next, next_ref[...], 0.0)], axis=0)

    def taps(ext, w_ref, lo, n):
        acc = None
        for k in range(kk):
            term = w_ref[k:k + 1, :] * ext[lo + k - half:lo + k - half + n]
            acc = term if acc is None else acc + term
        return acc

    def forward(x, w, b):
        def body(x_ref, xp_ref, xn_ref, w_ref, b_ref, o_ref):
            ext = extended(x_ref, xp_ref, xn_ref, pl.program_id(1))
            y = taps(ext, w_ref, 8, tile) + b_ref[...]
            o_ref[...] = _silu(y) if act else y

        return pl.pallas_call(body, name=name + "_fwd", grid=grid, out_shape=jax.ShapeDtypeStruct((t, c), F32),
                              in_specs=[cur, prev, nxt, w_spec, b_spec], out_specs=cur, compiler_params=_cparams())(x, x, x, w, b)

    def backward(x, w, b, g):
        m = tile + 2 * half

        def body(x_ref, xp_ref, xn_ref, g_ref, gp_ref, gn_ref, w_ref, b_ref, dx_ref, dw_ref, db_ref):
            i = pl.program_id(1)
            xe = extended(x_ref, xp_ref, xn_ref, i)
            dpre = extended(g_ref, gp_ref, gn_ref, i)[8 - half:8 - half + m]
            if act:
                pre = taps(xe, w_ref, 8 - half, m) + b_ref[...]
                sg = 1.0 / (1.0 + jnp.exp(-pre))
                dpre = dpre * (sg * (1.0 + pre * (1.0 - sg)))
            acc = None
            for k in range(kk):
                term = w_ref[k:k + 1, :] * dpre[2 * half - k:2 * half - k + tile]
                acc = term if acc is None else acc + term
            dx_ref[...] = acc
            dcur = dpre[half:half + tile]
            dw = jnp.concatenate([jnp.sum(dcur * xe[8 + k - half:8 + k - half + tile], axis=0, keepdims=True)
                                  for k in range(kk)], axis=0)
            db = jnp.sum(dcur, axis=0, keepdims=True)

            @pl.when(i == 0)
            def _():
                dw_ref[...] = dw
                db_ref[...] = db

            @pl.when(i != 0)
            def _():
                dw_ref[...] += dw
                db_ref[...] += db

        return pl.pallas_call(
            body, name=name + "_bwd", grid=grid,
            out_shape=(jax.ShapeDtypeStruct((t, c), F32), jax.ShapeDtypeStruct(w.shape, F32), jax.ShapeDtypeStruct(b.shape, F32)),
            in_specs=[cur, prev, nxt, cur, prev, nxt, w_spec, b_spec], out_specs=(cur, w_spec, b_spec),
            compiler_params=_cparams())(x, x, x, g, g, g, w, b)

    @jax.custom_vjp
    def conv(x, w, b):
        return forward(x, w, b)

    conv.defvjp(lambda x, w, b: (forward(x, w, b), (x, w, b)), lambda res, g: backward(*res, g))
    return conv(x, w, b)


def _odd_pre_tile(pch):
    half = pch.shape[1] // 2
    return (pch[:, :half] * pch[:, half:],)


def _odd_post_tile(pb, pg, yc):
    return ((_silu(pg) * (pb * yc)).astype(BF16),)


def _rms_rows(x):
    return x * lax.rsqrt(jnp.mean(x * x, axis=-1, keepdims=True) + EPS)


def _pre0_tile(x, scale, shift, g):
    return ((_rms_rows(x) * g * (1 + scale) + shift).astype(BF16),)


def _pre_tile(x, y_prev, gate, scale, shift, g):
    xn = x + gate * y_prev
    return xn, (_rms_rows(xn) * g * (1 + scale) + shift).astype(BF16)


def _loss_tile(x, y_prev, target, gate, weight):
    err = (x + gate * y_prev - target) * weight
    return (0.5 * jnp.mean(err * err, axis=-1, keepdims=True),)


def _mid_even_tile(pa, pd, qg, kg, dt_bias, a, *, naw, sw, nh):
    def heads_norm(u, g):
        return jnp.concatenate([_rms_rows(u[:, j:j + NA_HEAD_DIM]) * g for j in range(0, naw, NA_HEAD_DIM)], axis=1)

    q, gate, z = pa[:, :naw], pa[:, naw:2 * naw], pa[:, 2 * naw:2 * naw + sw]
    k, v = pa[:, 2 * naw + sw:3 * naw + sw], pa[:, 3 * naw + sw:]
    dt = _softplus(pd[:, :nh] + dt_bias)
    return heads_norm(q, qg), heads_norm(k, kg), v, _silu(gate), _silu(z), dt, dt * a


def _post_even_tile(ya, sg, y0, y1, xs, sz, dskip, g, *, sw):
    yz = (y0 + y1 + dskip * xs) * sz
    gw = sw // SSD_GROUPS
    yb = jnp.concatenate([_rms_rows(yz[:, j:j + gw]) for j in range(0, sw, gw)], axis=1) * g
    return (jnp.concatenate([ya * sg, yb], axis=1).astype(BF16),)


def _in_proj_split(a, w, gslot, widths, *, w_is_nk, tm, tn, name):
    t, kdim = a.shape
    starts = [sum(widths[:i]) for i in range(len(widths))]
    assert all(s % tn == 0 and wd % tn == 0 for s, wd in zip(starts, widths)), (starts, widths, tn)

    def piece(i, blocks):
        off = (starts[i] // blocks, 0) if w_is_nk else (0, starts[i] // blocks)
        ext = (widths[i], kdim) if w_is_nk else (kdim, widths[i])
        return off, ext

    def forward(a, w):
        outs = []
        for i in range(len(widths)):
            off, ext = piece(i, tn)
            outs.append(_matmul(a, w, tb=w_is_nk, tm=tm, tn=tn, tk=kdim, b_off=off, b_extent=ext, name=f"{name}_fwd{i}"))
        return tuple(outs)

    @jax.custom_vjp
    def proj(a, w, gslot):
        return forward(a, w)

    def bwd(res, gs):
        a, w = res
        tt = _pick(t, (768, 512, 256, 128))
        tkk = _pick(kdim, (2048, 1024, 512, 256, 128))
        tok = _pick(t, (1408, 768, 512, 256, 128))
        da, dws = None, []
        for i, g in enumerate(gs):
            big = 2 * tn if widths[i] % (2 * tn) == 0 and starts[i] % (2 * tn) == 0 else tn
            off, ext = piece(i, big)
            part = _matmul(g, w, tb=not w_is_nk, tm=tt, tn=tkk, tk=big, b_off=off, b_extent=ext, name=f"{name}_bwd_a{i}")
            da = part if da is None else da + part
            if w_is_nk:
                dws.append(_matmul(g, a, ta=True, tm=big, tn=tkk, tk=tok, name=f"{name}_bwd_w{i}"))
            else:
                dws.append(_matmul(a, g, ta=True, tm=_pick(kdim, (1024, 512, 256, 128)), tn=_pick(widths[i], (1024, 512, 256, 128)),
                                   tk=tok, name=f"{name}_bwd_w{i}"))
        dw = jnp.concatenate(dws, axis=0 if w_is_nk else 1)
        if dw.shape != w.shape:
            dw = jnp.pad(dw, [(0, w.shape[0] - dw.shape[0]), (0, w.shape[1] - dw.shape[1])])
        return da.astype(a.dtype), jnp.zeros_like(w), dw

    proj.defvjp(lambda a, w, gslot: (forward(a, w), (a, w)), bwd)
    return proj(a, w, gslot)


def _rms(x, g):
    return x * lax.rsqrt(jnp.mean(x * x, axis=-1, keepdims=True) + EPS) * g


def _dw_conv(x, w, b=None):
    k = w.shape[0]
    ln = x.shape[0]
    xp = jnp.pad(x, ((k // 2, k // 2), (0, 0)))
    y = sum(w[i][None, :] * xp[i:i + ln] for i in range(k))
    return y if b is None else y + b


def _conv_two(x, nctx, w, b=None):
    return jnp.concatenate([_dw_conv(x[:nctx], w, b), _dw_conv(x[nctx:], w, b)], axis=0)


def _mod_rows(nctx, seq, ctx_vec, lat_vec):
    return jnp.concatenate([jnp.broadcast_to(ctx_vec, (nctx, ctx_vec.shape[-1])),
                            jnp.broadcast_to(lat_vec, (seq, lat_vec.shape[-1]))], axis=0)


def _even_mixer(cfg, h, e, wd, update_ctx):
    d, nctx, seq, t = cfg.d, cfg.nctx, cfg.s, cfg.t
    naw, sw = cfg.na_width, cfg.ssd_width
    gn = SSD_GROUPS * SSD_STATE
    nh = 2 * SSD_GROUPS * cfg.hpg
    wa, wx = 4 * naw + sw, sw + 2 * gn
    pa, px, pd = _in_proj_split(h, wd["win_t"][e], wd["g_win_t"][e], (wa, wx, cfg.n_pad - wa - wx), w_is_nk=True,
                                tm=cfg.tiles_in[0], tn=cfg.tiles_in[1], name=f"in_even{e}")
    tile = cfg.tile_tok
    nct = nctx // tile
    a_neg = -jnp.exp(wd["ssd_a_log"][e]).reshape(1, nh)
    qn, kn, vv, sg, sz, dt, dta = _rowwise(
        f"mid_even{e}", functools.partial(_mid_even_tile, naw=naw, sw=sw, nh=nh), [pa, pd], [],
        [wd["q_norm_g"][e][None, :], wd["k_norm_g"][e][None, :], wd["ssd_dt_bias"][e].reshape(1, nh), a_neg],
        [naw, naw, naw, naw, sw, nh, nh], [F32] * 7, tile, nct)
    biasw = _bias_windows(wd["na_rpb"][e])
    ya = _na_attention(e, qn[nctx:], kn[nctx:], vv[nctx:], kn[:nctx], vv[:nctx], biasw)
    if update_ctx:
        qc = qn[:nctx].reshape(nctx, NA_HEADS, NA_HEAD_DIM)
        kc = kn[:nctx].reshape(nctx, NA_HEADS, NA_HEAD_DIM)
        vc = vv[:nctx].reshape(nctx, NA_HEADS, NA_HEAD_DIM)
        sc = jnp.einsum("qhd,khd->hqk", qc, kc).astype(F32) * NA_HEAD_DIM ** -0.5
        yac = jnp.einsum("hqk,khd->qhd", jax.nn.softmax(sc, axis=-1), vc).reshape(nctx, naw)
    else:
        yac = jnp.zeros((nctx, naw), F32)
    ya = jnp.concatenate([yac, ya], axis=0)

    xbc = _dwconv(f"ssd_conv{e}", px, wd["ssd_conv_w"][e], wd["ssd_conv_b"][e][None, :], True, nctx)
    xs, bm, cm = xbc[:, :sw], xbc[:, sw:sw + gn], xbc[:, sw + gn:]

    def arrange(u):
        u4 = u.reshape(t, 2, SSD_GROUPS, cfg.hpg)
        return u4.transpose(1, 2, 0, 3), jnp.pad(u4.transpose(1, 2, 3, 0), ((0, 0), (0, 0), (0, 8 - cfg.hpg), (0, 0)))

    dt4, dt_t = arrange(dt)
    cs4, cs_t = arrange(_ssd_cumsum(e, dta))
    y2 = _ssd_scan(cfg, e, xs, bm, cm, dt4, dt_t, cs4, cs_t)
    dskip = jnp.repeat(wd["ssd_d"][e], SSD_HEAD_DIM)[None, :]
    (ycat,) = _rowwise(f"post_even{e}", functools.partial(_post_even_tile, sw=sw), [ya, sg, y2[0], y2[1], xs, sz], [],
                       [dskip, wd["ssd_norm_g"][e][None, :]], [naw + sw], [BF16], tile, nct)
    return _linear(ycat, wd["wout"][e], wd["g_wout"][e], w_is_nk=False, tiles=cfg.tiles_out_even, name=f"out_even{e}")


def _odd_mixer(cfg, h, o, wd):
    d, nctx = cfg.d, cfg.nctx
    tile = cfg.tile_tok
    nct = nctx // tile
    pb, pch, pg = _in_proj_split(h, wd["sc_win"][o], wd["g_sc_win"][o], (d, 2 * d, d), w_is_nk=False,
                                 tm=cfg.tiles_in_odd[0], tn=_pick(d, (1024, 512, 256, 128)), name=f"in_odd{o}")
    (cv,) = _rowwise(f"odd_pre{o}", _odd_pre_tile, [pch], [], [], [d], [F32], tile, nct)
    yc = _dwconv(f"sc_conv{o}", cv, wd["sc_conv_w"][o], jnp.zeros((1, d), F32), False, nctx)
    (u,) = _rowwise(f"odd_post{o}", _odd_post_tile, [pb, pg, yc], [], [], [d], [BF16], tile, nct)
    return _linear(u, wd["sc_wout"][o], wd["g_sc_wout"][o], w_is_nk=False, tiles=cfg.tiles_out_odd, name=f"out_odd{o}")


def _local_loss(cfg, x, ctx, target, mods, mods_c, wd):
    d, nctx, seq = cfg.d, cfg.nctx, cfg.s
    tile = cfg.tile_res
    nct = nctx // tile
    xx = jnp.concatenate([ctx, x], axis=0)
    y_prev = gate_prev = None
    for i in range(DEPTH):
        update_ctx = any(j % 2 == 0 for j in range(i + 1, DEPTH))
        shift = jnp.stack([mods_c[i, :d], mods[i, :d]])
        scale = jnp.stack([mods_c[i, d:2 * d], mods[i, d:2 * d]])
        g = wd["norm_g"][i][None, :]
        if y_prev is None:
            (h,) = _rowwise(f"pre{i}", _pre0_tile, [xx], [scale, shift], [g], [d], [BF16], tile, nct)
        else:
            xx, h = _rowwise(f"pre{i}", _pre_tile, [xx, y_prev], [gate_prev, scale, shift], [g], [d, d], [F32, BF16], tile, nct)
        y_prev = _even_mixer(cfg, h, i // 2, wd, update_ctx) if i % 2 == 0 else _odd_mixer(cfg, h, i // 2, wd)
        gate_c = mods_c[i, 2 * d:] if update_ctx else jnp.zeros((d,), F32)
        gate_prev = jnp.stack([gate_c, mods[i, 2 * d:]])
    target_rows = jnp.concatenate([jnp.zeros((nctx, d), F32), target], axis=0)
    weight = jnp.stack([jnp.zeros((d,), F32), jnp.ones((d,), F32)])
    (row_loss,) = _rowwise("loss", _loss_tile, [xx, y_prev, target_rows], [gate_prev, weight], [], [1], [F32], tile, nct)
    return jnp.sum(row_loss)


SMALL_REPLICATED = ["norm_g", "ssd_conv_b", "ssd_a_log", "ssd_dt_bias", "ssd_d", "ssd_norm_g", "q_norm_g", "k_norm_g", "na_rpb"]
WEIGHT_ORDER = ["c_ctx", "ada_w", "ada_b", "norm_g", "na_ssd_w_in", "ssd_conv_w", "ssd_conv_b", "ssd_a_log", "ssd_dt_bias",
                "ssd_d", "ssd_norm_g", "q_norm_g", "k_norm_g", "na_rpb", "na_ssd_w_out", "sc_w_in", "sc_conv_w", "sc_w_out"]


def kernel(x, c, ctx, c_ctx, ada_w, ada_b, norm_g, na_ssd_w_in, ssd_conv_w, ssd_conv_b, ssd_a_log, ssd_dt_bias, ssd_d, ssd_norm_g, q_norm_g, k_norm_g, na_rpb, na_ssd_w_out, sc_w_in, sc_conv_w, sc_w_out, loss_target, m_c_ctx, m_ada_w, m_ada_b, m_norm_g, m_na_ssd_w_in, m_ssd_conv_w, m_ssd_conv_b, m_ssd_a_log, m_ssd_dt_bias, m_ssd_d, m_ssd_norm_g, m_q_norm_g, m_k_norm_g, m_na_rpb, m_na_ssd_w_out, m_sc_w_in, m_sc_conv_w, m_sc_w_out, v_c_ctx, v_ada_w, v_ada_b, v_norm_g, v_na_ssd_w_in, v_ssd_conv_w, v_ssd_conv_b, v_ssd_a_log, v_ssd_dt_bias, v_ssd_d, v_ssd_norm_g, v_q_norm_g, v_k_norm_g, v_na_rpb, v_na_ssd_w_out, v_sc_w_in, v_sc_conv_w, v_sc_w_out):
    given = dict(locals())
    weights = {n: given[n] for n in WEIGHT_ORDER}
    mom_m = {n: given["m_" + n] for n in WEIGHT_ORDER}
    mom_v = {n: given["v_" + n] for n in WEIGHT_ORDER}

    d = x.shape[-1]
    seq, nctx = x.shape[1], ctx.shape[1]
    n_in_shard = na_ssd_w_in.shape[-1]
    n_in = n_in_shard * NDEV
    n_pad = -(-n_in // PAD_TO) * PAD_TO
    hpg = (d // SSD_HEAD_DIM) // SSD_GROUPS
    t = nctx + seq
    tm = _pick(t, (1408, 768, 512, 256, 128))
    cfg = SimpleNamespace(
        d=d, s=seq, nctx=nctx, t=t, hpg=hpg, na_width=NA_HEADS * NA_HEAD_DIM, ssd_width=d, n_in=n_in, n_pad=n_pad,
        tiles_in=(tm, _pick(n_pad, (512, 256, 128)), d),
        tiles_out_even=(tm, _pick(d, (1024, 512, 256, 128)), _pick(NA_HEADS * NA_HEAD_DIM + d, (1024, 512, 256, 128))),
        tiles_in_odd=(tm, _pick(4 * d, (1024, 512, 256, 128)), d),
        tiles_out_odd=(tm, _pick(d, (1024, 512, 256, 128)), d),
        tile_tok=128, tile_res=256,
    )
    me = _my_index()
    xl, cl, ctxl, tgt = x[0], c, ctx[0], loss_target[0]

    ncol = ada_w.shape[-1]
    c_rows = -(-d // 128)
    c_all = _small_allgather(jnp.pad(cl.reshape(-1), (0, c_rows * 128 - d)).reshape(c_rows, 128), "gather_c")[0]
    c_all = c_all.reshape(NDEV, -1)[:, :d]
    cond = jnp.concatenate([c_all, c_ctx[None, :], jnp.zeros((16 - NDEV - 1, d), F32)], axis=0)
    s16 = jax.nn.silu(cond)
    ada_b_mine = lax.dynamic_slice_in_dim(ada_b, me * ncol, ncol, axis=1)
    mod_part = jnp.stack([
        _matmul(s16, ada_w[i], tm=16, tn=_pick(ncol, (768, 512, 256, 128)), tk=d, name=f"adaln{i}") + ada_b_mine[i][None, :]
        for i in range(DEPTH)])
    mp_rows = DEPTH * 16 * ncol // 128
    mod_all = _small_allgather(mod_part.reshape(mp_rows, 128), "gather_mod")[0]
    mod_all = mod_all.reshape(NDEV, DEPTH, 16, ncol).transpose(1, 2, 0, 3).reshape(DEPTH, 16, NDEV * ncol)
    mods = lax.dynamic_index_in_dim(mod_all, me, axis=1, keepdims=False)
    mods_c = mod_all[:, NDEV]

    packed = _pack_transposed(na_ssd_w_in, "pack_w_in")
    wout_b = _cast_bf16(na_ssd_w_out, "cast_w_out")
    scwin_b = _cast_bf16(sc_w_in, "cast_sc_w_in")
    scwout_b = _cast_bf16(sc_w_out, "cast_sc_w_out")
    packed_all, wout_all, scwin_all, scwout_all = _big_allgather(
        [packed, wout_b, scwin_b, scwout_b], [1, 1, 2, 1], "gather_weights")
    win_t = _unpack(packed_all, n_pad, "unpack_w_in")
    conv_shapes = [ssd_conv_w.shape, sc_conv_w.shape]
    conv_all = _small_allgather(_flatten([ssd_conv_w, sc_conv_w]), "gather_conv")[0]
    conv_parts = [_unflatten(conv_all[j], conv_shapes) for j in range(NDEV)]
    ssd_conv_full = jnp.concatenate([cp[0] for cp in conv_parts], axis=-1)
    sc_conv_full = jnp.concatenate([cp[1] for cp in conv_parts], axis=-1)

    small = {n: weights[n] for n in SMALL_REPLICATED}
    small["ssd_conv_w"] = ssd_conv_full
    small["sc_conv_w"] = sc_conv_full
    gslots = dict(g_win_t=jnp.zeros(win_t.shape, F32), g_wout=jnp.zeros(wout_all.shape, F32),
                  g_sc_win=jnp.zeros(scwin_all.shape, F32), g_sc_wout=jnp.zeros(scwout_all.shape, F32))
    frozen = dict(win_t=win_t, wout=wout_all, sc_win=scwin_all, sc_wout=scwout_all)

    def loss_fn(xl, mods, mods_c, small, gslots):
        return _local_loss(cfg, xl, ctxl, tgt, mods, mods_c, {**small, **gslots, **frozen})

    loss_local, (g_x, g_mods, g_mods_c, g_small, g_big) = jax.value_and_grad(loss_fn, argnums=(0, 1, 2, 3, 4))(
        xl, mods, mods_c, small, gslots)

    small_names = SMALL_REPLICATED + ["ssd_conv_w", "sc_conv_w"]
    small_shapes = [g_small[n].shape for n in small_names] + [g_mods_c.shape]
    flat_small = _flatten([g_small[n] for n in small_names] + [g_mods_c])
    _, small_sum = _small_allgather(flat_small, "gather_small_grads")
    summed = _unflatten(small_sum, small_shapes)
    g_rep = dict(zip(small_names, summed[:-1]))
    g_mods_c_tot = summed[-1]
    gm_rows = DEPTH * NDEV * ncol // 128
    gm_all = _small_allgather(g_mods.reshape(gm_rows, 128), "gather_mod_grads")[0].reshape(NDEV, DEPTH, NDEV * ncol)
    dm = jnp.concatenate([gm_all.transpose(1, 0, 2), g_mods_c_tot[:, None, :],
                          jnp.zeros((DEPTH, 16 - NDEV - 1, NDEV * ncol), F32)], axis=1)
    grad_ada_b = jnp.sum(dm, axis=1)
    dm_mine = lax.dynamic_slice_in_dim(dm, me * ncol, ncol, axis=2)
    grad_ada_w = jnp.stack([
        _matmul(s16, dm_mine[i], ta=True, tm=_pick(d, (512, 256, 128)), tn=_pick(ncol, (768, 512, 256, 128)), tk=16,
                name=f"adaln_gw{i}") for i in range(DEPTH)])
    ds_part = sum(_matmul(dm_mine[i], ada_w[i], tb=True, tm=16, tn=_pick(d, (2048, 1024, 512, 256, 128)),
                          tk=_pick(ncol, (768, 512, 256, 128)), name=f"adaln_gs{i}") for i in range(DEPTH))[NDEV]
    ds_ctx = _small_allgather(jnp.pad(ds_part, (0, c_rows * 128 - d)).reshape(c_rows, 128), "gather_c_ctx_grad")[1]
    ds_ctx = ds_ctx.reshape(-1)[:d]
    sig = jax.nn.sigmoid(c_ctx)
    grad_c_ctx = ds_ctx * (sig * (1 + c_ctx * (1 - sig)))

    big = [g_big["g_win_t"], g_big["g_wout"], g_big["g_sc_win"], g_big["g_sc_wout"]]
    axes = [1, 1, 2, 1]
    lens = [n_in_shard, na_ssd_w_out.shape[1], sc_w_in.shape[2], sc_w_out.shape[1]]
    px, py, pc = _pos()
    kself = 2 * px + py
    kothers = jnp.stack([2 * (1 - px) + py, 2 * px + (1 - py), 2 * (1 - px) + (1 - py)]).astype(jnp.int32)
    k4 = jnp.arange(4, dtype=jnp.int32)
    kme = kself.astype(jnp.int32).reshape(1)
    sends = [_pick_blocks(g, None, ax, ln, 2 * k4 + (1 - pc), k4, BF16, f"rs_cast{i}")
             for i, (g, ax, ln) in enumerate(zip(big, axes, lens))]
    recv_a = _rs_stage_a(sends, "reduce_scatter_d2d")
    part_a = [_pick_blocks(g, r, ax, ln, 2 * kme + pc, kme, F32, f"pair_sum_mine{i}")
              for i, (g, r, ax, ln) in enumerate(zip(big, recv_a, axes, lens))]
    part_s = [_pick_blocks(g, r, ax, ln, 2 * kothers + pc, kothers, BF16, f"pair_sum_send{i}")
              for i, (g, r, ax, ln) in enumerate(zip(big, recv_a, axes, lens))]
    part_b = _rs_stage_b(part_s, "reduce_scatter_ici")

    res = {}
    res["na_ssd_w_in"] = _adam_sharded(na_ssd_w_in, m_na_ssd_w_in, v_na_ssd_w_in, part_a[0], part_b[0], transposed=True, name="adam_w_in")
    res["na_ssd_w_out"] = _adam_sharded(na_ssd_w_out, m_na_ssd_w_out, v_na_ssd_w_out, part_a[1], part_b[1], transposed=False, name="adam_w_out")
    res["sc_w_in"] = _adam_sharded(sc_w_in, m_sc_w_in, v_sc_w_in, part_a[2], part_b[2], transposed=False, name="adam_sc_w_in")
    res["sc_w_out"] = _adam_sharded(sc_w_out, m_sc_w_out, v_sc_w_out, part_a[3], part_b[3], transposed=False, name="adam_sc_w_out")

    grads = dict(g_rep)
    grads["ssd_conv_w"] = lax.dynamic_slice_in_dim(g_rep["ssd_conv_w"], me * ssd_conv_w.shape[-1], ssd_conv_w.shape[-1], axis=2)
    grads["sc_conv_w"] = lax.dynamic_slice_in_dim(g_rep["sc_conv_w"], me * sc_conv_w.shape[-1], sc_conv_w.shape[-1], axis=2)
    grads["c_ctx"] = grad_c_ctx
    res["ada_w"] = (grad_ada_w, *_adam_tiled(ada_w, grad_ada_w, m_ada_w, v_ada_w, "adam_ada_w"))
    grads["ada_b"] = grad_ada_b
    rest = [n for n in WEIGHT_ORDER if n not in res]
    shapes = [weights[n].shape for n in rest]
    d_flat, m_flat, v_flat = _adam_small(_flatten([weights[n] for n in rest]), _flatten([grads[n] for n in rest]),
                                         _flatten([mom_m[n] for n in rest]), _flatten([mom_v[n] for n in rest]), "adam_small")
    for n, dd, mm, vv in zip(rest, _unflatten(d_flat, shapes), _unflatten(m_flat, shapes), _unflatten(v_flat, shapes)):
        res[n] = (grads[n], dd, mm, vv)

    loss = lax.psum(loss_local, ("x", "y", "c"))
    return (loss, g_x[None], *[res[n][0] for n in WEIGHT_ORDER], *[res[n][1] for n in WEIGHT_ORDER],
            *[res[n][2] for n in WEIGHT_ORDER], *[res[n][3] for n in WEIGHT_ORDER])
```

```python
import functools
import math
from types import SimpleNamespace

import jax
import jax.numpy as jnp
from jax import lax
from jax.experimental import pallas as pl
from jax.experimental.pallas import tpu as pltpu

F32 = jnp.float32
BF16 = jnp.bfloat16
U32 = jnp.uint32
HIGHEST = lax.Precision.HIGHEST
MESH = pl.DeviceIdType.MESH
ANY = pl.BlockSpec(memory_space=pl.ANY)
VMEM = pl.BlockSpec(memory_space=pltpu.VMEM)

NDEV = 8
DEPTH = 4
GRID_W = 64
EPS = 1e-6
NA_HEADS = 16
NA_HEAD_DIM = 128
NA_KH = 8
NA_KW = 16
SSD_HEAD_DIM = 64
SSD_GROUPS = 8
SSD_STATE = 128
SSD_CONV = 5
SSD_CHUNK = 128
SC_CONV = 3
ADAM_LR = 0.001
ADAM_B1 = 0.9
ADAM_B2 = 0.999
ADAM_EPS = 1e-08
ADAM_WD = 0.01
ADAM_STEP = 10
NEG = -1e30
VMEM_LIMIT = 56 * 1024 * 1024
PAD_TO = 512


def _pos():
    return lax.axis_index("x"), lax.axis_index("y"), lax.axis_index("c")


def _my_index():
    x, y, c = _pos()
    return 4 * x + 2 * y + c


def _pick(n, prefs):
    for p in prefs:
        if n % p == 0:
            return p
    return n


def _cparams(**kw):
    return pltpu.CompilerParams(vmem_limit_bytes=VMEM_LIMIT, **kw)


def _small_allgather(v, name):
    rows, lanes = v.shape

    def body(x_ref, out_ref, sum_ref, send_sems, recv_sems):
        x, y, c = _pos()
        me = 4 * x + 2 * y + c
        out_ref[me] = x_ref[...]
        copies = []
        for k in range(1, NDEV):
            peer = (1 - x if k & 4 else x, 1 - y if k & 2 else y, 1 - c if k & 1 else c)
            cp = pltpu.make_async_remote_copy(src_ref=x_ref, dst_ref=out_ref.at[me], send_sem=send_sems.at[k - 1],
                                              recv_sem=recv_sems.at[k - 1], device_id=peer, device_id_type=MESH)
            cp.start()
            copies.append(cp)
        for cp in copies:
            cp.wait()
        acc = out_ref[0]
        for j in range(1, NDEV):
            acc = acc + out_ref[j]
        sum_ref[...] = acc

    return pl.pallas_call(
        body, name=name,
        out_shape=(jax.ShapeDtypeStruct((NDEV, rows, lanes), v.dtype), jax.ShapeDtypeStruct((rows, lanes), v.dtype)),
        in_specs=[VMEM], out_specs=(VMEM, VMEM),
        scratch_shapes=[pltpu.SemaphoreType.DMA((NDEV - 1,)), pltpu.SemaphoreType.DMA((NDEV - 1,))],
        compiler_params=_cparams(),
    )(v)


def _window(ref, axis, idx, length):
    sl = [slice(None)] * len(ref.shape)
    sl[axis] = pl.ds(pl.multiple_of(idx * length, min(length & -length, 1024)), length)
    return ref.at[tuple(sl)]


def _big_allgather(shards, axes, name):
    n = len(shards)
    out_shapes = []
    for s, ax in zip(shards, axes):
        shp = list(s.shape)
        shp[ax] *= NDEV
        out_shapes.append(jax.ShapeDtypeStruct(tuple(shp), s.dtype))

    def body(*refs):
        xs, outs = refs[:n], refs[n:2 * n]
        send_sems, recv_sems, local_sems = refs[2 * n:]
        x, y, c = _pos()
        me, sib = (x, y, c), (x, y, 1 - c)
        chips = [(1 - x, y), (x, 1 - y), (1 - x, 1 - y)]

        def win(a, px, py, pc):
            return _window(outs[a], axes[a], 4 * px + 2 * py + pc, shards[a].shape[axes[a]])

        def copy(a, k, block, to, src=None):
            return pltpu.make_async_remote_copy(src_ref=win(a, *block) if src is None else src, dst_ref=win(a, *block),
                                                send_sem=send_sems.at[a * 7 + k], recv_sem=recv_sems.at[a * 7 + k],
                                                device_id=to, device_id_type=MESH)

        mine = [pltpu.make_async_copy(xs[a], win(a, *me), local_sems.at[a]) for a in range(n)]
        for cp in mine:
            cp.start()
        first = []
        for a in range(n):
            first.append(copy(a, 0, me, sib, src=xs[a]))
            first += [copy(a, 1 + j, me, (*chip, c), src=xs[a]) for j, chip in enumerate(chips)]
        for cp in first:
            cp.start()
        passed = []
        for j, chip in enumerate(chips):
            for a in range(n):
                copy(a, 1 + j, (*chip, c), me).wait_recv()
                cp = copy(a, 4 + j, (*chip, c), sib)
                cp.start()
                passed.append(cp)
        for a in range(n):
            copy(a, 0, sib, me).wait_recv()
            for j, chip in enumerate(chips):
                copy(a, 4 + j, (*chip, 1 - c), me).wait_recv()
        for cp in first + passed:
            cp.wait_send()
        for cp in mine:
            cp.wait()

    return pl.pallas_call(
        body, name=name, out_shape=tuple(out_shapes), in_specs=[ANY] * n, out_specs=tuple([ANY] * n),
        scratch_shapes=[pltpu.SemaphoreType.DMA((7 * n,)), pltpu.SemaphoreType.DMA((7 * n,)), pltpu.SemaphoreType.DMA((n,))],
        compiler_params=_cparams(),
    )(*shards)


def _rs_stage_a(sends, name):
    n = len(sends)
    out_shapes = [jax.ShapeDtypeStruct(s.shape, s.dtype) for s in sends]

    def body(*refs):
        ss, outs = refs[:n], refs[n:2 * n]
        send_sems, recv_sems = refs[2 * n:]
        x, y, c = _pos()
        sib = (x, y, 1 - c)
        copies = []
        for a in range(n):
            for k in range(4):
                cp = pltpu.make_async_remote_copy(src_ref=ss[a].at[k], dst_ref=outs[a].at[k], send_sem=send_sems.at[a * 4 + k],
                                                  recv_sem=recv_sems.at[a * 4 + k], device_id=sib, device_id_type=MESH)
                cp.start()
                copies.append(cp)
        for cp in copies:
            cp.wait()

    return pl.pallas_call(
        body, name=name, out_shape=tuple(out_shapes), in_specs=[ANY] * n, out_specs=tuple([ANY] * n),
        scratch_shapes=[pltpu.SemaphoreType.DMA((4 * n,)), pltpu.SemaphoreType.DMA((4 * n,))],
        compiler_params=_cparams(),
    )(*sends)


def _rs_stage_b(parts, name):
    n = len(parts)
    out_shapes = [jax.ShapeDtypeStruct(p.shape, p.dtype) for p in parts]

    def body(*refs):
        ps, outs = refs[:n], refs[n:2 * n]
        send_sems, recv_sems = refs[2 * n:]
        x, y, c = _pos()
        chips = [(1 - x, y), (x, 1 - y), (1 - x, 1 - y)]
        copies = []
        for a in range(n):
            for j, (px, py) in enumerate(chips):
                cp = pltpu.make_async_remote_copy(src_ref=ps[a].at[j], dst_ref=outs[a].at[j],
                                                  send_sem=send_sems.at[a * 3 + j], recv_sem=recv_sems.at[a * 3 + j],
                                                  device_id=(px, py, c), device_id_type=MESH)
                cp.start()
                copies.append(cp)
        for cp in copies:
            cp.wait()

    return pl.pallas_call(
        body, name=name, out_shape=tuple(out_shapes), in_specs=[ANY] * n, out_specs=tuple([ANY] * n),
        scratch_shapes=[pltpu.SemaphoreType.DMA((3 * n,)), pltpu.SemaphoreType.DMA((3 * n,))],
        compiler_params=_cparams(),
    )(*parts)


def _matmul(a, b, *, ta=False, tb=False, tm, tn, tk, name, b_off=(0, 0), b_extent=None):
    m, kdim = (a.shape[1], a.shape[0]) if ta else a.shape
    b_shape = b.shape if b_extent is None else b_extent
    n = b_shape[0] if tb else b_shape[1]
    assert (b_shape[1] if tb else b_shape[0]) == kdim, (a.shape, b_shape, ta, tb)
    assert m % tm == 0 and n % tn == 0 and kdim % tk == 0, (m, n, kdim, tm, tn, tk)
    o0, o1 = b_off
    nk = kdim // tk
    dn = (((0 if ta else 1,), (1 if tb else 0,)), ((), ()))

    def body(a_ref, b_ref, o_ref, acc_ref):
        k = pl.program_id(2)
        part = lax.dot_general(a_ref[...].astype(BF16), b_ref[...].astype(BF16), dn, preferred_element_type=F32)
        if nk == 1:
            o_ref[...] = part
        else:
            @pl.when(k == 0)
            def _():
                acc_ref[...] = part

            @pl.when(k > 0)
            def _():
                acc_ref[...] += part

            @pl.when(k == nk - 1)
            def _():
                o_ref[...] = acc_ref[...]

    a_spec = pl.BlockSpec((tk, tm), lambda i, j, k: (k, i)) if ta else pl.BlockSpec((tm, tk), lambda i, j, k: (i, k))
    b_spec = (pl.BlockSpec((tn, tk), lambda i, j, k: (j + o0, k + o1)) if tb
              else pl.BlockSpec((tk, tn), lambda i, j, k: (k + o0, j + o1)))
    acc_shape = (tm, tn) if nk > 1 else (8, 128)
    return pl.pallas_call(
        body, name=name, grid=(m // tm, n // tn, nk), out_shape=jax.ShapeDtypeStruct((m, n), F32),
        in_specs=[a_spec, b_spec], out_specs=pl.BlockSpec((tm, tn), lambda i, j, k: (i, j)),
        scratch_shapes=[pltpu.VMEM(acc_shape, F32)], compiler_params=_cparams(),
    )(a, b)


def _linear(a, w, gslot, *, w_is_nk, tiles, name):
    tm, tn, tk = tiles

    @jax.custom_vjp
    def lin(a, w, gslot):
        return _matmul(a, w, tb=w_is_nk, tm=tm, tn=tn, tk=tk, name=name + "_fwd")

    def fwd(a, w, gslot):
        return lin(a, w, gslot), (a, w)

    def bwd(res, g):
        a, w = res
        gb = g
        t, kdim = a.shape
        n = g.shape[1]
        tt = _pick(t, (768, 512, 256, 128))
        tkk = _pick(kdim, (2048, 1024, 512, 256, 128))
        tnn = _pick(n, (1024, 512, 256, 128))
        da = _matmul(gb, w, tb=not w_is_nk, tm=tt, tn=tkk, tk=tnn, name=name + "_bwd_a")
        tok = _pick(t, (1408, 768, 512, 256, 128))
        if w_is_nk:
            dw = _matmul(gb, a, ta=True, tm=tnn, tn=tkk, tk=tok, name=name + "_bwd_w")
        else:
            tkw = _pick(kdim, (1024, 512, 256, 128))
            tnw = _pick(n, (1024, 512, 256, 128))
            dw = _matmul(a, gb, ta=True, tm=tkw, tn=tnw, tk=tok, name=name + "_bwd_w")
        return da.astype(a.dtype), jnp.zeros_like(w), dw

    lin.defvjp(fwd, bwd)
    return lin(a, w, gslot)


def _pack_transposed(w, name):
    nl, kdim, r = w.shape
    half = kdim // 2
    tc = _pick(half, (256, 128))

    def body(lo_ref, hi_ref, o_ref):
        lo = pltpu.bitcast(lo_ref[0].astype(BF16).astype(F32).T, U32) >> 16
        hi = pltpu.bitcast(hi_ref[0].astype(BF16).astype(F32).T, U32) & jnp.uint32(0xFFFF0000)
        o_ref[0] = pltpu.bitcast(hi | lo, F32)

    nb = half // tc
    return pl.pallas_call(
        body, name=name, grid=(nl, nb), out_shape=jax.ShapeDtypeStruct((nl, r, half), F32),
        in_specs=[pl.BlockSpec((1, tc, r), lambda l, t: (l, t, 0)), pl.BlockSpec((1, tc, r), lambda l, t: (l, t + nb, 0))],
        out_specs=pl.BlockSpec((1, r, tc), lambda l, t: (l, 0, t)), compiler_params=_cparams(),
    )(w, w)


def _unpack(packed, n_pad, name):
    nl, n, half = packed.shape
    tr = math.gcd(math.gcd(n, n_pad - n), 64) if n_pad > n else _pick(n, (64, 32, 16))
    nin = n // tr

    def body(p_ref, o_ref):
        t = pl.program_id(1)

        @pl.when(t < nin)
        def _():
            u = pltpu.bitcast(p_ref[0], U32)
            o_ref[0, :, :half] = pltpu.bitcast(u << 16, F32).astype(BF16)
            o_ref[0, :, half:] = pltpu.bitcast(u & jnp.uint32(0xFFFF0000), F32).astype(BF16)

        @pl.when(t >= nin)
        def _():
            o_ref[...] = jnp.zeros_like(o_ref)

    return pl.pallas_call(
        body, name=name, grid=(nl, n_pad // tr), out_shape=jax.ShapeDtypeStruct((nl, n_pad, 2 * half), BF16),
        in_specs=[pl.BlockSpec((1, tr, half), lambda l, t: (l, jnp.minimum(t, nin - 1), 0))],
        out_specs=pl.BlockSpec((1, tr, 2 * half), lambda l, t: (l, t, 0)), compiler_params=_cparams(),
    )(packed)


def _cast_bf16(w, name):
    nl, r, c = w.shape
    tr = _pick(r, (512, 256, 128, 64, 32, 16))

    def body(w_ref, o_ref):
        o_ref[...] = w_ref[...].astype(BF16)

    return pl.pallas_call(
        body, name=name, grid=(nl, r // tr), out_shape=jax.ShapeDtypeStruct(w.shape, BF16),
        in_specs=[pl.BlockSpec((1, tr, c), lambda l, t: (l, t, 0))], out_specs=pl.BlockSpec((1, tr, c), lambda l, t: (l, t, 0)),
        compiler_params=_cparams(),
    )(w)


NA_ROWS_PER_STEP = 4


def _row_block(i):
    return slice(i * GRID_W, (i + 1) * GRID_W)


def _na_probs(qs, kb_ref, b_ref, step, rows, nctx):
    scale = NA_HEAD_DIM ** -0.5
    nt = (((1,), (1,)), ((), ()))
    kc = kb_ref[0:nctx, :]
    kws, starts, offs, s1 = [], [], [], []
    for i in range(NA_ROWS_PER_STEP):
        r = step * NA_ROWS_PER_STEP + i
        rs = jnp.clip(r - NA_KH // 2, 0, rows - NA_KH)
        offs.append(rs - r + NA_KH - 1)
        starts.append(pl.multiple_of(nctx + rs * GRID_W, GRID_W))
        kws.append(kb_ref[pl.ds(starts[i], NA_KH * GRID_W), :])
        s1.append(lax.dot_general(qs[_row_block(i)], kws[i], nt, preferred_element_type=F32) * scale + b_ref[0, offs[i]])
    s1 = jnp.concatenate(s1, axis=0)
    s2 = lax.dot_general(qs, kc, nt, preferred_element_type=F32) * scale
    m = jnp.maximum(jnp.max(s1, axis=-1, keepdims=True), jnp.max(s2, axis=-1, keepdims=True))
    e1 = jnp.exp(s1 - m)
    e2 = jnp.exp(s2 - m)
    inv = 1.0 / (jnp.sum(e1, axis=-1, keepdims=True) + jnp.sum(e2, axis=-1, keepdims=True))
    return kc, kws, starts, offs, e1 * inv, e2 * inv


def _ctx_probs(qs, kc):
    s = lax.dot_general(qs, kc, (((1,), (1,)), ((), ())), preferred_element_type=F32) * NA_HEAD_DIM ** -0.5
    e = jnp.exp(s - jnp.max(s, axis=-1, keepdims=True))
    return e * (1.0 / jnp.sum(e, axis=-1, keepdims=True))


def _na_geometry(t):
    blk = NA_ROWS_PER_STEP * GRID_W
    rows = (t - blk) // GRID_W
    assert rows % NA_ROWS_PER_STEP == 0 and blk + rows * GRID_W == t, (t, blk)
    return blk, rows


def _na_forward(name, update_ctx, q, k, v, biasw):
    t, width = q.shape
    dh, win = NA_HEAD_DIM, NA_KH * GRID_W
    blk, rows = _na_geometry(t)

    def body(q_ref, k_ref, v_ref, b_ref, o_ref, kb_ref, vb_ref):
        j = pl.program_id(1)
        qs = q_ref[...].astype(BF16)

        @pl.when(j == 0)
        def _():
            kb_ref[...] = k_ref[...].astype(BF16)
            vb_ref[...] = v_ref[...].astype(BF16)
            if update_ctx:
                p = _ctx_probs(qs, kb_ref[0:blk, :])
                o_ref[...] = jnp.dot(p.astype(BF16), vb_ref[0:blk, :], preferred_element_type=F32)
            else:
                o_ref[...] = jnp.zeros_like(o_ref)

        @pl.when(j > 0)
        def _():
            _, _, starts, _, p1, p2 = _na_probs(qs, kb_ref, b_ref, j - 1, rows, blk)
            p1b = p1.astype(BF16)
            o1 = [jnp.dot(p1b[_row_block(i)], vb_ref[pl.ds(starts[i], win), :], preferred_element_type=F32)
                  for i in range(NA_ROWS_PER_STEP)]
            o_ref[...] = jnp.concatenate(o1, axis=0) + jnp.dot(p2.astype(BF16), vb_ref[0:blk, :], preferred_element_type=F32)

    row_spec = pl.BlockSpec((blk, dh), lambda h, j: (j, h))
    all_spec = pl.BlockSpec((t, dh), lambda h, j: (0, h))
    b_spec = pl.BlockSpec((1, NA_KH, GRID_W, win), lambda h, j: (h, 0, 0, 0))
    return pl.pallas_call(
        body, name=name, grid=(NA_HEADS, 1 + rows // NA_ROWS_PER_STEP), out_shape=jax.ShapeDtypeStruct((t, width), F32),
        in_specs=[row_spec, all_spec, all_spec, b_spec], out_specs=row_spec,
        scratch_shapes=[pltpu.VMEM((t, dh), BF16), pltpu.VMEM((t, dh), BF16)], compiler_params=_cparams(),
    )(q, k, v, biasw)


def _na_backward(name, update_ctx, q, k, v, biasw, do):
    t, width = q.shape
    dh, win = NA_HEAD_DIM, NA_KH * GRID_W
    blk, rows = _na_geometry(t)
    scale = dh ** -0.5
    nt = (((1,), (1,)), ((), ()))
    tn = (((0,), (0,)), ((), ()))

    def body(q_ref, k_ref, v_ref, b_ref, do_ref, dq_ref, dk_ref, dv_ref, db_ref, kb_ref, vb_ref):
        j = pl.program_id(1)
        qs = q_ref[...].astype(BF16)
        dob = do_ref[...].astype(BF16)
        ctx = slice(0, blk)

        @pl.when(j == 0)
        def _():
            kb_ref[...] = k_ref[...].astype(BF16)
            vb_ref[...] = v_ref[...].astype(BF16)
            dk_ref[...] = jnp.zeros_like(dk_ref)
            dv_ref[...] = jnp.zeros_like(dv_ref)
            db_ref[...] = jnp.zeros_like(db_ref)
            if update_ctx:
                kc, vc = kb_ref[ctx, :], vb_ref[ctx, :]
                p = _ctx_probs(qs, kc)
                dp = lax.dot_general(dob, vc, nt, preferred_element_type=F32)
                ds = p * (dp - jnp.sum(dp * p, axis=-1, keepdims=True))
                dsb = (ds * scale).astype(BF16)
                dq_ref[...] = jnp.dot(dsb, kc, preferred_element_type=F32)
                dv_ref[ctx, :] += lax.dot_general(p.astype(BF16), dob, tn, preferred_element_type=F32)
                dk_ref[ctx, :] += lax.dot_general(dsb, qs, tn, preferred_element_type=F32)
            else:
                dq_ref[...] = jnp.zeros_like(dq_ref)

        @pl.when(j > 0)
        def _():
            nr = range(NA_ROWS_PER_STEP)
            kc, kws, starts, offs, p1, p2 = _na_probs(qs, kb_ref, b_ref, j - 1, rows, blk)
            vc = vb_ref[ctx, :]
            p1b, p2b = p1.astype(BF16), p2.astype(BF16)
            dp1 = jnp.concatenate([lax.dot_general(dob[_row_block(i)], vb_ref[pl.ds(starts[i], win), :], nt,
                                                   preferred_element_type=F32) for i in nr], axis=0)
            dp2 = lax.dot_general(dob, vc, nt, preferred_element_type=F32)
            delta = jnp.sum(dp1 * p1, axis=-1, keepdims=True) + jnp.sum(dp2 * p2, axis=-1, keepdims=True)
            ds1 = p1 * (dp1 - delta)
            ds2 = p2 * (dp2 - delta)
            ds1b = (ds1 * scale).astype(BF16)
            ds2b = (ds2 * scale).astype(BF16)
            dq1 = [jnp.dot(ds1b[_row_block(i)], kws[i], preferred_element_type=F32) for i in nr]
            dq_ref[...] = jnp.concatenate(dq1, axis=0) + jnp.dot(ds2b, kc, preferred_element_type=F32)
            dv_ref[ctx, :] += lax.dot_general(p2b, dob, tn, preferred_element_type=F32)
            dk_ref[ctx, :] += lax.dot_general(ds2b, qs, tn, preferred_element_type=F32)
            for i in nr:
                sl = pl.ds(starts[i], win)
                db_ref[0, offs[i]] += ds1[_row_block(i)]
                dv_ref[sl, :] += lax.dot_general(p1b[_row_block(i)], dob[_row_block(i)], tn, preferred_element_type=F32)
                dk_ref[sl, :] += lax.dot_general(ds1b[_row_block(i)], qs[_row_block(i)], tn, preferred_element_type=F32)

    row_spec = pl.BlockSpec((blk, dh), lambda h, j: (j, h))
    all_spec = pl.BlockSpec((t, dh), lambda h, j: (0, h))
    b_spec = pl.BlockSpec((1, NA_KH, GRID_W, win), lambda h, j: (h, 0, 0, 0))
    full = jax.ShapeDtypeStruct((t, width), F32)
    return pl.pallas_call(
        body, name=name, grid=(NA_HEADS, 1 + rows // NA_ROWS_PER_STEP),
        out_shape=(full, full, full, jax.ShapeDtypeStruct(biasw.shape, F32)),
        in_specs=[row_spec, all_spec, all_spec, b_spec, row_spec], out_specs=(row_spec, all_spec, all_spec, b_spec),
        scratch_shapes=[pltpu.VMEM((t, dh), BF16), pltpu.VMEM((t, dh), BF16)], compiler_params=_cparams(),
    )(q, k, v, biasw, do)


def _na_attention(e, update_ctx, q, k, v, biasw):
    @jax.custom_vjp
    def attn(q, k, v, biasw):
        return _na_forward(f"na_fwd{e}", update_ctx, q, k, v, biasw)

    attn.defvjp(lambda q, k, v, biasw: (_na_forward(f"na_fwd{e}", update_ctx, q, k, v, biasw), (q, k, v, biasw)),
                lambda res, do: _na_backward(f"na_bwd{e}", update_ctx, *res, do))
    return attn(q, k, v, biasw)


def _bias_windows(rpb):
    col = jnp.arange(GRID_W)
    dc = jnp.clip(col[None, :] - col[:, None], -(NA_KW - 1), NA_KW - 1) + NA_KW - 1
    onehot = (dc[None] == jnp.arange(2 * NA_KW - 1)[:, None, None]).astype(F32)
    tq = jnp.einsum("hrd,dqk->hrqk", rpb, onehot, precision=HIGHEST)
    col_start = jnp.clip(col - NA_KW // 2, 0, GRID_W - NA_KW)
    in_win = (col[None, :] >= col_start[:, None]) & (col[None, :] < col_start[:, None] + NA_KW)
    wins = jnp.stack([tq[:, off:off + NA_KH] for off in range(NA_KH)], axis=1)
    wins = jnp.where(in_win[None, None, None], wins, NEG)
    return wins.transpose(0, 1, 3, 2, 4).reshape(rpb.shape[0], NA_KH, GRID_W, NA_KH * GRID_W)


def _chunk_cumsum(u, reverse, name):
    t, nh = u.shape
    ln = SSD_CHUNK

    def body(u_ref, o_ref):
        row = lax.broadcasted_iota(jnp.int32, (ln, ln), 0)
        col = lax.broadcasted_iota(jnp.int32, (ln, ln), 1)
        uu = u_ref[...]
        down = jnp.dot((col <= row).astype(F32), uu, precision=HIGHEST, preferred_element_type=F32)
        up = jnp.dot((col >= row).astype(F32), uu, precision=HIGHEST, preferred_element_type=F32)
        first = lax.broadcasted_iota(jnp.int32, (ln, nh), 1) < nh // 2
        o_ref[...] = jnp.where(first, up, down) if reverse else jnp.where(first, down, up)

    spec = pl.BlockSpec((ln, nh), lambda i: (i, 0))
    return pl.pallas_call(body, name=name, grid=(t // ln,), out_shape=jax.ShapeDtypeStruct(u.shape, F32),
                          in_specs=[spec], out_specs=spec, compiler_params=_cparams())(u)


def _ssd_cumsum(e, u):
    @jax.custom_vjp
    def cs(u):
        return _chunk_cumsum(u, False, f"ssd_cumsum{e}")

    cs.defvjp(lambda u: (_chunk_cumsum(u, False, f"ssd_cumsum{e}"), None),
              lambda _, g: (_chunk_cumsum(g, True, f"ssd_cumsum_bwd{e}"),))
    return cs(u)


def _mxu_dots():
    c_nn, c_nt, c_tn = (((1,), (0,)), ((), ())), (((1,), (1,)), ((), ())), (((0,), (0,)), ((), ()))

    def dot(a, b, dn):
        return lax.dot_general(a.astype(BF16), b.astype(BF16), dn, preferred_element_type=F32)

    def make(dn, dn_da, a_first, dn_db, b_first):
        @jax.custom_vjp
        def f(a, b):
            return dot(a, b, dn)

        def bwd(res, g):
            a, b = res
            da = dot(g, b, dn_da) if a_first else dot(b, g, dn_da)
            db = dot(g, a, dn_db) if b_first else dot(a, g, dn_db)
            return da, db

        f.defvjp(lambda a, b: (dot(a, b, dn), (a, b)), bwd)
        return f

    nn = make(c_nn, c_nt, True, c_tn, False)
    nt = make(c_nt, c_nn, True, c_tn, True)
    tn = make(c_tn, c_nt, False, c_nn, False)
    return nn, nt, tn


def _ssd_step(h, x, bm, cm, dt, dt_t, cs, cs_t, *, direction, hpg, pdim):
    ln = x.shape[0]
    hp = hpg * pdim
    nn, nt, tn = _mxu_dots()
    row = lax.broadcasted_iota(jnp.int32, (ln, ln), 0)
    colm = lax.broadcasted_iota(jnp.int32, (ln, ln), 1)
    valid = (colm - row) * (1 - 2 * direction) <= 0
    last = (ln - 1) * (1 - direction)
    tot = jnp.sum(jnp.where(lax.broadcasted_iota(jnp.int32, (ln, hpg), 0) == last, cs, 0.0), axis=0, keepdims=True)
    cbm = nt(cm, bm)
    lane_head = lax.broadcasted_iota(jnp.int32, (1, hp), 1) // pdim
    sub_head = lax.broadcasted_iota(jnp.int32, (hp, 1), 0) // pdim
    y = jnp.zeros((ln, hp), F32)
    es = jnp.zeros((ln, hp), F32)
    we = jnp.zeros((ln, hp), F32)
    dend = jnp.zeros((hp, 1), F32)
    for r in range(hpg):
        cc = cs[:, r:r + 1]
        cr = cs_t[r:r + 1, :]
        decay = jnp.exp(jnp.where(valid, cc - cr, NEG))
        mask = lane_head == r
        y = y + nn(cbm * decay * dt_t[r:r + 1, :], jnp.where(mask, x, 0.0))
        es = es + jnp.where(mask, jnp.exp(cc), 0.0)
        we = we + jnp.where(mask, jnp.exp(tot[:, r:r + 1] - cc) * dt[:, r:r + 1], 0.0)
        dend = dend + jnp.where(sub_head == r, jnp.exp(tot[:, r:r + 1]), 0.0)
    y = y + es * nt(cm, h)
    h_new = h * dend + tn(x * we, bm)
    return y, h_new


def _ssd_chunk_of(d, s, ncc, nc):
    return jnp.where(d == 0, s, jnp.where(s < ncc, ncc - 1 - s, nc - 1 - s + ncc))


def _ssd_specs(cfg, step_of):
    hp, n, ln, hpg = cfg.hpg * SSD_HEAD_DIM, SSD_STATE, SSD_CHUNK, cfg.hpg
    ncc, nc = cfg.nctx // ln, cfg.t // ln

    def ch(d, s):
        return _ssd_chunk_of(d, step_of(s), ncc, nc)

    return dict(
        x=pl.BlockSpec((ln, hp), lambda d, g, s: (ch(d, s), g)),
        bc=pl.BlockSpec((ln, n), lambda d, g, s: (ch(d, s), g)),
        dt=pl.BlockSpec((1, 1, ln, hpg), lambda d, g, s: (d, g, ch(d, s), 0)),
        dt_t=pl.BlockSpec((1, 1, 8, ln), lambda d, g, s: (d, g, 0, ch(d, s))),
        y=pl.BlockSpec((1, ln, hp), lambda d, g, s: (d, ch(d, s), g)),
        bc2=pl.BlockSpec((1, ln, n), lambda d, g, s: (d, ch(d, s), g)),
        h=pl.BlockSpec((1, 1, 1, hp, n), lambda d, g, s: (d, g, step_of(s), 0, 0)),
    )


def _ssd_forward(cfg, name, xs, bm, cm, dt, dt_t, dta, dta_t):
    hp, n, hpg = cfg.hpg * SSD_HEAD_DIM, SSD_STATE, cfg.hpg
    nc = cfg.t // SSD_CHUNK
    sp = _ssd_specs(cfg, lambda s: s)

    def body(x_ref, b_ref, c_ref, dt_ref, dtt_ref, dta_ref, dtat_ref, y_ref, hs_ref, h_ref):
        d, s = pl.program_id(0), pl.program_id(2)

        @pl.when(s == 0)
        def _():
            h_ref[...] = jnp.zeros_like(h_ref)

        h = h_ref[...]
        hs_ref[0, 0, 0] = h
        y, h_new = _ssd_step(h, x_ref[...], b_ref[...], c_ref[...], dt_ref[0, 0], dtt_ref[0, 0], dta_ref[0, 0],
                             dtat_ref[0, 0], direction=d, hpg=hpg, pdim=SSD_HEAD_DIM)
        y_ref[0] = y
        h_ref[...] = h_new

    return pl.pallas_call(
        body, name=name, grid=(2, SSD_GROUPS, nc),
        out_shape=(jax.ShapeDtypeStruct((2, cfg.t, cfg.ssd_width), F32),
                   jax.ShapeDtypeStruct((2, SSD_GROUPS, nc, hp, n), F32)),
        in_specs=[sp["x"], sp["bc"], sp["bc"], sp["dt"], sp["dt_t"], sp["dt"], sp["dt_t"]],
        out_specs=(sp["y"], sp["h"]), scratch_shapes=[pltpu.VMEM((hp, n), F32)], compiler_params=_cparams(),
    )(xs, bm, cm, dt, dt_t, dta, dta_t)


def _ssd_backward(cfg, name, xs, bm, cm, dt, dt_t, dta, dta_t, hsave, dy):
    hp, n, hpg = cfg.hpg * SSD_HEAD_DIM, SSD_STATE, cfg.hpg
    nc = cfg.t // SSD_CHUNK
    sp = _ssd_specs(cfg, lambda s: nc - 1 - s)

    def body(x_ref, b_ref, c_ref, dt_ref, dtt_ref, dta_ref, dtat_ref, hs_ref, dy_ref,
             dx_ref, db_ref, dc_ref, ddt_ref, ddtt_ref, ddta_ref, ddtat_ref, dh_ref):
        d, s = pl.program_id(0), pl.program_id(2)

        @pl.when(s == 0)
        def _():
            dh_ref[...] = jnp.zeros_like(dh_ref)

        step = functools.partial(_ssd_step, direction=d, hpg=hpg, pdim=SSD_HEAD_DIM)
        _, vjp = jax.vjp(step, hs_ref[0, 0, 0], x_ref[...], b_ref[...], c_ref[...], dt_ref[0, 0], dtt_ref[0, 0],
                         dta_ref[0, 0], dtat_ref[0, 0])
        dh, dx, db, dc, ddt, ddtt, ddta, ddtat = vjp((dy_ref[0], dh_ref[...]))
        dh_ref[...] = dh
        dx_ref[0] = dx
        db_ref[0] = db
        dc_ref[0] = dc
        ddt_ref[0, 0] = ddt
        ddtt_ref[0, 0] = ddtt
        ddta_ref[0, 0] = ddta
        ddtat_ref[0, 0] = ddtat

    gn = SSD_GROUPS * n
    return pl.pallas_call(
        body, name=name, grid=(2, SSD_GROUPS, nc),
        out_shape=(jax.ShapeDtypeStruct((2, cfg.t, cfg.ssd_width), F32), jax.ShapeDtypeStruct((2, cfg.t, gn), F32),
                   jax.ShapeDtypeStruct((2, cfg.t, gn), F32), jax.ShapeDtypeStruct(dt.shape, F32),
                   jax.ShapeDtypeStruct(dt_t.shape, F32), jax.ShapeDtypeStruct(dt.shape, F32),
                   jax.ShapeDtypeStruct(dt_t.shape, F32)),
        in_specs=[sp["x"], sp["bc"], sp["bc"], sp["dt"], sp["dt_t"], sp["dt"], sp["dt_t"], sp["h"], sp["y"]],
        out_specs=(sp["y"], sp["bc2"], sp["bc2"], sp["dt"], sp["dt_t"], sp["dt"], sp["dt_t"]),
        scratch_shapes=[pltpu.VMEM((hp, n), F32)], compiler_params=_cparams(),
    )(xs, bm, cm, dt, dt_t, dta, dta_t, hsave, dy)


def _ssd_scan(cfg, e, xs, bm, cm, dt, dt_t, dta, dta_t):
    @jax.custom_vjp
    def scan(xs, bm, cm, dt, dt_t, dta, dta_t):
        return _ssd_forward(cfg, f"ssd_fwd{e}", xs, bm, cm, dt, dt_t, dta, dta_t)[0]

    def fwd(xs, bm, cm, dt, dt_t, dta, dta_t):
        y, hsave = _ssd_forward(cfg, f"ssd_fwd{e}", xs, bm, cm, dt, dt_t, dta, dta_t)
        return y, (xs, bm, cm, dt, dt_t, dta, dta_t, hsave)

    def bwd(res, dy):
        dx, db, dc, ddt, ddtt, ddta, ddtat = _ssd_backward(cfg, f"ssd_bwd{e}", *res, dy)
        return dx[0] + dx[1], db[0] + db[1], dc[0] + dc[1], ddt, ddtt, ddta, ddtat

    scan.defvjp(fwd, bwd)
    return scan(xs, bm, cm, dt, dt_t, dta, dta_t)


def _adam_math(w, g, m, v):
    m2 = ADAM_B1 * m + (1.0 - ADAM_B1) * g
    v2 = ADAM_B2 * v + (1.0 - ADAM_B2) * (g * g)
    m_hat = m2 / (1.0 - ADAM_B1 ** ADAM_STEP)
    v_hat = v2 / (1.0 - ADAM_B2 ** ADAM_STEP)
    delta = -ADAM_LR * (m_hat / (jnp.sqrt(v_hat) + ADAM_EPS) + ADAM_WD * w)
    return delta, m2, v2


def _adam_small(w, g, m, v, name):
    def body(w_ref, g_ref, m_ref, v_ref, d_ref, m2_ref, v2_ref):
        d_ref[...], m2_ref[...], v2_ref[...] = _adam_math(w_ref[...], g_ref[...], m_ref[...], v_ref[...])

    shp = jax.ShapeDtypeStruct(w.shape, F32)
    return pl.pallas_call(body, name=name, out_shape=(shp, shp, shp), in_specs=[VMEM] * 4, out_specs=(VMEM, VMEM, VMEM),
                          compiler_params=_cparams())(w, g, m, v)


def _adam_tiled(w, g, m, v, name):
    nl, r, c = w.shape
    tr = _pick(r, (256, 128, 64, 32, 16, 8))
    spec = pl.BlockSpec((1, tr, c), lambda l, t: (l, t, 0))

    def body(w_ref, g_ref, m_ref, v_ref, d_ref, m2_ref, v2_ref):
        d_ref[...], m2_ref[...], v2_ref[...] = _adam_math(w_ref[...], g_ref[...], m_ref[...], v_ref[...])

    shp = jax.ShapeDtypeStruct(w.shape, F32)
    return pl.pallas_call(body, name=name, grid=(nl, r // tr), out_shape=(shp, shp, shp), in_specs=[spec] * 4,
                          out_specs=(spec, spec, spec), compiler_params=_cparams())(w, g, m, v)


def _adam_sharded(w, m, v, part_a, part_b, *, transposed, name):
    nl, r, c = w.shape
    tr = _pick(r, (256, 128)) if transposed else _pick(r, (256, 128, 64, 32, 16))
    w_spec = pl.BlockSpec((1, tr, c), lambda l, t: (l, t, 0))
    if transposed:
        pa_spec = pl.BlockSpec((1, 1, c, tr), lambda l, t: (0, l, 0, t))
        pb_spec = pl.BlockSpec((3, 1, c, tr), lambda l, t: (0, l, 0, t))
    else:
        pa_spec = pl.BlockSpec((1, 1, tr, c), lambda l, t: (0, l, t, 0))
        pb_spec = pl.BlockSpec((3, 1, tr, c), lambda l, t: (0, l, t, 0))

    def body(w_ref, m_ref, v_ref, pa_ref, pb_ref, g_ref, d_ref, m2_ref, v2_ref):
        g = pa_ref[0, 0] + pb_ref[0, 0].astype(F32) + pb_ref[1, 0].astype(F32) + pb_ref[2, 0].astype(F32)
        if transposed:
            g = g.T
        g_ref[0] = g
        d_ref[0], m2_ref[0], v2_ref[0] = _adam_math(w_ref[0], g, m_ref[0], v_ref[0])

    shp = jax.ShapeDtypeStruct(w.shape, F32)
    return pl.pallas_call(
        body, name=name, grid=(nl, r // tr), out_shape=(shp, shp, shp, shp),
        in_specs=[w_spec, w_spec, w_spec, pa_spec, pb_spec], out_specs=(w_spec, w_spec, w_spec, w_spec),
        compiler_params=_cparams(),
    )(w, m, v, part_a, part_b)


def _pick_blocks(g, recv, axis, length, mode, out_dtype, name):
    nl = g.shape[0]
    n = {"cast": 4, "mine": 1, "send": 3}[mode]
    pos = [p.astype(jnp.int32).reshape(1) for p in _pos()]

    def chip(i, x, y, c):
        if mode == "cast":
            return i
        if mode == "mine":
            return 2 * x[0] + y[0]
        return 2 * jnp.where(i == 1, x[0], 1 - x[0]) + jnp.where(i == 0, y[0], 1 - y[0])

    def gb(i, x, y, c):
        return 2 * chip(i, x, y, c) + (1 - c[0] if mode == "cast" else c[0])

    blk_shape = list(g.shape)
    blk_shape[axis] = length
    if axis == 1:
        cols = g.shape[2]
        tc = _pick(cols, (512, 256, 128))
        g_spec = pl.BlockSpec((1, length, tc), lambda i, l, t, x, y, c: (l, gb(i, x, y, c), t))
        r_spec = pl.BlockSpec((1, 1, length, tc), lambda i, l, t, x, y, c: (chip(i, x, y, c), l, 0, t))
        o_spec = pl.BlockSpec((1, 1, length, tc), lambda i, l, t, x, y, c: (i, l, 0, t))
        grid = (n, nl, cols // tc)
    else:
        rows = g.shape[1]
        tr = _pick(rows, (512, 256, 128, 64, 32, 16))
        g_spec = pl.BlockSpec((1, tr, length), lambda i, l, t, x, y, c: (l, t, gb(i, x, y, c)))
        r_spec = pl.BlockSpec((1, 1, tr, length), lambda i, l, t, x, y, c: (chip(i, x, y, c), l, t, 0))
        o_spec = pl.BlockSpec((1, 1, tr, length), lambda i, l, t, x, y, c: (i, l, t, 0))
        grid = (n, nl, rows // tr)

    if recv is None:
        def body(x_ref, y_ref, c_ref, g_ref, o_ref):
            o_ref[0] = g_ref[...].astype(out_dtype)
        in_specs, args = [g_spec], (g,)
    else:
        def body(x_ref, y_ref, c_ref, g_ref, r_ref, o_ref):
            o_ref[0] = (g_ref[...] + r_ref[0].astype(F32)).astype(out_dtype)
        in_specs, args = [g_spec, r_spec], (g, recv)

    return pl.pallas_call(
        body, name=name, out_shape=jax.ShapeDtypeStruct((n, *blk_shape), out_dtype),
        grid_spec=pltpu.PrefetchScalarGridSpec(num_scalar_prefetch=3, grid=grid, in_specs=in_specs, out_specs=o_spec),
        compiler_params=_cparams(),
    )(*pos, *args)


def _flatten(arrs):
    flat = jnp.concatenate([a.reshape(-1).astype(F32) for a in arrs])
    n = flat.shape[0]
    n_pad = -(-n // 1024) * 1024
    return jnp.pad(flat, (0, n_pad - n)).reshape(n_pad // 128, 128)


def _unflatten(buf, shapes):
    flat = buf.reshape(-1)
    out, o = [], 0
    for s in shapes:
        n = math.prod(s)
        out.append(flat[o:o + n].reshape(s))
        o += n
    return out


def _rowwise(name, fn, rows, seg, shared, out_cols, out_dtypes, tile, nct):
    t = rows[0].shape[0]
    nr, ns, nsh, no = len(rows), len(seg), len(shared), len(out_cols)
    n_in = nr + ns + nsh

    def row_spec(c):
        return pl.BlockSpec((tile, c), lambda i: (i, 0))

    def seg_spec(c):
        return pl.BlockSpec((1, 1, c), lambda i: (jnp.where(i < nct, 0, 1), 0, 0))

    def whole_spec(shape):
        return pl.BlockSpec(shape, lambda i: (0, 0))

    in_specs = ([row_spec(r.shape[1]) for r in rows] + [seg_spec(s.shape[1]) for s in seg]
                + [whole_spec(s.shape) for s in shared])
    out_shapes = tuple(jax.ShapeDtypeStruct((t, c), dt) for c, dt in zip(out_cols, out_dtypes))
    out_specs = tuple(row_spec(c) for c in out_cols)

    def load(refs):
        return [r[0] if nr <= j < nr + ns else r[...] for j, r in enumerate(refs[:n_in])]

    def lift(args):
        return [a[:, None, :] if nr <= j < nr + ns else a for j, a in enumerate(args)]

    def forward(*args):
        def body(*refs):
            outs = fn(*load(refs))
            for o_ref, o in zip(refs[n_in:], outs):
                o_ref[...] = o

        return pl.pallas_call(body, name=name + "_fwd", grid=(t // tile,), out_shape=out_shapes, in_specs=in_specs,
                              out_specs=out_specs, compiler_params=_cparams())(*lift(args))

    def backward(args, cts):
        def body(*refs):
            i = pl.program_id(0)
            ct = tuple(r[...] for r in refs[n_in:n_in + no])
            d_refs = refs[n_in + no:]
            _, vjp = jax.vjp(fn, *load(refs))
            grads = vjp(ct)
            for ref, g in zip(d_refs[:nr], grads[:nr]):
                ref[...] = g
            first_seg = jnp.logical_or(i == 0, i == nct)
            for j in range(nr, n_in):
                ref, g = d_refs[j], grads[j]
                first = first_seg if j < nr + ns else i == 0
                g = g[None] if j < nr + ns else g

                @pl.when(first)
                def _(ref=ref, g=g):
                    ref[...] = g

                @pl.when(jnp.logical_not(first))
                def _(ref=ref, g=g):
                    ref[...] += g

        largs = lift(args)
        d_shapes = tuple(jax.ShapeDtypeStruct(a.shape, F32) for a in largs)
        ct_specs = [row_spec(c) for c in out_cols]
        outs = pl.pallas_call(body, name=name + "_bwd", grid=(t // tile,), out_shape=d_shapes,
                              in_specs=in_specs + ct_specs, out_specs=tuple(in_specs), compiler_params=_cparams())(*largs, *cts)
        return [o[:, 0, :] if nr <= j < nr + ns else o for j, o in enumerate(outs)]

    @jax.custom_vjp
    def prim(*args):
        return tuple(forward(*args))

    prim.defvjp(lambda *args: (tuple(forward(*args)), args), lambda args, cts: tuple(backward(args, cts)))
    return prim(*rows, *seg, *shared)


def _silu(x):
    return x * (1.0 / (1.0 + jnp.exp(-x)))


def _softplus(x):
    return jnp.maximum(x, 0.0) + jnp.log(1.0 + jnp.exp(-jnp.abs(x)))


def _dwconv(name, x, w, b, act, nctx):
    t, c = x.shape
    kk = w.shape[0]
    half = kk // 2
    tile = 128
    tc = _pick(c, (1024, 512, 256, 128))
    nt, nct, hb = t // tile, nctx // tile, tile // 8
    cur = pl.BlockSpec((tile, tc), lambda j, i: (i, j))
    prev = pl.BlockSpec((8, tc), lambda j, i: (jnp.maximum(i * hb - 1, 0), j))
    nxt = pl.BlockSpec((8, tc), lambda j, i: (jnp.minimum((i + 1) * hb, t // 8 - 1), j))
    w_spec = pl.BlockSpec((kk, tc), lambda j, i: (0, j))
    b_spec = pl.BlockSpec((1, tc), lambda j, i: (0, j))
    grid = (c // tc, nt)

    def extended(cur_ref, prev_ref, next_ref, i):
        has_prev = jnp.logical_and(i != 0, i != nct)
        has_next = jnp.logical_and(i != nct - 1, i != nt - 1)
        return jnp.concatenate([jnp.where(has_prev, prev_ref[...], 0.0), cur_ref[...],
                                jnp.where(has_next, next_ref[...], 0.0)], axis=0)

    def taps(ext, w_ref, lo, n):
        acc = None
        for k in range(kk):
            term = w_ref[k:k + 1, :] * ext[lo + k - half:lo + k - half + n]
            acc = term if acc is None else acc + term
        return acc

    def forward(x, w, b):
        def body(x_ref, xp_ref, xn_ref, w_ref, b_ref, o_ref):
            ext = extended(x_ref, xp_ref, xn_ref, pl.program_id(1))
            y = taps(ext, w_ref, 8, tile) + b_ref[...]
            o_ref[...] = _silu(y) if act else y

        return pl.pallas_call(body, name=name + "_fwd", grid=grid, out_shape=jax.ShapeDtypeStruct((t, c), F32),
                              in_specs=[cur, prev, nxt, w_spec, b_spec], out_specs=cur, compiler_params=_cparams())(x, x, x, w, b)

    def backward(x, w, b, g):
        m = tile + 2 * half

        def body(x_ref, xp_ref, xn_ref, g_ref, gp_ref, gn_ref, w_ref, b_ref, dx_ref, dw_ref, db_ref):
            i = pl.program_id(1)
            xe = extended(x_ref, xp_ref, xn_ref, i)
            dpre = extended(g_ref, gp_ref, gn_ref, i)[8 - half:8 - half + m]
            if act:
                pre = taps(xe, w_ref, 8 - half, m) + b_ref[...]
                sg = 1.0 / (1.0 + jnp.exp(-pre))
                dpre = dpre * (sg * (1.0 + pre * (1.0 - sg)))
            acc = None
            for k in range(kk):
                term = w_ref[k:k + 1, :] * dpre[2 * half - k:2 * half - k + tile]
                acc = term if acc is None else acc + term
            dx_ref[...] = acc
            dcur = dpre[half:half + tile]
            dw = jnp.concatenate([jnp.sum(dcur * xe[8 + k - half:8 + k - half + tile], axis=0, keepdims=True)
                                  for k in range(kk)], axis=0)
            db = jnp.sum(dcur, axis=0, keepdims=True)

            @pl.when(i == 0)
            def _():
                dw_ref[...] = dw
                db_ref[...] = db

            @pl.when(i != 0)
            def _():
                dw_ref[...] += dw
                db_ref[...] += db

        return pl.pallas_call(
            body, name=name + "_bwd", grid=grid,
            out_shape=(jax.ShapeDtypeStruct((t, c), F32), jax.ShapeDtypeStruct(w.shape, F32), jax.ShapeDtypeStruct(b.shape, F32)),
            in_specs=[cur, prev, nxt, cur, prev, nxt, w_spec, b_spec], out_specs=(cur, w_spec, b_spec),
            compiler_params=_cparams())(x, x, x, g, g, g, w, b)

    @jax.custom_vjp
    def conv(x, w, b):
        return forward(x, w, b)

    conv.defvjp(lambda x, w, b: (forward(x, w, b), (x, w, b)), lambda res, g: backward(*res, g))
    return conv(x, w, b)


def _odd_pre_tile(pch):
    half = pch.shape[1] // 2
    return (pch[:, :half] * pch[:, half:],)


def _odd_post_tile(pb, pg, yc):
    return ((_silu(pg) * (pb * yc)).astype(BF16),)


def _rms_rows(x):
    return x * lax.rsqrt(jnp.mean(x * x, axis=-1, keepdims=True) + EPS)


def _pre0_tile(x, scale, shift, g):
    return ((_rms_rows(x) * g * (1 + scale) + shift).astype(BF16),)


def _pre_tile(x, y_prev, gate, scale, shift, g):
    xn = x + gate * y_prev
    return xn, (_rms_rows(xn) * g * (1 + scale) + shift).astype(BF16)


def _loss_tile(x, y_prev, target, gate, weight):
    err = (x + gate * y_prev - target) * weight
    return (0.5 * jnp.mean(err * err, axis=-1, keepdims=True),)


def _mid_even_tile(pa, pd, qg, kg, dt_bias, a, *, naw, sw, nh):
    def heads_norm(u, g):
        return jnp.concatenate([_rms_rows(u[:, j:j + NA_HEAD_DIM]) * g for j in range(0, naw, NA_HEAD_DIM)], axis=1)

    q, gate, z = pa[:, :naw], pa[:, naw:2 * naw], pa[:, 2 * naw:2 * naw + sw]
    k, v = pa[:, 2 * naw + sw:3 * naw + sw], pa[:, 3 * naw + sw:]
    dt = _softplus(pd[:, :nh] + dt_bias)
    return heads_norm(q, qg), heads_norm(k, kg), v, _silu(gate), _silu(z), dt, dt * a


def _post_even_tile(ya, sg, y0, y1, xs, sz, dskip, g, *, sw):
    yz = (y0 + y1 + dskip * xs) * sz
    gw = sw // SSD_GROUPS
    yb = jnp.concatenate([_rms_rows(yz[:, j:j + gw]) for j in range(0, sw, gw)], axis=1) * g
    return (jnp.concatenate([ya * sg, yb], axis=1).astype(BF16),)


def _in_proj_split(a, w, gslot, widths, *, w_is_nk, tm, tn, name):
    t, kdim = a.shape
    starts = [sum(widths[:i]) for i in range(len(widths))]
    assert all(s % tn == 0 and wd % tn == 0 for s, wd in zip(starts, widths)), (starts, widths, tn)

    def piece(i, blocks):
        off = (starts[i] // blocks, 0) if w_is_nk else (0, starts[i] // blocks)
        ext = (widths[i], kdim) if w_is_nk else (kdim, widths[i])
        return off, ext

    def forward(a, w):
        outs = []
        for i in range(len(widths)):
            off, ext = piece(i, tn)
            outs.append(_matmul(a, w, tb=w_is_nk, tm=tm, tn=tn, tk=kdim, b_off=off, b_extent=ext, name=f"{name}_fwd{i}"))
        return tuple(outs)

    @jax.custom_vjp
    def proj(a, w, gslot):
        return forward(a, w)

    def bwd(res, gs):
        a, w = res
        tt = _pick(t, (768, 512, 256, 128))
        tkk = _pick(kdim, (2048, 1024, 512, 256, 128))
        tok = _pick(t, (1408, 768, 512, 256, 128))
        da, dws = None, []
        for i, g in enumerate(gs):
            big = 2 * tn if widths[i] % (2 * tn) == 0 and starts[i] % (2 * tn) == 0 else tn
            off, ext = piece(i, big)
            part = _matmul(g, w, tb=not w_is_nk, tm=tt, tn=tkk, tk=big, b_off=off, b_extent=ext, name=f"{name}_bwd_a{i}")
            da = part if da is None else da + part
            if w_is_nk:
                dws.append(_matmul(g, a, ta=True, tm=big, tn=tkk, tk=tok, name=f"{name}_bwd_w{i}"))
            else:
                dws.append(_matmul(a, g, ta=True, tm=_pick(kdim, (1024, 512, 256, 128)), tn=_pick(widths[i], (1024, 512, 256, 128)),
                                   tk=tok, name=f"{name}_bwd_w{i}"))
        dw = jnp.concatenate(dws, axis=0 if w_is_nk else 1)
        if dw.shape != w.shape:
            dw = jnp.pad(dw, [(0, w.shape[0] - dw.shape[0]), (0, w.shape[1] - dw.shape[1])])
        return da.astype(a.dtype), jnp.zeros_like(w), dw

    proj.defvjp(lambda a, w, gslot: (forward(a, w), (a, w)), bwd)
    return proj(a, w, gslot)


def _rms(x, g):
    return x * lax.rsqrt(jnp.mean(x * x, axis=-1, keepdims=True) + EPS) * g


def _dw_conv(x, w, b=None):
    k = w.shape[0]
    ln = x.shape[0]
    xp = jnp.pad(x, ((k // 2, k // 2), (0, 0)))
    y = sum(w[i][None, :] * xp[i:i + ln] for i in range(k))
    return y if b is None else y + b


def _conv_two(x, nctx, w, b=None):
    return jnp.concatenate([_dw_conv(x[:nctx], w, b), _dw_conv(x[nctx:], w, b)], axis=0)


def _mod_rows(nctx, seq, ctx_vec, lat_vec):
    return jnp.concatenate([jnp.broadcast_to(ctx_vec, (nctx, ctx_vec.shape[-1])),
                            jnp.broadcast_to(lat_vec, (seq, lat_vec.shape[-1]))], axis=0)


def _even_mixer(cfg, h, e, wd, update_ctx):
    d, nctx, seq, t = cfg.d, cfg.nctx, cfg.s, cfg.t
    naw, sw = cfg.na_width, cfg.ssd_width
    gn = SSD_GROUPS * SSD_STATE
    nh = 2 * SSD_GROUPS * cfg.hpg
    wa, wx = 4 * naw + sw, sw + 2 * gn
    pa, px, pd = _in_proj_split(h, wd["win_t"][e], wd["g_win_t"][e], (wa, wx, cfg.n_pad - wa - wx), w_is_nk=True,
                                tm=cfg.tiles_in[0], tn=cfg.tiles_in[1], name=f"in_even{e}")
    tile = cfg.tile_tok
    nct = nctx // tile
    a_neg = -jnp.exp(wd["ssd_a_log"][e]).reshape(1, nh)
    qn, kn, vv, sg, sz, dt, dta = _rowwise(
        f"mid_even{e}", functools.partial(_mid_even_tile, naw=naw, sw=sw, nh=nh), [pa, pd], [],
        [wd["q_norm_g"][e][None, :], wd["k_norm_g"][e][None, :], wd["ssd_dt_bias"][e].reshape(1, nh), a_neg],
        [naw, naw, naw, naw, sw, nh, nh], [F32] * 7, tile, nct)
    biasw = _bias_windows(wd["na_rpb"][e])
    ya = _na_attention(e, update_ctx, qn, kn, vv, biasw)

    xbc = _dwconv(f"ssd_conv{e}", px, wd["ssd_conv_w"][e], wd["ssd_conv_b"][e][None, :], True, nctx)
    xs, bm, cm = xbc[:, :sw], xbc[:, sw:sw + gn], xbc[:, sw + gn:]

    def arrange(u):
        u4 = u.reshape(t, 2, SSD_GROUPS, cfg.hpg)
        return u4.transpose(1, 2, 0, 3), jnp.pad(u4.transpose(1, 2, 3, 0), ((0, 0), (0, 0), (0, 8 - cfg.hpg), (0, 0)))

    dt4, dt_t = arrange(dt)
    cs4, cs_t = arrange(_ssd_cumsum(e, dta))
    y2 = _ssd_scan(cfg, e, xs, bm, cm, dt4, dt_t, cs4, cs_t)
    dskip = jnp.repeat(wd["ssd_d"][e], SSD_HEAD_DIM)[None, :]
    (ycat,) = _rowwise(f"post_even{e}", functools.partial(_post_even_tile, sw=sw), [ya, sg, y2[0], y2[1], xs, sz], [],
                       [dskip, wd["ssd_norm_g"][e][None, :]], [naw + sw], [BF16], tile, nct)
    return _linear(ycat, wd["wout"][e], wd["g_wout"][e], w_is_nk=False, tiles=cfg.tiles_out_even, name=f"out_even{e}")


def _odd_mixer(cfg, h, o, wd):
    d, nctx = cfg.d, cfg.nctx
    tile = cfg.tile_tok
    nct = nctx // tile
    pb, pch, pg = _in_proj_split(h, wd["sc_win"][o], wd["g_sc_win"][o], (d, 2 * d, d), w_is_nk=False,
                                 tm=cfg.tiles_in_odd[0], tn=_pick(d, (1024, 512, 256, 128)), name=f"in_odd{o}")
    (cv,) = _rowwise(f"odd_pre{o}", _odd_pre_tile, [pch], [], [], [d], [F32], tile, nct)
    yc = _dwconv(f"sc_conv{o}", cv, wd["sc_conv_w"][o], jnp.zeros((1, d), F32), False, nctx)
    (u,) = _rowwise(f"odd_post{o}", _odd_post_tile, [pb, pg, yc], [], [], [d], [BF16], tile, nct)
    return _linear(u, wd["sc_wout"][o], wd["g_sc_wout"][o], w_is_nk=False, tiles=cfg.tiles_out_odd, name=f"out_odd{o}")


def _local_loss(cfg, x, ctx, target, mods, mods_c, wd):
    d, nctx, seq = cfg.d, cfg.nctx, cfg.s
    tile = cfg.tile_res
    nct = nctx // tile
    xx = jnp.concatenate([ctx, x], axis=0)
    y_prev = gate_prev = None
    for i in range(DEPTH):
        update_ctx = any(j % 2 == 0 for j in range(i + 1, DEPTH))
        shift = jnp.stack([mods_c[i, :d], mods[i, :d]])
        scale = jnp.stack([mods_c[i, d:2 * d], mods[i, d:2 * d]])
        g = wd["norm_g"][i][None, :]
        if y_prev is None:
            (h,) = _rowwise(f"pre{i}", _pre0_tile, [xx], [scale, shift], [g], [d], [BF16], tile, nct)
        else:
            xx, h = _rowwise(f"pre{i}", _pre_tile, [xx, y_prev], [gate_prev, scale, shift], [g], [d, d], [F32, BF16], tile, nct)
        y_prev = _even_mixer(cfg, h, i // 2, wd, update_ctx) if i % 2 == 0 else _odd_mixer(cfg, h, i // 2, wd)
        gate_c = mods_c[i, 2 * d:] if update_ctx else jnp.zeros((d,), F32)
        gate_prev = jnp.stack([gate_c, mods[i, 2 * d:]])
    target_rows = jnp.concatenate([jnp.zeros((nctx, d), F32), target], axis=0)
    weight = jnp.stack([jnp.zeros((d,), F32), jnp.ones((d,), F32)])
    (row_loss,) = _rowwise("loss", _loss_tile, [xx, y_prev, target_rows], [gate_prev, weight], [], [1], [F32], tile, nct)
    return jnp.sum(row_loss)


SMALL_REPLICATED = ["norm_g", "ssd_conv_b", "ssd_a_log", "ssd_dt_bias", "ssd_d", "ssd_norm_g", "q_norm_g", "k_norm_g", "na_rpb"]
WEIGHT_ORDER = ["c_ctx", "ada_w", "ada_b", "norm_g", "na_ssd_w_in", "ssd_conv_w", "ssd_conv_b", "ssd_a_log", "ssd_dt_bias",
                "ssd_d", "ssd_norm_g", "q_norm_g", "k_norm_g", "na_rpb", "na_ssd_w_out", "sc_w_in", "sc_conv_w", "sc_w_out"]


def kernel(x, c, ctx, c_ctx, ada_w, ada_b, norm_g, na_ssd_w_in, ssd_conv_w, ssd_conv_b, ssd_a_log, ssd_dt_bias, ssd_d, ssd_norm_g, q_norm_g, k_norm_g, na_rpb, na_ssd_w_out, sc_w_in, sc_conv_w, sc_w_out, loss_target, m_c_ctx, m_ada_w, m_ada_b, m_norm_g, m_na_ssd_w_in, m_ssd_conv_w, m_ssd_conv_b, m_ssd_a_log, m_ssd_dt_bias, m_ssd_d, m_ssd_norm_g, m_q_norm_g, m_k_norm_g, m_na_rpb, m_na_ssd_w_out, m_sc_w_in, m_sc_conv_w, m_sc_w_out, v_c_ctx, v_ada_w, v_ada_b, v_norm_g, v_na_ssd_w_in, v_ssd_conv_w, v_ssd_conv_b, v_ssd_a_log, v_ssd_dt_bias, v_ssd_d, v_ssd_norm_g, v_q_norm_g, v_k_norm_g, v_na_rpb, v_na_ssd_w_out, v_sc_w_in, v_sc_conv_w, v_sc_w_out):
    given = dict(locals())
    weights = {n: given[n] for n in WEIGHT_ORDER}
    mom_m = {n: given["m_" + n] for n in WEIGHT_ORDER}
    mom_v = {n: given["v_" + n] for n in WEIGHT_ORDER}

    d = x.shape[-1]
    seq, nctx = x.shape[1], ctx.shape[1]
    n_in_shard = na_ssd_w_in.shape[-1]
    n_in = n_in_shard * NDEV
    n_pad = -(-n_in // PAD_TO) * PAD_TO
    hpg = (d // SSD_HEAD_DIM) // SSD_GROUPS
    t = nctx + seq
    tm = _pick(t, (1408, 768, 512, 256, 128))
    cfg = SimpleNamespace(
        d=d, s=seq, nctx=nctx, t=t, hpg=hpg, na_width=NA_HEADS * NA_HEAD_DIM, ssd_width=d, n_in=n_in, n_pad=n_pad,
        tiles_in=(tm, _pick(n_pad, (512, 256, 128)), d),
        tiles_out_even=(tm, _pick(d, (1024, 512, 256, 128)), _pick(NA_HEADS * NA_HEAD_DIM + d, (1024, 512, 256, 128))),
        tiles_in_odd=(tm, _pick(4 * d, (1024, 512, 256, 128)), d),
        tiles_out_odd=(tm, _pick(d, (1024, 512, 256, 128)), d),
        tile_tok=128, tile_res=256,
    )
    me = _my_index()
    xl, cl, ctxl, tgt = x[0], c, ctx[0], loss_target[0]

    ncol = ada_w.shape[-1]
    c_rows = -(-d // 128)
    c_all = _small_allgather(jnp.pad(cl.reshape(-1), (0, c_rows * 128 - d)).reshape(c_rows, 128), "gather_c")[0]
    c_all = c_all.reshape(NDEV, -1)[:, :d]
    cond = jnp.concatenate([c_all, c_ctx[None, :], jnp.zeros((16 - NDEV - 1, d), F32)], axis=0)
    s16 = jax.nn.silu(cond)
    ada_b_mine = lax.dynamic_slice_in_dim(ada_b, me * ncol, ncol, axis=1)
    mod_part = jnp.stack([
        _matmul(s16, ada_w[i], tm=16, tn=_pick(ncol, (768, 512, 256, 128)), tk=d, name=f"adaln{i}") + ada_b_mine[i][None, :]
        for i in range(DEPTH)])
    mp_rows = DEPTH * 16 * ncol // 128
    mod_all = _small_allgather(mod_part.reshape(mp_rows, 128), "gather_mod")[0]
    mod_all = mod_all.reshape(NDEV, DEPTH, 16, ncol).transpose(1, 2, 0, 3).reshape(DEPTH, 16, NDEV * ncol)
    mods = lax.dynamic_index_in_dim(mod_all, me, axis=1, keepdims=False)
    mods_c = mod_all[:, NDEV]

    packed = _pack_transposed(na_ssd_w_in, "pack_w_in")
    wout_b = _cast_bf16(na_ssd_w_out, "cast_w_out")
    scwin_b = _cast_bf16(sc_w_in, "cast_sc_w_in")
    scwout_b = _cast_bf16(sc_w_out, "cast_sc_w_out")
    packed_all, wout_all, scwin_all, scwout_all = _big_allgather(
        [packed, wout_b, scwin_b, scwout_b], [1, 1, 2, 1], "gather_weights")
    win_t = _unpack(packed_all, n_pad, "unpack_w_in")
    conv_shapes = [ssd_conv_w.shape, sc_conv_w.shape]
    conv_all = _small_allgather(_flatten([ssd_conv_w, sc_conv_w]), "gather_conv")[0]
    conv_parts = [_unflatten(conv_all[j], conv_shapes) for j in range(NDEV)]
    ssd_conv_full = jnp.concatenate([cp[0] for cp in conv_parts], axis=-1)
    sc_conv_full = jnp.concatenate([cp[1] for cp in conv_parts], axis=-1)

    small = {n: weights[n] for n in SMALL_REPLICATED}
    small["ssd_conv_w"] = ssd_conv_full
    small["sc_conv_w"] = sc_conv_full
    gslots = dict(g_win_t=jnp.zeros(win_t.shape, F32), g_wout=jnp.zeros(wout_all.shape, F32),
                  g_sc_win=jnp.zeros(scwin_all.shape, F32), g_sc_wout=jnp.zeros(scwout_all.shape, F32))
    frozen = dict(win_t=win_t, wout=wout_all, sc_win=scwin_all, sc_wout=scwout_all)

    def loss_fn(xl, mods, mods_c, small, gslots):
        return _local_loss(cfg, xl, ctxl, tgt, mods, mods_c, {**small, **gslots, **frozen})

    loss_local, (g_x, g_mods, g_mods_c, g_small, g_big) = jax.value_and_grad(loss_fn, argnums=(0, 1, 2, 3, 4))(
        xl, mods, mods_c, small, gslots)

    small_names = SMALL_REPLICATED + ["ssd_conv_w", "sc_conv_w"]
    small_shapes = [g_small[n].shape for n in small_names] + [g_mods_c.shape]
    flat_small = _flatten([g_small[n] for n in small_names] + [g_mods_c])
    _, small_sum = _small_allgather(flat_small, "gather_small_grads")
    summed = _unflatten(small_sum, small_shapes)
    g_rep = dict(zip(small_names, summed[:-1]))
    g_mods_c_tot = summed[-1]
    gm_rows = DEPTH * NDEV * ncol // 128
    gm_all = _small_allgather(g_mods.reshape(gm_rows, 128), "gather_mod_grads")[0].reshape(NDEV, DEPTH, NDEV * ncol)
    dm = jnp.concatenate([gm_all.transpose(1, 0, 2), g_mods_c_tot[:, None, :],
                          jnp.zeros((DEPTH, 16 - NDEV - 1, NDEV * ncol), F32)], axis=1)
    grad_ada_b = jnp.sum(dm, axis=1)
    dm_mine = lax.dynamic_slice_in_dim(dm, me * ncol, ncol, axis=2)
    grad_ada_w = jnp.stack([
        _matmul(s16, dm_mine[i], ta=True, tm=_pick(d, (512, 256, 128)), tn=_pick(ncol, (768, 512, 256, 128)), tk=16,
                name=f"adaln_gw{i}") for i in range(DEPTH)])
    ds_part = sum(_matmul(dm_mine[i], ada_w[i], tb=True, tm=16, tn=_pick(d, (2048, 1024, 512, 256, 128)),
                          tk=_pick(ncol, (768, 512, 256, 128)), name=f"adaln_gs{i}") for i in range(DEPTH))[NDEV]
    ds_ctx = _small_allgather(jnp.pad(ds_part, (0, c_rows * 128 - d)).reshape(c_rows, 128), "gather_c_ctx_grad")[1]
    ds_ctx = ds_ctx.reshape(-1)[:d]
    sig = jax.nn.sigmoid(c_ctx)
    grad_c_ctx = ds_ctx * (sig * (1 + c_ctx * (1 - sig)))

    big = [g_big["g_win_t"], g_big["g_wout"], g_big["g_sc_win"], g_big["g_sc_wout"]]
    axes = [1, 1, 2, 1]
    lens = [n_in_shard, na_ssd_w_out.shape[1], sc_w_in.shape[2], sc_w_out.shape[1]]
    sends = [_pick_blocks(g, None, ax, ln, "cast", BF16, f"rs_cast{i}") for i, (g, ax, ln) in enumerate(zip(big, axes, lens))]
    recv_a = _rs_stage_a(sends, "reduce_scatter_d2d")
    part_a = [_pick_blocks(g, r, ax, ln, "mine", F32, f"pair_sum_mine{i}")
              for i, (g, r, ax, ln) in enumerate(zip(big, recv_a, axes, lens))]
    part_s = [_pick_blocks(g, r, ax, ln, "send", BF16, f"pair_sum_send{i}")
              for i, (g, r, ax, ln) in enumerate(zip(big, recv_a, axes, lens))]
    part_b = _rs_stage_b(part_s, "reduce_scatter_ici")

    res = {}
    res["na_ssd_w_in"] = _adam_sharded(na_ssd_w_in, m_na_ssd_w_in, v_na_ssd_w_in, part_a[0], part_b[0], transposed=True, name="adam_w_in")
    res["na_ssd_w_out"] = _adam_sharded(na_ssd_w_out, m_na_ssd_w_out, v_na_ssd_w_out, part_a[1], part_b[1], transposed=False, name="adam_w_out")
    res["sc_w_in"] = _adam_sharded(sc_w_in, m_sc_w_in, v_sc_w_in, part_a[2], part_b[2], transposed=False, name="adam_sc_w_in")
    res["sc_w_out"] = _adam_sharded(sc_w_out, m_sc_w_out, v_sc_w_out, part_a[3], part_b[3], transposed=False, name="adam_sc_w_out")

    grads = dict(g_rep)
    grads["ssd_conv_w"] = lax.dynamic_slice_in_dim(g_rep["ssd_conv_w"], me * ssd_conv_w.shape[-1], ssd_conv_w.shape[-1], axis=2)
    grads["sc_conv_w"] = lax.dynamic_slice_in_dim(g_rep["sc_conv_w"], me * sc_conv_w.shape[-1], sc_conv_w.shape[-1], axis=2)
    grads["c_ctx"] = grad_c_ctx
    res["ada_w"] = (grad_ada_w, *_adam_tiled(ada_w, grad_ada_w, m_ada_w, v_ada_w, "adam_ada_w"))
    grads["ada_b"] = grad_ada_b
    rest = [n for n in WEIGHT_ORDER if n not in res]
    shapes = [weights[n].shape for n in rest]
    d_flat, m_flat, v_flat = _adam_small(_flatten([weights[n] for n in rest]), _flatten([grads[n] for n in rest]),
                                         _flatten([mom_m[n] for n in rest]), _flatten([mom_v[n] for n in rest]), "adam_small")
    for n, dd, mm, vv in zip(rest, _unflatten(d_flat, shapes), _unflatten(m_flat, shapes), _unflatten(v_flat, shapes)):
        res[n] = (grads[n], dd, mm, vv)

    loss = lax.psum(loss_local, ("x", "y", "c"))
    return (loss, g_x[None], *[res[n][0] for n in WEIGHT_ORDER], *[res[n][1] for n in WEIGHT_ORDER],
            *[res[n][2] for n in WEIGHT_ORDER], *[res[n][3] for n in WEIGHT_ORDER])
```

```python
import functools
import math
from types import SimpleNamespace

import jax
import jax.numpy as jnp
from jax import lax
from jax.experimental import pallas as pl
from jax.experimental.pallas import tpu as pltpu

F32 = jnp.float32
BF16 = jnp.bfloat16
U32 = jnp.uint32
HIGHEST = lax.Precision.HIGHEST
MESH = pl.DeviceIdType.MESH
ANY = pl.BlockSpec(memory_space=pl.ANY)
VMEM = pl.BlockSpec(memory_space=pltpu.VMEM)

NDEV = 8
DEPTH = 4
GRID_W = 64
EPS = 1e-6
NA_HEADS = 16
NA_HEAD_DIM = 128
NA_KH = 8
NA_KW = 16
SSD_HEAD_DIM = 64
SSD_GROUPS = 8
SSD_STATE = 128
SSD_CONV = 5
SSD_CHUNK = 128
SC_CONV = 3
ADAM_LR = 0.001
ADAM_B1 = 0.9
ADAM_B2 = 0.999
ADAM_EPS = 1e-08
ADAM_WD = 0.01
ADAM_STEP = 10
NEG = -1e30
VMEM_LIMIT = 56 * 1024 * 1024
PAD_TO = 512


def _pos():
    return lax.axis_index("x"), lax.axis_index("y"), lax.axis_index("c")


def _my_index():
    x, y, c = _pos()
    return 4 * x + 2 * y + c


def _pick(n, prefs):
    for p in prefs:
        if n % p == 0:
            return p
    return n


def _cparams(**kw):
    return pltpu.CompilerParams(vmem_limit_bytes=VMEM_LIMIT, **kw)


def _small_allgather(v, name):
    rows, lanes = v.shape

    def body(x_ref, out_ref, sum_ref, send_sems, recv_sems):
        x, y, c = _pos()
        me = 4 * x + 2 * y + c
        out_ref[me] = x_ref[...]
        copies = []
        for k in range(1, NDEV):
            peer = (1 - x if k & 4 else x, 1 - y if k & 2 else y, 1 - c if k & 1 else c)
            cp = pltpu.make_async_remote_copy(src_ref=x_ref, dst_ref=out_ref.at[me], send_sem=send_sems.at[k - 1],
                                              recv_sem=recv_sems.at[k - 1], device_id=peer, device_id_type=MESH)
            cp.start()
            copies.append(cp)
        for cp in copies:
            cp.wait()
        acc = out_ref[0]
        for j in range(1, NDEV):
            acc = acc + out_ref[j]
        sum_ref[...] = acc

    return pl.pallas_call(
        body, name=name,
        out_shape=(jax.ShapeDtypeStruct((NDEV, rows, lanes), v.dtype), jax.ShapeDtypeStruct((rows, lanes), v.dtype)),
        in_specs=[VMEM], out_specs=(VMEM, VMEM),
        scratch_shapes=[pltpu.SemaphoreType.DMA((NDEV - 1,)), pltpu.SemaphoreType.DMA((NDEV - 1,))],
        compiler_params=_cparams(),
    )(v)


def _window(ref, axis, idx, length):
    sl = [slice(None)] * len(ref.shape)
    sl[axis] = pl.ds(pl.multiple_of(idx * length, min(length & -length, 1024)), length)
    return ref.at[tuple(sl)]


def _big_allgather(shards, axes, name):
    n = len(shards)
    out_shapes = []
    for s, ax in zip(shards, axes):
        shp = list(s.shape)
        shp[ax] *= NDEV
        out_shapes.append(jax.ShapeDtypeStruct(tuple(shp), s.dtype))

    def body(*refs):
        xs, outs = refs[:n], refs[n:2 * n]
        send_sems, recv_sems, local_sems = refs[2 * n:]
        x, y, c = _pos()
        me, sib = (x, y, c), (x, y, 1 - c)
        chips = [(1 - x, y), (x, 1 - y), (1 - x, 1 - y)]

        def win(a, px, py, pc):
            return _window(outs[a], axes[a], 4 * px + 2 * py + pc, shards[a].shape[axes[a]])

        def copy(a, k, block, to, src=None):
            return pltpu.make_async_remote_copy(src_ref=win(a, *block) if src is None else src, dst_ref=win(a, *block),
                                                send_sem=send_sems.at[a * 7 + k], recv_sem=recv_sems.at[a * 7 + k],
                                                device_id=to, device_id_type=MESH)

        mine = [pltpu.make_async_copy(xs[a], win(a, *me), local_sems.at[a]) for a in range(n)]
        for cp in mine:
            cp.start()
        first = []
        for a in range(n):
            first.append(copy(a, 0, me, sib, src=xs[a]))
            first += [copy(a, 1 + j, me, (*chip, c), src=xs[a]) for j, chip in enumerate(chips)]
        for cp in first:
            cp.start()
        passed = []
        for j, chip in enumerate(chips):
            for a in range(n):
                copy(a, 1 + j, (*chip, c), me).wait_recv()
                cp = copy(a, 4 + j, (*chip, c), sib)
                cp.start()
                passed.append(cp)
        for a in range(n):
            copy(a, 0, sib, me).wait_recv()
            for j, chip in enumerate(chips):
                copy(a, 4 + j, (*chip, 1 - c), me).wait_recv()
        for cp in first + passed:
            cp.wait_send()
        for cp in mine:
            cp.wait()

    return pl.pallas_call(
        body, name=name, out_shape=tuple(out_shapes), in_specs=[ANY] * n, out_specs=tuple([ANY] * n),
        scratch_shapes=[pltpu.SemaphoreType.DMA((7 * n,)), pltpu.SemaphoreType.DMA((7 * n,)), pltpu.SemaphoreType.DMA((n,))],
        compiler_params=_cparams(),
    )(*shards)


def _rs_stage_a(sends, name):
    n = len(sends)
    out_shapes = [jax.ShapeDtypeStruct(s.shape, s.dtype) for s in sends]

    def body(*refs):
        ss, outs = refs[:n], refs[n:2 * n]
        send_sems, recv_sems = refs[2 * n:]
        x, y, c = _pos()
        sib = (x, y, 1 - c)
        copies = []
        for a in range(n):
            for k in range(4):
                cp = pltpu.make_async_remote_copy(src_ref=ss[a].at[k], dst_ref=outs[a].at[k], send_sem=send_sems.at[a * 4 + k],
                                                  recv_sem=recv_sems.at[a * 4 + k], device_id=sib, device_id_type=MESH)
                cp.start()
                copies.append(cp)
        for cp in copies:
            cp.wait()

    return pl.pallas_call(
        body, name=name, out_shape=tuple(out_shapes), in_specs=[ANY] * n, out_specs=tuple([ANY] * n),
        scratch_shapes=[pltpu.SemaphoreType.DMA((4 * n,)), pltpu.SemaphoreType.DMA((4 * n,))],
        compiler_params=_cparams(),
    )(*sends)


def _rs_stage_b(parts, name):
    n = len(parts)
    out_shapes = [jax.ShapeDtypeStruct(p.shape, p.dtype) for p in parts]

    def body(*refs):
        ps, outs = refs[:n], refs[n:2 * n]
        send_sems, recv_sems = refs[2 * n:]
        x, y, c = _pos()
        chips = [(1 - x, y), (x, 1 - y), (1 - x, 1 - y)]
        copies = []
        for a in range(n):
            for j, (px, py) in enumerate(chips):
                cp = pltpu.make_async_remote_copy(src_ref=ps[a].at[j], dst_ref=outs[a].at[j],
                                                  send_sem=send_sems.at[a * 3 + j], recv_sem=recv_sems.at[a * 3 + j],
                                                  device_id=(px, py, c), device_id_type=MESH)
                cp.start()
                copies.append(cp)
        for cp in copies:
            cp.wait()

    return pl.pallas_call(
        body, name=name, out_shape=tuple(out_shapes), in_specs=[ANY] * n, out_specs=tuple([ANY] * n),
        scratch_shapes=[pltpu.SemaphoreType.DMA((3 * n,)), pltpu.SemaphoreType.DMA((3 * n,))],
        compiler_params=_cparams(),
    )(*parts)


def _matmul(a, b, *, ta=False, tb=False, tm, tn, tk, name, b_off=(0, 0), b_extent=None):
    m, kdim = (a.shape[1], a.shape[0]) if ta else a.shape
    b_shape = b.shape if b_extent is None else b_extent
    n = b_shape[0] if tb else b_shape[1]
    assert (b_shape[1] if tb else b_shape[0]) == kdim, (a.shape, b_shape, ta, tb)
    assert m % tm == 0 and n % tn == 0 and kdim % tk == 0, (m, n, kdim, tm, tn, tk)
    o0, o1 = b_off
    nk = kdim // tk
    dn = (((0 if ta else 1,), (1 if tb else 0,)), ((), ()))

    def body(a_ref, b_ref, o_ref, acc_ref):
        k = pl.program_id(2)
        part = lax.dot_general(a_ref[...].astype(BF16), b_ref[...].astype(BF16), dn, preferred_element_type=F32)
        if nk == 1:
            o_ref[...] = part
        else:
            @pl.when(k == 0)
            def _():
                acc_ref[...] = part

            @pl.when(k > 0)
            def _():
                acc_ref[...] += part

            @pl.when(k == nk - 1)
            def _():
                o_ref[...] = acc_ref[...]

    a_spec = pl.BlockSpec((tk, tm), lambda i, j, k: (k, i)) if ta else pl.BlockSpec((tm, tk), lambda i, j, k: (i, k))
    b_spec = (pl.BlockSpec((tn, tk), lambda i, j, k: (j + o0, k + o1)) if tb
              else pl.BlockSpec((tk, tn), lambda i, j, k: (k + o0, j + o1)))
    acc_shape = (tm, tn) if nk > 1 else (8, 128)
    return pl.pallas_call(
        body, name=name, grid=(m // tm, n // tn, nk), out_shape=jax.ShapeDtypeStruct((m, n), F32),
        in_specs=[a_spec, b_spec], out_specs=pl.BlockSpec((tm, tn), lambda i, j, k: (i, j)),
        scratch_shapes=[pltpu.VMEM(acc_shape, F32)], compiler_params=_cparams(),
    )(a, b)


def _linear(a, w, gslot, *, w_is_nk, tiles, name):
    tm, tn, tk = tiles

    @jax.custom_vjp
    def lin(a, w, gslot):
        return _matmul(a, w, tb=w_is_nk, tm=tm, tn=tn, tk=tk, name=name + "_fwd")

    def fwd(a, w, gslot):
        return lin(a, w, gslot), (a, w)

    def bwd(res, g):
        a, w = res
        gb = g
        t, kdim = a.shape
        n = g.shape[1]
        tt = _pick(t, (768, 512, 256, 128))
        tkk = _pick(kdim, (2048, 1024, 512, 256, 128))
        tnn = _pick(n, (1024, 512, 256, 128))
        da = _matmul(gb, w, tb=not w_is_nk, tm=tt, tn=tkk, tk=tnn, name=name + "_bwd_a")
        tok = _pick(t, (1408, 768, 512, 256, 128))
        if w_is_nk:
            dw = _matmul(gb, a, ta=True, tm=tnn, tn=tkk, tk=tok, name=name + "_bwd_w")
        else:
            tkw = _pick(kdim, (1024, 512, 256, 128))
            tnw = _pick(n, (1024, 512, 256, 128))
            dw = _matmul(a, gb, ta=True, tm=tkw, tn=tnw, tk=tok, name=name + "_bwd_w")
        return da.astype(a.dtype), jnp.zeros_like(w), dw

    lin.defvjp(fwd, bwd)
    return lin(a, w, gslot)


def _pack_transposed(w, name):
    nl, kdim, r = w.shape
    half = kdim // 2
    tc = _pick(half, (256, 128))

    def body(lo_ref, hi_ref, o_ref):
        lo = pltpu.bitcast(lo_ref[0].astype(BF16).astype(F32).T, U32) >> 16
        hi = pltpu.bitcast(hi_ref[0].astype(BF16).astype(F32).T, U32) & jnp.uint32(0xFFFF0000)
        o_ref[0] = pltpu.bitcast(hi | lo, F32)

    nb = half // tc
    return pl.pallas_call(
        body, name=name, grid=(nl, nb), out_shape=jax.ShapeDtypeStruct((nl, r, half), F32),
        in_specs=[pl.BlockSpec((1, tc, r), lambda l, t: (l, t, 0)), pl.BlockSpec((1, tc, r), lambda l, t: (l, t + nb, 0))],
        out_specs=pl.BlockSpec((1, r, tc), lambda l, t: (l, 0, t)), compiler_params=_cparams(),
    )(w, w)


def _unpack(packed, n_pad, name):
    nl, n, half = packed.shape
    tr = math.gcd(math.gcd(n, n_pad - n), 64) if n_pad > n else _pick(n, (64, 32, 16))
    nin = n // tr

    def body(p_ref, o_ref):
        t = pl.program_id(1)

        @pl.when(t < nin)
        def _():
            u = pltpu.bitcast(p_ref[0], U32)
            o_ref[0, :, :half] = pltpu.bitcast(u << 16, F32).astype(BF16)
            o_ref[0, :, half:] = pltpu.bitcast(u & jnp.uint32(0xFFFF0000), F32).astype(BF16)

        @pl.when(t >= nin)
        def _():
            o_ref[...] = jnp.zeros_like(o_ref)

    return pl.pallas_call(
        body, name=name, grid=(nl, n_pad // tr), out_shape=jax.ShapeDtypeStruct((nl, n_pad, 2 * half), BF16),
        in_specs=[pl.BlockSpec((1, tr, half), lambda l, t: (l, jnp.minimum(t, nin - 1), 0))],
        out_specs=pl.BlockSpec((1, tr, 2 * half), lambda l, t: (l, t, 0)), compiler_params=_cparams(),
    )(packed)


def _cast_bf16(w, name):
    nl, r, c = w.shape
    tr = _pick(r, (512, 256, 128, 64, 32, 16))

    def body(w_ref, o_ref):
        o_ref[...] = w_ref[...].astype(BF16)

    return pl.pallas_call(
        body, name=name, grid=(nl, r // tr), out_shape=jax.ShapeDtypeStruct(w.shape, BF16),
        in_specs=[pl.BlockSpec((1, tr, c), lambda l, t: (l, t, 0))], out_specs=pl.BlockSpec((1, tr, c), lambda l, t: (l, t, 0)),
        compiler_params=_cparams(),
    )(w)


NA_ROWS_PER_STEP = 4


def _row_block(i):
    return slice(i * GRID_W, (i + 1) * GRID_W)


def _na_probs(qs, kb_ref, b_ref, step, rows, nctx):
    scale = NA_HEAD_DIM ** -0.5
    nt = (((1,), (1,)), ((), ()))
    kc = kb_ref[0:nctx, :]
    kws, starts, offs, s1 = [], [], [], []
    for i in range(NA_ROWS_PER_STEP):
        r = step * NA_ROWS_PER_STEP + i
        rs = jnp.clip(r - NA_KH // 2, 0, rows - NA_KH)
        offs.append(rs - r + NA_KH - 1)
        starts.append(pl.multiple_of(nctx + rs * GRID_W, GRID_W))
        kws.append(kb_ref[pl.ds(starts[i], NA_KH * GRID_W), :])
        s1.append(lax.dot_general(qs[_row_block(i)], kws[i], nt, preferred_element_type=F32) * scale + b_ref[0, offs[i]])
    s1 = jnp.concatenate(s1, axis=0)
    s2 = lax.dot_general(qs, kc, nt, preferred_element_type=F32) * scale
    m = jnp.maximum(jnp.max(s1, axis=-1, keepdims=True), jnp.max(s2, axis=-1, keepdims=True))
    e1 = jnp.exp(s1 - m)
    e2 = jnp.exp(s2 - m)
    inv = 1.0 / (jnp.sum(e1, axis=-1, keepdims=True) + jnp.sum(e2, axis=-1, keepdims=True))
    return kc, kws, starts, offs, e1 * inv, e2 * inv


def _ctx_probs(qs, kc):
    s = lax.dot_general(qs, kc, (((1,), (1,)), ((), ())), preferred_element_type=F32) * NA_HEAD_DIM ** -0.5
    e = jnp.exp(s - jnp.max(s, axis=-1, keepdims=True))
    return e * (1.0 / jnp.sum(e, axis=-1, keepdims=True))


def _na_geometry(t):
    blk = NA_ROWS_PER_STEP * GRID_W
    rows = (t - blk) // GRID_W
    assert rows % NA_ROWS_PER_STEP == 0 and blk + rows * GRID_W == t, (t, blk)
    return blk, rows


def _na_forward(name, update_ctx, q, k, v, biasw):
    t, width = q.shape
    dh, win = NA_HEAD_DIM, NA_KH * GRID_W
    blk, rows = _na_geometry(t)

    def body(q_ref, k_ref, v_ref, b_ref, o_ref, kb_ref, vb_ref):
        j = pl.program_id(1)
        qs = q_ref[...].astype(BF16)

        @pl.when(j == 0)
        def _():
            kb_ref[...] = k_ref[...].astype(BF16)
            vb_ref[...] = v_ref[...].astype(BF16)
            if update_ctx:
                p = _ctx_probs(qs, kb_ref[0:blk, :])
                o_ref[...] = jnp.dot(p.astype(BF16), vb_ref[0:blk, :], preferred_element_type=F32)
            else:
                o_ref[...] = jnp.zeros_like(o_ref)

        @pl.when(j > 0)
        def _():
            _, _, starts, _, p1, p2 = _na_probs(qs, kb_ref, b_ref, j - 1, rows, blk)
            p1b = p1.astype(BF16)
            o1 = [jnp.dot(p1b[_row_block(i)], vb_ref[pl.ds(starts[i], win), :], preferred_element_type=F32)
                  for i in range(NA_ROWS_PER_STEP)]
            o_ref[...] = jnp.concatenate(o1, axis=0) + jnp.dot(p2.astype(BF16), vb_ref[0:blk, :], preferred_element_type=F32)

    row_spec = pl.BlockSpec((blk, dh), lambda h, j: (j, h))
    all_spec = pl.BlockSpec((t, dh), lambda h, j: (0, h))
    b_spec = pl.BlockSpec((1, NA_KH, GRID_W, win), lambda h, j: (h, 0, 0, 0))
    return pl.pallas_call(
        body, name=name, grid=(NA_HEADS, 1 + rows // NA_ROWS_PER_STEP), out_shape=jax.ShapeDtypeStruct((t, width), F32),
        in_specs=[row_spec, all_spec, all_spec, b_spec], out_specs=row_spec,
        scratch_shapes=[pltpu.VMEM((t, dh), BF16), pltpu.VMEM((t, dh), BF16)], compiler_params=_cparams(),
    )(q, k, v, biasw)


def _na_backward(name, update_ctx, q, k, v, biasw, do):
    t, width = q.shape
    dh, win = NA_HEAD_DIM, NA_KH * GRID_W
    blk, rows = _na_geometry(t)
    scale = dh ** -0.5
    nt = (((1,), (1,)), ((), ()))
    tn = (((0,), (0,)), ((), ()))

    def body(q_ref, k_ref, v_ref, b_ref, do_ref, dq_ref, dk_ref, dv_ref, db_ref, kb_ref, vb_ref):
        j = pl.program_id(1)
        qs = q_ref[...].astype(BF16)
        dob = do_ref[...].astype(BF16)
        ctx = slice(0, blk)

        @pl.when(j == 0)
        def _():
            kb_ref[...] = k_ref[...].astype(BF16)
            vb_ref[...] = v_ref[...].astype(BF16)
            dk_ref[...] = jnp.zeros_like(dk_ref)
            dv_ref[...] = jnp.zeros_like(dv_ref)
            db_ref[...] = jnp.zeros_like(db_ref)
            if update_ctx:
                kc, vc = kb_ref[ctx, :], vb_ref[ctx, :]
                p = _ctx_probs(qs, kc)
                dp = lax.dot_general(dob, vc, nt, preferred_element_type=F32)
                ds = p * (dp - jnp.sum(dp * p, axis=-1, keepdims=True))
                dsb = (ds * scale).astype(BF16)
                dq_ref[...] = jnp.dot(dsb, kc, preferred_element_type=F32)
                dv_ref[ctx, :] += lax.dot_general(p.astype(BF16), dob, tn, preferred_element_type=F32)
                dk_ref[ctx, :] += lax.dot_general(dsb, qs, tn, preferred_element_type=F32)
            else:
                dq_ref[...] = jnp.zeros_like(dq_ref)

        @pl.when(j > 0)
        def _():
            nr = range(NA_ROWS_PER_STEP)
            kc, kws, starts, offs, p1, p2 = _na_probs(qs, kb_ref, b_ref, j - 1, rows, blk)
            vc = vb_ref[ctx, :]
            p1b, p2b = p1.astype(BF16), p2.astype(BF16)
            dp1 = jnp.concatenate([lax.dot_general(dob[_row_block(i)], vb_ref[pl.ds(starts[i], win), :], nt,
                                                   preferred_element_type=F32) for i in nr], axis=0)
            dp2 = lax.dot_general(dob, vc, nt, preferred_element_type=F32)
            delta = jnp.sum(dp1 * p1, axis=-1, keepdims=True) + jnp.sum(dp2 * p2, axis=-1, keepdims=True)
            ds1 = p1 * (dp1 - delta)
            ds2 = p2 * (dp2 - delta)
            ds1b = (ds1 * scale).astype(BF16)
            ds2b = (ds2 * scale).astype(BF16)
            dq1 = [jnp.dot(ds1b[_row_block(i)], kws[i], preferred_element_type=F32) for i in nr]
            dq_ref[...] = jnp.concatenate(dq1, axis=0) + jnp.dot(ds2b, kc, preferred_element_type=F32)
            dv_ref[ctx, :] += lax.dot_general(p2b, dob, tn, preferred_element_type=F32)
            dk_ref[ctx, :] += lax.dot_general(ds2b, qs, tn, preferred_element_type=F32)
            for i in nr:
                sl = pl.ds(starts[i], win)
                db_ref[0, offs[i]] += ds1[_row_block(i)]
                dv_ref[sl, :] += lax.dot_general(p1b[_row_block(i)], dob[_row_block(i)], tn, preferred_element_type=F32)
                dk_ref[sl, :] += lax.dot_general(ds1b[_row_block(i)], qs[_row_block(i)], tn, preferred_element_type=F32)

    row_spec = pl.BlockSpec((blk, dh), lambda h, j: (j, h))
    all_spec = pl.BlockSpec((t, dh), lambda h, j: (0, h))
    b_spec = pl.BlockSpec((1, NA_KH, GRID_W, win), lambda h, j: (h, 0, 0, 0))
    full = jax.ShapeDtypeStruct((t, width), F32)
    return pl.pallas_call(
        body, name=name, grid=(NA_HEADS, 1 + rows // NA_ROWS_PER_STEP),
        out_shape=(full, full, full, jax.ShapeDtypeStruct(biasw.shape, F32)),
        in_specs=[row_spec, all_spec, all_spec, b_spec, row_spec], out_specs=(row_spec, all_spec, all_spec, b_spec),
        scratch_shapes=[pltpu.VMEM((t, dh), BF16), pltpu.VMEM((t, dh), BF16)], compiler_params=_cparams(),
    )(q, k, v, biasw, do)


def _na_attention(e, update_ctx, q, k, v, biasw):
    @jax.custom_vjp
    def attn(q, k, v, biasw):
        return _na_forward(f"na_fwd{e}", update_ctx, q, k, v, biasw)

    attn.defvjp(lambda q, k, v, biasw: (_na_forward(f"na_fwd{e}", update_ctx, q, k, v, biasw), (q, k, v, biasw)),
                lambda res, do: _na_backward(f"na_bwd{e}", update_ctx, *res, do))
    return attn(q, k, v, biasw)


def _bias_windows(rpb):
    col = jnp.arange(GRID_W)
    dc = jnp.clip(col[None, :] - col[:, None], -(NA_KW - 1), NA_KW - 1) + NA_KW - 1
    onehot = (dc[None] == jnp.arange(2 * NA_KW - 1)[:, None, None]).astype(F32)
    tq = jnp.einsum("hrd,dqk->hrqk", rpb, onehot, precision=HIGHEST)
    col_start = jnp.clip(col - NA_KW // 2, 0, GRID_W - NA_KW)
    in_win = (col[None, :] >= col_start[:, None]) & (col[None, :] < col_start[:, None] + NA_KW)
    wins = jnp.stack([tq[:, off:off + NA_KH] for off in range(NA_KH)], axis=1)
    wins = jnp.where(in_win[None, None, None], wins, NEG)
    return wins.transpose(0, 1, 3, 2, 4).reshape(rpb.shape[0], NA_KH, GRID_W, NA_KH * GRID_W)


def _chunk_cumsum(u, reverse, name):
    t, nh = u.shape
    ln = SSD_CHUNK

    def body(u_ref, o_ref):
        row = lax.broadcasted_iota(jnp.int32, (ln, ln), 0)
        col = lax.broadcasted_iota(jnp.int32, (ln, ln), 1)
        uu = u_ref[...]
        down = jnp.dot((col <= row).astype(F32), uu, precision=HIGHEST, preferred_element_type=F32)
        up = jnp.dot((col >= row).astype(F32), uu, precision=HIGHEST, preferred_element_type=F32)
        first = lax.broadcasted_iota(jnp.int32, (ln, nh), 1) < nh // 2
        o_ref[...] = jnp.where(first, up, down) if reverse else jnp.where(first, down, up)

    spec = pl.BlockSpec((ln, nh), lambda i: (i, 0))
    return pl.pallas_call(body, name=name, grid=(t // ln,), out_shape=jax.ShapeDtypeStruct(u.shape, F32),
                          in_specs=[spec], out_specs=spec, compiler_params=_cparams())(u)


def _ssd_cumsum(e, u):
    @jax.custom_vjp
    def cs(u):
        return _chunk_cumsum(u, False, f"ssd_cumsum{e}")

    cs.defvjp(lambda u: (_chunk_cumsum(u, False, f"ssd_cumsum{e}"), None),
              lambda _, g: (_chunk_cumsum(g, True, f"ssd_cumsum_bwd{e}"),))
    return cs(u)


def _mxu_dots():
    c_nn, c_nt, c_tn = (((1,), (0,)), ((), ())), (((1,), (1,)), ((), ())), (((0,), (0,)), ((), ()))

    def dot(a, b, dn):
        return lax.dot_general(a.astype(BF16), b.astype(BF16), dn, preferred_element_type=F32)

    def make(dn, dn_da, a_first, dn_db, b_first):
        @jax.custom_vjp
        def f(a, b):
            return dot(a, b, dn)

        def bwd(res, g):
            a, b = res
            da = dot(g, b, dn_da) if a_first else dot(b, g, dn_da)
            db = dot(g, a, dn_db) if b_first else dot(a, g, dn_db)
            return da, db

        f.defvjp(lambda a, b: (dot(a, b, dn), (a, b)), bwd)
        return f

    nn = make(c_nn, c_nt, True, c_tn, False)
    nt = make(c_nt, c_nn, True, c_tn, True)
    tn = make(c_tn, c_nt, False, c_nn, False)
    return nn, nt, tn


def _ssd_step(h, x, bm, cm, dt, dt_t, cs, cs_t, *, direction, hpg, pdim):
    ln = x.shape[0]
    hp = hpg * pdim
    nn, nt, tn = _mxu_dots()
    row = lax.broadcasted_iota(jnp.int32, (ln, ln), 0)
    colm = lax.broadcasted_iota(jnp.int32, (ln, ln), 1)
    valid = (colm - row) * (1 - 2 * direction) <= 0
    last = (ln - 1) * (1 - direction)
    tot = jnp.sum(jnp.where(lax.broadcasted_iota(jnp.int32, (ln, hpg), 0) == last, cs, 0.0), axis=0, keepdims=True)
    cbm = nt(cm, bm)
    lane_head = lax.broadcasted_iota(jnp.int32, (1, hp), 1) // pdim
    sub_head = lax.broadcasted_iota(jnp.int32, (hp, 1), 0) // pdim
    y = jnp.zeros((ln, hp), F32)
    es = jnp.zeros((ln, hp), F32)
    we = jnp.zeros((ln, hp), F32)
    dend = jnp.zeros((hp, 1), F32)
    for r in range(hpg):
        cc = cs[:, r:r + 1]
        cr = cs_t[r:r + 1, :]
        decay = jnp.exp(jnp.where(valid, cc - cr, NEG))
        mask = lane_head == r
        y = y + nn(cbm * decay * dt_t[r:r + 1, :], jnp.where(mask, x, 0.0))
        es = es + jnp.where(mask, jnp.exp(cc), 0.0)
        we = we + jnp.where(mask, jnp.exp(tot[:, r:r + 1] - cc) * dt[:, r:r + 1], 0.0)
        dend = dend + jnp.where(sub_head == r, jnp.exp(tot[:, r:r + 1]), 0.0)
    y = y + es * nt(cm, h)
    h_new = h * dend + tn(x * we, bm)
    return y, h_new


def _ssd_chunk_of(d, s, ncc, nc):
    return jnp.where(d == 0, s, jnp.where(s < ncc, ncc - 1 - s, nc - 1 - s + ncc))


SSD_GROUPS_PER_STEP = 2


def _ssd_specs(cfg, step_of):
    gp = SSD_GROUPS_PER_STEP
    hp, n, ln, hpg = cfg.hpg * SSD_HEAD_DIM, SSD_STATE, SSD_CHUNK, cfg.hpg
    ncc, nc = cfg.nctx // ln, cfg.t // ln
    b_off = cfg.ssd_width // (gp * n)
    c_off = (cfg.ssd_width + SSD_GROUPS * n) // (gp * n)
    assert SSD_GROUPS % gp == 0 and cfg.ssd_width % (gp * n) == 0 and (SSD_GROUPS * n) % (gp * n) == 0

    def ch(d, s):
        return _ssd_chunk_of(d, step_of(s), ncc, nc)

    return dict(
        x=pl.BlockSpec((ln, gp * hp), lambda d, g, s: (ch(d, s), g)),
        bm=pl.BlockSpec((ln, gp * n), lambda d, g, s: (ch(d, s), b_off + g)),
        cm=pl.BlockSpec((ln, gp * n), lambda d, g, s: (ch(d, s), c_off + g)),
        dt=pl.BlockSpec((1, gp, ln, hpg), lambda d, g, s: (d, g, ch(d, s), 0)),
        dt_t=pl.BlockSpec((1, gp, 8, ln), lambda d, g, s: (d, g, 0, ch(d, s))),
        y=pl.BlockSpec((1, ln, gp * hp), lambda d, g, s: (d, ch(d, s), g)),
        bc2=pl.BlockSpec((1, ln, gp * n), lambda d, g, s: (d, ch(d, s), g)),
        h=pl.BlockSpec((1, gp, 1, hp, n), lambda d, g, s: (d, g, step_of(s), 0, 0)),
    )


def _ssd_forward(cfg, name, xbc, dt, dt_t, cs, cs_t):
    gp = SSD_GROUPS_PER_STEP
    hp, n, hpg = cfg.hpg * SSD_HEAD_DIM, SSD_STATE, cfg.hpg
    nc = cfg.t // SSD_CHUNK
    sp = _ssd_specs(cfg, lambda s: s)

    def body(x_ref, b_ref, c_ref, dt_ref, dtt_ref, cs_ref, cst_ref, y_ref, hs_ref, h_ref):
        d, s = pl.program_id(0), pl.program_id(2)

        @pl.when(s == 0)
        def _():
            h_ref[...] = jnp.zeros_like(h_ref)

        for q in range(gp):
            h = h_ref[q]
            hs_ref[0, q, 0] = h
            y, h_new = _ssd_step(h, x_ref[:, q * hp:(q + 1) * hp], b_ref[:, q * n:(q + 1) * n], c_ref[:, q * n:(q + 1) * n],
                                 dt_ref[0, q], dtt_ref[0, q], cs_ref[0, q], cst_ref[0, q], direction=d, hpg=hpg,
                                 pdim=SSD_HEAD_DIM)
            y_ref[0, :, q * hp:(q + 1) * hp] = y
            h_ref[q] = h_new

    return pl.pallas_call(
        body, name=name, grid=(2, SSD_GROUPS // gp, nc),
        out_shape=(jax.ShapeDtypeStruct((2, cfg.t, cfg.ssd_width), F32),
                   jax.ShapeDtypeStruct((2, SSD_GROUPS, nc, hp, n), F32)),
        in_specs=[sp["x"], sp["bm"], sp["cm"], sp["dt"], sp["dt_t"], sp["dt"], sp["dt_t"]],
        out_specs=(sp["y"], sp["h"]), scratch_shapes=[pltpu.VMEM((gp, hp, n), F32)], compiler_params=_cparams(),
    )(xbc, xbc, xbc, dt, dt_t, cs, cs_t)


def _ssd_backward(cfg, name, xbc, dt, dt_t, cs, cs_t, hsave, dy):
    gp = SSD_GROUPS_PER_STEP
    hp, n, hpg = cfg.hpg * SSD_HEAD_DIM, SSD_STATE, cfg.hpg
    nc = cfg.t // SSD_CHUNK
    sp = _ssd_specs(cfg, lambda s: nc - 1 - s)

    def body(x_ref, b_ref, c_ref, dt_ref, dtt_ref, cs_ref, cst_ref, hs_ref, dy_ref,
             dx_ref, db_ref, dc_ref, ddt_ref, ddtt_ref, dcs_ref, dcst_ref, dh_ref):
        d, s = pl.program_id(0), pl.program_id(2)

        @pl.when(s == 0)
        def _():
            dh_ref[...] = jnp.zeros_like(dh_ref)

        step = functools.partial(_ssd_step, direction=d, hpg=hpg, pdim=SSD_HEAD_DIM)
        for q in range(gp):
            xc, nc_ = slice(q * hp, (q + 1) * hp), slice(q * n, (q + 1) * n)
            _, vjp = jax.vjp(step, hs_ref[0, q, 0], x_ref[:, xc], b_ref[:, nc_], c_ref[:, nc_], dt_ref[0, q], dtt_ref[0, q],
                             cs_ref[0, q], cst_ref[0, q])
            dh, dx, db, dc, ddt, ddtt, dcs, dcst = vjp((dy_ref[0, :, xc], dh_ref[q]))
            dh_ref[q] = dh
            dx_ref[0, :, xc] = dx
            db_ref[0, :, nc_] = db
            dc_ref[0, :, nc_] = dc
            ddt_ref[0, q] = ddt
            ddtt_ref[0, q] = ddtt
            dcs_ref[0, q] = dcs
            dcst_ref[0, q] = dcst

    gn = SSD_GROUPS * n
    return pl.pallas_call(
        body, name=name, grid=(2, SSD_GROUPS // gp, nc),
        out_shape=(jax.ShapeDtypeStruct((2, cfg.t, cfg.ssd_width), F32), jax.ShapeDtypeStruct((2, cfg.t, gn), F32),
                   jax.ShapeDtypeStruct((2, cfg.t, gn), F32), jax.ShapeDtypeStruct(dt.shape, F32),
                   jax.ShapeDtypeStruct(dt_t.shape, F32), jax.ShapeDtypeStruct(dt.shape, F32),
                   jax.ShapeDtypeStruct(dt_t.shape, F32)),
        in_specs=[sp["x"], sp["bm"], sp["cm"], sp["dt"], sp["dt_t"], sp["dt"], sp["dt_t"], sp["h"], sp["y"]],
        out_specs=(sp["y"], sp["bc2"], sp["bc2"], sp["dt"], sp["dt_t"], sp["dt"], sp["dt_t"]),
        scratch_shapes=[pltpu.VMEM((gp, hp, n), F32)], compiler_params=_cparams(),
    )(xbc, xbc, xbc, dt, dt_t, cs, cs_t, hsave, dy)


def _ssd_scan(cfg, e, xbc, dt, dt_t, cs, cs_t):
    @jax.custom_vjp
    def scan(xbc, dt, dt_t, cs, cs_t):
        return _ssd_forward(cfg, f"ssd_fwd{e}", xbc, dt, dt_t, cs, cs_t)[0]

    def fwd(xbc, dt, dt_t, cs, cs_t):
        y, hsave = _ssd_forward(cfg, f"ssd_fwd{e}", xbc, dt, dt_t, cs, cs_t)
        return y, (xbc, dt, dt_t, cs, cs_t, hsave)

    def bwd(res, dy):
        dx, db, dc, ddt, ddtt, dcs, dcst = _ssd_backward(cfg, f"ssd_bwd{e}", *res, dy)
        return jnp.concatenate([dx[0] + dx[1], db[0] + db[1], dc[0] + dc[1]], axis=1), ddt, ddtt, dcs, dcst

    scan.defvjp(fwd, bwd)
    return scan(xbc, dt, dt_t, cs, cs_t)


def _adam_math(w, g, m, v):
    m2 = ADAM_B1 * m + (1.0 - ADAM_B1) * g
    v2 = ADAM_B2 * v + (1.0 - ADAM_B2) * (g * g)
    m_hat = m2 / (1.0 - ADAM_B1 ** ADAM_STEP)
    v_hat = v2 / (1.0 - ADAM_B2 ** ADAM_STEP)
    delta = -ADAM_LR * (m_hat / (jnp.sqrt(v_hat) + ADAM_EPS) + ADAM_WD * w)
    return delta, m2, v2


def _adam_small(w, g, m, v, name):
    def body(w_ref, g_ref, m_ref, v_ref, d_ref, m2_ref, v2_ref):
        d_ref[...], m2_ref[...], v2_ref[...] = _adam_math(w_ref[...], g_ref[...], m_ref[...], v_ref[...])

    shp = jax.ShapeDtypeStruct(w.shape, F32)
    return pl.pallas_call(body, name=name, out_shape=(shp, shp, shp), in_specs=[VMEM] * 4, out_specs=(VMEM, VMEM, VMEM),
                          compiler_params=_cparams())(w, g, m, v)


def _adam_tiled(w, g, m, v, name):
    nl, r, c = w.shape
    tr = _pick(r, (256, 128, 64, 32, 16, 8))
    spec = pl.BlockSpec((1, tr, c), lambda l, t: (l, t, 0))

    def body(w_ref, g_ref, m_ref, v_ref, d_ref, m2_ref, v2_ref):
        d_ref[...], m2_ref[...], v2_ref[...] = _adam_math(w_ref[...], g_ref[...], m_ref[...], v_ref[...])

    shp = jax.ShapeDtypeStruct(w.shape, F32)
    return pl.pallas_call(body, name=name, grid=(nl, r // tr), out_shape=(shp, shp, shp), in_specs=[spec] * 4,
                          out_specs=(spec, spec, spec), compiler_params=_cparams())(w, g, m, v)


def _adam_sharded(w, m, v, part_a, part_b, *, transposed, name):
    nl, r, c = w.shape
    tr = _pick(r, (256, 128)) if transposed else _pick(r, (256, 128, 64, 32, 16))
    w_spec = pl.BlockSpec((1, tr, c), lambda l, t: (l, t, 0))
    if transposed:
        pa_spec = pl.BlockSpec((1, 1, c, tr), lambda l, t: (0, l, 0, t))
        pb_spec = pl.BlockSpec((3, 1, c, tr), lambda l, t: (0, l, 0, t))
    else:
        pa_spec = pl.BlockSpec((1, 1, tr, c), lambda l, t: (0, l, t, 0))
        pb_spec = pl.BlockSpec((3, 1, tr, c), lambda l, t: (0, l, t, 0))

    def body(w_ref, m_ref, v_ref, pa_ref, pb_ref, g_ref, d_ref, m2_ref, v2_ref):
        g = pa_ref[0, 0] + pb_ref[0, 0].astype(F32) + pb_ref[1, 0].astype(F32) + pb_ref[2, 0].astype(F32)
        if transposed:
            g = g.T
        g_ref[0] = g
        d_ref[0], m2_ref[0], v2_ref[0] = _adam_math(w_ref[0], g, m_ref[0], v_ref[0])

    shp = jax.ShapeDtypeStruct(w.shape, F32)
    return pl.pallas_call(
        body, name=name, grid=(nl, r // tr), out_shape=(shp, shp, shp, shp),
        in_specs=[w_spec, w_spec, w_spec, pa_spec, pb_spec], out_specs=(w_spec, w_spec, w_spec, w_spec),
        compiler_params=_cparams(),
    )(w, m, v, part_a, part_b)


def _pick_blocks(g, recv, axis, length, mode, out_dtype, name):
    nl = g.shape[0]
    n = {"cast": 4, "mine": 1, "send": 3}[mode]
    pos = [p.astype(jnp.int32).reshape(1) for p in _pos()]

    def chip(i, x, y, c):
        if mode == "cast":
            return i
        if mode == "mine":
            return 2 * x[0] + y[0]
        return 2 * jnp.where(i == 1, x[0], 1 - x[0]) + jnp.where(i == 0, y[0], 1 - y[0])

    def gb(i, x, y, c):
        return 2 * chip(i, x, y, c) + (1 - c[0] if mode == "cast" else c[0])

    blk_shape = list(g.shape)
    blk_shape[axis] = length
    if axis == 1:
        cols = g.shape[2]
        tc = _pick(cols, (512, 256, 128))
        g_spec = pl.BlockSpec((1, length, tc), lambda i, l, t, x, y, c: (l, gb(i, x, y, c), t))
        r_spec = pl.BlockSpec((1, 1, length, tc), lambda i, l, t, x, y, c: (chip(i, x, y, c), l, 0, t))
        o_spec = pl.BlockSpec((1, 1, length, tc), lambda i, l, t, x, y, c: (i, l, 0, t))
        grid = (n, nl, cols // tc)
    else:
        rows = g.shape[1]
        tr = _pick(rows, (512, 256, 128, 64, 32, 16))
        g_spec = pl.BlockSpec((1, tr, length), lambda i, l, t, x, y, c: (l, t, gb(i, x, y, c)))
        r_spec = pl.BlockSpec((1, 1, tr, length), lambda i, l, t, x, y, c: (chip(i, x, y, c), l, t, 0))
        o_spec = pl.BlockSpec((1, 1, tr, length), lambda i, l, t, x, y, c: (i, l, t, 0))
        grid = (n, nl, rows // tr)

    if recv is None:
        def body(x_ref, y_ref, c_ref, g_ref, o_ref):
            o_ref[0] = g_ref[...].astype(out_dtype)
        in_specs, args = [g_spec], (g,)
    else:
        def body(x_ref, y_ref, c_ref, g_ref, r_ref, o_ref):
            o_ref[0] = (g_ref[...] + r_ref[0].astype(F32)).astype(out_dtype)
        in_specs, args = [g_spec, r_spec], (g, recv)

    return pl.pallas_call(
        body, name=name, out_shape=jax.ShapeDtypeStruct((n, *blk_shape), out_dtype),
        grid_spec=pltpu.PrefetchScalarGridSpec(num_scalar_prefetch=3, grid=grid, in_specs=in_specs, out_specs=o_spec),
        compiler_params=_cparams(),
    )(*pos, *args)


def _flatten(arrs):
    flat = jnp.concatenate([a.reshape(-1).astype(F32) for a in arrs])
    n = flat.shape[0]
    n_pad = -(-n // 1024) * 1024
    return jnp.pad(flat, (0, n_pad - n)).reshape(n_pad // 128, 128)


def _unflatten(buf, shapes):
    flat = buf.reshape(-1)
    out, o = [], 0
    for s in shapes:
        n = math.prod(s)
        out.append(flat[o:o + n].reshape(s))
        o += n
    return out


def _rowwise(name, fn, rows, seg, shared, out_cols, out_dtypes, tile, nct):
    t = rows[0].shape[0]
    nr, ns, nsh, no = len(rows), len(seg), len(shared), len(out_cols)
    n_in = nr + ns + nsh

    def row_spec(c):
        return pl.BlockSpec((tile, c), lambda i: (i, 0))

    def seg_spec(c):
        return pl.BlockSpec((1, 1, c), lambda i: (jnp.where(i < nct, 0, 1), 0, 0))

    def whole_spec(shape):
        return pl.BlockSpec(shape, lambda i: (0, 0))

    in_specs = ([row_spec(r.shape[1]) for r in rows] + [seg_spec(s.shape[1]) for s in seg]
                + [whole_spec(s.shape) for s in shared])
    out_shapes = tuple(jax.ShapeDtypeStruct((t, c), dt) for c, dt in zip(out_cols, out_dtypes))
    out_specs = tuple(row_spec(c) for c in out_cols)

    def load(refs):
        return [r[0] if nr <= j < nr + ns else r[...] for j, r in enumerate(refs[:n_in])]

    def lift(args):
        return [a[:, None, :] if nr <= j < nr + ns else a for j, a in enumerate(args)]

    def forward(*args):
        def body(*refs):
            outs = fn(*load(refs))
            for o_ref, o in zip(refs[n_in:], outs):
                o_ref[...] = o

        return pl.pallas_call(body, name=name + "_fwd", grid=(t // tile,), out_shape=out_shapes, in_specs=in_specs,
                              out_specs=out_specs, compiler_params=_cparams())(*lift(args))

    def backward(args, cts):
        def body(*refs):
            i = pl.program_id(0)
            ct = tuple(r[...] for r in refs[n_in:n_in + no])
            d_refs = refs[n_in + no:]
            _, vjp = jax.vjp(fn, *load(refs))
            grads = vjp(ct)
            for ref, g in zip(d_refs[:nr], grads[:nr]):
                ref[...] = g
            first_seg = jnp.logical_or(i == 0, i == nct)
            for j in range(nr, n_in):
                ref, g = d_refs[j], grads[j]
                first = first_seg if j < nr + ns else i == 0
                g = g[None] if j < nr + ns else g

                @pl.when(first)
                def _(ref=ref, g=g):
                    ref[...] = g

                @pl.when(jnp.logical_not(first))
                def _(ref=ref, g=g):
                    ref[...] += g

        largs = lift(args)
        d_shapes = tuple(jax.ShapeDtypeStruct(a.shape, F32) for a in largs)
        ct_specs = [row_spec(c) for c in out_cols]
        outs = pl.pallas_call(body, name=name + "_bwd", grid=(t // tile,), out_shape=d_shapes,
                              in_specs=in_specs + ct_specs, out_specs=tuple(in_specs), compiler_params=_cparams())(*largs, *cts)
        return [o[:, 0, :] if nr <= j < nr + ns else o for j, o in enumerate(outs)]

    @jax.custom_vjp
    def prim(*args):
        return tuple(forward(*args))

    prim.defvjp(lambda *args: (tuple(forward(*args)), args), lambda args, cts: tuple(backward(args, cts)))
    return prim(*rows, *seg, *shared)


def _silu(x):
    return x * (1.0 / (1.0 + jnp.exp(-x)))


def _softplus(x):
    return jnp.maximum(x, 0.0) + jnp.log(1.0 + jnp.exp(-jnp.abs(x)))


def _dwconv(name, x, w, b, act, nctx):
    t, c = x.shape
    kk = w.shape[0]
    half = kk // 2
    tile = 128
    tc = _pick(c, (1024, 512, 256, 128))
    nt, nct, hb = t // tile, nctx // tile, tile // 8
    cur = pl.BlockSpec((tile, tc), lambda j, i: (i, j))
    prev = pl.BlockSpec((8, tc), lambda j, i: (jnp.maximum(i * hb - 1, 0), j))
    nxt = pl.BlockSpec((8, tc), lambda j, i: (jnp.minimum((i + 1) * hb, t // 8 - 1), j))
    w_spec = pl.BlockSpec((kk, tc), lambda j, i: (0, j))
    b_spec = pl.BlockSpec((1, tc), lambda j, i: (0, j))
    grid = (c // tc, nt)

    def extended(cur_ref, prev_ref, next_ref, i):
        has_prev = jnp.logical_and(i != 0, i != nct)
        has_next = jnp.logical_and(i != nct - 1, i != nt - 1)
        return jnp.concatenate([jnp.where(has_prev, prev_ref[...], 0.0), cur_ref[...],
                                jnp.where(has_next, next_ref[...], 0.0)], axis=0)

    def taps(ext, w_ref, lo, n):
        acc = None
        for k in range(kk):
            term = w_ref[k:k + 1, :] * ext[lo + k - half:lo + k - half + n]
            acc = term if acc is None else acc + term
        return acc

    def forward(x, w, b):
        def body(x_ref, xp_ref, xn_ref, w_ref, b_ref, o_ref):
            ext = extended(x_ref, xp_ref, xn_ref, pl.program_id(1))
            y = taps(ext, w_ref, 8, tile) + b_ref[...]
            o_ref[...] = _silu(y) if act else y

        return pl.pallas_call(body, name=name + "_fwd", grid=grid, out_shape=jax.ShapeDtypeStruct((t, c), F32),
                              in_specs=[cur, prev, nxt, w_spec, b_spec], out_specs=cur, compiler_params=_cparams())(x, x, x, w, b)

    def backward(x, w, b, g):
        m = tile + 2 * half

        def body(x_ref, xp_ref, xn_ref, g_ref, gp_ref, gn_ref, w_ref, b_ref, dx_ref, dw_ref, db_ref):
            i = pl.program_id(1)
            xe = extended(x_ref, xp_ref, xn_ref, i)
            dpre = extended(g_ref, gp_ref, gn_ref, i)[8 - half:8 - half + m]
            if act:
                pre = taps(xe, w_ref, 8 - half, m) + b_ref[...]
                sg = 1.0 / (1.0 + jnp.exp(-pre))
                dpre = dpre * (sg * (1.0 + pre * (1.0 - sg)))
            acc = None
            for k in range(kk):
                term = w_ref[k:k + 1, :] * dpre[2 * half - k:2 * half - k + tile]
                acc = term if acc is None else acc + term
            dx_ref[...] = acc
            dcur = dpre[half:half + tile]
            dw = jnp.concatenate([jnp.sum(dcur * xe[8 + k - half:8 + k - half + tile], axis=0, keepdims=True)
                                  for k in range(kk)], axis=0)
            db = jnp.sum(dcur, axis=0, keepdims=True)

            @pl.when(i == 0)
            def _():
                dw_ref[...] = dw
                db_ref[...] = db

            @pl.when(i != 0)
            def _():
                dw_ref[...] += dw
                db_ref[...] += db

        return pl.pallas_call(
            body, name=name + "_bwd", grid=grid,
            out_shape=(jax.ShapeDtypeStruct((t, c), F32), jax.ShapeDtypeStruct(w.shape, F32), jax.ShapeDtypeStruct(b.shape, F32)),
            in_specs=[cur, prev, nxt, cur, prev, nxt, w_spec, b_spec], out_specs=(cur, w_spec, b_spec),
            compiler_params=_cparams())(x, x, x, g, g, g, w, b)

    @jax.custom_vjp
    def conv(x, w, b):
        return forward(x, w, b)

    conv.defvjp(lambda x, w, b: (forward(x, w, b), (x, w, b)), lambda res, g: backward(*res, g))
    return conv(x, w, b)


def _odd_pre_tile(pch):
    half = pch.shape[1] // 2
    return (pch[:, :half] * pch[:, half:],)


def _odd_post_tile(pb, pg, yc):
    return ((_silu(pg) * (pb * yc)).astype(BF16),)


def _rms_rows(x):
    return x * lax.rsqrt(jnp.mean(x * x, axis=-1, keepdims=True) + EPS)


def _pre0_tile(x, scale, shift, g):
    return ((_rms_rows(x) * g * (1 + scale) + shift).astype(BF16),)


def _pre_tile(x, y_prev, gate, scale, shift, g):
    xn = x + gate * y_prev
    return xn, (_rms_rows(xn) * g * (1 + scale) + shift).astype(BF16)


def _loss_tile(x, y_prev, target, gate, weight):
    err = (x + gate * y_prev - target) * weight
    return (0.5 * jnp.mean(err * err, axis=-1, keepdims=True),)


def _mid_even_tile(pa, pd, qg, kg, dt_bias, a, *, naw, sw, nh):
    def heads_norm(u, g):
        return jnp.concatenate([_rms_rows(u[:, j:j + NA_HEAD_DIM]) * g for j in range(0, naw, NA_HEAD_DIM)], axis=1)

    q, gate, z = pa[:, :naw], pa[:, naw:2 * naw], pa[:, 2 * naw:2 * naw + sw]
    k, v = pa[:, 2 * naw + sw:3 * naw + sw], pa[:, 3 * naw + sw:]
    dt = _softplus(pd[:, :nh] + dt_bias)
    return heads_norm(q, qg), heads_norm(k, kg), v, _silu(gate), _silu(z), dt, dt * a


def _post_even_tile(ya, sg, y0, y1, xs, sz, dskip, g, *, sw):
    yz = (y0 + y1 + dskip * xs) * sz
    gw = sw // SSD_GROUPS
    yb = jnp.concatenate([_rms_rows(yz[:, j:j + gw]) for j in range(0, sw, gw)], axis=1) * g
    return (jnp.concatenate([ya * sg, yb], axis=1).astype(BF16),)


def _in_proj_split(a, w, gslot, widths, *, w_is_nk, tm, tn, name):
    t, kdim = a.shape
    starts = [sum(widths[:i]) for i in range(len(widths))]
    assert all(s % tn == 0 and wd % tn == 0 for s, wd in zip(starts, widths)), (starts, widths, tn)

    def piece(i, blocks):
        off = (starts[i] // blocks, 0) if w_is_nk else (0, starts[i] // blocks)
        ext = (widths[i], kdim) if w_is_nk else (kdim, widths[i])
        return off, ext

    def forward(a, w):
        outs = []
        for i in range(len(widths)):
            off, ext = piece(i, tn)
            outs.append(_matmul(a, w, tb=w_is_nk, tm=tm, tn=tn, tk=kdim, b_off=off, b_extent=ext, name=f"{name}_fwd{i}"))
        return tuple(outs)

    @jax.custom_vjp
    def proj(a, w, gslot):
        return forward(a, w)

    def bwd(res, gs):
        a, w = res
        tt = _pick(t, (768, 512, 256, 128))
        tkk = _pick(kdim, (2048, 1024, 512, 256, 128))
        tok = _pick(t, (1408, 768, 512, 256, 128))
        da, dws = None, []
        for i, g in enumerate(gs):
            big = 2 * tn if widths[i] % (2 * tn) == 0 and starts[i] % (2 * tn) == 0 else tn
            off, ext = piece(i, big)
            part = _matmul(g, w, tb=not w_is_nk, tm=tt, tn=tkk, tk=big, b_off=off, b_extent=ext, name=f"{name}_bwd_a{i}")
            da = part if da is None else da + part
            if w_is_nk:
                dws.append(_matmul(g, a, ta=True, tm=big, tn=tkk, tk=tok, name=f"{name}_bwd_w{i}"))
            else:
                dws.append(_matmul(a, g, ta=True, tm=_pick(kdim, (1024, 512, 256, 128)), tn=_pick(widths[i], (1024, 512, 256, 128)),
                                   tk=tok, name=f"{name}_bwd_w{i}"))
        dw = jnp.concatenate(dws, axis=0 if w_is_nk else 1)
        if dw.shape != w.shape:
            dw = jnp.pad(dw, [(0, w.shape[0] - dw.shape[0]), (0, w.shape[1] - dw.shape[1])])
        return da.astype(a.dtype), jnp.zeros_like(w), dw

    proj.defvjp(lambda a, w, gslot: (forward(a, w), (a, w)), bwd)
    return proj(a, w, gslot)


def _rms(x, g):
    return x * lax.rsqrt(jnp.mean(x * x, axis=-1, keepdims=True) + EPS) * g


def _dw_conv(x, w, b=None):
    k = w.shape[0]
    ln = x.shape[0]
    xp = jnp.pad(x, ((k // 2, k // 2), (0, 0)))
    y = sum(w[i][None, :] * xp[i:i + ln] for i in range(k))
    return y if b is None else y + b


def _conv_two(x, nctx, w, b=None):
    return jnp.concatenate([_dw_conv(x[:nctx], w, b), _dw_conv(x[nctx:], w, b)], axis=0)


def _mod_rows(nctx, seq, ctx_vec, lat_vec):
    return jnp.concatenate([jnp.broadcast_to(ctx_vec, (nctx, ctx_vec.shape[-1])),
                            jnp.broadcast_to(lat_vec, (seq, lat_vec.shape[-1]))], axis=0)


def _even_mixer(cfg, h, e, wd, update_ctx):
    d, nctx, seq, t = cfg.d, cfg.nctx, cfg.s, cfg.t
    naw, sw = cfg.na_width, cfg.ssd_width
    gn = SSD_GROUPS * SSD_STATE
    nh = 2 * SSD_GROUPS * cfg.hpg
    wa, wx = 4 * naw + sw, sw + 2 * gn
    pa, px, pd = _in_proj_split(h, wd["win_t"][e], wd["g_win_t"][e], (wa, wx, cfg.n_pad - wa - wx), w_is_nk=True,
                                tm=cfg.tiles_in[0], tn=cfg.tiles_in[1], name=f"in_even{e}")
    tile = cfg.tile_tok
    nct = nctx // tile
    a_neg = -jnp.exp(wd["ssd_a_log"][e]).reshape(1, nh)
    qn, kn, vv, sg, sz, dt, dta = _rowwise(
        f"mid_even{e}", functools.partial(_mid_even_tile, naw=naw, sw=sw, nh=nh), [pa, pd], [],
        [wd["q_norm_g"][e][None, :], wd["k_norm_g"][e][None, :], wd["ssd_dt_bias"][e].reshape(1, nh), a_neg],
        [naw, naw, naw, naw, sw, nh, nh], [F32] * 7, tile, nct)
    biasw = _bias_windows(wd["na_rpb"][e])
    ya = _na_attention(e, update_ctx, qn, kn, vv, biasw)

    xbc = _dwconv(f"ssd_conv{e}", px, wd["ssd_conv_w"][e], wd["ssd_conv_b"][e][None, :], True, nctx)
    xs = xbc[:, :sw]

    def arrange(u):
        u4 = u.reshape(t, 2, SSD_GROUPS, cfg.hpg)
        return u4.transpose(1, 2, 0, 3), jnp.pad(u4.transpose(1, 2, 3, 0), ((0, 0), (0, 0), (0, 8 - cfg.hpg), (0, 0)))

    dt4, dt_t = arrange(dt)
    cs4, cs_t = arrange(_ssd_cumsum(e, dta))
    y2 = _ssd_scan(cfg, e, xbc, dt4, dt_t, cs4, cs_t)
    dskip = jnp.repeat(wd["ssd_d"][e], SSD_HEAD_DIM)[None, :]
    (ycat,) = _rowwise(f"post_even{e}", functools.partial(_post_even_tile, sw=sw), [ya, sg, y2[0], y2[1], xs, sz], [],
                       [dskip, wd["ssd_norm_g"][e][None, :]], [naw + sw], [BF16], tile, nct)
    return _linear(ycat, wd["wout"][e], wd["g_wout"][e], w_is_nk=False, tiles=cfg.tiles_out_even, name=f"out_even{e}")


def _odd_mixer(cfg, h, o, wd):
    d, nctx = cfg.d, cfg.nctx
    tile = cfg.tile_tok
    nct = nctx // tile
    pb, pch, pg = _in_proj_split(h, wd["sc_win"][o], wd["g_sc_win"][o], (d, 2 * d, d), w_is_nk=False,
                                 tm=cfg.tiles_in_odd[0], tn=_pick(d, (1024, 512, 256, 128)), name=f"in_odd{o}")
    (cv,) = _rowwise(f"odd_pre{o}", _odd_pre_tile, [pch], [], [], [d], [F32], tile, nct)
    yc = _dwconv(f"sc_conv{o}", cv, wd["sc_conv_w"][o], jnp.zeros((1, d), F32), False, nctx)
    (u,) = _rowwise(f"odd_post{o}", _odd_post_tile, [pb, pg, yc], [], [], [d], [BF16], tile, nct)
    return _linear(u, wd["sc_wout"][o], wd["g_sc_wout"][o], w_is_nk=False, tiles=cfg.tiles_out_odd, name=f"out_odd{o}")


def _local_loss(cfg, x, ctx, target, mods, mods_c, wd):
    d, nctx, seq = cfg.d, cfg.nctx, cfg.s
    tile = cfg.tile_res
    nct = nctx // tile
    xx = jnp.concatenate([ctx, x], axis=0)
    y_prev = gate_prev = None
    for i in range(DEPTH):
        update_ctx = any(j % 2 == 0 for j in range(i + 1, DEPTH))
        shift = jnp.stack([mods_c[i, :d], mods[i, :d]])
        scale = jnp.stack([mods_c[i, d:2 * d], mods[i, d:2 * d]])
        g = wd["norm_g"][i][None, :]
        if y_prev is None:
            (h,) = _rowwise(f"pre{i}", _pre0_tile, [xx], [scale, shift], [g], [d], [BF16], tile, nct)
        else:
            xx, h = _rowwise(f"pre{i}", _pre_tile, [xx, y_prev], [gate_prev, scale, shift], [g], [d, d], [F32, BF16], tile, nct)
        y_prev = _even_mixer(cfg, h, i // 2, wd, update_ctx) if i % 2 == 0 else _odd_mixer(cfg, h, i // 2, wd)
        gate_c = mods_c[i, 2 * d:] if update_ctx else jnp.zeros((d,), F32)
        gate_prev = jnp.stack([gate_c, mods[i, 2 * d:]])
    target_rows = jnp.concatenate([jnp.zeros((nctx, d), F32), target], axis=0)
    weight = jnp.stack([jnp.zeros((d,), F32), jnp.ones((d,), F32)])
    (row_loss,) = _rowwise("loss", _loss_tile, [xx, y_prev, target_rows], [gate_prev, weight], [], [1], [F32], tile, nct)
    return jnp.sum(row_loss)


SMALL_REPLICATED = ["norm_g", "ssd_conv_b", "ssd_a_log", "ssd_dt_bias", "ssd_d", "ssd_norm_g", "q_norm_g", "k_norm_g", "na_rpb"]
WEIGHT_ORDER = ["c_ctx", "ada_w", "ada_b", "norm_g", "na_ssd_w_in", "ssd_conv_w", "ssd_conv_b", "ssd_a_log", "ssd_dt_bias",
                "ssd_d", "ssd_norm_g", "q_norm_g", "k_norm_g", "na_rpb", "na_ssd_w_out", "sc_w_in", "sc_conv_w", "sc_w_out"]


def kernel(x, c, ctx, c_ctx, ada_w, ada_b, norm_g, na_ssd_w_in, ssd_conv_w, ssd_conv_b, ssd_a_log, ssd_dt_bias, ssd_d, ssd_norm_g, q_norm_g, k_norm_g, na_rpb, na_ssd_w_out, sc_w_in, sc_conv_w, sc_w_out, loss_target, m_c_ctx, m_ada_w, m_ada_b, m_norm_g, m_na_ssd_w_in, m_ssd_conv_w, m_ssd_conv_b, m_ssd_a_log, m_ssd_dt_bias, m_ssd_d, m_ssd_norm_g, m_q_norm_g, m_k_norm_g, m_na_rpb, m_na_ssd_w_out, m_sc_w_in, m_sc_conv_w, m_sc_w_out, v_c_ctx, v_ada_w, v_ada_b, v_norm_g, v_na_ssd_w_in, v_ssd_conv_w, v_ssd_conv_b, v_ssd_a_log, v_ssd_dt_bias, v_ssd_d, v_ssd_norm_g, v_q_norm_g, v_k_norm_g, v_na_rpb, v_na_ssd_w_out, v_sc_w_in, v_sc_conv_w, v_sc_w_out):
    given = dict(locals())
    weights = {n: given[n] for n in WEIGHT_ORDER}
    mom_m = {n: given["m_" + n] for n in WEIGHT_ORDER}
    mom_v = {n: given["v_" + n] for n in WEIGHT_ORDER}

    d = x.shape[-1]
    seq, nctx = x.shape[1], ctx.shape[1]
    n_in_shard = na_ssd_w_in.shape[-1]
    n_in = n_in_shard * NDEV
    n_pad = -(-n_in // PAD_TO) * PAD_TO
    hpg = (d // SSD_HEAD_DIM) // SSD_GROUPS
    t = nctx + seq
    tm = _pick(t, (1408, 768, 512, 256, 128))
    cfg = SimpleNamespace(
        d=d, s=seq, nctx=nctx, t=t, hpg=hpg, na_width=NA_HEADS * NA_HEAD_DIM, ssd_width=d, n_in=n_in, n_pad=n_pad,
        tiles_in=(tm, _pick(n_pad, (512, 256, 128)), d),
        tiles_out_even=(tm, _pick(d, (1024, 512, 256, 128)), _pick(NA_HEADS * NA_HEAD_DIM + d, (1024, 512, 256, 128))),
        tiles_in_odd=(tm, _pick(4 * d, (1024, 512, 256, 128)), d),
        tiles_out_odd=(tm, _pick(d, (1024, 512, 256, 128)), d),
        tile_tok=128, tile_res=256,
    )
    me = _my_index()
    xl, cl, ctxl, tgt = x[0], c, ctx[0], loss_target[0]

    ncol = ada_w.shape[-1]
    c_rows = -(-d // 128)
    c_all = _small_allgather(jnp.pad(cl.reshape(-1), (0, c_rows * 128 - d)).reshape(c_rows, 128), "gather_c")[0]
    c_all = c_all.reshape(NDEV, -1)[:, :d]
    cond = jnp.concatenate([c_all, c_ctx[None, :], jnp.zeros((16 - NDEV - 1, d), F32)], axis=0)
    s16 = jax.nn.silu(cond)
    ada_b_mine = lax.dynamic_slice_in_dim(ada_b, me * ncol, ncol, axis=1)
    mod_part = jnp.stack([
        _matmul(s16, ada_w[i], tm=16, tn=_pick(ncol, (768, 512, 256, 128)), tk=d, name=f"adaln{i}") + ada_b_mine[i][None, :]
        for i in range(DEPTH)])
    mp_rows = DEPTH * 16 * ncol // 128
    mod_all = _small_allgather(mod_part.reshape(mp_rows, 128), "gather_mod")[0]
    mod_all = mod_all.reshape(NDEV, DEPTH, 16, ncol).transpose(1, 2, 0, 3).reshape(DEPTH, 16, NDEV * ncol)
    mods = lax.dynamic_index_in_dim(mod_all, me, axis=1, keepdims=False)
    mods_c = mod_all[:, NDEV]

    packed = _pack_transposed(na_ssd_w_in, "pack_w_in")
    wout_b = _cast_bf16(na_ssd_w_out, "cast_w_out")
    scwin_b = _cast_bf16(sc_w_in, "cast_sc_w_in")
    scwout_b = _cast_bf16(sc_w_out, "cast_sc_w_out")
    packed_all, wout_all, scwin_all, scwout_all = _big_allgather(
        [packed, wout_b, scwin_b, scwout_b], [1, 1, 2, 1], "gather_weights")
    win_t = _unpack(packed_all, n_pad, "unpack_w_in")
    conv_shapes = [ssd_conv_w.shape, sc_conv_w.shape]
    conv_all = _small_allgather(_flatten([ssd_conv_w, sc_conv_w]), "gather_conv")[0]
    conv_parts = [_unflatten(conv_all[j], conv_shapes) for j in range(NDEV)]
    ssd_conv_full = jnp.concatenate([cp[0] for cp in conv_parts], axis=-1)
    sc_conv_full = jnp.concatenate([cp[1] for cp in conv_parts], axis=-1)

    small = {n: weights[n] for n in SMALL_REPLICATED}
    small["ssd_conv_w"] = ssd_conv_full
    small["sc_conv_w"] = sc_conv_full
    gslots = dict(g_win_t=jnp.zeros(win_t.shape, F32), g_wout=jnp.zeros(wout_all.shape, F32),
                  g_sc_win=jnp.zeros(scwin_all.shape, F32), g_sc_wout=jnp.zeros(scwout_all.shape, F32))
    frozen = dict(win_t=win_t, wout=wout_all, sc_win=scwin_all, sc_wout=scwout_all)

    def loss_fn(xl, mods, mods_c, small, gslots):
        return _local_loss(cfg, xl, ctxl, tgt, mods, mods_c, {**small, **gslots, **frozen})

    loss_local, (g_x, g_mods, g_mods_c, g_small, g_big) = jax.value_and_grad(loss_fn, argnums=(0, 1, 2, 3, 4))(
        xl, mods, mods_c, small, gslots)

    small_names = SMALL_REPLICATED + ["ssd_conv_w", "sc_conv_w"]
    small_shapes = [g_small[n].shape for n in small_names] + [g_mods_c.shape]
    flat_small = _flatten([g_small[n] for n in small_names] + [g_mods_c])
    _, small_sum = _small_allgather(flat_small, "gather_small_grads")
    summed = _unflatten(small_sum, small_shapes)
    g_rep = dict(zip(small_names, summed[:-1]))
    g_mods_c_tot = summed[-1]
    gm_rows = DEPTH * NDEV * ncol // 128
    gm_all = _small_allgather(g_mods.reshape(gm_rows, 128), "gather_mod_grads")[0].reshape(NDEV, DEPTH, NDEV * ncol)
    dm = jnp.concatenate([gm_all.transpose(1, 0, 2), g_mods_c_tot[:, None, :],
                          jnp.zeros((DEPTH, 16 - NDEV - 1, NDEV * ncol), F32)], axis=1)
    grad_ada_b = jnp.sum(dm, axis=1)
    dm_mine = lax.dynamic_slice_in_dim(dm, me * ncol, ncol, axis=2)
    grad_ada_w = jnp.stack([
        _matmul(s16, dm_mine[i], ta=True, tm=_pick(d, (512, 256, 128)), tn=_pick(ncol, (768, 512, 256, 128)), tk=16,
                name=f"adaln_gw{i}") for i in range(DEPTH)])
    ds_part = sum(_matmul(dm_mine[i], ada_w[i], tb=True, tm=16, tn=_pick(d, (2048, 1024, 512, 256, 128)),
                          tk=_pick(ncol, (768, 512, 256, 128)), name=f"adaln_gs{i}") for i in range(DEPTH))[NDEV]
    ds_ctx = _small_allgather(jnp.pad(ds_part, (0, c_rows * 128 - d)).reshape(c_rows, 128), "gather_c_ctx_grad")[1]
    ds_ctx = ds_ctx.reshape(-1)[:d]
    sig = jax.nn.sigmoid(c_ctx)
    grad_c_ctx = ds_ctx * (sig * (1 + c_ctx * (1 - sig)))

    big = [g_big["g_win_t"], g_big["g_wout"], g_big["g_sc_win"], g_big["g_sc_wout"]]
    axes = [1, 1, 2, 1]
    lens = [n_in_shard, na_ssd_w_out.shape[1], sc_w_in.shape[2], sc_w_out.shape[1]]
    sends = [_pick_blocks(g, None, ax, ln, "cast", BF16, f"rs_cast{i}") for i, (g, ax, ln) in enumerate(zip(big, axes, lens))]
    recv_a = _rs_stage_a(sends, "reduce_scatter_d2d")
    part_a = [_pick_blocks(g, r, ax, ln, "mine", F32, f"pair_sum_mine{i}")
              for i, (g, r, ax, ln) in enumerate(zip(big, recv_a, axes, lens))]
    part_s = [_pick_blocks(g, r, ax, ln, "send", BF16, f"pair_sum_send{i}")
              for i, (g, r, ax, ln) in enumerate(zip(big, recv_a, axes, lens))]
    part_b = _rs_stage_b(part_s, "reduce_scatter_ici")

    res = {}
    res["na_ssd_w_in"] = _adam_sharded(na_ssd_w_in, m_na_ssd_w_in, v_na_ssd_w_in, part_a[0], part_b[0], transposed=True, name="adam_w_in")
    res["na_ssd_w_out"] = _adam_sharded(na_ssd_w_out, m_na_ssd_w_out, v_na_ssd_w_out, part_a[1], part_b[1], transposed=False, name="adam_w_out")
    res["sc_w_in"] = _adam_sharded(sc_w_in, m_sc_w_in, v_sc_w_in, part_a[2], part_b[2], transposed=False, name="adam_sc_w_in")
    res["sc_w_out"] = _adam_sharded(sc_w_out, m_sc_w_out, v_sc_w_out, part_a[3], part_b[3], transposed=False, name="adam_sc_w_out")

    grads = dict(g_rep)
    grads["ssd_conv_w"] = lax.dynamic_slice_in_dim(g_rep["ssd_conv_w"], me * ssd_conv_w.shape[-1], ssd_conv_w.shape[-1], axis=2)
    grads["sc_conv_w"] = lax.dynamic_slice_in_dim(g_rep["sc_conv_w"], me * sc_conv_w.shape[-1], sc_conv_w.shape[-1], axis=2)
    grads["c_ctx"] = grad_c_ctx
    res["ada_w"] = (grad_ada_w, *_adam_tiled(ada_w, grad_ada_w, m_ada_w, v_ada_w, "adam_ada_w"))
    grads["ada_b"] = grad_ada_b
    rest = [n for n in WEIGHT_ORDER if n not in res]
    shapes = [weights[n].shape for n in rest]
    d_flat, m_flat, v_flat = _adam_small(_flatten([weights[n] for n in rest]), _flatten([grads[n] for n in rest]),
                                         _flatten([mom_m[n] for n in rest]), _flatten([mom_v[n] for n in rest]), "adam_small")
    for n, dd, mm, vv in zip(rest, _unflatten(d_flat, shapes), _unflatten(m_flat, shapes), _unflatten(v_flat, shapes)):
        res[n] = (grads[n], dd, mm, vv)

    loss = lax.psum(loss_local, ("x", "y", "c"))
    return (loss, g_x[None], *[res[n][0] for n in WEIGHT_ORDER], *[res[n][1] for n in WEIGHT_ORDER],
            *[res[n][2] for n in WEIGHT_ORDER], *[res[n][3] for n in WEIGHT_ORDER])
```

```python
import functools
import math
from types import SimpleNamespace

import jax
import jax.numpy as jnp
from jax import lax
from jax.experimental import pallas as pl
from jax.experimental.pallas import tpu as pltpu

F32 = jnp.float32
BF16 = jnp.bfloat16
U32 = jnp.uint32
HIGHEST = lax.Precision.HIGHEST
MESH = pl.DeviceIdType.MESH
ANY = pl.BlockSpec(memory_space=pl.ANY)
VMEM = pl.BlockSpec(memory_space=pltpu.VMEM)

NDEV = 8
DEPTH = 4
GRID_W = 64
EPS = 1e-6
NA_HEADS = 16
NA_HEAD_DIM = 128
NA_KH = 8
NA_KW = 16
SSD_HEAD_DIM = 64
SSD_GROUPS = 8
SSD_STATE = 128
SSD_CONV = 5
SSD_CHUNK = 128
SC_CONV = 3
ADAM_LR = 0.001
ADAM_B1 = 0.9
ADAM_B2 = 0.999
ADAM_EPS = 1e-08
ADAM_WD = 0.01
ADAM_STEP = 10
NEG = -1e30
VMEM_LIMIT = 56 * 1024 * 1024
PAD_TO = 512


def _pos():
    return lax.axis_index("x"), lax.axis_index("y"), lax.axis_index("c")


def _my_index():
    x, y, c = _pos()
    return 4 * x + 2 * y + c


def _pick(n, prefs):
    for p in prefs:
        if n % p == 0:
            return p
    return n


def _cparams(**kw):
    return pltpu.CompilerParams(vmem_limit_bytes=VMEM_LIMIT, **kw)


def _small_allgather(v, name):
    rows, lanes = v.shape

    def body(x_ref, out_ref, sum_ref, send_sems, recv_sems):
        x, y, c = _pos()
        me = 4 * x + 2 * y + c
        out_ref[me] = x_ref[...]
        copies = []
        for k in range(1, NDEV):
            peer = (1 - x if k & 4 else x, 1 - y if k & 2 else y, 1 - c if k & 1 else c)
            cp = pltpu.make_async_remote_copy(src_ref=x_ref, dst_ref=out_ref.at[me], send_sem=send_sems.at[k - 1],
                                              recv_sem=recv_sems.at[k - 1], device_id=peer, device_id_type=MESH)
            cp.start()
            copies.append(cp)
        for cp in copies:
            cp.wait()
        acc = out_ref[0]
        for j in range(1, NDEV):
            acc = acc + out_ref[j]
        sum_ref[...] = acc

    return pl.pallas_call(
        body, name=name,
        out_shape=(jax.ShapeDtypeStruct((NDEV, rows, lanes), v.dtype), jax.ShapeDtypeStruct((rows, lanes), v.dtype)),
        in_specs=[VMEM], out_specs=(VMEM, VMEM),
        scratch_shapes=[pltpu.SemaphoreType.DMA((NDEV - 1,)), pltpu.SemaphoreType.DMA((NDEV - 1,))],
        compiler_params=_cparams(),
    )(v)


def _window(ref, axis, idx, length):
    sl = [slice(None)] * len(ref.shape)
    sl[axis] = pl.ds(pl.multiple_of(idx * length, min(length & -length, 1024)), length)
    return ref.at[tuple(sl)]


def _big_allgather(shards, axes, name):
    n = len(shards)
    out_shapes = []
    for s, ax in zip(shards, axes):
        shp = list(s.shape)
        shp[ax] *= NDEV
        out_shapes.append(jax.ShapeDtypeStruct(tuple(shp), s.dtype))

    def body(*refs):
        xs, outs = refs[:n], refs[n:2 * n]
        send_sems, recv_sems, local_sems = refs[2 * n:]
        x, y, c = _pos()
        me, sib = (x, y, c), (x, y, 1 - c)
        chips = [(1 - x, y), (x, 1 - y), (1 - x, 1 - y)]

        def win(a, px, py, pc):
            return _window(outs[a], axes[a], 4 * px + 2 * py + pc, shards[a].shape[axes[a]])

        def copy(a, k, block, to, src=None):
            return pltpu.make_async_remote_copy(src_ref=win(a, *block) if src is None else src, dst_ref=win(a, *block),
                                                send_sem=send_sems.at[a * 7 + k], recv_sem=recv_sems.at[a * 7 + k],
                                                device_id=to, device_id_type=MESH)

        mine = [pltpu.make_async_copy(xs[a], win(a, *me), local_sems.at[a]) for a in range(n)]
        for cp in mine:
            cp.start()
        first = []
        for a in range(n):
            first.append(copy(a, 0, me, sib, src=xs[a]))
            first += [copy(a, 1 + j, me, (*chip, c), src=xs[a]) for j, chip in enumerate(chips)]
        for cp in first:
            cp.start()
        passed = []
        for j, chip in enumerate(chips):
            for a in range(n):
                copy(a, 1 + j, (*chip, c), me).wait_recv()
                cp = copy(a, 4 + j, (*chip, c), sib)
                cp.start()
                passed.append(cp)
        for a in range(n):
            copy(a, 0, sib, me).wait_recv()
            for j, chip in enumerate(chips):
                copy(a, 4 + j, (*chip, 1 - c), me).wait_recv()
        for cp in first + passed:
            cp.wait_send()
        for cp in mine:
            cp.wait()

    return pl.pallas_call(
        body, name=name, out_shape=tuple(out_shapes), in_specs=[ANY] * n, out_specs=tuple([ANY] * n),
        scratch_shapes=[pltpu.SemaphoreType.DMA((7 * n,)), pltpu.SemaphoreType.DMA((7 * n,)), pltpu.SemaphoreType.DMA((n,))],
        compiler_params=_cparams(),
    )(*shards)


def _rs_stage_a(sends, name):
    n = len(sends)
    out_shapes = [jax.ShapeDtypeStruct(s.shape, s.dtype) for s in sends]

    def body(*refs):
        ss, outs = refs[:n], refs[n:2 * n]
        send_sems, recv_sems = refs[2 * n:]
        x, y, c = _pos()
        sib = (x, y, 1 - c)
        copies = []
        for a in range(n):
            for k in range(4):
                cp = pltpu.make_async_remote_copy(src_ref=ss[a].at[k], dst_ref=outs[a].at[k], send_sem=send_sems.at[a * 4 + k],
                                                  recv_sem=recv_sems.at[a * 4 + k], device_id=sib, device_id_type=MESH)
                cp.start()
                copies.append(cp)
        for cp in copies:
            cp.wait()

    return pl.pallas_call(
        body, name=name, out_shape=tuple(out_shapes), in_specs=[ANY] * n, out_specs=tuple([ANY] * n),
        scratch_shapes=[pltpu.SemaphoreType.DMA((4 * n,)), pltpu.SemaphoreType.DMA((4 * n,))],
        compiler_params=_cparams(),
    )(*sends)


def _rs_stage_b(parts, name):
    n = len(parts)
    out_shapes = [jax.ShapeDtypeStruct(p.shape, p.dtype) for p in parts]

    def body(*refs):
        ps, outs = refs[:n], refs[n:2 * n]
        send_sems, recv_sems = refs[2 * n:]
        x, y, c = _pos()
        chips = [(1 - x, y), (x, 1 - y), (1 - x, 1 - y)]
        copies = []
        for a in range(n):
            for j, (px, py) in enumerate(chips):
                cp = pltpu.make_async_remote_copy(src_ref=ps[a].at[j], dst_ref=outs[a].at[j],
                                                  send_sem=send_sems.at[a * 3 + j], recv_sem=recv_sems.at[a * 3 + j],
                                                  device_id=(px, py, c), device_id_type=MESH)
                cp.start()
                copies.append(cp)
        for cp in copies:
            cp.wait()

    return pl.pallas_call(
        body, name=name, out_shape=tuple(out_shapes), in_specs=[ANY] * n, out_specs=tuple([ANY] * n),
        scratch_shapes=[pltpu.SemaphoreType.DMA((3 * n,)), pltpu.SemaphoreType.DMA((3 * n,))],
        compiler_params=_cparams(),
    )(*parts)


def _matmul(a, b, *, ta=False, tb=False, tm, tn, tk, name, b_off=(0, 0), b_extent=None):
    m, kdim = (a.shape[1], a.shape[0]) if ta else a.shape
    b_shape = b.shape if b_extent is None else b_extent
    n = b_shape[0] if tb else b_shape[1]
    assert (b_shape[1] if tb else b_shape[0]) == kdim, (a.shape, b_shape, ta, tb)
    assert m % tm == 0 and n % tn == 0 and kdim % tk == 0, (m, n, kdim, tm, tn, tk)
    o0, o1 = b_off
    nk = kdim // tk
    dn = (((0 if ta else 1,), (1 if tb else 0,)), ((), ()))

    def body(a_ref, b_ref, o_ref, acc_ref):
        k = pl.program_id(2)
        part = lax.dot_general(a_ref[...].astype(BF16), b_ref[...].astype(BF16), dn, preferred_element_type=F32)
        if nk == 1:
            o_ref[...] = part
        else:
            @pl.when(k == 0)
            def _():
                acc_ref[...] = part

            @pl.when(k > 0)
            def _():
                acc_ref[...] += part

            @pl.when(k == nk - 1)
            def _():
                o_ref[...] = acc_ref[...]

    a_spec = pl.BlockSpec((tk, tm), lambda i, j, k: (k, i)) if ta else pl.BlockSpec((tm, tk), lambda i, j, k: (i, k))
    b_spec = (pl.BlockSpec((tn, tk), lambda i, j, k: (j + o0, k + o1)) if tb
              else pl.BlockSpec((tk, tn), lambda i, j, k: (k + o0, j + o1)))
    acc_shape = (tm, tn) if nk > 1 else (8, 128)
    return pl.pallas_call(
        body, name=name, grid=(m // tm, n // tn, nk), out_shape=jax.ShapeDtypeStruct((m, n), F32),
        in_specs=[a_spec, b_spec], out_specs=pl.BlockSpec((tm, tn), lambda i, j, k: (i, j)),
        scratch_shapes=[pltpu.VMEM(acc_shape, F32)], compiler_params=_cparams(),
    )(a, b)


def _linear(a, w, gslot, *, w_is_nk, tiles, name):
    tm, tn, tk = tiles

    @jax.custom_vjp
    def lin(a, w, gslot):
        return _matmul(a, w, tb=w_is_nk, tm=tm, tn=tn, tk=tk, name=name + "_fwd")

    def fwd(a, w, gslot):
        return lin(a, w, gslot), (a, w)

    def bwd(res, g):
        a, w = res
        gb = g
        t, kdim = a.shape
        n = g.shape[1]
        tt = _pick(t, (768, 512, 256, 128))
        tkk = _pick(kdim, (2048, 1024, 512, 256, 128))
        tnn = _pick(n, (1024, 512, 256, 128))
        da = _matmul(gb, w, tb=not w_is_nk, tm=tt, tn=tkk, tk=tnn, name=name + "_bwd_a")
        tok = _pick(t, (1408, 768, 512, 256, 128))
        if w_is_nk:
            dw = _matmul(gb, a, ta=True, tm=tnn, tn=tkk, tk=tok, name=name + "_bwd_w")
        else:
            tkw = _pick(kdim, (1024, 512, 256, 128))
            tnw = _pick(n, (1024, 512, 256, 128))
            dw = _matmul(a, gb, ta=True, tm=tkw, tn=tnw, tk=tok, name=name + "_bwd_w")
        return da.astype(a.dtype), jnp.zeros_like(w), dw

    lin.defvjp(fwd, bwd)
    return lin(a, w, gslot)


def _pack_transposed(w, name):
    nl, kdim, r = w.shape
    half = kdim // 2
    tc = _pick(half, (256, 128))

    def body(lo_ref, hi_ref, o_ref):
        lo = pltpu.bitcast(lo_ref[0].astype(BF16).astype(F32).T, U32) >> 16
        hi = pltpu.bitcast(hi_ref[0].astype(BF16).astype(F32).T, U32) & jnp.uint32(0xFFFF0000)
        o_ref[0] = pltpu.bitcast(hi | lo, F32)

    nb = half // tc
    return pl.pallas_call(
        body, name=name, grid=(nl, nb), out_shape=jax.ShapeDtypeStruct((nl, r, half), F32),
        in_specs=[pl.BlockSpec((1, tc, r), lambda l, t: (l, t, 0)), pl.BlockSpec((1, tc, r), lambda l, t: (l, t + nb, 0))],
        out_specs=pl.BlockSpec((1, r, tc), lambda l, t: (l, 0, t)), compiler_params=_cparams(),
    )(w, w)


def _unpack(packed, n_pad, name):
    nl, n, half = packed.shape
    tr = math.gcd(math.gcd(n, n_pad - n), 64) if n_pad > n else _pick(n, (64, 32, 16))
    nin = n // tr

    def body(p_ref, o_ref):
        t = pl.program_id(1)

        @pl.when(t < nin)
        def _():
            u = pltpu.bitcast(p_ref[0], U32)
            o_ref[0, :, :half] = pltpu.bitcast(u << 16, F32).astype(BF16)
            o_ref[0, :, half:] = pltpu.bitcast(u & jnp.uint32(0xFFFF0000), F32).astype(BF16)

        @pl.when(t >= nin)
        def _():
            o_ref[...] = jnp.zeros_like(o_ref)

    return pl.pallas_call(
        body, name=name, grid=(nl, n_pad // tr), out_shape=jax.ShapeDtypeStruct((nl, n_pad, 2 * half), BF16),
        in_specs=[pl.BlockSpec((1, tr, half), lambda l, t: (l, jnp.minimum(t, nin - 1), 0))],
        out_specs=pl.BlockSpec((1, tr, 2 * half), lambda l, t: (l, t, 0)), compiler_params=_cparams(),
    )(packed)


def _cast_bf16(w, name):
    nl, r, c = w.shape
    tr = _pick(r, (512, 256, 128, 64, 32, 16))

    def body(w_ref, o_ref):
        o_ref[...] = w_ref[...].astype(BF16)

    return pl.pallas_call(
        body, name=name, grid=(nl, r // tr), out_shape=jax.ShapeDtypeStruct(w.shape, BF16),
        in_specs=[pl.BlockSpec((1, tr, c), lambda l, t: (l, t, 0))], out_specs=pl.BlockSpec((1, tr, c), lambda l, t: (l, t, 0)),
        compiler_params=_cparams(),
    )(w)


NA_ROWS_PER_STEP = 4


def _row_block(i):
    return slice(i * GRID_W, (i + 1) * GRID_W)


def _na_probs(qs, kb_ref, b_ref, step, rows, nctx):
    scale = NA_HEAD_DIM ** -0.5
    nt = (((1,), (1,)), ((), ()))
    kc = kb_ref[0:nctx, :]
    kws, starts, offs, s1 = [], [], [], []
    for i in range(NA_ROWS_PER_STEP):
        r = step * NA_ROWS_PER_STEP + i
        rs = jnp.clip(r - NA_KH // 2, 0, rows - NA_KH)
        offs.append(rs - r + NA_KH - 1)
        starts.append(pl.multiple_of(nctx + rs * GRID_W, GRID_W))
        kws.append(kb_ref[pl.ds(starts[i], NA_KH * GRID_W), :])
        s1.append(lax.dot_general(qs[_row_block(i)], kws[i], nt, preferred_element_type=F32) * scale + b_ref[0, offs[i]])
    s1 = jnp.concatenate(s1, axis=0)
    s2 = lax.dot_general(qs, kc, nt, preferred_element_type=F32) * scale
    m = jnp.maximum(jnp.max(s1, axis=-1, keepdims=True), jnp.max(s2, axis=-1, keepdims=True))
    e1 = jnp.exp(s1 - m)
    e2 = jnp.exp(s2 - m)
    inv = 1.0 / (jnp.sum(e1, axis=-1, keepdims=True) + jnp.sum(e2, axis=-1, keepdims=True))
    return kc, kws, starts, offs, e1 * inv, e2 * inv


def _ctx_probs(qs, kc):
    s = lax.dot_general(qs, kc, (((1,), (1,)), ((), ())), preferred_element_type=F32) * NA_HEAD_DIM ** -0.5
    e = jnp.exp(s - jnp.max(s, axis=-1, keepdims=True))
    return e * (1.0 / jnp.sum(e, axis=-1, keepdims=True))


def _na_geometry(t):
    blk = NA_ROWS_PER_STEP * GRID_W
    rows = (t - blk) // GRID_W
    assert rows % NA_ROWS_PER_STEP == 0 and blk + rows * GRID_W == t, (t, blk)
    return blk, rows


def _na_forward(name, update_ctx, q, k, v, biasw):
    t, width = q.shape
    dh, win = NA_HEAD_DIM, NA_KH * GRID_W
    blk, rows = _na_geometry(t)

    def body(q_ref, k_ref, v_ref, b_ref, o_ref, kb_ref, vb_ref):
        j = pl.program_id(1)
        qs = q_ref[...].astype(BF16)

        @pl.when(j == 0)
        def _():
            kb_ref[...] = k_ref[...].astype(BF16)
            vb_ref[...] = v_ref[...].astype(BF16)
            if update_ctx:
                p = _ctx_probs(qs, kb_ref[0:blk, :])
                o_ref[...] = jnp.dot(p.astype(BF16), vb_ref[0:blk, :], preferred_element_type=F32)
            else:
                o_ref[...] = jnp.zeros_like(o_ref)

        @pl.when(j > 0)
        def _():
            _, _, starts, _, p1, p2 = _na_probs(qs, kb_ref, b_ref, j - 1, rows, blk)
            p1b = p1.astype(BF16)
            o1 = [jnp.dot(p1b[_row_block(i)], vb_ref[pl.ds(starts[i], win), :], preferred_element_type=F32)
                  for i in range(NA_ROWS_PER_STEP)]
            o_ref[...] = jnp.concatenate(o1, axis=0) + jnp.dot(p2.astype(BF16), vb_ref[0:blk, :], preferred_element_type=F32)

    row_spec = pl.BlockSpec((blk, dh), lambda h, j: (j, h))
    all_spec = pl.BlockSpec((t, dh), lambda h, j: (0, h))
    b_spec = pl.BlockSpec((1, NA_KH, GRID_W, win), lambda h, j: (h, 0, 0, 0))
    return pl.pallas_call(
        body, name=name, grid=(NA_HEADS, 1 + rows // NA_ROWS_PER_STEP), out_shape=jax.ShapeDtypeStruct((t, width), F32),
        in_specs=[row_spec, all_spec, all_spec, b_spec], out_specs=row_spec,
        scratch_shapes=[pltpu.VMEM((t, dh), BF16), pltpu.VMEM((t, dh), BF16)], compiler_params=_cparams(),
    )(q, k, v, biasw)


def _na_backward(name, update_ctx, q, k, v, biasw, do):
    t, width = q.shape
    dh, win = NA_HEAD_DIM, NA_KH * GRID_W
    blk, rows = _na_geometry(t)
    scale = dh ** -0.5
    nt = (((1,), (1,)), ((), ()))
    tn = (((0,), (0,)), ((), ()))

    def body(q_ref, k_ref, v_ref, b_ref, do_ref, dq_ref, dk_ref, dv_ref, db_ref, kb_ref, vb_ref):
        j = pl.program_id(1)
        qs = q_ref[...].astype(BF16)
        dob = do_ref[...].astype(BF16)
        ctx = slice(0, blk)

        @pl.when(j == 0)
        def _():
            kb_ref[...] = k_ref[...].astype(BF16)
            vb_ref[...] = v_ref[...].astype(BF16)
            dk_ref[...] = jnp.zeros_like(dk_ref)
            dv_ref[...] = jnp.zeros_like(dv_ref)
            db_ref[...] = jnp.zeros_like(db_ref)
            if update_ctx:
                kc, vc = kb_ref[ctx, :], vb_ref[ctx, :]
                p = _ctx_probs(qs, kc)
                dp = lax.dot_general(dob, vc, nt, preferred_element_type=F32)
                ds = p * (dp - jnp.sum(dp * p, axis=-1, keepdims=True))
                dsb = (ds * scale).astype(BF16)
                dq_ref[...] = jnp.dot(dsb, kc, preferred_element_type=F32)
                dv_ref[ctx, :] += lax.dot_general(p.astype(BF16), dob, tn, preferred_element_type=F32)
                dk_ref[ctx, :] += lax.dot_general(dsb, qs, tn, preferred_element_type=F32)
            else:
                dq_ref[...] = jnp.zeros_like(dq_ref)

        @pl.when(j > 0)
        def _():
            nr = range(NA_ROWS_PER_STEP)
            kc, kws, starts, offs, p1, p2 = _na_probs(qs, kb_ref, b_ref, j - 1, rows, blk)
            vc = vb_ref[ctx, :]
            p1b, p2b = p1.astype(BF16), p2.astype(BF16)
            dp1 = jnp.concatenate([lax.dot_general(dob[_row_block(i)], vb_ref[pl.ds(starts[i], win), :], nt,
                                                   preferred_element_type=F32) for i in nr], axis=0)
            dp2 = lax.dot_general(dob, vc, nt, preferred_element_type=F32)
            delta = jnp.sum(dp1 * p1, axis=-1, keepdims=True) + jnp.sum(dp2 * p2, axis=-1, keepdims=True)
            ds1 = p1 * (dp1 - delta)
            ds2 = p2 * (dp2 - delta)
            ds1b = (ds1 * scale).astype(BF16)
            ds2b = (ds2 * scale).astype(BF16)
            dq1 = [jnp.dot(ds1b[_row_block(i)], kws[i], preferred_element_type=F32) for i in nr]
            dq_ref[...] = jnp.concatenate(dq1, axis=0) + jnp.dot(ds2b, kc, preferred_element_type=F32)
            dv_ref[ctx, :] += lax.dot_general(p2b, dob, tn, preferred_element_type=F32)
            dk_ref[ctx, :] += lax.dot_general(ds2b, qs, tn, preferred_element_type=F32)
            for i in nr:
                sl = pl.ds(starts[i], win)
                db_ref[0, offs[i]] += ds1[_row_block(i)]
                dv_ref[sl, :] += lax.dot_general(p1b[_row_block(i)], dob[_row_block(i)], tn, preferred_element_type=F32)
                dk_ref[sl, :] += lax.dot_general(ds1b[_row_block(i)], qs[_row_block(i)], tn, preferred_element_type=F32)

    row_spec = pl.BlockSpec((blk, dh), lambda h, j: (j, h))
    all_spec = pl.BlockSpec((t, dh), lambda h, j: (0, h))
    b_spec = pl.BlockSpec((1, NA_KH, GRID_W, win), lambda h, j: (h, 0, 0, 0))
    full = jax.ShapeDtypeStruct((t, width), F32)
    return pl.pallas_call(
        body, name=name, grid=(NA_HEADS, 1 + rows // NA_ROWS_PER_STEP),
        out_shape=(full, full, full, jax.ShapeDtypeStruct(biasw.shape, F32)),
        in_specs=[row_spec, all_spec, all_spec, b_spec, row_spec], out_specs=(row_spec, all_spec, all_spec, b_spec),
        scratch_shapes=[pltpu.VMEM((t, dh), BF16), pltpu.VMEM((t, dh), BF16)], compiler_params=_cparams(),
    )(q, k, v, biasw, do)


def _na_attention(e, update_ctx, q, k, v, biasw):
    @jax.custom_vjp
    def attn(q, k, v, biasw):
        return _na_forward(f"na_fwd{e}", update_ctx, q, k, v, biasw)

    attn.defvjp(lambda q, k, v, biasw: (_na_forward(f"na_fwd{e}", update_ctx, q, k, v, biasw), (q, k, v, biasw)),
                lambda res, do: _na_backward(f"na_bwd{e}", update_ctx, *res, do))
    return attn(q, k, v, biasw)


def _bias_windows(rpb):
    col = jnp.arange(GRID_W)
    dc = jnp.clip(col[None, :] - col[:, None], -(NA_KW - 1), NA_KW - 1) + NA_KW - 1
    onehot = (dc[None] == jnp.arange(2 * NA_KW - 1)[:, None, None]).astype(F32)
    tq = jnp.einsum("hrd,dqk->hrqk", rpb, onehot, precision=HIGHEST)
    col_start = jnp.clip(col - NA_KW // 2, 0, GRID_W - NA_KW)
    in_win = (col[None, :] >= col_start[:, None]) & (col[None, :] < col_start[:, None] + NA_KW)
    wins = jnp.stack([tq[:, off:off + NA_KH] for off in range(NA_KH)], axis=1)
    wins = jnp.where(in_win[None, None, None], wins, NEG)
    return wins.transpose(0, 1, 3, 2, 4).reshape(rpb.shape[0], NA_KH, GRID_W, NA_KH * GRID_W)


def _chunk_cumsum(u, reverse, name):
    t, nh = u.shape
    ln = SSD_CHUNK

    def body(u_ref, o_ref):
        row = lax.broadcasted_iota(jnp.int32, (ln, ln), 0)
        col = lax.broadcasted_iota(jnp.int32, (ln, ln), 1)
        uu = u_ref[...]
        down = jnp.dot((col <= row).astype(F32), uu, precision=HIGHEST, preferred_element_type=F32)
        up = jnp.dot((col >= row).astype(F32), uu, precision=HIGHEST, preferred_element_type=F32)
        first = lax.broadcasted_iota(jnp.int32, (ln, nh), 1) < nh // 2
        o_ref[...] = jnp.where(first, up, down) if reverse else jnp.where(first, down, up)

    spec = pl.BlockSpec((ln, nh), lambda i: (i, 0))
    return pl.pallas_call(body, name=name, grid=(t // ln,), out_shape=jax.ShapeDtypeStruct(u.shape, F32),
                          in_specs=[spec], out_specs=spec, compiler_params=_cparams())(u)


def _ssd_cumsum(e, u):
    @jax.custom_vjp
    def cs(u):
        return _chunk_cumsum(u, False, f"ssd_cumsum{e}")

    cs.defvjp(lambda u: (_chunk_cumsum(u, False, f"ssd_cumsum{e}"), None),
              lambda _, g: (_chunk_cumsum(g, True, f"ssd_cumsum_bwd{e}"),))
    return cs(u)


def _mxu_dots():
    c_nn, c_nt, c_tn = (((1,), (0,)), ((), ())), (((1,), (1,)), ((), ())), (((0,), (0,)), ((), ()))

    def dot(a, b, dn):
        return lax.dot_general(a.astype(BF16), b.astype(BF16), dn, preferred_element_type=F32)

    def make(dn, dn_da, a_first, dn_db, b_first):
        @jax.custom_vjp
        def f(a, b):
            return dot(a, b, dn)

        def bwd(res, g):
            a, b = res
            da = dot(g, b, dn_da) if a_first else dot(b, g, dn_da)
            db = dot(g, a, dn_db) if b_first else dot(a, g, dn_db)
            return da, db

        f.defvjp(lambda a, b: (dot(a, b, dn), (a, b)), bwd)
        return f

    nn = make(c_nn, c_nt, True, c_tn, False)
    nt = make(c_nt, c_nn, True, c_tn, True)
    tn = make(c_tn, c_nt, False, c_nn, False)
    return nn, nt, tn


def _ssd_step(h, x, bm, cm, dt, dt_t, cs, cs_t, *, direction, hpg, pdim):
    ln = x.shape[0]
    hp = hpg * pdim
    nn, nt, tn = _mxu_dots()
    row = lax.broadcasted_iota(jnp.int32, (ln, ln), 0)
    colm = lax.broadcasted_iota(jnp.int32, (ln, ln), 1)
    valid = (colm - row) * (1 - 2 * direction) <= 0
    last = (ln - 1) * (1 - direction)
    tot = jnp.sum(jnp.where(lax.broadcasted_iota(jnp.int32, (ln, hpg), 0) == last, cs, 0.0), axis=0, keepdims=True)
    cbm = nt(cm, bm)
    lane_head = lax.broadcasted_iota(jnp.int32, (1, hp), 1) // pdim
    sub_head = lax.broadcasted_iota(jnp.int32, (hp, 1), 0) // pdim
    y = jnp.zeros((ln, hp), F32)
    es = jnp.zeros((ln, hp), F32)
    we = jnp.zeros((ln, hp), F32)
    dend = jnp.zeros((hp, 1), F32)
    for r in range(hpg):
        cc = cs[:, r:r + 1]
        cr = cs_t[r:r + 1, :]
        decay = jnp.exp(jnp.where(valid, cc - cr, NEG))
        mask = lane_head == r
        y = y + nn(cbm * decay * dt_t[r:r + 1, :], jnp.where(mask, x, 0.0))
        es = es + jnp.where(mask, jnp.exp(cc), 0.0)
        we = we + jnp.where(mask, jnp.exp(tot[:, r:r + 1] - cc) * dt[:, r:r + 1], 0.0)
        dend = dend + jnp.where(sub_head == r, jnp.exp(tot[:, r:r + 1]), 0.0)
    y = y + es * nt(cm, h)
    h_new = h * dend + tn(x * we, bm)
    return y, h_new


def _ssd_chunk_of(d, s, ncc, nc):
    return jnp.where(d == 0, s, jnp.where(s < ncc, ncc - 1 - s, nc - 1 - s + ncc))


SSD_GROUPS_PER_STEP = 4


def _groups_per_step():
    return math.gcd(SSD_GROUPS, SSD_GROUPS_PER_STEP)


def _ssd_specs(cfg, step_of):
    gp = _groups_per_step()
    hp, n, ln, hpg = cfg.hpg * SSD_HEAD_DIM, SSD_STATE, SSD_CHUNK, cfg.hpg
    ncc, nc = cfg.nctx // ln, cfg.t // ln
    b_off = cfg.ssd_width // (gp * n)
    c_off = (cfg.ssd_width + SSD_GROUPS * n) // (gp * n)
    assert SSD_GROUPS % gp == 0 and cfg.ssd_width % (gp * n) == 0 and (SSD_GROUPS * n) % (gp * n) == 0

    def ch(d, s):
        return _ssd_chunk_of(d, step_of(s), ncc, nc)

    return dict(
        x=pl.BlockSpec((ln, gp * hp), lambda d, g, s: (ch(d, s), g)),
        bm=pl.BlockSpec((ln, gp * n), lambda d, g, s: (ch(d, s), b_off + g)),
        cm=pl.BlockSpec((ln, gp * n), lambda d, g, s: (ch(d, s), c_off + g)),
        dt=pl.BlockSpec((1, gp, ln, hpg), lambda d, g, s: (d, g, ch(d, s), 0)),
        dt_t=pl.BlockSpec((1, gp, 8, ln), lambda d, g, s: (d, g, 0, ch(d, s))),
        y=pl.BlockSpec((1, ln, gp * hp), lambda d, g, s: (d, ch(d, s), g)),
        bc2=pl.BlockSpec((1, ln, gp * n), lambda d, g, s: (d, ch(d, s), g)),
        h=pl.BlockSpec((1, gp, 1, hp, n), lambda d, g, s: (d, g, step_of(s), 0, 0)),
    )


def _ssd_forward(cfg, name, xbc, dt, dt_t, cs, cs_t):
    gp = _groups_per_step()
    hp, n, hpg = cfg.hpg * SSD_HEAD_DIM, SSD_STATE, cfg.hpg
    nc = cfg.t // SSD_CHUNK
    sp = _ssd_specs(cfg, lambda s: s)

    def body(x_ref, b_ref, c_ref, dt_ref, dtt_ref, cs_ref, cst_ref, y_ref, hs_ref, h_ref):
        d, s = pl.program_id(0), pl.program_id(2)

        @pl.when(s == 0)
        def _():
            h_ref[...] = jnp.zeros_like(h_ref)

        for q in range(gp):
            h = h_ref[q]
            hs_ref[0, q, 0] = h
            y, h_new = _ssd_step(h, x_ref[:, q * hp:(q + 1) * hp], b_ref[:, q * n:(q + 1) * n], c_ref[:, q * n:(q + 1) * n],
                                 dt_ref[0, q], dtt_ref[0, q], cs_ref[0, q], cst_ref[0, q], direction=d, hpg=hpg,
                                 pdim=SSD_HEAD_DIM)
            y_ref[0, :, q * hp:(q + 1) * hp] = y
            h_ref[q] = h_new

    return pl.pallas_call(
        body, name=name, grid=(2, SSD_GROUPS // gp, nc),
        out_shape=(jax.ShapeDtypeStruct((2, cfg.t, cfg.ssd_width), F32),
                   jax.ShapeDtypeStruct((2, SSD_GROUPS, nc, hp, n), F32)),
        in_specs=[sp["x"], sp["bm"], sp["cm"], sp["dt"], sp["dt_t"], sp["dt"], sp["dt_t"]],
        out_specs=(sp["y"], sp["h"]), scratch_shapes=[pltpu.VMEM((gp, hp, n), F32)], compiler_params=_cparams(),
    )(xbc, xbc, xbc, dt, dt_t, cs, cs_t)


def _ssd_backward(cfg, name, xbc, dt, dt_t, cs, cs_t, hsave, dy):
    gp = _groups_per_step()
    hp, n, hpg = cfg.hpg * SSD_HEAD_DIM, SSD_STATE, cfg.hpg
    nc = cfg.t // SSD_CHUNK
    sp = _ssd_specs(cfg, lambda s: nc - 1 - s)

    def body(x_ref, b_ref, c_ref, dt_ref, dtt_ref, cs_ref, cst_ref, hs_ref, dy_ref,
             dx_ref, db_ref, dc_ref, ddt_ref, ddtt_ref, dcs_ref, dcst_ref, dh_ref):
        d, s = pl.program_id(0), pl.program_id(2)

        @pl.when(s == 0)
        def _():
            dh_ref[...] = jnp.zeros_like(dh_ref)

        step = functools.partial(_ssd_step, direction=d, hpg=hpg, pdim=SSD_HEAD_DIM)
        for q in range(gp):
            xc, nc_ = slice(q * hp, (q + 1) * hp), slice(q * n, (q + 1) * n)
            _, vjp = jax.vjp(step, hs_ref[0, q, 0], x_ref[:, xc], b_ref[:, nc_], c_ref[:, nc_], dt_ref[0, q], dtt_ref[0, q],
                             cs_ref[0, q], cst_ref[0, q])
            dh, dx, db, dc, ddt, ddtt, dcs, dcst = vjp((dy_ref[0, :, xc], dh_ref[q]))
            dh_ref[q] = dh
            dx_ref[0, :, xc] = dx
            db_ref[0, :, nc_] = db
            dc_ref[0, :, nc_] = dc
            ddt_ref[0, q] = ddt
            ddtt_ref[0, q] = ddtt
            dcs_ref[0, q] = dcs
            dcst_ref[0, q] = dcst

    gn = SSD_GROUPS * n
    return pl.pallas_call(
        body, name=name, grid=(2, SSD_GROUPS // gp, nc),
        out_shape=(jax.ShapeDtypeStruct((2, cfg.t, cfg.ssd_width), F32), jax.ShapeDtypeStruct((2, cfg.t, gn), F32),
                   jax.ShapeDtypeStruct((2, cfg.t, gn), F32), jax.ShapeDtypeStruct(dt.shape, F32),
                   jax.ShapeDtypeStruct(dt_t.shape, F32), jax.ShapeDtypeStruct(dt.shape, F32),
                   jax.ShapeDtypeStruct(dt_t.shape, F32)),
        in_specs=[sp["x"], sp["bm"], sp["cm"], sp["dt"], sp["dt_t"], sp["dt"], sp["dt_t"], sp["h"], sp["y"]],
        out_specs=(sp["y"], sp["bc2"], sp["bc2"], sp["dt"], sp["dt_t"], sp["dt"], sp["dt_t"]),
        scratch_shapes=[pltpu.VMEM((gp, hp, n), F32)], compiler_params=_cparams(),
    )(xbc, xbc, xbc, dt, dt_t, cs, cs_t, hsave, dy)


def _ssd_scan(cfg, e, xbc, dt, dt_t, cs, cs_t):
    @jax.custom_vjp
    def scan(xbc, dt, dt_t, cs, cs_t):
        return _ssd_forward(cfg, f"ssd_fwd{e}", xbc, dt, dt_t, cs, cs_t)[0]

    def fwd(xbc, dt, dt_t, cs, cs_t):
        y, hsave = _ssd_forward(cfg, f"ssd_fwd{e}", xbc, dt, dt_t, cs, cs_t)
        return y, (xbc, dt, dt_t, cs, cs_t, hsave)

    def bwd(res, dy):
        dx, db, dc, ddt, ddtt, dcs, dcst = _ssd_backward(cfg, f"ssd_bwd{e}", *res, dy)
        return jnp.concatenate([dx[0] + dx[1], db[0] + db[1], dc[0] + dc[1]], axis=1), ddt, ddtt, dcs, dcst

    scan.defvjp(fwd, bwd)
    return scan(xbc, dt, dt_t, cs, cs_t)


def _adam_math(w, g, m, v):
    m2 = ADAM_B1 * m + (1.0 - ADAM_B1) * g
    v2 = ADAM_B2 * v + (1.0 - ADAM_B2) * (g * g)
    m_hat = m2 / (1.0 - ADAM_B1 ** ADAM_STEP)
    v_hat = v2 / (1.0 - ADAM_B2 ** ADAM_STEP)
    delta = -ADAM_LR * (m_hat / (jnp.sqrt(v_hat) + ADAM_EPS) + ADAM_WD * w)
    return delta, m2, v2


def _adam_small(w, g, m, v, name):
    def body(w_ref, g_ref, m_ref, v_ref, d_ref, m2_ref, v2_ref):
        d_ref[...], m2_ref[...], v2_ref[...] = _adam_math(w_ref[...], g_ref[...], m_ref[...], v_ref[...])

    shp = jax.ShapeDtypeStruct(w.shape, F32)
    return pl.pallas_call(body, name=name, out_shape=(shp, shp, shp), in_specs=[VMEM] * 4, out_specs=(VMEM, VMEM, VMEM),
                          compiler_params=_cparams())(w, g, m, v)


def _adam_tiled(w, g, m, v, name):
    nl, r, c = w.shape
    tr = _pick(r, (256, 128, 64, 32, 16, 8))
    spec = pl.BlockSpec((1, tr, c), lambda l, t: (l, t, 0))

    def body(w_ref, g_ref, m_ref, v_ref, d_ref, m2_ref, v2_ref):
        d_ref[...], m2_ref[...], v2_ref[...] = _adam_math(w_ref[...], g_ref[...], m_ref[...], v_ref[...])

    shp = jax.ShapeDtypeStruct(w.shape, F32)
    return pl.pallas_call(body, name=name, grid=(nl, r // tr), out_shape=(shp, shp, shp), in_specs=[spec] * 4,
                          out_specs=(spec, spec, spec), compiler_params=_cparams())(w, g, m, v)


def _adam_sharded(w, m, v, part_a, part_b, *, transposed, name):
    nl, r, c = w.shape
    tr = _pick(r, (256, 128)) if transposed else _pick(r, (256, 128, 64, 32, 16))
    w_spec = pl.BlockSpec((1, tr, c), lambda l, t: (l, t, 0))
    if transposed:
        pa_spec = pl.BlockSpec((1, 1, c, tr), lambda l, t: (0, l, 0, t))
        pb_spec = pl.BlockSpec((3, 1, c, tr), lambda l, t: (0, l, 0, t))
    else:
        pa_spec = pl.BlockSpec((1, 1, tr, c), lambda l, t: (0, l, t, 0))
        pb_spec = pl.BlockSpec((3, 1, tr, c), lambda l, t: (0, l, t, 0))

    def body(w_ref, m_ref, v_ref, pa_ref, pb_ref, g_ref, d_ref, m2_ref, v2_ref):
        g = pa_ref[0, 0] + pb_ref[0, 0].astype(F32) + pb_ref[1, 0].astype(F32) + pb_ref[2, 0].astype(F32)
        if transposed:
            g = g.T
        g_ref[0] = g
        d_ref[0], m2_ref[0], v2_ref[0] = _adam_math(w_ref[0], g, m_ref[0], v_ref[0])

    shp = jax.ShapeDtypeStruct(w.shape, F32)
    return pl.pallas_call(
        body, name=name, grid=(nl, r // tr), out_shape=(shp, shp, shp, shp),
        in_specs=[w_spec, w_spec, w_spec, pa_spec, pb_spec], out_specs=(w_spec, w_spec, w_spec, w_spec),
        compiler_params=_cparams(),
    )(w, m, v, part_a, part_b)


def _pick_blocks(g, recv, axis, length, mode, out_dtype, name):
    nl = g.shape[0]
    n = {"cast": 4, "mine": 1, "send": 3}[mode]
    pos = [p.astype(jnp.int32).reshape(1) for p in _pos()]

    def chip(i, x, y, c):
        if mode == "cast":
            return i
        if mode == "mine":
            return 2 * x[0] + y[0]
        return 2 * jnp.where(i == 1, x[0], 1 - x[0]) + jnp.where(i == 0, y[0], 1 - y[0])

    def gb(i, x, y, c):
        return 2 * chip(i, x, y, c) + (1 - c[0] if mode == "cast" else c[0])

    blk_shape = list(g.shape)
    blk_shape[axis] = length
    if axis == 1:
        cols = g.shape[2]
        tc = _pick(cols, (512, 256, 128))
        g_spec = pl.BlockSpec((1, length, tc), lambda i, l, t, x, y, c: (l, gb(i, x, y, c), t))
        r_spec = pl.BlockSpec((1, 1, length, tc), lambda i, l, t, x, y, c: (chip(i, x, y, c), l, 0, t))
        o_spec = pl.BlockSpec((1, 1, length, tc), lambda i, l, t, x, y, c: (i, l, 0, t))
        grid = (n, nl, cols // tc)
    else:
        rows = g.shape[1]
        tr = _pick(rows, (512, 256, 128, 64, 32, 16))
        g_spec = pl.BlockSpec((1, tr, length), lambda i, l, t, x, y, c: (l, t, gb(i, x, y, c)))
        r_spec = pl.BlockSpec((1, 1, tr, length), lambda i, l, t, x, y, c: (chip(i, x, y, c), l, t, 0))
        o_spec = pl.BlockSpec((1, 1, tr, length), lambda i, l, t, x, y, c: (i, l, t, 0))
        grid = (n, nl, rows // tr)

    if recv is None:
        def body(x_ref, y_ref, c_ref, g_ref, o_ref):
            o_ref[0] = g_ref[...].astype(out_dtype)
        in_specs, args = [g_spec], (g,)
    else:
        def body(x_ref, y_ref, c_ref, g_ref, r_ref, o_ref):
            o_ref[0] = (g_ref[...] + r_ref[0].astype(F32)).astype(out_dtype)
        in_specs, args = [g_spec, r_spec], (g, recv)

    return pl.pallas_call(
        body, name=name, out_shape=jax.ShapeDtypeStruct((n, *blk_shape), out_dtype),
        grid_spec=pltpu.PrefetchScalarGridSpec(num_scalar_prefetch=3, grid=grid, in_specs=in_specs, out_specs=o_spec),
        compiler_params=_cparams(),
    )(*pos, *args)


def _flatten(arrs):
    flat = jnp.concatenate([a.reshape(-1).astype(F32) for a in arrs])
    n = flat.shape[0]
    n_pad = -(-n // 1024) * 1024
    return jnp.pad(flat, (0, n_pad - n)).reshape(n_pad // 128, 128)


def _unflatten(buf, shapes):
    flat = buf.reshape(-1)
    out, o = [], 0
    for s in shapes:
        n = math.prod(s)
        out.append(flat[o:o + n].reshape(s))
        o += n
    return out


def _rowwise(name, fn, rows, seg, shared, out_cols, out_dtypes, tile, nct):
    t = rows[0].shape[0]
    nr, ns, nsh, no = len(rows), len(seg), len(shared), len(out_cols)
    n_in = nr + ns + nsh

    def row_spec(c):
        return pl.BlockSpec((tile, c), lambda i: (i, 0))

    def seg_spec(c):
        return pl.BlockSpec((1, 1, c), lambda i: (jnp.where(i < nct, 0, 1), 0, 0))

    def whole_spec(shape):
        return pl.BlockSpec(shape, lambda i: (0, 0))

    in_specs = ([row_spec(r.shape[1]) for r in rows] + [seg_spec(s.shape[1]) for s in seg]
                + [whole_spec(s.shape) for s in shared])
    out_shapes = tuple(jax.ShapeDtypeStruct((t, c), dt) for c, dt in zip(out_cols, out_dtypes))
    out_specs = tuple(row_spec(c) for c in out_cols)

    def load(refs):
        return [r[0] if nr <= j < nr + ns else r[...] for j, r in enumerate(refs[:n_in])]

    def lift(args):
        return [a[:, None, :] if nr <= j < nr + ns else a for j, a in enumerate(args)]

    def forward(*args):
        def body(*refs):
            outs = fn(*load(refs))
            for o_ref, o in zip(refs[n_in:], outs):
                o_ref[...] = o

        return pl.pallas_call(body, name=name + "_fwd", grid=(t // tile,), out_shape=out_shapes, in_specs=in_specs,
                              out_specs=out_specs, compiler_params=_cparams())(*lift(args))

    def backward(args, cts):
        def body(*refs):
            i = pl.program_id(0)
            ct = tuple(r[...] for r in refs[n_in:n_in + no])
            d_refs = refs[n_in + no:]
            _, vjp = jax.vjp(fn, *load(refs))
            grads = vjp(ct)
            for ref, g in zip(d_refs[:nr], grads[:nr]):
                ref[...] = g
            first_seg = jnp.logical_or(i == 0, i == nct)
            for j in range(nr, n_in):
                ref, g = d_refs[j], grads[j]
                first = first_seg if j < nr + ns else i == 0
                g = g[None] if j < nr + ns else g

                @pl.when(first)
                def _(ref=ref, g=g):
                    ref[...] = g

                @pl.when(jnp.logical_not(first))
                def _(ref=ref, g=g):
                    ref[...] += g

        largs = lift(args)
        d_shapes = tuple(jax.ShapeDtypeStruct(a.shape, F32) for a in largs)
        ct_specs = [row_spec(c) for c in out_cols]
        outs = pl.pallas_call(body, name=name + "_bwd", grid=(t // tile,), out_shape=d_shapes,
                              in_specs=in_specs + ct_specs, out_specs=tuple(in_specs), compiler_params=_cparams())(*largs, *cts)
        return [o[:, 0, :] if nr <= j < nr + ns else o for j, o in enumerate(outs)]

    @jax.custom_vjp
    def prim(*args):
        return tuple(forward(*args))

    prim.defvjp(lambda *args: (tuple(forward(*args)), args), lambda args, cts: tuple(backward(args, cts)))
    return prim(*rows, *seg, *shared)


def _silu(x):
    return x * (1.0 / (1.0 + jnp.exp(-x)))


def _softplus(x):
    return jnp.maximum(x, 0.0) + jnp.log(1.0 + jnp.exp(-jnp.abs(x)))


def _dwconv(name, x, w, b, act, nctx):
    t, c = x.shape
    kk = w.shape[0]
    half = kk // 2
    tile = 128
    tc = _pick(c, (1024, 512, 256, 128))
    nt, nct, hb = t // tile, nctx // tile, tile // 8
    cur = pl.BlockSpec((tile, tc), lambda j, i: (i, j))
    prev = pl.BlockSpec((8, tc), lambda j, i: (jnp.maximum(i * hb - 1, 0), j))
    nxt = pl.BlockSpec((8, tc), lambda j, i: (jnp.minimum((i + 1) * hb, t // 8 - 1), j))
    w_spec = pl.BlockSpec((kk, tc), lambda j, i: (0, j))
    b_spec = pl.BlockSpec((1, tc), lambda j, i: (0, j))
    grid = (c // tc, nt)

    def extended(cur_ref, prev_ref, next_ref, i):
        has_prev = jnp.logical_and(i != 0, i != nct)
        has_next = jnp.logical_and(i != nct - 1, i != nt - 1)
        return jnp.concatenate([jnp.where(has_prev, prev_ref[...], 0.0), cur_ref[...],
                                jnp.where(has_next, next_ref[...], 0.0)], axis=0)

    def taps(ext, w_ref, lo, n):
        acc = None
        for k in range(kk):
            term = w_ref[k:k + 1, :] * ext[lo + k - half:lo + k - half + n]
            acc = term if acc is None else acc + term
        return acc

    def forward(x, w, b):
        def body(x_ref, xp_ref, xn_ref, w_ref, b_ref, o_ref):
            ext = extended(x_ref, xp_ref, xn_ref, pl.program_id(1))
            y = taps(ext, w_ref, 8, tile) + b_ref[...]
            o_ref[...] = _silu(y) if act else y

        return pl.pallas_call(body, name=name + "_fwd", grid=grid, out_shape=jax.ShapeDtypeStruct((t, c), F32),
                              in_specs=[cur, prev, nxt, w_spec, b_spec], out_specs=cur, compiler_params=_cparams())(x, x, x, w, b)

    def backward(x, w, b, g):
        m = tile + 2 * half

        def body(x_ref, xp_ref, xn_ref, g_ref, gp_ref, gn_ref, w_ref, b_ref, dx_ref, dw_ref, db_ref):
            i = pl.program_id(1)
            xe = extended(x_ref, xp_ref, xn_ref, i)
            dpre = extended(g_ref, gp_ref, gn_ref, i)[8 - half:8 - half + m]
            if act:
                pre = taps(xe, w_ref, 8 - half, m) + b_ref[...]
                sg = 1.0 / (1.0 + jnp.exp(-pre))
                dpre = dpre * (sg * (1.0 + pre * (1.0 - sg)))
            acc = None
            for k in range(kk):
                term = w_ref[k:k + 1, :] * dpre[2 * half - k:2 * half - k + tile]
                acc = term if acc is None else acc + term
            dx_ref[...] = acc
            dcur = dpre[half:half + tile]
            dw = jnp.concatenate([jnp.sum(dcur * xe[8 + k - half:8 + k - half + tile], axis=0, keepdims=True)
                                  for k in range(kk)], axis=0)
            db = jnp.sum(dcur, axis=0, keepdims=True)

            @pl.when(i == 0)
            def _():
                dw_ref[...] = dw
                db_ref[...] = db

            @pl.when(i != 0)
            def _():
                dw_ref[...] += dw
                db_ref[...] += db

        return pl.pallas_call(
            body, name=name + "_bwd", grid=grid,
            out_shape=(jax.ShapeDtypeStruct((t, c), F32), jax.ShapeDtypeStruct(w.shape, F32), jax.ShapeDtypeStruct(b.shape, F32)),
            in_specs=[cur, prev, nxt, cur, prev, nxt, w_spec, b_spec], out_specs=(cur, w_spec, b_spec),
            compiler_params=_cparams())(x, x, x, g, g, g, w, b)

    @jax.custom_vjp
    def conv(x, w, b):
        return forward(x, w, b)

    conv.defvjp(lambda x, w, b: (forward(x, w, b), (x, w, b)), lambda res, g: backward(*res, g))
    return conv(x, w, b)


def _odd_pre_tile(pch):
    half = pch.shape[1] // 2
    return (pch[:, :half] * pch[:, half:],)


def _odd_post_tile(pb, pg, yc):
    return ((_silu(pg) * (pb * yc)).astype(BF16),)


def _rms_rows(x):
    return x * lax.rsqrt(jnp.mean(x * x, axis=-1, keepdims=True) + EPS)


def _pre0_tile(x, scale, shift, g):
    return ((_rms_rows(x) * g * (1 + scale) + shift).astype(BF16),)


def _pre_tile(x, y_prev, gate, scale, shift, g):
    xn = x + gate * y_prev
    return xn, (_rms_rows(xn) * g * (1 + scale) + shift).astype(BF16)


def _loss_tile(x, y_prev, target, gate, weight):
    err = (x + gate * y_prev - target) * weight
    return (0.5 * jnp.mean(err * err, axis=-1, keepdims=True),)


def _mid_even_tile(pa, pd, qg, kg, dt_bias, a, *, naw, sw, nh):
    def heads_norm(u, g):
        return jnp.concatenate([_rms_rows(u[:, j:j + NA_HEAD_DIM]) * g for j in range(0, naw, NA_HEAD_DIM)], axis=1)

    q, gate, z = pa[:, :naw], pa[:, naw:2 * naw], pa[:, 2 * naw:2 * naw + sw]
    k, v = pa[:, 2 * naw + sw:3 * naw + sw], pa[:, 3 * naw + sw:]
    dt = _softplus(pd[:, :nh] + dt_bias)
    return heads_norm(q, qg), heads_norm(k, kg), v, _silu(gate), _silu(z), dt, dt * a


def _post_even_tile(ya, sg, y0, y1, xs, sz, dskip, g, *, sw):
    yz = (y0 + y1 + dskip * xs) * sz
    gw = sw // SSD_GROUPS
    yb = jnp.concatenate([_rms_rows(yz[:, j:j + gw]) for j in range(0, sw, gw)], axis=1) * g
    return (jnp.concatenate([ya * sg, yb], axis=1).astype(BF16),)


def _in_proj_split(a, w, gslot, widths, *, w_is_nk, tm, tn, name):
    t, kdim = a.shape
    starts = [sum(widths[:i]) for i in range(len(widths))]
    assert all(s % tn == 0 and wd % tn == 0 for s, wd in zip(starts, widths)), (starts, widths, tn)

    def piece(i, blocks):
        off = (starts[i] // blocks, 0) if w_is_nk else (0, starts[i] // blocks)
        ext = (widths[i], kdim) if w_is_nk else (kdim, widths[i])
        return off, ext

    def forward(a, w):
        outs = []
        for i in range(len(widths)):
            off, ext = piece(i, tn)
            outs.append(_matmul(a, w, tb=w_is_nk, tm=tm, tn=tn, tk=kdim, b_off=off, b_extent=ext, name=f"{name}_fwd{i}"))
        return tuple(outs)

    @jax.custom_vjp
    def proj(a, w, gslot):
        return forward(a, w)

    def bwd(res, gs):
        a, w = res
        tt = _pick(t, (768, 512, 256, 128))
        tkk = _pick(kdim, (2048, 1024, 512, 256, 128))
        tok = _pick(t, (1408, 768, 512, 256, 128))
        da, dws = None, []
        for i, g in enumerate(gs):
            big = 2 * tn if widths[i] % (2 * tn) == 0 and starts[i] % (2 * tn) == 0 else tn
            off, ext = piece(i, big)
            part = _matmul(g, w, tb=not w_is_nk, tm=tt, tn=tkk, tk=big, b_off=off, b_extent=ext, name=f"{name}_bwd_a{i}")
            da = part if da is None else da + part
            if w_is_nk:
                dws.append(_matmul(g, a, ta=True, tm=big, tn=tkk, tk=tok, name=f"{name}_bwd_w{i}"))
            else:
                dws.append(_matmul(a, g, ta=True, tm=_pick(kdim, (1024, 512, 256, 128)), tn=_pick(widths[i], (1024, 512, 256, 128)),
                                   tk=tok, name=f"{name}_bwd_w{i}"))
        dw = jnp.concatenate(dws, axis=0 if w_is_nk else 1)
        if dw.shape != w.shape:
            dw = jnp.pad(dw, [(0, w.shape[0] - dw.shape[0]), (0, w.shape[1] - dw.shape[1])])
        return da.astype(a.dtype), jnp.zeros_like(w), dw

    proj.defvjp(lambda a, w, gslot: (forward(a, w), (a, w)), bwd)
    return proj(a, w, gslot)


def _rms(x, g):
    return x * lax.rsqrt(jnp.mean(x * x, axis=-1, keepdims=True) + EPS) * g


def _dw_conv(x, w, b=None):
    k = w.shape[0]
    ln = x.shape[0]
    xp = jnp.pad(x, ((k // 2, k // 2), (0, 0)))
    y = sum(w[i][None, :] * xp[i:i + ln] for i in range(k))
    return y if b is None else y + b


def _conv_two(x, nctx, w, b=None):
    return jnp.concatenate([_dw_conv(x[:nctx], w, b), _dw_conv(x[nctx:], w, b)], axis=0)


def _mod_rows(nctx, seq, ctx_vec, lat_vec):
    return jnp.concatenate([jnp.broadcast_to(ctx_vec, (nctx, ctx_vec.shape[-1])),
                            jnp.broadcast_to(lat_vec, (seq, lat_vec.shape[-1]))], axis=0)


def _even_mixer(cfg, h, e, wd, update_ctx):
    d, nctx, seq, t = cfg.d, cfg.nctx, cfg.s, cfg.t
    naw, sw = cfg.na_width, cfg.ssd_width
    gn = SSD_GROUPS * SSD_STATE
    nh = 2 * SSD_GROUPS * cfg.hpg
    wa, wx = 4 * naw + sw, sw + 2 * gn
    pa, px, pd = _in_proj_split(h, wd["win_t"][e], wd["g_win_t"][e], (wa, wx, cfg.n_pad - wa - wx), w_is_nk=True,
                                tm=cfg.tiles_in[0], tn=cfg.tiles_in[1], name=f"in_even{e}")
    tile = cfg.tile_tok
    nct = nctx // tile
    a_neg = -jnp.exp(wd["ssd_a_log"][e]).reshape(1, nh)
    qn, kn, vv, sg, sz, dt, dta = _rowwise(
        f"mid_even{e}", functools.partial(_mid_even_tile, naw=naw, sw=sw, nh=nh), [pa, pd], [],
        [wd["q_norm_g"][e][None, :], wd["k_norm_g"][e][None, :], wd["ssd_dt_bias"][e].reshape(1, nh), a_neg],
        [naw, naw, naw, naw, sw, nh, nh], [F32] * 7, tile, nct)
    biasw = _bias_windows(wd["na_rpb"][e])
    ya = _na_attention(e, update_ctx, qn, kn, vv, biasw)

    xbc = _dwconv(f"ssd_conv{e}", px, wd["ssd_conv_w"][e], wd["ssd_conv_b"][e][None, :], True, nctx)
    xs = xbc[:, :sw]

    def arrange(u):
        u4 = u.reshape(t, 2, SSD_GROUPS, cfg.hpg)
        return u4.transpose(1, 2, 0, 3), jnp.pad(u4.transpose(1, 2, 3, 0), ((0, 0), (0, 0), (0, 8 - cfg.hpg), (0, 0)))

    dt4, dt_t = arrange(dt)
    cs4, cs_t = arrange(_ssd_cumsum(e, dta))
    y2 = _ssd_scan(cfg, e, xbc, dt4, dt_t, cs4, cs_t)
    dskip = jnp.repeat(wd["ssd_d"][e], SSD_HEAD_DIM)[None, :]
    (ycat,) = _rowwise(f"post_even{e}", functools.partial(_post_even_tile, sw=sw), [ya, sg, y2[0], y2[1], xs, sz], [],
                       [dskip, wd["ssd_norm_g"][e][None, :]], [naw + sw], [BF16], tile, nct)
    return _linear(ycat, wd["wout"][e], wd["g_wout"][e], w_is_nk=False, tiles=cfg.tiles_out_even, name=f"out_even{e}")


def _odd_mixer(cfg, h, o, wd):
    d, nctx = cfg.d, cfg.nctx
    tile = cfg.tile_tok
    nct = nctx // tile
    pb, pch, pg = _in_proj_split(h, wd["sc_win"][o], wd["g_sc_win"][o], (d, 2 * d, d), w_is_nk=False,
                                 tm=cfg.tiles_in_odd[0], tn=_pick(d, (1024, 512, 256, 128)), name=f"in_odd{o}")
    (cv,) = _rowwise(f"odd_pre{o}", _odd_pre_tile, [pch], [], [], [d], [F32], tile, nct)
    yc = _dwconv(f"sc_conv{o}", cv, wd["sc_conv_w"][o], jnp.zeros((1, d), F32), False, nctx)
    (u,) = _rowwise(f"odd_post{o}", _odd_post_tile, [pb, pg, yc], [], [], [d], [BF16], tile, nct)
    return _linear(u, wd["sc_wout"][o], wd["g_sc_wout"][o], w_is_nk=False, tiles=cfg.tiles_out_odd, name=f"out_odd{o}")


def _local_loss(cfg, x, ctx, target, mods, mods_c, wd):
    d, nctx, seq = cfg.d, cfg.nctx, cfg.s
    tile = cfg.tile_res
    nct = nctx // tile
    xx = jnp.concatenate([ctx, x], axis=0)
    y_prev = gate_prev = None
    for i in range(DEPTH):
        update_ctx = any(j % 2 == 0 for j in range(i + 1, DEPTH))
        shift = jnp.stack([mods_c[i, :d], mods[i, :d]])
        scale = jnp.stack([mods_c[i, d:2 * d], mods[i, d:2 * d]])
        g = wd["norm_g"][i][None, :]
        if y_prev is None:
            (h,) = _rowwise(f"pre{i}", _pre0_tile, [xx], [scale, shift], [g], [d], [BF16], tile, nct)
        else:
            xx, h = _rowwise(f"pre{i}", _pre_tile, [xx, y_prev], [gate_prev, scale, shift], [g], [d, d], [F32, BF16], tile, nct)
        y_prev = _even_mixer(cfg, h, i // 2, wd, update_ctx) if i % 2 == 0 else _odd_mixer(cfg, h, i // 2, wd)
        gate_c = mods_c[i, 2 * d:] if update_ctx else jnp.zeros((d,), F32)
        gate_prev = jnp.stack([gate_c, mods[i, 2 * d:]])
    target_rows = jnp.concatenate([jnp.zeros((nctx, d), F32), target], axis=0)
    weight = jnp.stack([jnp.zeros((d,), F32), jnp.ones((d,), F32)])
    (row_loss,) = _rowwise("loss", _loss_tile, [xx, y_prev, target_rows], [gate_prev, weight], [], [1], [F32], tile, nct)
    return jnp.sum(row_loss)


SMALL_REPLICATED = ["norm_g", "ssd_conv_b", "ssd_a_log", "ssd_dt_bias", "ssd_d", "ssd_norm_g", "q_norm_g", "k_norm_g", "na_rpb"]
WEIGHT_ORDER = ["c_ctx", "ada_w", "ada_b", "norm_g", "na_ssd_w_in", "ssd_conv_w", "ssd_conv_b", "ssd_a_log", "ssd_dt_bias",
                "ssd_d", "ssd_norm_g", "q_norm_g", "k_norm_g", "na_rpb", "na_ssd_w_out", "sc_w_in", "sc_conv_w", "sc_w_out"]


def kernel(x, c, ctx, c_ctx, ada_w, ada_b, norm_g, na_ssd_w_in, ssd_conv_w, ssd_conv_b, ssd_a_log, ssd_dt_bias, ssd_d, ssd_norm_g, q_norm_g, k_norm_g, na_rpb, na_ssd_w_out, sc_w_in, sc_conv_w, sc_w_out, loss_target, m_c_ctx, m_ada_w, m_ada_b, m_norm_g, m_na_ssd_w_in, m_ssd_conv_w, m_ssd_conv_b, m_ssd_a_log, m_ssd_dt_bias, m_ssd_d, m_ssd_norm_g, m_q_norm_g, m_k_norm_g, m_na_rpb, m_na_ssd_w_out, m_sc_w_in, m_sc_conv_w, m_sc_w_out, v_c_ctx, v_ada_w, v_ada_b, v_norm_g, v_na_ssd_w_in, v_ssd_conv_w, v_ssd_conv_b, v_ssd_a_log, v_ssd_dt_bias, v_ssd_d, v_ssd_norm_g, v_q_norm_g, v_k_norm_g, v_na_rpb, v_na_ssd_w_out, v_sc_w_in, v_sc_conv_w, v_sc_w_out):
    given = dict(locals())
    weights = {n: given[n] for n in WEIGHT_ORDER}
    mom_m = {n: given["m_" + n] for n in WEIGHT_ORDER}
    mom_v = {n: given["v_" + n] for n in WEIGHT_ORDER}

    d = x.shape[-1]
    seq, nctx = x.shape[1], ctx.shape[1]
    n_in_shard = na_ssd_w_in.shape[-1]
    n_in = n_in_shard * NDEV
    n_pad = -(-n_in // PAD_TO) * PAD_TO
    hpg = (d // SSD_HEAD_DIM) // SSD_GROUPS
    t = nctx + seq
    tm = _pick(t, (1408, 768, 512, 256, 128))
    cfg = SimpleNamespace(
        d=d, s=seq, nctx=nctx, t=t, hpg=hpg, na_width=NA_HEADS * NA_HEAD_DIM, ssd_width=d, n_in=n_in, n_pad=n_pad,
        tiles_in=(tm, _pick(n_pad, (512, 256, 128)), d),
        tiles_out_even=(tm, _pick(d, (1024, 512, 256, 128)), _pick(NA_HEADS * NA_HEAD_DIM + d, (1024, 512, 256, 128))),
        tiles_in_odd=(tm, _pick(4 * d, (1024, 512, 256, 128)), d),
        tiles_out_odd=(tm, _pick(d, (1024, 512, 256, 128)), d),
        tile_tok=128, tile_res=256,
    )
    me = _my_index()
    xl, cl, ctxl, tgt = x[0], c, ctx[0], loss_target[0]

    ncol = ada_w.shape[-1]
    c_rows = -(-d // 128)
    c_all = _small_allgather(jnp.pad(cl.reshape(-1), (0, c_rows * 128 - d)).reshape(c_rows, 128), "gather_c")[0]
    c_all = c_all.reshape(NDEV, -1)[:, :d]
    cond = jnp.concatenate([c_all, c_ctx[None, :], jnp.zeros((16 - NDEV - 1, d), F32)], axis=0)
    s16 = jax.nn.silu(cond)
    ada_b_mine = lax.dynamic_slice_in_dim(ada_b, me * ncol, ncol, axis=1)
    mod_part = jnp.stack([
        _matmul(s16, ada_w[i], tm=16, tn=_pick(ncol, (768, 512, 256, 128)), tk=d, name=f"adaln{i}") + ada_b_mine[i][None, :]
        for i in range(DEPTH)])
    mp_rows = DEPTH * 16 * ncol // 128
    mod_all = _small_allgather(mod_part.reshape(mp_rows, 128), "gather_mod")[0]
    mod_all = mod_all.reshape(NDEV, DEPTH, 16, ncol).transpose(1, 2, 0, 3).reshape(DEPTH, 16, NDEV * ncol)
    mods = lax.dynamic_index_in_dim(mod_all, me, axis=1, keepdims=False)
    mods_c = mod_all[:, NDEV]

    packed = _pack_transposed(na_ssd_w_in, "pack_w_in")
    wout_b = _cast_bf16(na_ssd_w_out, "cast_w_out")
    scwin_b = _cast_bf16(sc_w_in, "cast_sc_w_in")
    scwout_b = _cast_bf16(sc_w_out, "cast_sc_w_out")
    packed_all, wout_all, scwin_all, scwout_all = _big_allgather(
        [packed, wout_b, scwin_b, scwout_b], [1, 1, 2, 1], "gather_weights")
    win_t = _unpack(packed_all, n_pad, "unpack_w_in")
    conv_shapes = [ssd_conv_w.shape, sc_conv_w.shape]
    conv_all = _small_allgather(_flatten([ssd_conv_w, sc_conv_w]), "gather_conv")[0]
    conv_parts = [_unflatten(conv_all[j], conv_shapes) for j in range(NDEV)]
    ssd_conv_full = jnp.concatenate([cp[0] for cp in conv_parts], axis=-1)
    sc_conv_full = jnp.concatenate([cp[1] for cp in conv_parts], axis=-1)

    small = {n: weights[n] for n in SMALL_REPLICATED}
    small["ssd_conv_w"] = ssd_conv_full
    small["sc_conv_w"] = sc_conv_full
    gslots = dict(g_win_t=jnp.zeros(win_t.shape, F32), g_wout=jnp.zeros(wout_all.shape, F32),
                  g_sc_win=jnp.zeros(scwin_all.shape, F32), g_sc_wout=jnp.zeros(scwout_all.shape, F32))
    frozen = dict(win_t=win_t, wout=wout_all, sc_win=scwin_all, sc_wout=scwout_all)

    def loss_fn(xl, mods, mods_c, small, gslots):
        return _local_loss(cfg, xl, ctxl, tgt, mods, mods_c, {**small, **gslots, **frozen})

    loss_local, (g_x, g_mods, g_mods_c, g_small, g_big) = jax.value_and_grad(loss_fn, argnums=(0, 1, 2, 3, 4))(
        xl, mods, mods_c, small, gslots)

    small_names = SMALL_REPLICATED + ["ssd_conv_w", "sc_conv_w"]
    small_shapes = [g_small[n].shape for n in small_names] + [g_mods_c.shape]
    flat_small = _flatten([g_small[n] for n in small_names] + [g_mods_c])
    _, small_sum = _small_allgather(flat_small, "gather_small_grads")
    summed = _unflatten(small_sum, small_shapes)
    g_rep = dict(zip(small_names, summed[:-1]))
    g_mods_c_tot = summed[-1]
    gm_rows = DEPTH * NDEV * ncol // 128
    gm_all = _small_allgather(g_mods.reshape(gm_rows, 128), "gather_mod_grads")[0].reshape(NDEV, DEPTH, NDEV * ncol)
    dm = jnp.concatenate([gm_all.transpose(1, 0, 2), g_mods_c_tot[:, None, :],
                          jnp.zeros((DEPTH, 16 - NDEV - 1, NDEV * ncol), F32)], axis=1)
    grad_ada_b = jnp.sum(dm, axis=1)
    dm_mine = lax.dynamic_slice_in_dim(dm, me * ncol, ncol, axis=2)
    grad_ada_w = jnp.stack([
        _matmul(s16, dm_mine[i], ta=True, tm=_pick(d, (512, 256, 128)), tn=_pick(ncol, (768, 512, 256, 128)), tk=16,
                name=f"adaln_gw{i}") for i in range(DEPTH)])
    ds_part = sum(_matmul(dm_mine[i], ada_w[i], tb=True, tm=16, tn=_pick(d, (2048, 1024, 512, 256, 128)),
                          tk=_pick(ncol, (768, 512, 256, 128)), name=f"adaln_gs{i}") for i in range(DEPTH))[NDEV]
    ds_ctx = _small_allgather(jnp.pad(ds_part, (0, c_rows * 128 - d)).reshape(c_rows, 128), "gather_c_ctx_grad")[1]
    ds_ctx = ds_ctx.reshape(-1)[:d]
    sig = jax.nn.sigmoid(c_ctx)
    grad_c_ctx = ds_ctx * (sig * (1 + c_ctx * (1 - sig)))

    big = [g_big["g_win_t"], g_big["g_wout"], g_big["g_sc_win"], g_big["g_sc_wout"]]
    axes = [1, 1, 2, 1]
    lens = [n_in_shard, na_ssd_w_out.shape[1], sc_w_in.shape[2], sc_w_out.shape[1]]
    sends = [_pick_blocks(g, None, ax, ln, "cast", BF16, f"rs_cast{i}") for i, (g, ax, ln) in enumerate(zip(big, axes, lens))]
    recv_a = _rs_stage_a(sends, "reduce_scatter_d2d")
    part_a = [_pick_blocks(g, r, ax, ln, "mine", F32, f"pair_sum_mine{i}")
              for i, (g, r, ax, ln) in enumerate(zip(big, recv_a, axes, lens))]
    part_s = [_pick_blocks(g, r, ax, ln, "send", BF16, f"pair_sum_send{i}")
              for i, (g, r, ax, ln) in enumerate(zip(big, recv_a, axes, lens))]
    part_b = _rs_stage_b(part_s, "reduce_scatter_ici")

    res = {}
    res["na_ssd_w_in"] = _adam_sharded(na_ssd_w_in, m_na_ssd_w_in, v_na_ssd_w_in, part_a[0], part_b[0], transposed=True, name="adam_w_in")
    res["na_ssd_w_out"] = _adam_sharded(na_ssd_w_out, m_na_ssd_w_out, v_na_ssd_w_out, part_a[1], part_b[1], transposed=False, name="adam_w_out")
    res["sc_w_in"] = _adam_sharded(sc_w_in, m_sc_w_in, v_sc_w_in, part_a[2], part_b[2], transposed=False, name="adam_sc_w_in")
    res["sc_w_out"] = _adam_sharded(sc_w_out, m_sc_w_out, v_sc_w_out, part_a[3], part_b[3], transposed=False, name="adam_sc_w_out")

    grads = dict(g_rep)
    grads["ssd_conv_w"] = lax.dynamic_slice_in_dim(g_rep["ssd_conv_w"], me * ssd_conv_w.shape[-1], ssd_conv_w.shape[-1], axis=2)
    grads["sc_conv_w"] = lax.dynamic_slice_in_dim(g_rep["sc_conv_w"], me * sc_conv_w.shape[-1], sc_conv_w.shape[-1], axis=2)
    grads["c_ctx"] = grad_c_ctx
    res["ada_w"] = (grad_ada_w, *_adam_tiled(ada_w, grad_ada_w, m_ada_w, v_ada_w, "adam_ada_w"))
    grads["ada_b"] = grad_ada_b
    rest = [n for n in WEIGHT_ORDER if n not in res]
    shapes = [weights[n].shape for n in rest]
    d_flat, m_flat, v_flat = _adam_small(_flatten([weights[n] for n in rest]), _flatten([grads[n] for n in rest]),
                                         _flatten([mom_m[n] for n in rest]), _flatten([mom_v[n] for n in rest]), "adam_small")
    for n, dd, mm, vv in zip(rest, _unflatten(d_flat, shapes), _unflatten(m_flat, shapes), _unflatten(v_flat, shapes)):
        res[n] = (grads[n], dd, mm, vv)

    loss = lax.psum(loss_local, ("x", "y", "c"))
    return (loss, g_x[None], *[res[n][0] for n in WEIGHT_ORDER], *[res[n][1] for n in WEIGHT_ORDER],
            *[res[n][2] for n in WEIGHT_ORDER], *[res[n][3] for n in WEIGHT_ORDER])
```

```python
import functools
import math
from types import SimpleNamespace

import jax
import jax.numpy as jnp
from jax import lax
from jax.experimental import pallas as pl
from jax.experimental.pallas import tpu as pltpu

F32 = jnp.float32
BF16 = jnp.bfloat16
U32 = jnp.uint32
HIGHEST = lax.Precision.HIGHEST
MESH = pl.DeviceIdType.MESH
ANY = pl.BlockSpec(memory_space=pl.ANY)
VMEM = pl.BlockSpec(memory_space=pltpu.VMEM)

NDEV = 8
DEPTH = 4
GRID_W = 64
EPS = 1e-6
NA_HEADS = 16
NA_HEAD_DIM = 128
NA_KH = 8
NA_KW = 16
SSD_HEAD_DIM = 64
SSD_GROUPS = 8
SSD_STATE = 128
SSD_CONV = 5
SSD_CHUNK = 128
SC_CONV = 3
ADAM_LR = 0.001
ADAM_B1 = 0.9
ADAM_B2 = 0.999
ADAM_EPS = 1e-08
ADAM_WD = 0.01
ADAM_STEP = 10
NEG = -1e30
VMEM_LIMIT = 56 * 1024 * 1024
PAD_TO = 512


def _pos():
    return lax.axis_index("x"), lax.axis_index("y"), lax.axis_index("c")


def _my_index():
    x, y, c = _pos()
    return 4 * x + 2 * y + c


def _pick(n, prefs):
    for p in prefs:
        if n % p == 0:
            return p
    return n


def _cparams(**kw):
    return pltpu.CompilerParams(vmem_limit_bytes=VMEM_LIMIT, **kw)


def _small_allgather(v, name):
    rows, lanes = v.shape

    def body(x_ref, out_ref, sum_ref, send_sems, recv_sems):
        x, y, c = _pos()
        me = 4 * x + 2 * y + c
        out_ref[me] = x_ref[...]
        copies = []
        for k in range(1, NDEV):
            peer = (1 - x if k & 4 else x, 1 - y if k & 2 else y, 1 - c if k & 1 else c)
            cp = pltpu.make_async_remote_copy(src_ref=x_ref, dst_ref=out_ref.at[me], send_sem=send_sems.at[k - 1],
                                              recv_sem=recv_sems.at[k - 1], device_id=peer, device_id_type=MESH)
            cp.start()
            copies.append(cp)
        for cp in copies:
            cp.wait()
        acc = out_ref[0]
        for j in range(1, NDEV):
            acc = acc + out_ref[j]
        sum_ref[...] = acc

    return pl.pallas_call(
        body, name=name,
        out_shape=(jax.ShapeDtypeStruct((NDEV, rows, lanes), v.dtype), jax.ShapeDtypeStruct((rows, lanes), v.dtype)),
        in_specs=[VMEM], out_specs=(VMEM, VMEM),
        scratch_shapes=[pltpu.SemaphoreType.DMA((NDEV - 1,)), pltpu.SemaphoreType.DMA((NDEV - 1,))],
        compiler_params=_cparams(),
    )(v)


def _window(ref, axis, idx, length):
    sl = [slice(None)] * len(ref.shape)
    sl[axis] = pl.ds(pl.multiple_of(idx * length, min(length & -length, 1024)), length)
    return ref.at[tuple(sl)]


def _big_allgather(shards, axes, name):
    n = len(shards)
    out_shapes = []
    for s, ax in zip(shards, axes):
        shp = list(s.shape)
        shp[ax] *= NDEV
        out_shapes.append(jax.ShapeDtypeStruct(tuple(shp), s.dtype))

    def body(*refs):
        xs, outs = refs[:n], refs[n:2 * n]
        send_sems, recv_sems, local_sems = refs[2 * n:]
        x, y, c = _pos()
        me, sib = (x, y, c), (x, y, 1 - c)
        chips = [(1 - x, y), (x, 1 - y), (1 - x, 1 - y)]

        def win(a, px, py, pc):
            return _window(outs[a], axes[a], 4 * px + 2 * py + pc, shards[a].shape[axes[a]])

        def copy(a, k, block, to, src=None):
            return pltpu.make_async_remote_copy(src_ref=win(a, *block) if src is None else src, dst_ref=win(a, *block),
                                                send_sem=send_sems.at[a * 7 + k], recv_sem=recv_sems.at[a * 7 + k],
                                                device_id=to, device_id_type=MESH)

        mine = [pltpu.make_async_copy(xs[a], win(a, *me), local_sems.at[a]) for a in range(n)]
        for cp in mine:
            cp.start()
        first = []
        for a in range(n):
            first.append(copy(a, 0, me, sib, src=xs[a]))
            first += [copy(a, 1 + j, me, (*chip, c), src=xs[a]) for j, chip in enumerate(chips)]
        for cp in first:
            cp.start()
        passed = []
        for j, chip in enumerate(chips):
            for a in range(n):
                copy(a, 1 + j, (*chip, c), me).wait_recv()
                cp = copy(a, 4 + j, (*chip, c), sib)
                cp.start()
                passed.append(cp)
        for a in range(n):
            copy(a, 0, sib, me).wait_recv()
            for j, chip in enumerate(chips):
                copy(a, 4 + j, (*chip, 1 - c), me).wait_recv()
        for cp in first + passed:
            cp.wait_send()
        for cp in mine:
            cp.wait()

    return pl.pallas_call(
        body, name=name, out_shape=tuple(out_shapes), in_specs=[ANY] * n, out_specs=tuple([ANY] * n),
        scratch_shapes=[pltpu.SemaphoreType.DMA((7 * n,)), pltpu.SemaphoreType.DMA((7 * n,)), pltpu.SemaphoreType.DMA((n,))],
        compiler_params=_cparams(),
    )(*shards)


def _rs_stage_a(sends, name):
    n = len(sends)
    out_shapes = [jax.ShapeDtypeStruct(s.shape, s.dtype) for s in sends]

    def body(*refs):
        ss, outs = refs[:n], refs[n:2 * n]
        send_sems, recv_sems = refs[2 * n:]
        x, y, c = _pos()
        sib = (x, y, 1 - c)
        copies = []
        for a in range(n):
            for k in range(4):
                cp = pltpu.make_async_remote_copy(src_ref=ss[a].at[k], dst_ref=outs[a].at[k], send_sem=send_sems.at[a * 4 + k],
                                                  recv_sem=recv_sems.at[a * 4 + k], device_id=sib, device_id_type=MESH)
                cp.start()
                copies.append(cp)
        for cp in copies:
            cp.wait()

    return pl.pallas_call(
        body, name=name, out_shape=tuple(out_shapes), in_specs=[ANY] * n, out_specs=tuple([ANY] * n),
        scratch_shapes=[pltpu.SemaphoreType.DMA((4 * n,)), pltpu.SemaphoreType.DMA((4 * n,))],
        compiler_params=_cparams(),
    )(*sends)


def _rs_stage_b(parts, name):
    n = len(parts)
    out_shapes = [jax.ShapeDtypeStruct(p.shape, p.dtype) for p in parts]

    def body(*refs):
        ps, outs = refs[:n], refs[n:2 * n]
        send_sems, recv_sems = refs[2 * n:]
        x, y, c = _pos()
        chips = [(1 - x, y), (x, 1 - y), (1 - x, 1 - y)]
        copies = []
        for a in range(n):
            for j, (px, py) in enumerate(chips):
                cp = pltpu.make_async_remote_copy(src_ref=ps[a].at[j], dst_ref=outs[a].at[j],
                                                  send_sem=send_sems.at[a * 3 + j], recv_sem=recv_sems.at[a * 3 + j],
                                                  device_id=(px, py, c), device_id_type=MESH)
                cp.start()
                copies.append(cp)
        for cp in copies:
            cp.wait()

    return pl.pallas_call(
        body, name=name, out_shape=tuple(out_shapes), in_specs=[ANY] * n, out_specs=tuple([ANY] * n),
        scratch_shapes=[pltpu.SemaphoreType.DMA((3 * n,)), pltpu.SemaphoreType.DMA((3 * n,))],
        compiler_params=_cparams(),
    )(*parts)


def _matmul(a, b, *, ta=False, tb=False, tm, tn, tk, name, b_off=(0, 0), b_extent=None):
    m, kdim = (a.shape[1], a.shape[0]) if ta else a.shape
    b_shape = b.shape if b_extent is None else b_extent
    n = b_shape[0] if tb else b_shape[1]
    assert (b_shape[1] if tb else b_shape[0]) == kdim, (a.shape, b_shape, ta, tb)
    assert m % tm == 0 and n % tn == 0 and kdim % tk == 0, (m, n, kdim, tm, tn, tk)
    o0, o1 = b_off
    nk = kdim // tk
    dn = (((0 if ta else 1,), (1 if tb else 0,)), ((), ()))

    def body(a_ref, b_ref, o_ref, acc_ref):
        k = pl.program_id(2)
        part = lax.dot_general(a_ref[...].astype(BF16), b_ref[...].astype(BF16), dn, preferred_element_type=F32)
        if nk == 1:
            o_ref[...] = part
        else:
            @pl.when(k == 0)
            def _():
                acc_ref[...] = part

            @pl.when(k > 0)
            def _():
                acc_ref[...] += part

            @pl.when(k == nk - 1)
            def _():
                o_ref[...] = acc_ref[...]

    a_spec = pl.BlockSpec((tk, tm), lambda i, j, k: (k, i)) if ta else pl.BlockSpec((tm, tk), lambda i, j, k: (i, k))
    b_spec = (pl.BlockSpec((tn, tk), lambda i, j, k: (j + o0, k + o1)) if tb
              else pl.BlockSpec((tk, tn), lambda i, j, k: (k + o0, j + o1)))
    acc_shape = (tm, tn) if nk > 1 else (8, 128)
    return pl.pallas_call(
        body, name=name, grid=(m // tm, n // tn, nk), out_shape=jax.ShapeDtypeStruct((m, n), F32),
        in_specs=[a_spec, b_spec], out_specs=pl.BlockSpec((tm, tn), lambda i, j, k: (i, j)),
        scratch_shapes=[pltpu.VMEM(acc_shape, F32)], compiler_params=_cparams(),
    )(a, b)


def _linear(a, w, gslot, *, w_is_nk, tiles, name):
    tm, tn, tk = tiles

    @jax.custom_vjp
    def lin(a, w, gslot):
        return _matmul(a, w, tb=w_is_nk, tm=tm, tn=tn, tk=tk, name=name + "_fwd")

    def fwd(a, w, gslot):
        return lin(a, w, gslot), (a, w)

    def bwd(res, g):
        a, w = res
        gb = g
        t, kdim = a.shape
        n = g.shape[1]
        tt = _pick(t, (768, 512, 256, 128))
        tkk = _pick(kdim, (2048, 1024, 512, 256, 128))
        tnn = _pick(n, (1024, 512, 256, 128))
        da = _matmul(gb, w, tb=not w_is_nk, tm=tt, tn=tkk, tk=tnn, name=name + "_bwd_a")
        tok = _pick(t, (1408, 768, 512, 256, 128))
        if w_is_nk:
            dw = _matmul(gb, a, ta=True, tm=tnn, tn=tkk, tk=tok, name=name + "_bwd_w")
        else:
            tkw = _pick(kdim, (1024, 512, 256, 128))
            tnw = _pick(n, (1024, 512, 256, 128))
            dw = _matmul(a, gb, ta=True, tm=tkw, tn=tnw, tk=tok, name=name + "_bwd_w")
        return da.astype(a.dtype), jnp.zeros_like(w), dw

    lin.defvjp(fwd, bwd)
    return lin(a, w, gslot)


def _pack_transposed(w, name):
    nl, kdim, r = w.shape
    half = kdim // 2
    tc = _pick(half, (256, 128))

    def body(lo_ref, hi_ref, o_ref):
        lo = pltpu.bitcast(lo_ref[0].astype(BF16).astype(F32).T, U32) >> 16
        hi = pltpu.bitcast(hi_ref[0].astype(BF16).astype(F32).T, U32) & jnp.uint32(0xFFFF0000)
        o_ref[0] = pltpu.bitcast(hi | lo, F32)

    nb = half // tc
    return pl.pallas_call(
        body, name=name, grid=(nl, nb), out_shape=jax.ShapeDtypeStruct((nl, r, half), F32),
        in_specs=[pl.BlockSpec((1, tc, r), lambda l, t: (l, t, 0)), pl.BlockSpec((1, tc, r), lambda l, t: (l, t + nb, 0))],
        out_specs=pl.BlockSpec((1, r, tc), lambda l, t: (l, 0, t)), compiler_params=_cparams(),
    )(w, w)


def _unpack(packed, n_pad, name):
    nl, n, half = packed.shape
    tr = math.gcd(math.gcd(n, n_pad - n), 64) if n_pad > n else _pick(n, (64, 32, 16))
    nin = n // tr

    def body(p_ref, o_ref):
        t = pl.program_id(1)

        @pl.when(t < nin)
        def _():
            u = pltpu.bitcast(p_ref[0], U32)
            o_ref[0, :, :half] = pltpu.bitcast(u << 16, F32).astype(BF16)
            o_ref[0, :, half:] = pltpu.bitcast(u & jnp.uint32(0xFFFF0000), F32).astype(BF16)

        @pl.when(t >= nin)
        def _():
            o_ref[...] = jnp.zeros_like(o_ref)

    return pl.pallas_call(
        body, name=name, grid=(nl, n_pad // tr), out_shape=jax.ShapeDtypeStruct((nl, n_pad, 2 * half), BF16),
        in_specs=[pl.BlockSpec((1, tr, half), lambda l, t: (l, jnp.minimum(t, nin - 1), 0))],
        out_specs=pl.BlockSpec((1, tr, 2 * half), lambda l, t: (l, t, 0)), compiler_params=_cparams(),
    )(packed)


def _cast_bf16(w, name):
    nl, r, c = w.shape
    tr = _pick(r, (512, 256, 128, 64, 32, 16))

    def body(w_ref, o_ref):
        o_ref[...] = w_ref[...].astype(BF16)

    return pl.pallas_call(
        body, name=name, grid=(nl, r // tr), out_shape=jax.ShapeDtypeStruct(w.shape, BF16),
        in_specs=[pl.BlockSpec((1, tr, c), lambda l, t: (l, t, 0))], out_specs=pl.BlockSpec((1, tr, c), lambda l, t: (l, t, 0)),
        compiler_params=_cparams(),
    )(w)


NA_ROWS_PER_STEP = 4


def _row_block(i):
    return slice(i * GRID_W, (i + 1) * GRID_W)


def _na_probs(qs, kb_ref, b_ref, step, rows, nctx):
    scale = NA_HEAD_DIM ** -0.5
    nt = (((1,), (1,)), ((), ()))
    kc = kb_ref[0:nctx, :]
    kws, starts, offs, s1 = [], [], [], []
    for i in range(NA_ROWS_PER_STEP):
        r = step * NA_ROWS_PER_STEP + i
        rs = jnp.clip(r - NA_KH // 2, 0, rows - NA_KH)
        offs.append(rs - r + NA_KH - 1)
        starts.append(pl.multiple_of(nctx + rs * GRID_W, GRID_W))
        kws.append(kb_ref[pl.ds(starts[i], NA_KH * GRID_W), :])
        s1.append(lax.dot_general(qs[_row_block(i)], kws[i], nt, preferred_element_type=F32) * scale + b_ref[0, offs[i]])
    s1 = jnp.concatenate(s1, axis=0)
    s2 = lax.dot_general(qs, kc, nt, preferred_element_type=F32) * scale
    m = jnp.maximum(jnp.max(s1, axis=-1, keepdims=True), jnp.max(s2, axis=-1, keepdims=True))
    e1 = jnp.exp(s1 - m)
    e2 = jnp.exp(s2 - m)
    inv = 1.0 / (jnp.sum(e1, axis=-1, keepdims=True) + jnp.sum(e2, axis=-1, keepdims=True))
    return kc, kws, starts, offs, e1 * inv, e2 * inv


def _ctx_probs(qs, kc):
    s = lax.dot_general(qs, kc, (((1,), (1,)), ((), ())), preferred_element_type=F32) * NA_HEAD_DIM ** -0.5
    e = jnp.exp(s - jnp.max(s, axis=-1, keepdims=True))
    return e * (1.0 / jnp.sum(e, axis=-1, keepdims=True))


def _na_geometry(t):
    blk = NA_ROWS_PER_STEP * GRID_W
    rows = (t - blk) // GRID_W
    assert rows % NA_ROWS_PER_STEP == 0 and blk + rows * GRID_W == t, (t, blk)
    return blk, rows


def _na_forward(name, update_ctx, q, k, v, biasw):
    t, width = q.shape
    dh, win = NA_HEAD_DIM, NA_KH * GRID_W
    blk, rows = _na_geometry(t)

    def body(q_ref, k_ref, v_ref, b_ref, o_ref, kb_ref, vb_ref):
        j = pl.program_id(1)
        qs = q_ref[...].astype(BF16)

        @pl.when(j == 0)
        def _():
            kb_ref[...] = k_ref[...].astype(BF16)
            vb_ref[...] = v_ref[...].astype(BF16)
            if update_ctx:
                p = _ctx_probs(qs, kb_ref[0:blk, :])
                o_ref[...] = jnp.dot(p.astype(BF16), vb_ref[0:blk, :], preferred_element_type=F32)
            else:
                o_ref[...] = jnp.zeros_like(o_ref)

        @pl.when(j > 0)
        def _():
            _, _, starts, _, p1, p2 = _na_probs(qs, kb_ref, b_ref, j - 1, rows, blk)
            p1b = p1.astype(BF16)
            o1 = [jnp.dot(p1b[_row_block(i)], vb_ref[pl.ds(starts[i], win), :], preferred_element_type=F32)
                  for i in range(NA_ROWS_PER_STEP)]
            o_ref[...] = jnp.concatenate(o1, axis=0) + jnp.dot(p2.astype(BF16), vb_ref[0:blk, :], preferred_element_type=F32)

    row_spec = pl.BlockSpec((blk, dh), lambda h, j: (j, h))
    all_spec = pl.BlockSpec((t, dh), lambda h, j: (0, h))
    b_spec = pl.BlockSpec((1, NA_KH, GRID_W, win), lambda h, j: (h, 0, 0, 0))
    return pl.pallas_call(
        body, name=name, grid=(NA_HEADS, 1 + rows // NA_ROWS_PER_STEP), out_shape=jax.ShapeDtypeStruct((t, width), F32),
        in_specs=[row_spec, all_spec, all_spec, b_spec], out_specs=row_spec,
        scratch_shapes=[pltpu.VMEM((t, dh), BF16), pltpu.VMEM((t, dh), BF16)], compiler_params=_cparams(),
    )(q, k, v, biasw)


def _na_backward(name, update_ctx, q, k, v, biasw, do):
    t, width = q.shape
    dh, win = NA_HEAD_DIM, NA_KH * GRID_W
    blk, rows = _na_geometry(t)
    scale = dh ** -0.5
    nt = (((1,), (1,)), ((), ()))
    tn = (((0,), (0,)), ((), ()))

    def body(q_ref, k_ref, v_ref, b_ref, do_ref, dq_ref, dk_ref, dv_ref, db_ref, kb_ref, vb_ref):
        j = pl.program_id(1)
        qs = q_ref[...].astype(BF16)
        dob = do_ref[...].astype(BF16)
        ctx = slice(0, blk)

        @pl.when(j == 0)
        def _():
            kb_ref[...] = k_ref[...].astype(BF16)
            vb_ref[...] = v_ref[...].astype(BF16)
            dk_ref[...] = jnp.zeros_like(dk_ref)
            dv_ref[...] = jnp.zeros_like(dv_ref)
            db_ref[...] = jnp.zeros_like(db_ref)
            if update_ctx:
                kc, vc = kb_ref[ctx, :], vb_ref[ctx, :]
                p = _ctx_probs(qs, kc)
                dp = lax.dot_general(dob, vc, nt, preferred_element_type=F32)
                ds = p * (dp - jnp.sum(dp * p, axis=-1, keepdims=True))
                dsb = (ds * scale).astype(BF16)
                dq_ref[...] = jnp.dot(dsb, kc, preferred_element_type=F32)
                dv_ref[ctx, :] += lax.dot_general(p.astype(BF16), dob, tn, preferred_element_type=F32)
                dk_ref[ctx, :] += lax.dot_general(dsb, qs, tn, preferred_element_type=F32)
            else:
                dq_ref[...] = jnp.zeros_like(dq_ref)

        @pl.when(j > 0)
        def _():
            nr = range(NA_ROWS_PER_STEP)
            kc, kws, starts, offs, p1, p2 = _na_probs(qs, kb_ref, b_ref, j - 1, rows, blk)
            vc = vb_ref[ctx, :]
            p1b, p2b = p1.astype(BF16), p2.astype(BF16)
            dp1 = jnp.concatenate([lax.dot_general(dob[_row_block(i)], vb_ref[pl.ds(starts[i], win), :], nt,
                                                   preferred_element_type=F32) for i in nr], axis=0)
            dp2 = lax.dot_general(dob, vc, nt, preferred_element_type=F32)
            delta = jnp.sum(dp1 * p1, axis=-1, keepdims=True) + jnp.sum(dp2 * p2, axis=-1, keepdims=True)
            ds1 = p1 * (dp1 - delta)
            ds2 = p2 * (dp2 - delta)
            ds1b = (ds1 * scale).astype(BF16)
            ds2b = (ds2 * scale).astype(BF16)
            dq1 = [jnp.dot(ds1b[_row_block(i)], kws[i], preferred_element_type=F32) for i in nr]
            dq_ref[...] = jnp.concatenate(dq1, axis=0) + jnp.dot(ds2b, kc, preferred_element_type=F32)
            dv_ref[ctx, :] += lax.dot_general(p2b, dob, tn, preferred_element_type=F32)
            dk_ref[ctx, :] += lax.dot_general(ds2b, qs, tn, preferred_element_type=F32)
            for i in nr:
                sl = pl.ds(starts[i], win)
                db_ref[0, offs[i]] += ds1[_row_block(i)]
                dv_ref[sl, :] += lax.dot_general(p1b[_row_block(i)], dob[_row_block(i)], tn, preferred_element_type=F32)
                dk_ref[sl, :] += lax.dot_general(ds1b[_row_block(i)], qs[_row_block(i)], tn, preferred_element_type=F32)

    row_spec = pl.BlockSpec((blk, dh), lambda h, j: (j, h))
    all_spec = pl.BlockSpec((t, dh), lambda h, j: (0, h))
    b_spec = pl.BlockSpec((1, NA_KH, GRID_W, win), lambda h, j: (h, 0, 0, 0))
    full = jax.ShapeDtypeStruct((t, width), F32)
    return pl.pallas_call(
        body, name=name, grid=(NA_HEADS, 1 + rows // NA_ROWS_PER_STEP),
        out_shape=(full, full, full, jax.ShapeDtypeStruct(biasw.shape, F32)),
        in_specs=[row_spec, all_spec, all_spec, b_spec, row_spec], out_specs=(row_spec, all_spec, all_spec, b_spec),
        scratch_shapes=[pltpu.VMEM((t, dh), BF16), pltpu.VMEM((t, dh), BF16)], compiler_params=_cparams(),
    )(q, k, v, biasw, do)


def _na_attention(e, update_ctx, q, k, v, biasw):
    @jax.custom_vjp
    def attn(q, k, v, biasw):
        return _na_forward(f"na_fwd{e}", update_ctx, q, k, v, biasw)

    attn.defvjp(lambda q, k, v, biasw: (_na_forward(f"na_fwd{e}", update_ctx, q, k, v, biasw), (q, k, v, biasw)),
                lambda res, do: _na_backward(f"na_bwd{e}", update_ctx, *res, do))
    return attn(q, k, v, biasw)


def _bias_windows(rpb):
    col = jnp.arange(GRID_W)
    dc = jnp.clip(col[None, :] - col[:, None], -(NA_KW - 1), NA_KW - 1) + NA_KW - 1
    onehot = (dc[None] == jnp.arange(2 * NA_KW - 1)[:, None, None]).astype(F32)
    tq = jnp.einsum("hrd,dqk->hrqk", rpb, onehot, precision=HIGHEST)
    col_start = jnp.clip(col - NA_KW // 2, 0, GRID_W - NA_KW)
    in_win = (col[None, :] >= col_start[:, None]) & (col[None, :] < col_start[:, None] + NA_KW)
    wins = jnp.stack([tq[:, off:off + NA_KH] for off in range(NA_KH)], axis=1)
    wins = jnp.where(in_win[None, None, None], wins, NEG)
    return wins.transpose(0, 1, 3, 2, 4).reshape(rpb.shape[0], NA_KH, GRID_W, NA_KH * GRID_W)


def _chunk_cumsum(u, reverse, name):
    t, nh = u.shape
    ln = SSD_CHUNK

    def body(u_ref, o_ref):
        row = lax.broadcasted_iota(jnp.int32, (ln, ln), 0)
        col = lax.broadcasted_iota(jnp.int32, (ln, ln), 1)
        uu = u_ref[...]
        down = jnp.dot((col <= row).astype(F32), uu, precision=HIGHEST, preferred_element_type=F32)
        up = jnp.dot((col >= row).astype(F32), uu, precision=HIGHEST, preferred_element_type=F32)
        first = lax.broadcasted_iota(jnp.int32, (ln, nh), 1) < nh // 2
        o_ref[...] = jnp.where(first, up, down) if reverse else jnp.where(first, down, up)

    spec = pl.BlockSpec((ln, nh), lambda i: (i, 0))
    return pl.pallas_call(body, name=name, grid=(t // ln,), out_shape=jax.ShapeDtypeStruct(u.shape, F32),
                          in_specs=[spec], out_specs=spec, compiler_params=_cparams())(u)


def _ssd_cumsum(e, u):
    @jax.custom_vjp
    def cs(u):
        return _chunk_cumsum(u, False, f"ssd_cumsum{e}")

    cs.defvjp(lambda u: (_chunk_cumsum(u, False, f"ssd_cumsum{e}"), None),
              lambda _, g: (_chunk_cumsum(g, True, f"ssd_cumsum_bwd{e}"),))
    return cs(u)


def _mxu_dots():
    c_nn, c_nt, c_tn = (((1,), (0,)), ((), ())), (((1,), (1,)), ((), ())), (((0,), (0,)), ((), ()))

    def dot(a, b, dn):
        return lax.dot_general(a.astype(BF16), b.astype(BF16), dn, preferred_element_type=F32)

    def make(dn, dn_da, a_first, dn_db, b_first):
        @jax.custom_vjp
        def f(a, b):
            return dot(a, b, dn)

        def bwd(res, g):
            a, b = res
            da = dot(g, b, dn_da) if a_first else dot(b, g, dn_da)
            db = dot(g, a, dn_db) if b_first else dot(a, g, dn_db)
            return da, db

        f.defvjp(lambda a, b: (dot(a, b, dn), (a, b)), bwd)
        return f

    nn = make(c_nn, c_nt, True, c_tn, False)
    nt = make(c_nt, c_nn, True, c_tn, True)
    tn = make(c_tn, c_nt, False, c_nn, False)
    return nn, nt, tn


def _ssd_step(h, x, bm, cm, dt, dt_t, cs, cs_t, *, direction, hpg, pdim):
    ln = x.shape[0]
    hp = hpg * pdim
    nn, nt, tn = _mxu_dots()
    row = lax.broadcasted_iota(jnp.int32, (ln, ln), 0)
    colm = lax.broadcasted_iota(jnp.int32, (ln, ln), 1)
    valid = (colm - row) * (1 - 2 * direction) <= 0
    last = (ln - 1) * (1 - direction)
    tot = jnp.sum(jnp.where(lax.broadcasted_iota(jnp.int32, (ln, hpg), 0) == last, cs, 0.0), axis=0, keepdims=True)
    cbm = nt(cm, bm)
    lane_head = lax.broadcasted_iota(jnp.int32, (1, hp), 1) // pdim
    sub_head = lax.broadcasted_iota(jnp.int32, (hp, 1), 0) // pdim
    y = jnp.zeros((ln, hp), F32)
    es = jnp.zeros((ln, hp), F32)
    we = jnp.zeros((ln, hp), F32)
    dend = jnp.zeros((hp, 1), F32)
    for r in range(hpg):
        cc = cs[:, r:r + 1]
        cr = cs_t[r:r + 1, :]
        decay = jnp.exp(jnp.where(valid, cc - cr, NEG))
        mask = lane_head == r
        y = y + nn(cbm * decay * dt_t[r:r + 1, :], jnp.where(mask, x, 0.0))
        es = es + jnp.where(mask, jnp.exp(cc), 0.0)
        we = we + jnp.where(mask, jnp.exp(tot[:, r:r + 1] - cc) * dt[:, r:r + 1], 0.0)
        dend = dend + jnp.where(sub_head == r, jnp.exp(tot[:, r:r + 1]), 0.0)
    y = y + es * nt(cm, h)
    h_new = h * dend + tn(x * we, bm)
    return y, h_new


def _ssd_chunk_of(d, s, ncc, nc):
    return jnp.where(d == 0, s, jnp.where(s < ncc, ncc - 1 - s, nc - 1 - s + ncc))


SSD_GROUPS_PER_STEP = 8


def _groups_per_step():
    return math.gcd(SSD_GROUPS, SSD_GROUPS_PER_STEP)


def _ssd_specs(cfg, step_of):
    gp = _groups_per_step()
    hp, n, ln, hpg = cfg.hpg * SSD_HEAD_DIM, SSD_STATE, SSD_CHUNK, cfg.hpg
    ncc, nc = cfg.nctx // ln, cfg.t // ln
    b_off = cfg.ssd_width // (gp * n)
    c_off = (cfg.ssd_width + SSD_GROUPS * n) // (gp * n)
    assert SSD_GROUPS % gp == 0 and cfg.ssd_width % (gp * n) == 0 and (SSD_GROUPS * n) % (gp * n) == 0

    def ch(d, s):
        return _ssd_chunk_of(d, step_of(s), ncc, nc)

    return dict(
        x=pl.BlockSpec((ln, gp * hp), lambda d, g, s: (ch(d, s), g)),
        bm=pl.BlockSpec((ln, gp * n), lambda d, g, s: (ch(d, s), b_off + g)),
        cm=pl.BlockSpec((ln, gp * n), lambda d, g, s: (ch(d, s), c_off + g)),
        dt=pl.BlockSpec((1, gp, ln, hpg), lambda d, g, s: (d, g, ch(d, s), 0)),
        dt_t=pl.BlockSpec((1, gp, 8, ln), lambda d, g, s: (d, g, 0, ch(d, s))),
        y=pl.BlockSpec((1, ln, gp * hp), lambda d, g, s: (d, ch(d, s), g)),
        bc2=pl.BlockSpec((1, ln, gp * n), lambda d, g, s: (d, ch(d, s), g)),
        h=pl.BlockSpec((1, gp, 1, hp, n), lambda d, g, s: (d, g, step_of(s), 0, 0)),
    )


def _ssd_forward(cfg, name, xbc, dt, dt_t, cs, cs_t):
    gp = _groups_per_step()
    hp, n, hpg = cfg.hpg * SSD_HEAD_DIM, SSD_STATE, cfg.hpg
    nc = cfg.t // SSD_CHUNK
    sp = _ssd_specs(cfg, lambda s: s)

    def body(x_ref, b_ref, c_ref, dt_ref, dtt_ref, cs_ref, cst_ref, y_ref, hs_ref, h_ref):
        d, s = pl.program_id(0), pl.program_id(2)

        @pl.when(s == 0)
        def _():
            h_ref[...] = jnp.zeros_like(h_ref)

        for q in range(gp):
            h = h_ref[q]
            hs_ref[0, q, 0] = h
            y, h_new = _ssd_step(h, x_ref[:, q * hp:(q + 1) * hp], b_ref[:, q * n:(q + 1) * n], c_ref[:, q * n:(q + 1) * n],
                                 dt_ref[0, q], dtt_ref[0, q], cs_ref[0, q], cst_ref[0, q], direction=d, hpg=hpg,
                                 pdim=SSD_HEAD_DIM)
            y_ref[0, :, q * hp:(q + 1) * hp] = y
            h_ref[q] = h_new

    return pl.pallas_call(
        body, name=name, grid=(2, SSD_GROUPS // gp, nc),
        out_shape=(jax.ShapeDtypeStruct((2, cfg.t, cfg.ssd_width), F32),
                   jax.ShapeDtypeStruct((2, SSD_GROUPS, nc, hp, n), F32)),
        in_specs=[sp["x"], sp["bm"], sp["cm"], sp["dt"], sp["dt_t"], sp["dt"], sp["dt_t"]],
        out_specs=(sp["y"], sp["h"]), scratch_shapes=[pltpu.VMEM((gp, hp, n), F32)], compiler_params=_cparams(),
    )(xbc, xbc, xbc, dt, dt_t, cs, cs_t)


def _ssd_backward(cfg, name, xbc, dt, dt_t, cs, cs_t, hsave, dy):
    gp = _groups_per_step()
    hp, n, hpg = cfg.hpg * SSD_HEAD_DIM, SSD_STATE, cfg.hpg
    nc = cfg.t // SSD_CHUNK
    sp = _ssd_specs(cfg, lambda s: nc - 1 - s)

    def body(x_ref, b_ref, c_ref, dt_ref, dtt_ref, cs_ref, cst_ref, hs_ref, dy_ref,
             dx_ref, db_ref, dc_ref, ddt_ref, ddtt_ref, dcs_ref, dcst_ref, dh_ref):
        d, s = pl.program_id(0), pl.program_id(2)

        @pl.when(s == 0)
        def _():
            dh_ref[...] = jnp.zeros_like(dh_ref)

        step = functools.partial(_ssd_step, direction=d, hpg=hpg, pdim=SSD_HEAD_DIM)
        for q in range(gp):
            xc, nc_ = slice(q * hp, (q + 1) * hp), slice(q * n, (q + 1) * n)
            _, vjp = jax.vjp(step, hs_ref[0, q, 0], x_ref[:, xc], b_ref[:, nc_], c_ref[:, nc_], dt_ref[0, q], dtt_ref[0, q],
                             cs_ref[0, q], cst_ref[0, q])
            dh, dx, db, dc, ddt, ddtt, dcs, dcst = vjp((dy_ref[0, :, xc], dh_ref[q]))
            dh_ref[q] = dh
            dx_ref[0, :, xc] = dx
            db_ref[0, :, nc_] = db
            dc_ref[0, :, nc_] = dc
            ddt_ref[0, q] = ddt
            ddtt_ref[0, q] = ddtt
            dcs_ref[0, q] = dcs
            dcst_ref[0, q] = dcst

    gn = SSD_GROUPS * n
    return pl.pallas_call(
        body, name=name, grid=(2, SSD_GROUPS // gp, nc),
        out_shape=(jax.ShapeDtypeStruct((2, cfg.t, cfg.ssd_width), F32), jax.ShapeDtypeStruct((2, cfg.t, gn), F32),
                   jax.ShapeDtypeStruct((2, cfg.t, gn), F32), jax.ShapeDtypeStruct(dt.shape, F32),
                   jax.ShapeDtypeStruct(dt_t.shape, F32), jax.ShapeDtypeStruct(dt.shape, F32),
                   jax.ShapeDtypeStruct(dt_t.shape, F32)),
        in_specs=[sp["x"], sp["bm"], sp["cm"], sp["dt"], sp["dt_t"], sp["dt"], sp["dt_t"], sp["h"], sp["y"]],
        out_specs=(sp["y"], sp["bc2"], sp["bc2"], sp["dt"], sp["dt_t"], sp["dt"], sp["dt_t"]),
        scratch_shapes=[pltpu.VMEM((gp, hp, n), F32)], compiler_params=_cparams(),
    )(xbc, xbc, xbc, dt, dt_t, cs, cs_t, hsave, dy)


def _ssd_scan(cfg, e, xbc, dt, dt_t, cs, cs_t):
    @jax.custom_vjp
    def scan(xbc, dt, dt_t, cs, cs_t):
        return _ssd_forward(cfg, f"ssd_fwd{e}", xbc, dt, dt_t, cs, cs_t)[0]

    def fwd(xbc, dt, dt_t, cs, cs_t):
        y, hsave = _ssd_forward(cfg, f"ssd_fwd{e}", xbc, dt, dt_t, cs, cs_t)
        return y, (xbc, dt, dt_t, cs, cs_t, hsave)

    def bwd(res, dy):
        dx, db, dc, ddt, ddtt, dcs, dcst = _ssd_backward(cfg, f"ssd_bwd{e}", *res, dy)
        return jnp.concatenate([dx[0] + dx[1], db[0] + db[1], dc[0] + dc[1]], axis=1), ddt, ddtt, dcs, dcst

    scan.defvjp(fwd, bwd)
    return scan(xbc, dt, dt_t, cs, cs_t)


def _adam_math(w, g, m, v):
    m2 = ADAM_B1 * m + (1.0 - ADAM_B1) * g
    v2 = ADAM_B2 * v + (1.0 - ADAM_B2) * (g * g)
    m_hat = m2 / (1.0 - ADAM_B1 ** ADAM_STEP)
    v_hat = v2 / (1.0 - ADAM_B2 ** ADAM_STEP)
    delta = -ADAM_LR * (m_hat / (jnp.sqrt(v_hat) + ADAM_EPS) + ADAM_WD * w)
    return delta, m2, v2


def _adam_small(w, g, m, v, name):
    def body(w_ref, g_ref, m_ref, v_ref, d_ref, m2_ref, v2_ref):
        d_ref[...], m2_ref[...], v2_ref[...] = _adam_math(w_ref[...], g_ref[...], m_ref[...], v_ref[...])

    shp = jax.ShapeDtypeStruct(w.shape, F32)
    return pl.pallas_call(body, name=name, out_shape=(shp, shp, shp), in_specs=[VMEM] * 4, out_specs=(VMEM, VMEM, VMEM),
                          compiler_params=_cparams())(w, g, m, v)


def _adam_tiled(w, g, m, v, name):
    nl, r, c = w.shape
    tr = _pick(r, (256, 128, 64, 32, 16, 8))
    spec = pl.BlockSpec((1, tr, c), lambda l, t: (l, t, 0))

    def body(w_ref, g_ref, m_ref, v_ref, d_ref, m2_ref, v2_ref):
        d_ref[...], m2_ref[...], v2_ref[...] = _adam_math(w_ref[...], g_ref[...], m_ref[...], v_ref[...])

    shp = jax.ShapeDtypeStruct(w.shape, F32)
    return pl.pallas_call(body, name=name, grid=(nl, r // tr), out_shape=(shp, shp, shp), in_specs=[spec] * 4,
                          out_specs=(spec, spec, spec), compiler_params=_cparams())(w, g, m, v)


def _adam_sharded(w, m, v, part_a, part_b, *, transposed, name):
    nl, r, c = w.shape
    tr = _pick(r, (256, 128)) if transposed else _pick(r, (256, 128, 64, 32, 16))
    w_spec = pl.BlockSpec((1, tr, c), lambda l, t: (l, t, 0))
    if transposed:
        pa_spec = pl.BlockSpec((1, 1, c, tr), lambda l, t: (0, l, 0, t))
        pb_spec = pl.BlockSpec((3, 1, c, tr), lambda l, t: (0, l, 0, t))
    else:
        pa_spec = pl.BlockSpec((1, 1, tr, c), lambda l, t: (0, l, t, 0))
        pb_spec = pl.BlockSpec((3, 1, tr, c), lambda l, t: (0, l, t, 0))

    def body(w_ref, m_ref, v_ref, pa_ref, pb_ref, g_ref, d_ref, m2_ref, v2_ref):
        g = pa_ref[0, 0] + pb_ref[0, 0].astype(F32) + pb_ref[1, 0].astype(F32) + pb_ref[2, 0].astype(F32)
        if transposed:
            g = g.T
        g_ref[0] = g
        d_ref[0], m2_ref[0], v2_ref[0] = _adam_math(w_ref[0], g, m_ref[0], v_ref[0])

    shp = jax.ShapeDtypeStruct(w.shape, F32)
    return pl.pallas_call(
        body, name=name, grid=(nl, r // tr), out_shape=(shp, shp, shp, shp),
        in_specs=[w_spec, w_spec, w_spec, pa_spec, pb_spec], out_specs=(w_spec, w_spec, w_spec, w_spec),
        compiler_params=_cparams(),
    )(w, m, v, part_a, part_b)


def _pick_blocks(g, recv, axis, length, mode, out_dtype, name):
    nl = g.shape[0]
    n = {"cast": 4, "mine": 1, "send": 3}[mode]
    pos = [p.astype(jnp.int32).reshape(1) for p in _pos()]

    def chip(i, x, y, c):
        if mode == "cast":
            return i
        if mode == "mine":
            return 2 * x[0] + y[0]
        return 2 * jnp.where(i == 1, x[0], 1 - x[0]) + jnp.where(i == 0, y[0], 1 - y[0])

    def gb(i, x, y, c):
        return 2 * chip(i, x, y, c) + (1 - c[0] if mode == "cast" else c[0])

    blk_shape = list(g.shape)
    blk_shape[axis] = length
    if axis == 1:
        cols = g.shape[2]
        tc = _pick(cols, (512, 256, 128))
        g_spec = pl.BlockSpec((1, length, tc), lambda i, l, t, x, y, c: (l, gb(i, x, y, c), t))
        r_spec = pl.BlockSpec((1, 1, length, tc), lambda i, l, t, x, y, c: (chip(i, x, y, c), l, 0, t))
        o_spec = pl.BlockSpec((1, 1, length, tc), lambda i, l, t, x, y, c: (i, l, 0, t))
        grid = (n, nl, cols // tc)
    else:
        rows = g.shape[1]
        tr = _pick(rows, (512, 256, 128, 64, 32, 16))
        g_spec = pl.BlockSpec((1, tr, length), lambda i, l, t, x, y, c: (l, t, gb(i, x, y, c)))
        r_spec = pl.BlockSpec((1, 1, tr, length), lambda i, l, t, x, y, c: (chip(i, x, y, c), l, t, 0))
        o_spec = pl.BlockSpec((1, 1, tr, length), lambda i, l, t, x, y, c: (i, l, t, 0))
        grid = (n, nl, rows // tr)

    if recv is None:
        def body(x_ref, y_ref, c_ref, g_ref, o_ref):
            o_ref[0] = g_ref[...].astype(out_dtype)
        in_specs, args = [g_spec], (g,)
    else:
        def body(x_ref, y_ref, c_ref, g_ref, r_ref, o_ref):
            o_ref[0] = (g_ref[...] + r_ref[0].astype(F32)).astype(out_dtype)
        in_specs, args = [g_spec, r_spec], (g, recv)

    return pl.pallas_call(
        body, name=name, out_shape=jax.ShapeDtypeStruct((n, *blk_shape), out_dtype),
        grid_spec=pltpu.PrefetchScalarGridSpec(num_scalar_prefetch=3, grid=grid, in_specs=in_specs, out_specs=o_spec),
        compiler_params=_cparams(),
    )(*pos, *args)


def _flatten(arrs):
    flat = jnp.concatenate([a.reshape(-1).astype(F32) for a in arrs])
    n = flat.shape[0]
    n_pad = -(-n // 1024) * 1024
    return jnp.pad(flat, (0, n_pad - n)).reshape(n_pad // 128, 128)


def _unflatten(buf, shapes):
    flat = buf.reshape(-1)
    out, o = [], 0
    for s in shapes:
        n = math.prod(s)
        out.append(flat[o:o + n].reshape(s))
        o += n
    return out


def _rowwise(name, fn, rows, seg, shared, out_cols, out_dtypes, tile, nct):
    t = rows[0].shape[0]
    nr, ns, nsh, no = len(rows), len(seg), len(shared), len(out_cols)
    n_in = nr + ns + nsh

    def row_spec(c):
        return pl.BlockSpec((tile, c), lambda i: (i, 0))

    def seg_spec(c):
        return pl.BlockSpec((1, 1, c), lambda i: (jnp.where(i < nct, 0, 1), 0, 0))

    def whole_spec(shape):
        return pl.BlockSpec(shape, lambda i: (0, 0))

    in_specs = ([row_spec(r.shape[1]) for r in rows] + [seg_spec(s.shape[1]) for s in seg]
                + [whole_spec(s.shape) for s in shared])
    out_shapes = tuple(jax.ShapeDtypeStruct((t, c), dt) for c, dt in zip(out_cols, out_dtypes))
    out_specs = tuple(row_spec(c) for c in out_cols)

    def load(refs):
        return [r[0] if nr <= j < nr + ns else r[...] for j, r in enumerate(refs[:n_in])]

    def lift(args):
        return [a[:, None, :] if nr <= j < nr + ns else a for j, a in enumerate(args)]

    def forward(*args):
        def body(*refs):
            outs = fn(*load(refs))
            for o_ref, o in zip(refs[n_in:], outs):
                o_ref[...] = o

        return pl.pallas_call(body, name=name + "_fwd", grid=(t // tile,), out_shape=out_shapes, in_specs=in_specs,
                              out_specs=out_specs, compiler_params=_cparams())(*lift(args))

    def backward(args, cts):
        def body(*refs):
            i = pl.program_id(0)
            ct = tuple(r[...] for r in refs[n_in:n_in + no])
            d_refs = refs[n_in + no:]
            _, vjp = jax.vjp(fn, *load(refs))
            grads = vjp(ct)
            for ref, g in zip(d_refs[:nr], grads[:nr]):
                ref[...] = g
            first_seg = jnp.logical_or(i == 0, i == nct)
            for j in range(nr, n_in):
                ref, g = d_refs[j], grads[j]
                first = first_seg if j < nr + ns else i == 0
                g = g[None] if j < nr + ns else g

                @pl.when(first)
                def _(ref=ref, g=g):
                    ref[...] = g

                @pl.when(jnp.logical_not(first))
                def _(ref=ref, g=g):
                    ref[...] += g

        largs = lift(args)
        d_shapes = tuple(jax.ShapeDtypeStruct(a.shape, F32) for a in largs)
        ct_specs = [row_spec(c) for c in out_cols]
        outs = pl.pallas_call(body, name=name + "_bwd", grid=(t // tile,), out_shape=d_shapes,
                              in_specs=in_specs + ct_specs, out_specs=tuple(in_specs), compiler_params=_cparams())(*largs, *cts)
        return [o[:, 0, :] if nr <= j < nr + ns else o for j, o in enumerate(outs)]

    @jax.custom_vjp
    def prim(*args):
        return tuple(forward(*args))

    prim.defvjp(lambda *args: (tuple(forward(*args)), args), lambda args, cts: tuple(backward(args, cts)))
    return prim(*rows, *seg, *shared)


def _silu(x):
    return x * (1.0 / (1.0 + jnp.exp(-x)))


def _softplus(x):
    return jnp.maximum(x, 0.0) + jnp.log(1.0 + jnp.exp(-jnp.abs(x)))


def _dwconv(name, x, w, b, act, nctx):
    t, c = x.shape
    kk = w.shape[0]
    half = kk // 2
    tile = 256
    tc = _pick(c, (1024, 512, 256, 128))
    nt, nct, hb = t // tile, nctx // tile, tile // 8
    cur = pl.BlockSpec((tile, tc), lambda j, i: (i, j))
    prev = pl.BlockSpec((8, tc), lambda j, i: (jnp.maximum(i * hb - 1, 0), j))
    nxt = pl.BlockSpec((8, tc), lambda j, i: (jnp.minimum((i + 1) * hb, t // 8 - 1), j))
    w_spec = pl.BlockSpec((kk, tc), lambda j, i: (0, j))
    b_spec = pl.BlockSpec((1, tc), lambda j, i: (0, j))
    grid = (c // tc, nt)

    def extended(cur_ref, prev_ref, next_ref, i):
        has_prev = jnp.logical_and(i != 0, i != nct)
        has_next = jnp.logical_and(i != nct - 1, i != nt - 1)
        return jnp.concatenate([jnp.where(has_prev, prev_ref[...], 0.0), cur_ref[...],
                                jnp.where(has_next, next_ref[...], 0.0)], axis=0)

    def taps(ext, w_ref, lo, n):
        acc = None
        for k in range(kk):
            term = w_ref[k:k + 1, :] * ext[lo + k - half:lo + k - half + n]
            acc = term if acc is None else acc + term
        return acc

    def forward(x, w, b):
        def body(x_ref, xp_ref, xn_ref, w_ref, b_ref, o_ref):
            ext = extended(x_ref, xp_ref, xn_ref, pl.program_id(1))
            y = taps(ext, w_ref, 8, tile) + b_ref[...]
            o_ref[...] = _silu(y) if act else y

        return pl.pallas_call(body, name=name + "_fwd", grid=grid, out_shape=jax.ShapeDtypeStruct((t, c), F32),
                              in_specs=[cur, prev, nxt, w_spec, b_spec], out_specs=cur, compiler_params=_cparams())(x, x, x, w, b)

    def backward(x, w, b, g):
        m = tile + 2 * half

        def body(x_ref, xp_ref, xn_ref, g_ref, gp_ref, gn_ref, w_ref, b_ref, dx_ref, dw_ref, db_ref):
            i = pl.program_id(1)
            xe = extended(x_ref, xp_ref, xn_ref, i)
            dpre = extended(g_ref, gp_ref, gn_ref, i)[8 - half:8 - half + m]
            if act:
                pre = taps(xe, w_ref, 8 - half, m) + b_ref[...]
                sg = 1.0 / (1.0 + jnp.exp(-pre))
                dpre = dpre * (sg * (1.0 + pre * (1.0 - sg)))
            acc = None
            for k in range(kk):
                term = w_ref[k:k + 1, :] * dpre[2 * half - k:2 * half - k + tile]
                acc = term if acc is None else acc + term
            dx_ref[...] = acc
            dcur = dpre[half:half + tile]
            dw = jnp.concatenate([jnp.sum(dcur * xe[8 + k - half:8 + k - half + tile], axis=0, keepdims=True)
                                  for k in range(kk)], axis=0)
            db = jnp.sum(dcur, axis=0, keepdims=True)

            @pl.when(i == 0)
            def _():
                dw_ref[...] = dw
                db_ref[...] = db

            @pl.when(i != 0)
            def _():
                dw_ref[...] += dw
                db_ref[...] += db

        return pl.pallas_call(
            body, name=name + "_bwd", grid=grid,
            out_shape=(jax.ShapeDtypeStruct((t, c), F32), jax.ShapeDtypeStruct(w.shape, F32), jax.ShapeDtypeStruct(b.shape, F32)),
            in_specs=[cur, prev, nxt, cur, prev, nxt, w_spec, b_spec], out_specs=(cur, w_spec, b_spec),
            compiler_params=_cparams())(x, x, x, g, g, g, w, b)

    @jax.custom_vjp
    def conv(x, w, b):
        return forward(x, w, b)

    conv.defvjp(lambda x, w, b: (forward(x, w, b), (x, w, b)), lambda res, g: backward(*res, g))
    return conv(x, w, b)


def _odd_pre_tile(pch):
    half = pch.shape[1] // 2
    return (pch[:, :half] * pch[:, half:],)


def _odd_post_tile(pb, pg, yc):
    return ((_silu(pg) * (pb * yc)).astype(BF16),)


def _rms_rows(x):
    return x * lax.rsqrt(jnp.mean(x * x, axis=-1, keepdims=True) + EPS)


def _pre0_tile(x, scale, shift, g):
    return ((_rms_rows(x) * g * (1 + scale) + shift).astype(BF16),)


def _pre_tile(x, y_prev, gate, scale, shift, g):
    xn = x + gate * y_prev
    return xn, (_rms_rows(xn) * g * (1 + scale) + shift).astype(BF16)


def _loss_tile(x, y_prev, target, gate, weight):
    err = (x + gate * y_prev - target) * weight
    return (0.5 * jnp.mean(err * err, axis=-1, keepdims=True),)


def _mid_even_tile(pa, pd, qg, kg, dt_bias, a, *, naw, sw, nh):
    def heads_norm(u, g):
        return jnp.concatenate([_rms_rows(u[:, j:j + NA_HEAD_DIM]) * g for j in range(0, naw, NA_HEAD_DIM)], axis=1)

    q, gate, z = pa[:, :naw], pa[:, naw:2 * naw], pa[:, 2 * naw:2 * naw + sw]
    k, v = pa[:, 2 * naw + sw:3 * naw + sw], pa[:, 3 * naw + sw:]
    dt = _softplus(pd[:, :nh] + dt_bias)
    return heads_norm(q, qg), heads_norm(k, kg), v, _silu(gate), _silu(z), dt, dt * a


def _post_even_tile(ya, sg, y0, y1, xs, sz, dskip, g, *, sw):
    yz = (y0 + y1 + dskip * xs) * sz
    gw = sw // SSD_GROUPS
    yb = jnp.concatenate([_rms_rows(yz[:, j:j + gw]) for j in range(0, sw, gw)], axis=1) * g
    return (jnp.concatenate([ya * sg, yb], axis=1).astype(BF16),)


def _in_proj_split(a, w, gslot, widths, *, w_is_nk, tm, tn, name):
    t, kdim = a.shape
    starts = [sum(widths[:i]) for i in range(len(widths))]
    assert all(s % tn == 0 and wd % tn == 0 for s, wd in zip(starts, widths)), (starts, widths, tn)

    def piece(i, blocks):
        off = (starts[i] // blocks, 0) if w_is_nk else (0, starts[i] // blocks)
        ext = (widths[i], kdim) if w_is_nk else (kdim, widths[i])
        return off, ext

    def forward(a, w):
        outs = []
        for i in range(len(widths)):
            off, ext = piece(i, tn)
            outs.append(_matmul(a, w, tb=w_is_nk, tm=tm, tn=tn, tk=kdim, b_off=off, b_extent=ext, name=f"{name}_fwd{i}"))
        return tuple(outs)

    @jax.custom_vjp
    def proj(a, w, gslot):
        return forward(a, w)

    def bwd(res, gs):
        a, w = res
        tt = _pick(t, (768, 512, 256, 128))
        tkk = _pick(kdim, (2048, 1024, 512, 256, 128))
        tok = _pick(t, (1408, 768, 512, 256, 128))
        da, dws = None, []
        for i, g in enumerate(gs):
            big = 2 * tn if widths[i] % (2 * tn) == 0 and starts[i] % (2 * tn) == 0 else tn
            off, ext = piece(i, big)
            part = _matmul(g, w, tb=not w_is_nk, tm=tt, tn=tkk, tk=big, b_off=off, b_extent=ext, name=f"{name}_bwd_a{i}")
            da = part if da is None else da + part
            if w_is_nk:
                dws.append(_matmul(g, a, ta=True, tm=big, tn=tkk, tk=tok, name=f"{name}_bwd_w{i}"))
            else:
                dws.append(_matmul(a, g, ta=True, tm=_pick(kdim, (1024, 512, 256, 128)), tn=_pick(widths[i], (1024, 512, 256, 128)),
                                   tk=tok, name=f"{name}_bwd_w{i}"))
        dw = jnp.concatenate(dws, axis=0 if w_is_nk else 1)
        if dw.shape != w.shape:
            dw = jnp.pad(dw, [(0, w.shape[0] - dw.shape[0]), (0, w.shape[1] - dw.shape[1])])
        return da.astype(a.dtype), jnp.zeros_like(w), dw

    proj.defvjp(lambda a, w, gslot: (forward(a, w), (a, w)), bwd)
    return proj(a, w, gslot)


def _rms(x, g):
    return x * lax.rsqrt(jnp.mean(x * x, axis=-1, keepdims=True) + EPS) * g


def _dw_conv(x, w, b=None):
    k = w.shape[0]
    ln = x.shape[0]
    xp = jnp.pad(x, ((k // 2, k // 2), (0, 0)))
    y = sum(w[i][None, :] * xp[i:i + ln] for i in range(k))
    return y if b is None else y + b


def _conv_two(x, nctx, w, b=None):
    return jnp.concatenate([_dw_conv(x[:nctx], w, b), _dw_conv(x[nctx:], w, b)], axis=0)


def _mod_rows(nctx, seq, ctx_vec, lat_vec):
    return jnp.concatenate([jnp.broadcast_to(ctx_vec, (nctx, ctx_vec.shape[-1])),
                            jnp.broadcast_to(lat_vec, (seq, lat_vec.shape[-1]))], axis=0)


def _even_mixer(cfg, h, e, wd, update_ctx):
    d, nctx, seq, t = cfg.d, cfg.nctx, cfg.s, cfg.t
    naw, sw = cfg.na_width, cfg.ssd_width
    gn = SSD_GROUPS * SSD_STATE
    nh = 2 * SSD_GROUPS * cfg.hpg
    wa, wx = 4 * naw + sw, sw + 2 * gn
    pa, px, pd = _in_proj_split(h, wd["win_t"][e], wd["g_win_t"][e], (wa, wx, cfg.n_pad - wa - wx), w_is_nk=True,
                                tm=cfg.tiles_in[0], tn=cfg.tiles_in[1], name=f"in_even{e}")
    tile = cfg.tile_tok
    nct = nctx // tile
    a_neg = -jnp.exp(wd["ssd_a_log"][e]).reshape(1, nh)
    qn, kn, vv, sg, sz, dt, dta = _rowwise(
        f"mid_even{e}", functools.partial(_mid_even_tile, naw=naw, sw=sw, nh=nh), [pa, pd], [],
        [wd["q_norm_g"][e][None, :], wd["k_norm_g"][e][None, :], wd["ssd_dt_bias"][e].reshape(1, nh), a_neg],
        [naw, naw, naw, naw, sw, nh, nh], [F32] * 7, tile, nct)
    biasw = _bias_windows(wd["na_rpb"][e])
    ya = _na_attention(e, update_ctx, qn, kn, vv, biasw)

    xbc = _dwconv(f"ssd_conv{e}", px, wd["ssd_conv_w"][e], wd["ssd_conv_b"][e][None, :], True, nctx)
    xs = xbc[:, :sw]

    def arrange(u):
        u4 = u.reshape(t, 2, SSD_GROUPS, cfg.hpg)
        return u4.transpose(1, 2, 0, 3), jnp.pad(u4.transpose(1, 2, 3, 0), ((0, 0), (0, 0), (0, 8 - cfg.hpg), (0, 0)))

    dt4, dt_t = arrange(dt)
    cs4, cs_t = arrange(_ssd_cumsum(e, dta))
    y2 = _ssd_scan(cfg, e, xbc, dt4, dt_t, cs4, cs_t)
    dskip = jnp.repeat(wd["ssd_d"][e], SSD_HEAD_DIM)[None, :]
    (ycat,) = _rowwise(f"post_even{e}", functools.partial(_post_even_tile, sw=sw), [ya, sg, y2[0], y2[1], xs, sz], [],
                       [dskip, wd["ssd_norm_g"][e][None, :]], [naw + sw], [BF16], tile, nct)
    return _linear(ycat, wd["wout"][e], wd["g_wout"][e], w_is_nk=False, tiles=cfg.tiles_out_even, name=f"out_even{e}")


def _odd_mixer(cfg, h, o, wd):
    d, nctx = cfg.d, cfg.nctx
    tile = cfg.tile_tok
    nct = nctx // tile
    pb, pch, pg = _in_proj_split(h, wd["sc_win"][o], wd["g_sc_win"][o], (d, 2 * d, d), w_is_nk=False,
                                 tm=cfg.tiles_in_odd[0], tn=_pick(d, (1024, 512, 256, 128)), name=f"in_odd{o}")
    (cv,) = _rowwise(f"odd_pre{o}", _odd_pre_tile, [pch], [], [], [d], [F32], tile, nct)
    yc = _dwconv(f"sc_conv{o}", cv, wd["sc_conv_w"][o], jnp.zeros((1, d), F32), False, nctx)
    (u,) = _rowwise(f"odd_post{o}", _odd_post_tile, [pb, pg, yc], [], [], [d], [BF16], tile, nct)
    return _linear(u, wd["sc_wout"][o], wd["g_sc_wout"][o], w_is_nk=False, tiles=cfg.tiles_out_odd, name=f"out_odd{o}")


def _local_loss(cfg, x, ctx, target, mods, mods_c, wd):
    d, nctx, seq = cfg.d, cfg.nctx, cfg.s
    tile = cfg.tile_res
    nct = nctx // tile
    xx = jnp.concatenate([ctx, x], axis=0)
    y_prev = gate_prev = None
    for i in range(DEPTH):
        update_ctx = any(j % 2 == 0 for j in range(i + 1, DEPTH))
        shift = jnp.stack([mods_c[i, :d], mods[i, :d]])
        scale = jnp.stack([mods_c[i, d:2 * d], mods[i, d:2 * d]])
        g = wd["norm_g"][i][None, :]
        if y_prev is None:
            (h,) = _rowwise(f"pre{i}", _pre0_tile, [xx], [scale, shift], [g], [d], [BF16], tile, nct)
        else:
            xx, h = _rowwise(f"pre{i}", _pre_tile, [xx, y_prev], [gate_prev, scale, shift], [g], [d, d], [F32, BF16], tile, nct)
        y_prev = _even_mixer(cfg, h, i // 2, wd, update_ctx) if i % 2 == 0 else _odd_mixer(cfg, h, i // 2, wd)
        gate_c = mods_c[i, 2 * d:] if update_ctx else jnp.zeros((d,), F32)
        gate_prev = jnp.stack([gate_c, mods[i, 2 * d:]])
    target_rows = jnp.concatenate([jnp.zeros((nctx, d), F32), target], axis=0)
    weight = jnp.stack([jnp.zeros((d,), F32), jnp.ones((d,), F32)])
    (row_loss,) = _rowwise("loss", _loss_tile, [xx, y_prev, target_rows], [gate_prev, weight], [], [1], [F32], tile, nct)
    return jnp.sum(row_loss)


SMALL_REPLICATED = ["norm_g", "ssd_conv_b", "ssd_a_log", "ssd_dt_bias", "ssd_d", "ssd_norm_g", "q_norm_g", "k_norm_g", "na_rpb"]
WEIGHT_ORDER = ["c_ctx", "ada_w", "ada_b", "norm_g", "na_ssd_w_in", "ssd_conv_w", "ssd_conv_b", "ssd_a_log", "ssd_dt_bias",
                "ssd_d", "ssd_norm_g", "q_norm_g", "k_norm_g", "na_rpb", "na_ssd_w_out", "sc_w_in", "sc_conv_w", "sc_w_out"]


def kernel(x, c, ctx, c_ctx, ada_w, ada_b, norm_g, na_ssd_w_in, ssd_conv_w, ssd_conv_b, ssd_a_log, ssd_dt_bias, ssd_d, ssd_norm_g, q_norm_g, k_norm_g, na_rpb, na_ssd_w_out, sc_w_in, sc_conv_w, sc_w_out, loss_target, m_c_ctx, m_ada_w, m_ada_b, m_norm_g, m_na_ssd_w_in, m_ssd_conv_w, m_ssd_conv_b, m_ssd_a_log, m_ssd_dt_bias, m_ssd_d, m_ssd_norm_g, m_q_norm_g, m_k_norm_g, m_na_rpb, m_na_ssd_w_out, m_sc_w_in, m_sc_conv_w, m_sc_w_out, v_c_ctx, v_ada_w, v_ada_b, v_norm_g, v_na_ssd_w_in, v_ssd_conv_w, v_ssd_conv_b, v_ssd_a_log, v_ssd_dt_bias, v_ssd_d, v_ssd_norm_g, v_q_norm_g, v_k_norm_g, v_na_rpb, v_na_ssd_w_out, v_sc_w_in, v_sc_conv_w, v_sc_w_out):
    given = dict(locals())
    weights = {n: given[n] for n in WEIGHT_ORDER}
    mom_m = {n: given["m_" + n] for n in WEIGHT_ORDER}
    mom_v = {n: given["v_" + n] for n in WEIGHT_ORDER}

    d = x.shape[-1]
    seq, nctx = x.shape[1], ctx.shape[1]
    n_in_shard = na_ssd_w_in.shape[-1]
    n_in = n_in_shard * NDEV
    n_pad = -(-n_in // PAD_TO) * PAD_TO
    hpg = (d // SSD_HEAD_DIM) // SSD_GROUPS
    t = nctx + seq
    tm = _pick(t, (1408, 768, 512, 256, 128))
    cfg = SimpleNamespace(
        d=d, s=seq, nctx=nctx, t=t, hpg=hpg, na_width=NA_HEADS * NA_HEAD_DIM, ssd_width=d, n_in=n_in, n_pad=n_pad,
        tiles_in=(tm, _pick(n_pad, (512, 256, 128)), d),
        tiles_out_even=(tm, _pick(d, (1024, 512, 256, 128)), _pick(NA_HEADS * NA_HEAD_DIM + d, (1024, 512, 256, 128))),
        tiles_in_odd=(tm, _pick(4 * d, (1024, 512, 256, 128)), d),
        tiles_out_odd=(tm, _pick(d, (1024, 512, 256, 128)), d),
        tile_tok=128, tile_res=256,
    )
    me = _my_index()
    xl, cl, ctxl, tgt = x[0], c, ctx[0], loss_target[0]

    ncol = ada_w.shape[-1]
    c_rows = -(-d // 128)
    c_all = _small_allgather(jnp.pad(cl.reshape(-1), (0, c_rows * 128 - d)).reshape(c_rows, 128), "gather_c")[0]
    c_all = c_all.reshape(NDEV, -1)[:, :d]
    cond = jnp.concatenate([c_all, c_ctx[None, :], jnp.zeros((16 - NDEV - 1, d), F32)], axis=0)
    s16 = jax.nn.silu(cond)
    ada_b_mine = lax.dynamic_slice_in_dim(ada_b, me * ncol, ncol, axis=1)
    mod_part = jnp.stack([
        _matmul(s16, ada_w[i], tm=16, tn=_pick(ncol, (768, 512, 256, 128)), tk=d, name=f"adaln{i}") + ada_b_mine[i][None, :]
        for i in range(DEPTH)])
    mp_rows = DEPTH * 16 * ncol // 128
    mod_all = _small_allgather(mod_part.reshape(mp_rows, 128), "gather_mod")[0]
    mod_all = mod_all.reshape(NDEV, DEPTH, 16, ncol).transpose(1, 2, 0, 3).reshape(DEPTH, 16, NDEV * ncol)
    mods = lax.dynamic_index_in_dim(mod_all, me, axis=1, keepdims=False)
    mods_c = mod_all[:, NDEV]

    packed = _pack_transposed(na_ssd_w_in, "pack_w_in")
    wout_b = _cast_bf16(na_ssd_w_out, "cast_w_out")
    scwin_b = _cast_bf16(sc_w_in, "cast_sc_w_in")
    scwout_b = _cast_bf16(sc_w_out, "cast_sc_w_out")
    packed_all, wout_all, scwin_all, scwout_all = _big_allgather(
        [packed, wout_b, scwin_b, scwout_b], [1, 1, 2, 1], "gather_weights")
    win_t = _unpack(packed_all, n_pad, "unpack_w_in")
    conv_shapes = [ssd_conv_w.shape, sc_conv_w.shape]
    conv_all = _small_allgather(_flatten([ssd_conv_w, sc_conv_w]), "gather_conv")[0]
    conv_parts = [_unflatten(conv_all[j], conv_shapes) for j in range(NDEV)]
    ssd_conv_full = jnp.concatenate([cp[0] for cp in conv_parts], axis=-1)
    sc_conv_full = jnp.concatenate([cp[1] for cp in conv_parts], axis=-1)

    small = {n: weights[n] for n in SMALL_REPLICATED}
    small["ssd_conv_w"] = ssd_conv_full
    small["sc_conv_w"] = sc_conv_full
    gslots = dict(g_win_t=jnp.zeros(win_t.shape, F32), g_wout=jnp.zeros(wout_all.shape, F32),
                  g_sc_win=jnp.zeros(scwin_all.shape, F32), g_sc_wout=jnp.zeros(scwout_all.shape, F32))
    frozen = dict(win_t=win_t, wout=wout_all, sc_win=scwin_all, sc_wout=scwout_all)

    def loss_fn(xl, mods, mods_c, small, gslots):
        return _local_loss(cfg, xl, ctxl, tgt, mods, mods_c, {**small, **gslots, **frozen})

    loss_local, (g_x, g_mods, g_mods_c, g_small, g_big) = jax.value_and_grad(loss_fn, argnums=(0, 1, 2, 3, 4))(
        xl, mods, mods_c, small, gslots)

    small_names = SMALL_REPLICATED + ["ssd_conv_w", "sc_conv_w"]
    small_shapes = [g_small[n].shape for n in small_names] + [g_mods_c.shape]
    flat_small = _flatten([g_small[n] for n in small_names] + [g_mods_c])
    _, small_sum = _small_allgather(flat_small, "gather_small_grads")
    summed = _unflatten(small_sum, small_shapes)
    g_rep = dict(zip(small_names, summed[:-1]))
    g_mods_c_tot = summed[-1]
    gm_rows = DEPTH * NDEV * ncol // 128
    gm_all = _small_allgather(g_mods.reshape(gm_rows, 128), "gather_mod_grads")[0].reshape(NDEV, DEPTH, NDEV * ncol)
    dm = jnp.concatenate([gm_all.transpose(1, 0, 2), g_mods_c_tot[:, None, :],
                          jnp.zeros((DEPTH, 16 - NDEV - 1, NDEV * ncol), F32)], axis=1)
    grad_ada_b = jnp.sum(dm, axis=1)
    dm_mine = lax.dynamic_slice_in_dim(dm, me * ncol, ncol, axis=2)
    grad_ada_w = jnp.stack([
        _matmul(s16, dm_mine[i], ta=True, tm=_pick(d, (512, 256, 128)), tn=_pick(ncol, (768, 512, 256, 128)), tk=16,
                name=f"adaln_gw{i}") for i in range(DEPTH)])
    ds_part = sum(_matmul(dm_mine[i], ada_w[i], tb=True, tm=16, tn=_pick(d, (2048, 1024, 512, 256, 128)),
                          tk=_pick(ncol, (768, 512, 256, 128)), name=f"adaln_gs{i}") for i in range(DEPTH))[NDEV]
    ds_ctx = _small_allgather(jnp.pad(ds_part, (0, c_rows * 128 - d)).reshape(c_rows, 128), "gather_c_ctx_grad")[1]
    ds_ctx = ds_ctx.reshape(-1)[:d]
    sig = jax.nn.sigmoid(c_ctx)
    grad_c_ctx = ds_ctx * (sig * (1 + c_ctx * (1 - sig)))

    big = [g_big["g_win_t"], g_big["g_wout"], g_big["g_sc_win"], g_big["g_sc_wout"]]
    axes = [1, 1, 2, 1]
    lens = [n_in_shard, na_ssd_w_out.shape[1], sc_w_in.shape[2], sc_w_out.shape[1]]
    sends = [_pick_blocks(g, None, ax, ln, "cast", BF16, f"rs_cast{i}") for i, (g, ax, ln) in enumerate(zip(big, axes, lens))]
    recv_a = _rs_stage_a(sends, "reduce_scatter_d2d")
    part_a = [_pick_blocks(g, r, ax, ln, "mine", F32, f"pair_sum_mine{i}")
              for i, (g, r, ax, ln) in enumerate(zip(big, recv_a, axes, lens))]
    part_s = [_pick_blocks(g, r, ax, ln, "send", BF16, f"pair_sum_send{i}")
              for i, (g, r, ax, ln) in enumerate(zip(big, recv_a, axes, lens))]
    part_b = _rs_stage_b(part_s, "reduce_scatter_ici")

    res = {}
    res["na_ssd_w_in"] = _adam_sharded(na_ssd_w_in, m_na_ssd_w_in, v_na_ssd_w_in, part_a[0], part_b[0], transposed=True, name="adam_w_in")
    res["na_ssd_w_out"] = _adam_sharded(na_ssd_w_out, m_na_ssd_w_out, v_na_ssd_w_out, part_a[1], part_b[1], transposed=False, name="adam_w_out")
    res["sc_w_in"] = _adam_sharded(sc_w_in, m_sc_w_in, v_sc_w_in, part_a[2], part_b[2], transposed=False, name="adam_sc_w_in")
    res["sc_w_out"] = _adam_sharded(sc_w_out, m_sc_w_out, v_sc_w_out, part_a[3], part_b[3], transposed=False, name="adam_sc_w_out")

    grads = dict(g_rep)
    grads["ssd_conv_w"] = lax.dynamic_slice_in_dim(g_rep["ssd_conv_w"], me * ssd_conv_w.shape[-1], ssd_conv_w.shape[-1], axis=2)
    grads["sc_conv_w"] = lax.dynamic_slice_in_dim(g_rep["sc_conv_w"], me * sc_conv_w.shape[-1], sc_conv_w.shape[-1], axis=2)
    grads["c_ctx"] = grad_c_ctx
    res["ada_w"] = (grad_ada_w, *_adam_tiled(ada_w, grad_ada_w, m_ada_w, v_ada_w, "adam_ada_w"))
    grads["ada_b"] = grad_ada_b
    rest = [n for n in WEIGHT_ORDER if n not in res]
    shapes = [weights[n].shape for n in rest]
    d_flat, m_flat, v_flat = _adam_small(_flatten([weights[n] for n in rest]), _flatten([grads[n] for n in rest]),
                                         _flatten([mom_m[n] for n in rest]), _flatten([mom_v[n] for n in rest]), "adam_small")
    for n, dd, mm, vv in zip(rest, _unflatten(d_flat, shapes), _unflatten(m_flat, shapes), _unflatten(v_flat, shapes)):
        res[n] = (grads[n], dd, mm, vv)

    loss = lax.psum(loss_local, ("x", "y", "c"))
    return (loss, g_x[None], *[res[n][0] for n in WEIGHT_ORDER], *[res[n][1] for n in WEIGHT_ORDER],
            *[res[n][2] for n in WEIGHT_ORDER], *[res[n][3] for n in WEIGHT_ORDER])
```

```python
import functools
import math
from types import SimpleNamespace

import jax
import jax.numpy as jnp
from jax import lax
from jax.experimental import pallas as pl
from jax.experimental.pallas import tpu as pltpu

F32 = jnp.float32
BF16 = jnp.bfloat16
U32 = jnp.uint32
HIGHEST = lax.Precision.HIGHEST
MESH = pl.DeviceIdType.MESH
ANY = pl.BlockSpec(memory_space=pl.ANY)
VMEM = pl.BlockSpec(memory_space=pltpu.VMEM)

NDEV = 8
DEPTH = 4
GRID_W = 64
EPS = 1e-6
NA_HEADS = 16
NA_HEAD_DIM = 128
NA_KH = 8
NA_KW = 16
SSD_HEAD_DIM = 64
SSD_GROUPS = 8
SSD_STATE = 128
SSD_CONV = 5
SSD_CHUNK = 128
SC_CONV = 3
ADAM_LR = 0.001
ADAM_B1 = 0.9
ADAM_B2 = 0.999
ADAM_EPS = 1e-08
ADAM_WD = 0.01
ADAM_STEP = 10
NEG = -1e30
VMEM_LIMIT = 56 * 1024 * 1024
PAD_TO = 512


def _pos():
    return lax.axis_index("x"), lax.axis_index("y"), lax.axis_index("c")


def _my_index():
    x, y, c = _pos()
    return 4 * x + 2 * y + c


def _pick(n, prefs):
    for p in prefs:
        if n % p == 0:
            return p
    return n


def _cparams(**kw):
    return pltpu.CompilerParams(vmem_limit_bytes=VMEM_LIMIT, **kw)


def _small_allgather(v, name):
    rows, lanes = v.shape

    def body(x_ref, out_ref, sum_ref, send_sems, recv_sems):
        x, y, c = _pos()
        me = 4 * x + 2 * y + c
        out_ref[me] = x_ref[...]
        copies = []
        for k in range(1, NDEV):
            peer = (1 - x if k & 4 else x, 1 - y if k & 2 else y, 1 - c if k & 1 else c)
            cp = pltpu.make_async_remote_copy(src_ref=x_ref, dst_ref=out_ref.at[me], send_sem=send_sems.at[k - 1],
                                              recv_sem=recv_sems.at[k - 1], device_id=peer, device_id_type=MESH)
            cp.start()
            copies.append(cp)
        for cp in copies:
            cp.wait()
        acc = out_ref[0]
        for j in range(1, NDEV):
            acc = acc + out_ref[j]
        sum_ref[...] = acc

    return pl.pallas_call(
        body, name=name,
        out_shape=(jax.ShapeDtypeStruct((NDEV, rows, lanes), v.dtype), jax.ShapeDtypeStruct((rows, lanes), v.dtype)),
        in_specs=[VMEM], out_specs=(VMEM, VMEM),
        scratch_shapes=[pltpu.SemaphoreType.DMA((NDEV - 1,)), pltpu.SemaphoreType.DMA((NDEV - 1,))],
        compiler_params=_cparams(),
    )(v)


def _window(ref, axis, idx, length):
    sl = [slice(None)] * len(ref.shape)
    sl[axis] = pl.ds(pl.multiple_of(idx * length, min(length & -length, 1024)), length)
    return ref.at[tuple(sl)]


def _big_allgather(shards, axes, name):
    n = len(shards)
    out_shapes = []
    for s, ax in zip(shards, axes):
        shp = list(s.shape)
        shp[ax] *= NDEV
        out_shapes.append(jax.ShapeDtypeStruct(tuple(shp), s.dtype))

    def body(*refs):
        xs, outs = refs[:n], refs[n:2 * n]
        send_sems, recv_sems, local_sems = refs[2 * n:]
        x, y, c = _pos()
        me, sib = (x, y, c), (x, y, 1 - c)
        chips = [(1 - x, y), (x, 1 - y), (1 - x, 1 - y)]

        def win(a, px, py, pc):
            return _window(outs[a], axes[a], 4 * px + 2 * py + pc, shards[a].shape[axes[a]])

        def copy(a, k, block, to, src=None):
            return pltpu.make_async_remote_copy(src_ref=win(a, *block) if src is None else src, dst_ref=win(a, *block),
                                                send_sem=send_sems.at[a * 7 + k], recv_sem=recv_sems.at[a * 7 + k],
                                                device_id=to, device_id_type=MESH)

        mine = [pltpu.make_async_copy(xs[a], win(a, *me), local_sems.at[a]) for a in range(n)]
        for cp in mine:
            cp.start()
        first = []
        for a in range(n):
            first.append(copy(a, 0, me, sib, src=xs[a]))
            first += [copy(a, 1 + j, me, (*chip, c), src=xs[a]) for j, chip in enumerate(chips)]
        for cp in first:
            cp.start()
        passed = []
        for j, chip in enumerate(chips):
            for a in range(n):
                copy(a, 1 + j, (*chip, c), me).wait_recv()
                cp = copy(a, 4 + j, (*chip, c), sib)
                cp.start()
                passed.append(cp)
        for a in range(n):
            copy(a, 0, sib, me).wait_recv()
            for j, chip in enumerate(chips):
                copy(a, 4 + j, (*chip, 1 - c), me).wait_recv()
        for cp in first + passed:
            cp.wait_send()
        for cp in mine:
            cp.wait()

    return pl.pallas_call(
        body, name=name, out_shape=tuple(out_shapes), in_specs=[ANY] * n, out_specs=tuple([ANY] * n),
        scratch_shapes=[pltpu.SemaphoreType.DMA((7 * n,)), pltpu.SemaphoreType.DMA((7 * n,)), pltpu.SemaphoreType.DMA((n,))],
        compiler_params=_cparams(),
    )(*shards)


def _rs_stage_a(sends, name):
    n = len(sends)
    out_shapes = [jax.ShapeDtypeStruct(s.shape, s.dtype) for s in sends]

    def body(*refs):
        ss, outs = refs[:n], refs[n:2 * n]
        send_sems, recv_sems = refs[2 * n:]
        x, y, c = _pos()
        sib = (x, y, 1 - c)
        copies = []
        for a in range(n):
            for k in range(4):
                cp = pltpu.make_async_remote_copy(src_ref=ss[a].at[k], dst_ref=outs[a].at[k], send_sem=send_sems.at[a * 4 + k],
                                                  recv_sem=recv_sems.at[a * 4 + k], device_id=sib, device_id_type=MESH)
                cp.start()
                copies.append(cp)
        for cp in copies:
            cp.wait()

    return pl.pallas_call(
        body, name=name, out_shape=tuple(out_shapes), in_specs=[ANY] * n, out_specs=tuple([ANY] * n),
        scratch_shapes=[pltpu.SemaphoreType.DMA((4 * n,)), pltpu.SemaphoreType.DMA((4 * n,))],
        compiler_params=_cparams(),
    )(*sends)


def _rs_stage_b(parts, name):
    n = len(parts)
    out_shapes = [jax.ShapeDtypeStruct(p.shape, p.dtype) for p in parts]

    def body(*refs):
        ps, outs = refs[:n], refs[n:2 * n]
        send_sems, recv_sems = refs[2 * n:]
        x, y, c = _pos()
        chips = [(1 - x, y), (x, 1 - y), (1 - x, 1 - y)]
        copies = []
        for a in range(n):
            for j, (px, py) in enumerate(chips):
                cp = pltpu.make_async_remote_copy(src_ref=ps[a].at[j], dst_ref=outs[a].at[j],
                                                  send_sem=send_sems.at[a * 3 + j], recv_sem=recv_sems.at[a * 3 + j],
                                                  device_id=(px, py, c), device_id_type=MESH)
                cp.start()
                copies.append(cp)
        for cp in copies:
            cp.wait()

    return pl.pallas_call(
        body, name=name, out_shape=tuple(out_shapes), in_specs=[ANY] * n, out_specs=tuple([ANY] * n),
        scratch_shapes=[pltpu.SemaphoreType.DMA((3 * n,)), pltpu.SemaphoreType.DMA((3 * n,))],
        compiler_params=_cparams(),
    )(*parts)


def _matmul(a, b, *, ta=False, tb=False, tm, tn, tk, name, b_off=(0, 0), b_extent=None):
    m, kdim = (a.shape[1], a.shape[0]) if ta else a.shape
    b_shape = b.shape if b_extent is None else b_extent
    n = b_shape[0] if tb else b_shape[1]
    assert (b_shape[1] if tb else b_shape[0]) == kdim, (a.shape, b_shape, ta, tb)
    assert m % tm == 0 and n % tn == 0 and kdim % tk == 0, (m, n, kdim, tm, tn, tk)
    o0, o1 = b_off
    nk = kdim // tk
    dn = (((0 if ta else 1,), (1 if tb else 0,)), ((), ()))

    def body(a_ref, b_ref, o_ref, acc_ref):
        k = pl.program_id(2)
        part = lax.dot_general(a_ref[...].astype(BF16), b_ref[...].astype(BF16), dn, preferred_element_type=F32)
        if nk == 1:
            o_ref[...] = part
        else:
            @pl.when(k == 0)
            def _():
                acc_ref[...] = part

            @pl.when(k > 0)
            def _():
                acc_ref[...] += part

            @pl.when(k == nk - 1)
            def _():
                o_ref[...] = acc_ref[...]

    a_spec = pl.BlockSpec((tk, tm), lambda i, j, k: (k, i)) if ta else pl.BlockSpec((tm, tk), lambda i, j, k: (i, k))
    b_spec = (pl.BlockSpec((tn, tk), lambda i, j, k: (j + o0, k + o1)) if tb
              else pl.BlockSpec((tk, tn), lambda i, j, k: (k + o0, j + o1)))
    acc_shape = (tm, tn) if nk > 1 else (8, 128)
    return pl.pallas_call(
        body, name=name, grid=(m // tm, n // tn, nk), out_shape=jax.ShapeDtypeStruct((m, n), F32),
        in_specs=[a_spec, b_spec], out_specs=pl.BlockSpec((tm, tn), lambda i, j, k: (i, j)),
        scratch_shapes=[pltpu.VMEM(acc_shape, F32)], compiler_params=_cparams(),
    )(a, b)


def _linear(a, w, gslot, *, w_is_nk, tiles, name):
    tm, tn, tk = tiles

    @jax.custom_vjp
    def lin(a, w, gslot):
        return _matmul(a, w, tb=w_is_nk, tm=tm, tn=tn, tk=tk, name=name + "_fwd")

    def fwd(a, w, gslot):
        return lin(a, w, gslot), (a, w)

    def bwd(res, g):
        a, w = res
        gb = g
        t, kdim = a.shape
        n = g.shape[1]
        tt = _pick(t, (768, 512, 256, 128))
        tkk = _pick(kdim, (2048, 1024, 512, 256, 128))
        tnn = _pick(n, (1024, 512, 256, 128))
        da = _matmul(gb, w, tb=not w_is_nk, tm=tt, tn=tkk, tk=tnn, name=name + "_bwd_a")
        tok = _pick(t, (1408, 768, 512, 256, 128))
        if w_is_nk:
            dw = _matmul(gb, a, ta=True, tm=tnn, tn=tkk, tk=tok, name=name + "_bwd_w")
        else:
            tkw = _pick(kdim, (1024, 512, 256, 128))
            tnw = _pick(n, (1024, 512, 256, 128))
            dw = _matmul(a, gb, ta=True, tm=tkw, tn=tnw, tk=tok, name=name + "_bwd_w")
        return da.astype(a.dtype), jnp.zeros_like(w), dw

    lin.defvjp(fwd, bwd)
    return lin(a, w, gslot)


def _pack_transposed(w, name):
    nl, kdim, r = w.shape
    half = kdim // 2
    tc = _pick(half, (256, 128))

    def body(lo_ref, hi_ref, o_ref):
        lo = pltpu.bitcast(lo_ref[0].astype(BF16).astype(F32).T, U32) >> 16
        hi = pltpu.bitcast(hi_ref[0].astype(BF16).astype(F32).T, U32) & jnp.uint32(0xFFFF0000)
        o_ref[0] = pltpu.bitcast(hi | lo, F32)

    nb = half // tc
    return pl.pallas_call(
        body, name=name, grid=(nl, nb), out_shape=jax.ShapeDtypeStruct((nl, r, half), F32),
        in_specs=[pl.BlockSpec((1, tc, r), lambda l, t: (l, t, 0)), pl.BlockSpec((1, tc, r), lambda l, t: (l, t + nb, 0))],
        out_specs=pl.BlockSpec((1, r, tc), lambda l, t: (l, 0, t)), compiler_params=_cparams(),
    )(w, w)


def _unpack(packed, n_pad, name):
    nl, n, half = packed.shape
    tr = math.gcd(math.gcd(n, n_pad - n), 64) if n_pad > n else _pick(n, (64, 32, 16))
    nin = n // tr

    def body(p_ref, o_ref):
        t = pl.program_id(1)

        @pl.when(t < nin)
        def _():
            u = pltpu.bitcast(p_ref[0], U32)
            o_ref[0, :, :half] = pltpu.bitcast(u << 16, F32).astype(BF16)
            o_ref[0, :, half:] = pltpu.bitcast(u & jnp.uint32(0xFFFF0000), F32).astype(BF16)

        @pl.when(t >= nin)
        def _():
            o_ref[...] = jnp.zeros_like(o_ref)

    return pl.pallas_call(
        body, name=name, grid=(nl, n_pad // tr), out_shape=jax.ShapeDtypeStruct((nl, n_pad, 2 * half), BF16),
        in_specs=[pl.BlockSpec((1, tr, half), lambda l, t: (l, jnp.minimum(t, nin - 1), 0))],
        out_specs=pl.BlockSpec((1, tr, 2 * half), lambda l, t: (l, t, 0)), compiler_params=_cparams(),
    )(packed)


def _cast_bf16(w, name):
    nl, r, c = w.shape
    tr = _pick(r, (512, 256, 128, 64, 32, 16))

    def body(w_ref, o_ref):
        o_ref[...] = w_ref[...].astype(BF16)

    return pl.pallas_call(
        body, name=name, grid=(nl, r // tr), out_shape=jax.ShapeDtypeStruct(w.shape, BF16),
        in_specs=[pl.BlockSpec((1, tr, c), lambda l, t: (l, t, 0))], out_specs=pl.BlockSpec((1, tr, c), lambda l, t: (l, t, 0)),
        compiler_params=_cparams(),
    )(w)


NA_ROWS_PER_STEP = 4


def _row_block(i):
    return slice(i * GRID_W, (i + 1) * GRID_W)


def _na_probs(qs, kb_ref, b_ref, step, rows, nctx):
    scale = NA_HEAD_DIM ** -0.5
    nt = (((1,), (1,)), ((), ()))
    kc = kb_ref[0:nctx, :]
    kws, starts, offs, s1 = [], [], [], []
    for i in range(NA_ROWS_PER_STEP):
        r = step * NA_ROWS_PER_STEP + i
        rs = jnp.clip(r - NA_KH // 2, 0, rows - NA_KH)
        offs.append(rs - r + NA_KH - 1)
        starts.append(pl.multiple_of(nctx + rs * GRID_W, GRID_W))
        kws.append(kb_ref[pl.ds(starts[i], NA_KH * GRID_W), :])
        s1.append(lax.dot_general(qs[_row_block(i)], kws[i], nt, preferred_element_type=F32) * scale + b_ref[0, offs[i]])
    s1 = jnp.concatenate(s1, axis=0)
    s2 = lax.dot_general(qs, kc, nt, preferred_element_type=F32) * scale
    m = jnp.maximum(jnp.max(s1, axis=-1, keepdims=True), jnp.max(s2, axis=-1, keepdims=True))
    e1 = jnp.exp(s1 - m)
    e2 = jnp.exp(s2 - m)
    inv = 1.0 / (jnp.sum(e1, axis=-1, keepdims=True) + jnp.sum(e2, axis=-1, keepdims=True))
    return kc, kws, starts, offs, e1 * inv, e2 * inv


def _ctx_probs(qs, kc):
    s = lax.dot_general(qs, kc, (((1,), (1,)), ((), ())), preferred_element_type=F32) * NA_HEAD_DIM ** -0.5
    e = jnp.exp(s - jnp.max(s, axis=-1, keepdims=True))
    return e * (1.0 / jnp.sum(e, axis=-1, keepdims=True))


def _na_geometry(t):
    blk = NA_ROWS_PER_STEP * GRID_W
    rows = (t - blk) // GRID_W
    assert rows % NA_ROWS_PER_STEP == 0 and blk + rows * GRID_W == t, (t, blk)
    return blk, rows


def _na_forward(name, update_ctx, q, k, v, biasw):
    t, width = q.shape
    dh, win = NA_HEAD_DIM, NA_KH * GRID_W
    blk, rows = _na_geometry(t)

    def body(q_ref, k_ref, v_ref, b_ref, o_ref, kb_ref, vb_ref):
        j = pl.program_id(1)
        qs = q_ref[...].astype(BF16)

        @pl.when(j == 0)
        def _():
            kb_ref[...] = k_ref[...].astype(BF16)
            vb_ref[...] = v_ref[...].astype(BF16)
            if update_ctx:
                p = _ctx_probs(qs, kb_ref[0:blk, :])
                o_ref[...] = jnp.dot(p.astype(BF16), vb_ref[0:blk, :], preferred_element_type=F32)
            else:
                o_ref[...] = jnp.zeros_like(o_ref)

        @pl.when(j > 0)
        def _():
            _, _, starts, _, p1, p2 = _na_probs(qs, kb_ref, b_ref, j - 1, rows, blk)
            p1b = p1.astype(BF16)
            o1 = [jnp.dot(p1b[_row_block(i)], vb_ref[pl.ds(starts[i], win), :], preferred_element_type=F32)
                  for i in range(NA_ROWS_PER_STEP)]
            o_ref[...] = jnp.concatenate(o1, axis=0) + jnp.dot(p2.astype(BF16), vb_ref[0:blk, :], preferred_element_type=F32)

    row_spec = pl.BlockSpec((blk, dh), lambda h, j: (j, h))
    all_spec = pl.BlockSpec((t, dh), lambda h, j: (0, h))
    b_spec = pl.BlockSpec((1, NA_KH, GRID_W, win), lambda h, j: (h, 0, 0, 0))
    return pl.pallas_call(
        body, name=name, grid=(NA_HEADS, 1 + rows // NA_ROWS_PER_STEP), out_shape=jax.ShapeDtypeStruct((t, width), F32),
        in_specs=[row_spec, all_spec, all_spec, b_spec], out_specs=row_spec,
        scratch_shapes=[pltpu.VMEM((t, dh), BF16), pltpu.VMEM((t, dh), BF16)], compiler_params=_cparams(),
    )(q, k, v, biasw)


def _na_backward(name, update_ctx, q, k, v, biasw, do):
    t, width = q.shape
    dh, win = NA_HEAD_DIM, NA_KH * GRID_W
    blk, rows = _na_geometry(t)
    scale = dh ** -0.5
    nt = (((1,), (1,)), ((), ()))
    tn = (((0,), (0,)), ((), ()))

    def body(q_ref, k_ref, v_ref, b_ref, do_ref, dq_ref, dk_ref, dv_ref, db_ref, kb_ref, vb_ref):
        j = pl.program_id(1)
        qs = q_ref[...].astype(BF16)
        dob = do_ref[...].astype(BF16)
        ctx = slice(0, blk)

        @pl.when(j == 0)
        def _():
            kb_ref[...] = k_ref[...].astype(BF16)
            vb_ref[...] = v_ref[...].astype(BF16)
            dk_ref[...] = jnp.zeros_like(dk_ref)
            dv_ref[...] = jnp.zeros_like(dv_ref)
            db_ref[...] = jnp.zeros_like(db_ref)
            if update_ctx:
                kc, vc = kb_ref[ctx, :], vb_ref[ctx, :]
                p = _ctx_probs(qs, kc)
                dp = lax.dot_general(dob, vc, nt, preferred_element_type=F32)
                ds = p * (dp - jnp.sum(dp * p, axis=-1, keepdims=True))
                dsb = (ds * scale).astype(BF16)
                dq_ref[...] = jnp.dot(dsb, kc, preferred_element_type=F32)
                dv_ref[ctx, :] += lax.dot_general(p.astype(BF16), dob, tn, preferred_element_type=F32)
                dk_ref[ctx, :] += lax.dot_general(dsb, qs, tn, preferred_element_type=F32)
            else:
                dq_ref[...] = jnp.zeros_like(dq_ref)

        @pl.when(j > 0)
        def _():
            nr = range(NA_ROWS_PER_STEP)
            kc, kws, starts, offs, p1, p2 = _na_probs(qs, kb_ref, b_ref, j - 1, rows, blk)
            vc = vb_ref[ctx, :]
            p1b, p2b = p1.astype(BF16), p2.astype(BF16)
            dp1 = jnp.concatenate([lax.dot_general(dob[_row_block(i)], vb_ref[pl.ds(starts[i], win), :], nt,
                                                   preferred_element_type=F32) for i in nr], axis=0)
            dp2 = lax.dot_general(dob, vc, nt, preferred_element_type=F32)
            delta = jnp.sum(dp1 * p1, axis=-1, keepdims=True) + jnp.sum(dp2 * p2, axis=-1, keepdims=True)
            ds1 = p1 * (dp1 - delta)
            ds2 = p2 * (dp2 - delta)
            ds1b = (ds1 * scale).astype(BF16)
            ds2b = (ds2 * scale).astype(BF16)
            dq1 = [jnp.dot(ds1b[_row_block(i)], kws[i], preferred_element_type=F32) for i in nr]
            dq_ref[...] = jnp.concatenate(dq1, axis=0) + jnp.dot(ds2b, kc, preferred_element_type=F32)
            dv_ref[ctx, :] += lax.dot_general(p2b, dob, tn, preferred_element_type=F32)
            dk_ref[ctx, :] += lax.dot_general(ds2b, qs, tn, preferred_element_type=F32)
            for i in nr:
                sl = pl.ds(starts[i], win)
                db_ref[0, offs[i]] += ds1[_row_block(i)]
                dv_ref[sl, :] += lax.dot_general(p1b[_row_block(i)], dob[_row_block(i)], tn, preferred_element_type=F32)
                dk_ref[sl, :] += lax.dot_general(ds1b[_row_block(i)], qs[_row_block(i)], tn, preferred_element_type=F32)

    row_spec = pl.BlockSpec((blk, dh), lambda h, j: (j, h))
    all_spec = pl.BlockSpec((t, dh), lambda h, j: (0, h))
    b_spec = pl.BlockSpec((1, NA_KH, GRID_W, win), lambda h, j: (h, 0, 0, 0))
    full = jax.ShapeDtypeStruct((t, width), F32)
    return pl.pallas_call(
        body, name=name, grid=(NA_HEADS, 1 + rows // NA_ROWS_PER_STEP),
        out_shape=(full, full, full, jax.ShapeDtypeStruct(biasw.shape, F32)),
        in_specs=[row_spec, all_spec, all_spec, b_spec, row_spec], out_specs=(row_spec, all_spec, all_spec, b_spec),
        scratch_shapes=[pltpu.VMEM((t, dh), BF16), pltpu.VMEM((t, dh), BF16)], compiler_params=_cparams(),
    )(q, k, v, biasw, do)


def _na_attention(e, update_ctx, q, k, v, biasw):
    @jax.custom_vjp
    def attn(q, k, v, biasw):
        return _na_forward(f"na_fwd{e}", update_ctx, q, k, v, biasw)

    attn.defvjp(lambda q, k, v, biasw: (_na_forward(f"na_fwd{e}", update_ctx, q, k, v, biasw), (q, k, v, biasw)),
                lambda res, do: _na_backward(f"na_bwd{e}", update_ctx, *res, do))
    return attn(q, k, v, biasw)


def _bias_windows(rpb):
    col = jnp.arange(GRID_W)
    dc = jnp.clip(col[None, :] - col[:, None], -(NA_KW - 1), NA_KW - 1) + NA_KW - 1
    onehot = (dc[None] == jnp.arange(2 * NA_KW - 1)[:, None, None]).astype(F32)
    tq = jnp.einsum("hrd,dqk->hrqk", rpb, onehot, precision=HIGHEST)
    col_start = jnp.clip(col - NA_KW // 2, 0, GRID_W - NA_KW)
    in_win = (col[None, :] >= col_start[:, None]) & (col[None, :] < col_start[:, None] + NA_KW)
    wins = jnp.stack([tq[:, off:off + NA_KH] for off in range(NA_KH)], axis=1)
    wins = jnp.where(in_win[None, None, None], wins, NEG)
    return wins.transpose(0, 1, 3, 2, 4).reshape(rpb.shape[0], NA_KH, GRID_W, NA_KH * GRID_W)


def _chunk_cumsum(u, reverse, name):
    t, nh = u.shape
    ln = SSD_CHUNK

    def body(u_ref, o_ref):
        row = lax.broadcasted_iota(jnp.int32, (ln, ln), 0)
        col = lax.broadcasted_iota(jnp.int32, (ln, ln), 1)
        uu = u_ref[...]
        down = jnp.dot((col <= row).astype(F32), uu, precision=HIGHEST, preferred_element_type=F32)
        up = jnp.dot((col >= row).astype(F32), uu, precision=HIGHEST, preferred_element_type=F32)
        first = lax.broadcasted_iota(jnp.int32, (ln, nh), 1) < nh // 2
        o_ref[...] = jnp.where(first, up, down) if reverse else jnp.where(first, down, up)

    spec = pl.BlockSpec((ln, nh), lambda i: (i, 0))
    return pl.pallas_call(body, name=name, grid=(t // ln,), out_shape=jax.ShapeDtypeStruct(u.shape, F32),
                          in_specs=[spec], out_specs=spec, compiler_params=_cparams())(u)


def _ssd_cumsum(e, u):
    @jax.custom_vjp
    def cs(u):
        return _chunk_cumsum(u, False, f"ssd_cumsum{e}")

    cs.defvjp(lambda u: (_chunk_cumsum(u, False, f"ssd_cumsum{e}"), None),
              lambda _, g: (_chunk_cumsum(g, True, f"ssd_cumsum_bwd{e}"),))
    return cs(u)


def _mxu_dots():
    c_nn, c_nt, c_tn = (((1,), (0,)), ((), ())), (((1,), (1,)), ((), ())), (((0,), (0,)), ((), ()))

    def dot(a, b, dn):
        return lax.dot_general(a.astype(BF16), b.astype(BF16), dn, preferred_element_type=F32)

    def make(dn, dn_da, a_first, dn_db, b_first):
        @jax.custom_vjp
        def f(a, b):
            return dot(a, b, dn)

        def bwd(res, g):
            a, b = res
            da = dot(g, b, dn_da) if a_first else dot(b, g, dn_da)
            db = dot(g, a, dn_db) if b_first else dot(a, g, dn_db)
            return da, db

        f.defvjp(lambda a, b: (dot(a, b, dn), (a, b)), bwd)
        return f

    nn = make(c_nn, c_nt, True, c_tn, False)
    nt = make(c_nt, c_nn, True, c_tn, True)
    tn = make(c_tn, c_nt, False, c_nn, False)
    return nn, nt, tn


def _ssd_step(h, x, bm, cm, dt, dt_t, cs, cs_t, *, direction, hpg, pdim):
    ln = x.shape[0]
    hp = hpg * pdim
    nn, nt, tn = _mxu_dots()
    row = lax.broadcasted_iota(jnp.int32, (ln, ln), 0)
    colm = lax.broadcasted_iota(jnp.int32, (ln, ln), 1)
    valid = (colm - row) * (1 - 2 * direction) <= 0
    last = (ln - 1) * (1 - direction)
    tot = jnp.sum(jnp.where(lax.broadcasted_iota(jnp.int32, (ln, hpg), 0) == last, cs, 0.0), axis=0, keepdims=True)
    cbm = nt(cm, bm)
    lane_head = lax.broadcasted_iota(jnp.int32, (1, hp), 1) // pdim
    sub_head = lax.broadcasted_iota(jnp.int32, (hp, 1), 0) // pdim
    y = jnp.zeros((ln, hp), F32)
    es = jnp.zeros((ln, hp), F32)
    we = jnp.zeros((ln, hp), F32)
    dend = jnp.zeros((hp, 1), F32)
    for r in range(hpg):
        cc = cs[:, r:r + 1]
        cr = cs_t[r:r + 1, :]
        decay = jnp.exp(jnp.where(valid, cc - cr, NEG))
        mask = lane_head == r
        y = y + nn(cbm * decay * dt_t[r:r + 1, :], jnp.where(mask, x, 0.0))
        es = es + jnp.where(mask, jnp.exp(cc), 0.0)
        we = we + jnp.where(mask, jnp.exp(tot[:, r:r + 1] - cc) * dt[:, r:r + 1], 0.0)
        dend = dend + jnp.where(sub_head == r, jnp.exp(tot[:, r:r + 1]), 0.0)
    y = y + es * nt(cm, h)
    h_new = h * dend + tn(x * we, bm)
    return y, h_new


def _ssd_chunk_of(d, s, ncc, nc):
    return jnp.where(d == 0, s, jnp.where(s < ncc, ncc - 1 - s, nc - 1 - s + ncc))


SSD_GROUPS_PER_STEP = 8


def _groups_per_step():
    return math.gcd(SSD_GROUPS, SSD_GROUPS_PER_STEP)


def _ssd_specs(cfg, step_of):
    gp = _groups_per_step()
    hp, n, ln, hpg = cfg.hpg * SSD_HEAD_DIM, SSD_STATE, SSD_CHUNK, cfg.hpg
    ncc, nc = cfg.nctx // ln, cfg.t // ln
    b_off = cfg.ssd_width // (gp * n)
    c_off = (cfg.ssd_width + SSD_GROUPS * n) // (gp * n)
    assert SSD_GROUPS % gp == 0 and cfg.ssd_width % (gp * n) == 0 and (SSD_GROUPS * n) % (gp * n) == 0

    def ch(d, s):
        return _ssd_chunk_of(d, step_of(s), ncc, nc)

    return dict(
        x=pl.BlockSpec((ln, gp * hp), lambda d, g, s: (ch(d, s), g)),
        bm=pl.BlockSpec((ln, gp * n), lambda d, g, s: (ch(d, s), b_off + g)),
        cm=pl.BlockSpec((ln, gp * n), lambda d, g, s: (ch(d, s), c_off + g)),
        dt=pl.BlockSpec((1, gp, ln, hpg), lambda d, g, s: (d, g, ch(d, s), 0)),
        dt_t=pl.BlockSpec((1, gp, 8, ln), lambda d, g, s: (d, g, 0, ch(d, s))),
        y=pl.BlockSpec((1, ln, gp * hp), lambda d, g, s: (d, ch(d, s), g)),
        bc2=pl.BlockSpec((1, ln, gp * n), lambda d, g, s: (d, ch(d, s), g)),
        h=pl.BlockSpec((1, gp, 1, hp, n), lambda d, g, s: (d, g, step_of(s), 0, 0)),
    )


def _ssd_forward(cfg, name, xbc, dt, dt_t, cs, cs_t):
    gp = _groups_per_step()
    hp, n, hpg = cfg.hpg * SSD_HEAD_DIM, SSD_STATE, cfg.hpg
    nc = cfg.t // SSD_CHUNK
    sp = _ssd_specs(cfg, lambda s: s)

    def body(x_ref, b_ref, c_ref, dt_ref, dtt_ref, cs_ref, cst_ref, y_ref, hs_ref, h_ref):
        d, s = pl.program_id(0), pl.program_id(2)

        @pl.when(s == 0)
        def _():
            h_ref[...] = jnp.zeros_like(h_ref)

        for q in range(gp):
            h = h_ref[q]
            hs_ref[0, q, 0] = h
            y, h_new = _ssd_step(h, x_ref[:, q * hp:(q + 1) * hp], b_ref[:, q * n:(q + 1) * n], c_ref[:, q * n:(q + 1) * n],
                                 dt_ref[0, q], dtt_ref[0, q], cs_ref[0, q], cst_ref[0, q], direction=d, hpg=hpg,
                                 pdim=SSD_HEAD_DIM)
            y_ref[0, :, q * hp:(q + 1) * hp] = y
            h_ref[q] = h_new

    return pl.pallas_call(
        body, name=name, grid=(2, SSD_GROUPS // gp, nc),
        out_shape=(jax.ShapeDtypeStruct((2, cfg.t, cfg.ssd_width), F32),
                   jax.ShapeDtypeStruct((2, SSD_GROUPS, nc, hp, n), F32)),
        in_specs=[sp["x"], sp["bm"], sp["cm"], sp["dt"], sp["dt_t"], sp["dt"], sp["dt_t"]],
        out_specs=(sp["y"], sp["h"]), scratch_shapes=[pltpu.VMEM((gp, hp, n), F32)], compiler_params=_cparams(),
    )(xbc, xbc, xbc, dt, dt_t, cs, cs_t)


def _ssd_backward(cfg, name, xbc, dt, dt_t, cs, cs_t, hsave, dy):
    gp = _groups_per_step()
    hp, n, hpg = cfg.hpg * SSD_HEAD_DIM, SSD_STATE, cfg.hpg
    nc = cfg.t // SSD_CHUNK
    sp = _ssd_specs(cfg, lambda s: nc - 1 - s)

    def body(x_ref, b_ref, c_ref, dt_ref, dtt_ref, cs_ref, cst_ref, hs_ref, dy_ref,
             dx_ref, db_ref, dc_ref, ddt_ref, ddtt_ref, dcs_ref, dcst_ref, dh_ref):
        d, s = pl.program_id(0), pl.program_id(2)

        @pl.when(s == 0)
        def _():
            dh_ref[...] = jnp.zeros_like(dh_ref)

        step = functools.partial(_ssd_step, direction=d, hpg=hpg, pdim=SSD_HEAD_DIM)
        for q in range(gp):
            xc, nc_ = slice(q * hp, (q + 1) * hp), slice(q * n, (q + 1) * n)
            _, vjp = jax.vjp(step, hs_ref[0, q, 0], x_ref[:, xc], b_ref[:, nc_], c_ref[:, nc_], dt_ref[0, q], dtt_ref[0, q],
                             cs_ref[0, q], cst_ref[0, q])
            dh, dx, db, dc, ddt, ddtt, dcs, dcst = vjp((dy_ref[0, :, xc], dh_ref[q]))
            dh_ref[q] = dh
            dx_ref[0, :, xc] = dx
            db_ref[0, :, nc_] = db
            dc_ref[0, :, nc_] = dc
            ddt_ref[0, q] = ddt
            ddtt_ref[0, q] = ddtt
            dcs_ref[0, q] = dcs
            dcst_ref[0, q] = dcst

    gn = SSD_GROUPS * n
    return pl.pallas_call(
        body, name=name, grid=(2, SSD_GROUPS // gp, nc),
        out_shape=(jax.ShapeDtypeStruct((2, cfg.t, cfg.ssd_width), F32), jax.ShapeDtypeStruct((2, cfg.t, gn), F32),
                   jax.ShapeDtypeStruct((2, cfg.t, gn), F32), jax.ShapeDtypeStruct(dt.shape, F32),
                   jax.ShapeDtypeStruct(dt_t.shape, F32), jax.ShapeDtypeStruct(dt.shape, F32),
                   jax.ShapeDtypeStruct(dt_t.shape, F32)),
        in_specs=[sp["x"], sp["bm"], sp["cm"], sp["dt"], sp["dt_t"], sp["dt"], sp["dt_t"], sp["h"], sp["y"]],
        out_specs=(sp["y"], sp["bc2"], sp["bc2"], sp["dt"], sp["dt_t"], sp["dt"], sp["dt_t"]),
        scratch_shapes=[pltpu.VMEM((gp, hp, n), F32)], compiler_params=_cparams(),
    )(xbc, xbc, xbc, dt, dt_t, cs, cs_t, hsave, dy)


def _ssd_scan(cfg, e, xbc, dt, dt_t, cs, cs_t):
    @jax.custom_vjp
    def scan(xbc, dt, dt_t, cs, cs_t):
        return _ssd_forward(cfg, f"ssd_fwd{e}", xbc, dt, dt_t, cs, cs_t)[0]

    def fwd(xbc, dt, dt_t, cs, cs_t):
        y, hsave = _ssd_forward(cfg, f"ssd_fwd{e}", xbc, dt, dt_t, cs, cs_t)
        return y, (xbc, dt, dt_t, cs, cs_t, hsave)

    def bwd(res, dy):
        dx, db, dc, ddt, ddtt, dcs, dcst = _ssd_backward(cfg, f"ssd_bwd{e}", *res, dy)
        return jnp.concatenate([dx[0] + dx[1], db[0] + db[1], dc[0] + dc[1]], axis=1), ddt, ddtt, dcs, dcst

    scan.defvjp(fwd, bwd)
    return scan(xbc, dt, dt_t, cs, cs_t)


def _adam_math(w, g, m, v):
    m2 = ADAM_B1 * m + (1.0 - ADAM_B1) * g
    v2 = ADAM_B2 * v + (1.0 - ADAM_B2) * (g * g)
    m_hat = m2 / (1.0 - ADAM_B1 ** ADAM_STEP)
    v_hat = v2 / (1.0 - ADAM_B2 ** ADAM_STEP)
    delta = -ADAM_LR * (m_hat / (jnp.sqrt(v_hat) + ADAM_EPS) + ADAM_WD * w)
    return delta, m2, v2


def _adam_small(w, g, m, v, name):
    def body(w_ref, g_ref, m_ref, v_ref, d_ref, m2_ref, v2_ref):
        d_ref[...], m2_ref[...], v2_ref[...] = _adam_math(w_ref[...], g_ref[...], m_ref[...], v_ref[...])

    shp = jax.ShapeDtypeStruct(w.shape, F32)
    return pl.pallas_call(body, name=name, out_shape=(shp, shp, shp), in_specs=[VMEM] * 4, out_specs=(VMEM, VMEM, VMEM),
                          compiler_params=_cparams())(w, g, m, v)


def _adam_tiled(w, g, m, v, name):
    nl, r, c = w.shape
    tr = _pick(r, (256, 128, 64, 32, 16, 8))
    spec = pl.BlockSpec((1, tr, c), lambda l, t: (l, t, 0))

    def body(w_ref, g_ref, m_ref, v_ref, d_ref, m2_ref, v2_ref):
        d_ref[...], m2_ref[...], v2_ref[...] = _adam_math(w_ref[...], g_ref[...], m_ref[...], v_ref[...])

    shp = jax.ShapeDtypeStruct(w.shape, F32)
    return pl.pallas_call(body, name=name, grid=(nl, r // tr), out_shape=(shp, shp, shp), in_specs=[spec] * 4,
                          out_specs=(spec, spec, spec), compiler_params=_cparams())(w, g, m, v)


def _adam_sharded(w, m, v, part_a, part_b, *, transposed, name):
    nl, r, c = w.shape
    tr = _pick(r, (256, 128)) if transposed else _pick(r, (256, 128, 64, 32, 16))
    w_spec = pl.BlockSpec((1, tr, c), lambda l, t: (l, t, 0))
    if transposed:
        pa_spec = pl.BlockSpec((1, 1, c, tr), lambda l, t: (0, l, 0, t))
        pb_spec = pl.BlockSpec((3, 1, c, tr), lambda l, t: (0, l, 0, t))
    else:
        pa_spec = pl.BlockSpec((1, 1, tr, c), lambda l, t: (0, l, t, 0))
        pb_spec = pl.BlockSpec((3, 1, tr, c), lambda l, t: (0, l, t, 0))

    def body(w_ref, m_ref, v_ref, pa_ref, pb_ref, g_ref, d_ref, m2_ref, v2_ref):
        g = pa_ref[0, 0] + pb_ref[0, 0].astype(F32) + pb_ref[1, 0].astype(F32) + pb_ref[2, 0].astype(F32)
        if transposed:
            g = g.T
        g_ref[0] = g
        d_ref[0], m2_ref[0], v2_ref[0] = _adam_math(w_ref[0], g, m_ref[0], v_ref[0])

    shp = jax.ShapeDtypeStruct(w.shape, F32)
    return pl.pallas_call(
        body, name=name, grid=(nl, r // tr), out_shape=(shp, shp, shp, shp),
        in_specs=[w_spec, w_spec, w_spec, pa_spec, pb_spec], out_specs=(w_spec, w_spec, w_spec, w_spec),
        compiler_params=_cparams(),
    )(w, m, v, part_a, part_b)


def _pick_blocks(g, recv, axis, length, mode, out_dtype, name):
    nl = g.shape[0]
    n = {"cast": 4, "mine": 1, "send": 3}[mode]
    pos = [p.astype(jnp.int32).reshape(1) for p in _pos()]

    def chip(i, x, y, c):
        if mode == "cast":
            return i
        if mode == "mine":
            return 2 * x[0] + y[0]
        return 2 * jnp.where(i == 1, x[0], 1 - x[0]) + jnp.where(i == 0, y[0], 1 - y[0])

    def gb(i, x, y, c):
        return 2 * chip(i, x, y, c) + (1 - c[0] if mode == "cast" else c[0])

    blk_shape = list(g.shape)
    blk_shape[axis] = length
    if axis == 1:
        cols = g.shape[2]
        tc = _pick(cols, (512, 256, 128))
        g_spec = pl.BlockSpec((1, length, tc), lambda i, l, t, x, y, c: (l, gb(i, x, y, c), t))
        r_spec = pl.BlockSpec((1, 1, length, tc), lambda i, l, t, x, y, c: (chip(i, x, y, c), l, 0, t))
        o_spec = pl.BlockSpec((1, 1, length, tc), lambda i, l, t, x, y, c: (i, l, 0, t))
        grid = (n, nl, cols // tc)
    else:
        rows = g.shape[1]
        tr = _pick(rows, (512, 256, 128, 64, 32, 16))
        g_spec = pl.BlockSpec((1, tr, length), lambda i, l, t, x, y, c: (l, t, gb(i, x, y, c)))
        r_spec = pl.BlockSpec((1, 1, tr, length), lambda i, l, t, x, y, c: (chip(i, x, y, c), l, t, 0))
        o_spec = pl.BlockSpec((1, 1, tr, length), lambda i, l, t, x, y, c: (i, l, t, 0))
        grid = (n, nl, rows // tr)

    if recv is None:
        def body(x_ref, y_ref, c_ref, g_ref, o_ref):
            o_ref[0] = g_ref[...].astype(out_dtype)
        in_specs, args = [g_spec], (g,)
    else:
        def body(x_ref, y_ref, c_ref, g_ref, r_ref, o_ref):
            o_ref[0] = (g_ref[...] + r_ref[0].astype(F32)).astype(out_dtype)
        in_specs, args = [g_spec, r_spec], (g, recv)

    return pl.pallas_call(
        body, name=name, out_shape=jax.ShapeDtypeStruct((n, *blk_shape), out_dtype),
        grid_spec=pltpu.PrefetchScalarGridSpec(num_scalar_prefetch=3, grid=grid, in_specs=in_specs, out_specs=o_spec),
        compiler_params=_cparams(),
    )(*pos, *args)


def _flatten(arrs):
    flat = jnp.concatenate([a.reshape(-1).astype(F32) for a in arrs])
    n = flat.shape[0]
    n_pad = -(-n // 1024) * 1024
    return jnp.pad(flat, (0, n_pad - n)).reshape(n_pad // 128, 128)


def _unflatten(buf, shapes):
    flat = buf.reshape(-1)
    out, o = [], 0
    for s in shapes:
        n = math.prod(s)
        out.append(flat[o:o + n].reshape(s))
        o += n
    return out


def _rowwise(name, fn, rows, seg, shared, out_cols, out_dtypes, tile, nct):
    t = rows[0].shape[0]
    nr, ns, nsh, no = len(rows), len(seg), len(shared), len(out_cols)
    n_in = nr + ns + nsh

    def row_spec(c):
        return pl.BlockSpec((tile, c), lambda i: (i, 0))

    def seg_spec(c):
        return pl.BlockSpec((1, 1, c), lambda i: (jnp.where(i < nct, 0, 1), 0, 0))

    def whole_spec(shape):
        return pl.BlockSpec(shape, lambda i: (0, 0))

    in_specs = ([row_spec(r.shape[1]) for r in rows] + [seg_spec(s.shape[1]) for s in seg]
                + [whole_spec(s.shape) for s in shared])
    out_shapes = tuple(jax.ShapeDtypeStruct((t, c), dt) for c, dt in zip(out_cols, out_dtypes))
    out_specs = tuple(row_spec(c) for c in out_cols)

    def load(refs):
        return [r[0] if nr <= j < nr + ns else r[...] for j, r in enumerate(refs[:n_in])]

    def lift(args):
        return [a[:, None, :] if nr <= j < nr + ns else a for j, a in enumerate(args)]

    def forward(*args):
        def body(*refs):
            outs = fn(*load(refs))
            for o_ref, o in zip(refs[n_in:], outs):
                o_ref[...] = o

        return pl.pallas_call(body, name=name + "_fwd", grid=(t // tile,), out_shape=out_shapes, in_specs=in_specs,
                              out_specs=out_specs, compiler_params=_cparams())(*lift(args))

    def backward(args, cts):
        def body(*refs):
            i = pl.program_id(0)
            ct = tuple(r[...] for r in refs[n_in:n_in + no])
            d_refs = refs[n_in + no:]
            _, vjp = jax.vjp(fn, *load(refs))
            grads = vjp(ct)
            for ref, g in zip(d_refs[:nr], grads[:nr]):
                ref[...] = g
            first_seg = jnp.logical_or(i == 0, i == nct)
            for j in range(nr, n_in):
                ref, g = d_refs[j], grads[j]
                first = first_seg if j < nr + ns else i == 0
                g = g[None] if j < nr + ns else g

                @pl.when(first)
                def _(ref=ref, g=g):
                    ref[...] = g

                @pl.when(jnp.logical_not(first))
                def _(ref=ref, g=g):
                    ref[...] += g

        largs = lift(args)
        d_shapes = tuple(jax.ShapeDtypeStruct(a.shape, F32) for a in largs)
        ct_specs = [row_spec(c) for c in out_cols]
        outs = pl.pallas_call(body, name=name + "_bwd", grid=(t // tile,), out_shape=d_shapes,
                              in_specs=in_specs + ct_specs, out_specs=tuple(in_specs), compiler_params=_cparams())(*largs, *cts)
        return [o[:, 0, :] if nr <= j < nr + ns else o for j, o in enumerate(outs)]

    @jax.custom_vjp
    def prim(*args):
        return tuple(forward(*args))

    prim.defvjp(lambda *args: (tuple(forward(*args)), args), lambda args, cts: tuple(backward(args, cts)))
    return prim(*rows, *seg, *shared)


def _silu(x):
    return x * (1.0 / (1.0 + jnp.exp(-x)))


def _softplus(x):
    return jnp.maximum(x, 0.0) + jnp.log(1.0 + jnp.exp(-jnp.abs(x)))


def _dwconv(name, x, w, b, act, nctx):
    t, c = x.shape
    kk = w.shape[0]
    half = kk // 2
    tile = 256
    tc = _pick(c, (2048, 1024, 512, 256, 128))
    nt, nct, hb = t // tile, nctx // tile, tile // 8
    cur = pl.BlockSpec((tile, tc), lambda j, i: (i, j))
    prev = pl.BlockSpec((8, tc), lambda j, i: (jnp.maximum(i * hb - 1, 0), j))
    nxt = pl.BlockSpec((8, tc), lambda j, i: (jnp.minimum((i + 1) * hb, t // 8 - 1), j))
    w_spec = pl.BlockSpec((kk, tc), lambda j, i: (0, j))
    b_spec = pl.BlockSpec((1, tc), lambda j, i: (0, j))
    grid = (c // tc, nt)

    def extended(cur_ref, prev_ref, next_ref, i):
        has_prev = jnp.logical_and(i != 0, i != nct)
        has_next = jnp.logical_and(i != nct - 1, i != nt - 1)
        return jnp.concatenate([jnp.where(has_prev, prev_ref[...], 0.0), cur_ref[...],
                                jnp.where(has_next, next_ref[...], 0.0)], axis=0)

    def taps(ext, w_ref, lo, n):
        acc = None
        for k in range(kk):
            term = w_ref[k:k + 1, :] * ext[lo + k - half:lo + k - half + n]
            acc = term if acc is None else acc + term
        return acc

    def forward(x, w, b):
        def body(x_ref, xp_ref, xn_ref, w_ref, b_ref, o_ref):
            ext = extended(x_ref, xp_ref, xn_ref, pl.program_id(1))
            y = taps(ext, w_ref, 8, tile) + b_ref[...]
            o_ref[...] = _silu(y) if act else y

        return pl.pallas_call(body, name=name + "_fwd", grid=grid, out_shape=jax.ShapeDtypeStruct((t, c), F32),
                              in_specs=[cur, prev, nxt, w_spec, b_spec], out_specs=cur, compiler_params=_cparams())(x, x, x, w, b)

    def backward(x, w, b, g):
        m = tile + 2 * half

        def body(x_ref, xp_ref, xn_ref, g_ref, gp_ref, gn_ref, w_ref, b_ref, dx_ref, dw_ref, db_ref):
            i = pl.program_id(1)
            xe = extended(x_ref, xp_ref, xn_ref, i)
            dpre = extended(g_ref, gp_ref, gn_ref, i)[8 - half:8 - half + m]
            if act:
                pre = taps(xe, w_ref, 8 - half, m) + b_ref[...]
                sg = 1.0 / (1.0 + jnp.exp(-pre))
                dpre = dpre * (sg * (1.0 + pre * (1.0 - sg)))
            acc = None
            for k in range(kk):
                term = w_ref[k:k + 1, :] * dpre[2 * half - k:2 * half - k + tile]
                acc = term if acc is None else acc + term
            dx_ref[...] = acc
            dcur = dpre[half:half + tile]
            dw = jnp.concatenate([jnp.sum(dcur * xe[8 + k - half:8 + k - half + tile], axis=0, keepdims=True)
                                  for k in range(kk)], axis=0)
            db = jnp.sum(dcur, axis=0, keepdims=True)

            @pl.when(i == 0)
            def _():
                dw_ref[...] = dw
                db_ref[...] = db

            @pl.when(i != 0)
            def _():
                dw_ref[...] += dw
                db_ref[...] += db

        return pl.pallas_call(
            body, name=name + "_bwd", grid=grid,
            out_shape=(jax.ShapeDtypeStruct((t, c), F32), jax.ShapeDtypeStruct(w.shape, F32), jax.ShapeDtypeStruct(b.shape, F32)),
            in_specs=[cur, prev, nxt, cur, prev, nxt, w_spec, b_spec], out_specs=(cur, w_spec, b_spec),
            compiler_params=_cparams())(x, x, x, g, g, g, w, b)

    @jax.custom_vjp
    def conv(x, w, b):
        return forward(x, w, b)

    conv.defvjp(lambda x, w, b: (forward(x, w, b), (x, w, b)), lambda res, g: backward(*res, g))
    return conv(x, w, b)


def _odd_pre_tile(pch):
    half = pch.shape[1] // 2
    return (pch[:, :half] * pch[:, half:],)


def _odd_post_tile(pb, pg, yc):
    return ((_silu(pg) * (pb * yc)).astype(BF16),)


def _rms_rows(x):
    return x * lax.rsqrt(jnp.mean(x * x, axis=-1, keepdims=True) + EPS)


def _pre0_tile(x, scale, shift, g):
    return ((_rms_rows(x) * g * (1 + scale) + shift).astype(BF16),)


def _pre_tile(x, y_prev, gate, scale, shift, g):
    xn = x + gate * y_prev
    return xn, (_rms_rows(xn) * g * (1 + scale) + shift).astype(BF16)


def _loss_tile(x, y_prev, target, gate, weight):
    err = (x + gate * y_prev - target) * weight
    return (0.5 * jnp.mean(err * err, axis=-1, keepdims=True),)


def _mid_even_tile(pa, pd, qg, kg, dt_bias, a, *, naw, sw, nh):
    def heads_norm(u, g):
        return jnp.concatenate([_rms_rows(u[:, j:j + NA_HEAD_DIM]) * g for j in range(0, naw, NA_HEAD_DIM)], axis=1)

    q, gate, z = pa[:, :naw], pa[:, naw:2 * naw], pa[:, 2 * naw:2 * naw + sw]
    k, v = pa[:, 2 * naw + sw:3 * naw + sw], pa[:, 3 * naw + sw:]
    dt = _softplus(pd[:, :nh] + dt_bias)
    return heads_norm(q, qg), heads_norm(k, kg), v, _silu(gate), _silu(z), dt, dt * a


def _post_even_tile(ya, sg, y0, y1, xs, sz, dskip, g, *, sw):
    yz = (y0 + y1 + dskip * xs) * sz
    gw = sw // SSD_GROUPS
    yb = jnp.concatenate([_rms_rows(yz[:, j:j + gw]) for j in range(0, sw, gw)], axis=1) * g
    return (jnp.concatenate([ya * sg, yb], axis=1).astype(BF16),)


def _in_proj_split(a, w, gslot, widths, *, w_is_nk, tm, tn, name):
    t, kdim = a.shape
    starts = [sum(widths[:i]) for i in range(len(widths))]
    assert all(s % tn == 0 and wd % tn == 0 for s, wd in zip(starts, widths)), (starts, widths, tn)

    def piece(i, blocks):
        off = (starts[i] // blocks, 0) if w_is_nk else (0, starts[i] // blocks)
        ext = (widths[i], kdim) if w_is_nk else (kdim, widths[i])
        return off, ext

    def forward(a, w):
        outs = []
        for i in range(len(widths)):
            off, ext = piece(i, tn)
            outs.append(_matmul(a, w, tb=w_is_nk, tm=tm, tn=tn, tk=kdim, b_off=off, b_extent=ext, name=f"{name}_fwd{i}"))
        return tuple(outs)

    @jax.custom_vjp
    def proj(a, w, gslot):
        return forward(a, w)

    def bwd(res, gs):
        a, w = res
        tt = _pick(t, (768, 512, 256, 128))
        tkk = _pick(kdim, (2048, 1024, 512, 256, 128))
        tok = _pick(t, (1408, 768, 512, 256, 128))
        da, dws = None, []
        for i, g in enumerate(gs):
            big = 2 * tn if widths[i] % (2 * tn) == 0 and starts[i] % (2 * tn) == 0 else tn
            off, ext = piece(i, big)
            part = _matmul(g, w, tb=not w_is_nk, tm=tt, tn=tkk, tk=big, b_off=off, b_extent=ext, name=f"{name}_bwd_a{i}")
            da = part if da is None else da + part
            if w_is_nk:
                dws.append(_matmul(g, a, ta=True, tm=big, tn=tkk, tk=tok, name=f"{name}_bwd_w{i}"))
            else:
                dws.append(_matmul(a, g, ta=True, tm=_pick(kdim, (1024, 512, 256, 128)), tn=_pick(widths[i], (1024, 512, 256, 128)),
                                   tk=tok, name=f"{name}_bwd_w{i}"))
        dw = jnp.concatenate(dws, axis=0 if w_is_nk else 1)
        if dw.shape != w.shape:
            dw = jnp.pad(dw, [(0, w.shape[0] - dw.shape[0]), (0, w.shape[1] - dw.shape[1])])
        return da.astype(a.dtype), jnp.zeros_like(w), dw

    proj.defvjp(lambda a, w, gslot: (forward(a, w), (a, w)), bwd)
    return proj(a, w, gslot)


def _rms(x, g):
    return x * lax.rsqrt(jnp.mean(x * x, axis=-1, keepdims=True) + EPS) * g


def _dw_conv(x, w, b=None):
    k = w.shape[0]
    ln = x.shape[0]
    xp = jnp.pad(x, ((k // 2, k // 2), (0, 0)))
    y = sum(w[i][None, :] * xp[i:i + ln] for i in range(k))
    return y if b is None else y + b


def _conv_two(x, nctx, w, b=None):
    return jnp.concatenate([_dw_conv(x[:nctx], w, b), _dw_conv(x[nctx:], w, b)], axis=0)


def _mod_rows(nctx, seq, ctx_vec, lat_vec):
    return jnp.concatenate([jnp.broadcast_to(ctx_vec, (nctx, ctx_vec.shape[-1])),
                            jnp.broadcast_to(lat_vec, (seq, lat_vec.shape[-1]))], axis=0)


def _even_mixer(cfg, h, e, wd, update_ctx):
    d, nctx, seq, t = cfg.d, cfg.nctx, cfg.s, cfg.t
    naw, sw = cfg.na_width, cfg.ssd_width
    gn = SSD_GROUPS * SSD_STATE
    nh = 2 * SSD_GROUPS * cfg.hpg
    wa, wx = 4 * naw + sw, sw + 2 * gn
    pa, px, pd = _in_proj_split(h, wd["win_t"][e], wd["g_win_t"][e], (wa, wx, cfg.n_pad - wa - wx), w_is_nk=True,
                                tm=cfg.tiles_in[0], tn=cfg.tiles_in[1], name=f"in_even{e}")
    tile = cfg.tile_tok
    nct = nctx // tile
    a_neg = -jnp.exp(wd["ssd_a_log"][e]).reshape(1, nh)
    qn, kn, vv, sg, sz, dt, dta = _rowwise(
        f"mid_even{e}", functools.partial(_mid_even_tile, naw=naw, sw=sw, nh=nh), [pa, pd], [],
        [wd["q_norm_g"][e][None, :], wd["k_norm_g"][e][None, :], wd["ssd_dt_bias"][e].reshape(1, nh), a_neg],
        [naw, naw, naw, naw, sw, nh, nh], [F32] * 7, tile, nct)
    biasw = _bias_windows(wd["na_rpb"][e])
    ya = _na_attention(e, update_ctx, qn, kn, vv, biasw)

    xbc = _dwconv(f"ssd_conv{e}", px, wd["ssd_conv_w"][e], wd["ssd_conv_b"][e][None, :], True, nctx)
    xs = xbc[:, :sw]

    def arrange(u):
        u4 = u.reshape(t, 2, SSD_GROUPS, cfg.hpg)
        return u4.transpose(1, 2, 0, 3), jnp.pad(u4.transpose(1, 2, 3, 0), ((0, 0), (0, 0), (0, 8 - cfg.hpg), (0, 0)))

    dt4, dt_t = arrange(dt)
    cs4, cs_t = arrange(_ssd_cumsum(e, dta))
    y2 = _ssd_scan(cfg, e, xbc, dt4, dt_t, cs4, cs_t)
    dskip = jnp.repeat(wd["ssd_d"][e], SSD_HEAD_DIM)[None, :]
    (ycat,) = _rowwise(f"post_even{e}", functools.partial(_post_even_tile, sw=sw), [ya, sg, y2[0], y2[1], xs, sz], [],
                       [dskip, wd["ssd_norm_g"][e][None, :]], [naw + sw], [BF16], tile, nct)
    return _linear(ycat, wd["wout"][e], wd["g_wout"][e], w_is_nk=False, tiles=cfg.tiles_out_even, name=f"out_even{e}")


def _odd_mixer(cfg, h, o, wd):
    d, nctx = cfg.d, cfg.nctx
    tile = cfg.tile_res
    nct = nctx // tile
    pb, pch, pg = _in_proj_split(h, wd["sc_win"][o], wd["g_sc_win"][o], (d, 2 * d, d), w_is_nk=False,
                                 tm=cfg.tiles_in_odd[0], tn=_pick(d, (1024, 512, 256, 128)), name=f"in_odd{o}")
    (cv,) = _rowwise(f"odd_pre{o}", _odd_pre_tile, [pch], [], [], [d], [F32], tile, nct)
    yc = _dwconv(f"sc_conv{o}", cv, wd["sc_conv_w"][o], jnp.zeros((1, d), F32), False, nctx)
    (u,) = _rowwise(f"odd_post{o}", _odd_post_tile, [pb, pg, yc], [], [], [d], [BF16], tile, nct)
    return _linear(u, wd["sc_wout"][o], wd["g_sc_wout"][o], w_is_nk=False, tiles=cfg.tiles_out_odd, name=f"out_odd{o}")


def _local_loss(cfg, x, ctx, target, mods, mods_c, wd):
    d, nctx, seq = cfg.d, cfg.nctx, cfg.s
    tile = cfg.tile_res
    nct = nctx // tile
    xx = jnp.concatenate([ctx, x], axis=0)
    y_prev = gate_prev = None
    for i in range(DEPTH):
        update_ctx = any(j % 2 == 0 for j in range(i + 1, DEPTH))
        shift = jnp.stack([mods_c[i, :d], mods[i, :d]])
        scale = jnp.stack([mods_c[i, d:2 * d], mods[i, d:2 * d]])
        g = wd["norm_g"][i][None, :]
        if y_prev is None:
            (h,) = _rowwise(f"pre{i}", _pre0_tile, [xx], [scale, shift], [g], [d], [BF16], tile, nct)
        else:
            xx, h = _rowwise(f"pre{i}", _pre_tile, [xx, y_prev], [gate_prev, scale, shift], [g], [d, d], [F32, BF16], tile, nct)
        y_prev = _even_mixer(cfg, h, i // 2, wd, update_ctx) if i % 2 == 0 else _odd_mixer(cfg, h, i // 2, wd)
        gate_c = mods_c[i, 2 * d:] if update_ctx else jnp.zeros((d,), F32)
        gate_prev = jnp.stack([gate_c, mods[i, 2 * d:]])
    target_rows = jnp.concatenate([jnp.zeros((nctx, d), F32), target], axis=0)
    weight = jnp.stack([jnp.zeros((d,), F32), jnp.ones((d,), F32)])
    (row_loss,) = _rowwise("loss", _loss_tile, [xx, y_prev, target_rows], [gate_prev, weight], [], [1], [F32], tile, nct)
    return jnp.sum(row_loss)


SMALL_REPLICATED = ["norm_g", "ssd_conv_b", "ssd_a_log", "ssd_dt_bias", "ssd_d", "ssd_norm_g", "q_norm_g", "k_norm_g", "na_rpb"]
WEIGHT_ORDER = ["c_ctx", "ada_w", "ada_b", "norm_g", "na_ssd_w_in", "ssd_conv_w", "ssd_conv_b", "ssd_a_log", "ssd_dt_bias",
                "ssd_d", "ssd_norm_g", "q_norm_g", "k_norm_g", "na_rpb", "na_ssd_w_out", "sc_w_in", "sc_conv_w", "sc_w_out"]


def kernel(x, c, ctx, c_ctx, ada_w, ada_b, norm_g, na_ssd_w_in, ssd_conv_w, ssd_conv_b, ssd_a_log, ssd_dt_bias, ssd_d, ssd_norm_g, q_norm_g, k_norm_g, na_rpb, na_ssd_w_out, sc_w_in, sc_conv_w, sc_w_out, loss_target, m_c_ctx, m_ada_w, m_ada_b, m_norm_g, m_na_ssd_w_in, m_ssd_conv_w, m_ssd_conv_b, m_ssd_a_log, m_ssd_dt_bias, m_ssd_d, m_ssd_norm_g, m_q_norm_g, m_k_norm_g, m_na_rpb, m_na_ssd_w_out, m_sc_w_in, m_sc_conv_w, m_sc_w_out, v_c_ctx, v_ada_w, v_ada_b, v_norm_g, v_na_ssd_w_in, v_ssd_conv_w, v_ssd_conv_b, v_ssd_a_log, v_ssd_dt_bias, v_ssd_d, v_ssd_norm_g, v_q_norm_g, v_k_norm_g, v_na_rpb, v_na_ssd_w_out, v_sc_w_in, v_sc_conv_w, v_sc_w_out):
    given = dict(locals())
    weights = {n: given[n] for n in WEIGHT_ORDER}
    mom_m = {n: given["m_" + n] for n in WEIGHT_ORDER}
    mom_v = {n: given["v_" + n] for n in WEIGHT_ORDER}

    d = x.shape[-1]
    seq, nctx = x.shape[1], ctx.shape[1]
    n_in_shard = na_ssd_w_in.shape[-1]
    n_in = n_in_shard * NDEV
    n_pad = -(-n_in // PAD_TO) * PAD_TO
    hpg = (d // SSD_HEAD_DIM) // SSD_GROUPS
    t = nctx + seq
    tm = _pick(t, (1408, 768, 512, 256, 128))
    cfg = SimpleNamespace(
        d=d, s=seq, nctx=nctx, t=t, hpg=hpg, na_width=NA_HEADS * NA_HEAD_DIM, ssd_width=d, n_in=n_in, n_pad=n_pad,
        tiles_in=(tm, _pick(n_pad, (512, 256, 128)), d),
        tiles_out_even=(tm, _pick(d, (1024, 512, 256, 128)), _pick(NA_HEADS * NA_HEAD_DIM + d, (1024, 512, 256, 128))),
        tiles_in_odd=(tm, _pick(4 * d, (1024, 512, 256, 128)), d),
        tiles_out_odd=(tm, _pick(d, (1024, 512, 256, 128)), d),
        tile_tok=128, tile_res=256,
    )
    me = _my_index()
    xl, cl, ctxl, tgt = x[0], c, ctx[0], loss_target[0]

    ncol = ada_w.shape[-1]
    c_rows = -(-d // 128)
    c_all = _small_allgather(jnp.pad(cl.reshape(-1), (0, c_rows * 128 - d)).reshape(c_rows, 128), "gather_c")[0]
    c_all = c_all.reshape(NDEV, -1)[:, :d]
    cond = jnp.concatenate([c_all, c_ctx[None, :], jnp.zeros((16 - NDEV - 1, d), F32)], axis=0)
    s16 = jax.nn.silu(cond)
    ada_b_mine = lax.dynamic_slice_in_dim(ada_b, me * ncol, ncol, axis=1)
    mod_part = jnp.stack([
        _matmul(s16, ada_w[i], tm=16, tn=_pick(ncol, (768, 512, 256, 128)), tk=d, name=f"adaln{i}") + ada_b_mine[i][None, :]
        for i in range(DEPTH)])
    mp_rows = DEPTH * 16 * ncol // 128
    mod_all = _small_allgather(mod_part.reshape(mp_rows, 128), "gather_mod")[0]
    mod_all = mod_all.reshape(NDEV, DEPTH, 16, ncol).transpose(1, 2, 0, 3).reshape(DEPTH, 16, NDEV * ncol)
    mods = lax.dynamic_index_in_dim(mod_all, me, axis=1, keepdims=False)
    mods_c = mod_all[:, NDEV]

    packed = _pack_transposed(na_ssd_w_in, "pack_w_in")
    wout_b = _cast_bf16(na_ssd_w_out, "cast_w_out")
    scwin_b = _cast_bf16(sc_w_in, "cast_sc_w_in")
    scwout_b = _cast_bf16(sc_w_out, "cast_sc_w_out")
    packed_all, wout_all, scwin_all, scwout_all = _big_allgather(
        [packed, wout_b, scwin_b, scwout_b], [1, 1, 2, 1], "gather_weights")
    win_t = _unpack(packed_all, n_pad, "unpack_w_in")
    conv_shapes = [ssd_conv_w.shape, sc_conv_w.shape]
    conv_all = _small_allgather(_flatten([ssd_conv_w, sc_conv_w]), "gather_conv")[0]
    conv_parts = [_unflatten(conv_all[j], conv_shapes) for j in range(NDEV)]
    ssd_conv_full = jnp.concatenate([cp[0] for cp in conv_parts], axis=-1)
    sc_conv_full = jnp.concatenate([cp[1] for cp in conv_parts], axis=-1)

    small = {n: weights[n] for n in SMALL_REPLICATED}
    small["ssd_conv_w"] = ssd_conv_full
    small["sc_conv_w"] = sc_conv_full
    gslots = dict(g_win_t=jnp.zeros(win_t.shape, F32), g_wout=jnp.zeros(wout_all.shape, F32),
                  g_sc_win=jnp.zeros(scwin_all.shape, F32), g_sc_wout=jnp.zeros(scwout_all.shape, F32))
    frozen = dict(win_t=win_t, wout=wout_all, sc_win=scwin_all, sc_wout=scwout_all)

    def loss_fn(xl, mods, mods_c, small, gslots):
        return _local_loss(cfg, xl, ctxl, tgt, mods, mods_c, {**small, **gslots, **frozen})

    loss_local, (g_x, g_mods, g_mods_c, g_small, g_big) = jax.value_and_grad(loss_fn, argnums=(0, 1, 2, 3, 4))(
        xl, mods, mods_c, small, gslots)

    small_names = SMALL_REPLICATED + ["ssd_conv_w", "sc_conv_w"]
    small_shapes = [g_small[n].shape for n in small_names] + [g_mods_c.shape]
    flat_small = _flatten([g_small[n] for n in small_names] + [g_mods_c])
    _, small_sum = _small_allgather(flat_small, "gather_small_grads")
    summed = _unflatten(small_sum, small_shapes)
    g_rep = dict(zip(small_names, summed[:-1]))
    g_mods_c_tot = summed[-1]
    gm_rows = DEPTH * NDEV * ncol // 128
    gm_all = _small_allgather(g_mods.reshape(gm_rows, 128), "gather_mod_grads")[0].reshape(NDEV, DEPTH, NDEV * ncol)
    dm = jnp.concatenate([gm_all.transpose(1, 0, 2), g_mods_c_tot[:, None, :],
                          jnp.zeros((DEPTH, 16 - NDEV - 1, NDEV * ncol), F32)], axis=1)
    grad_ada_b = jnp.sum(dm, axis=1)
    dm_mine = lax.dynamic_slice_in_dim(dm, me * ncol, ncol, axis=2)
    grad_ada_w = jnp.stack([
        _matmul(s16, dm_mine[i], ta=True, tm=_pick(d, (512, 256, 128)), tn=_pick(ncol, (768, 512, 256, 128)), tk=16,
                name=f"adaln_gw{i}") for i in range(DEPTH)])
    ds_part = sum(_matmul(dm_mine[i], ada_w[i], tb=True, tm=16, tn=_pick(d, (2048, 1024, 512, 256, 128)),
                          tk=_pick(ncol, (768, 512, 256, 128)), name=f"adaln_gs{i}") for i in range(DEPTH))[NDEV]
    ds_ctx = _small_allgather(jnp.pad(ds_part, (0, c_rows * 128 - d)).reshape(c_rows, 128), "gather_c_ctx_grad")[1]
    ds_ctx = ds_ctx.reshape(-1)[:d]
    sig = jax.nn.sigmoid(c_ctx)
    grad_c_ctx = ds_ctx * (sig * (1 + c_ctx * (1 - sig)))

    big = [g_big["g_win_t"], g_big["g_wout"], g_big["g_sc_win"], g_big["g_sc_wout"]]
    axes = [1, 1, 2, 1]
    lens = [n_in_shard, na_ssd_w_out.shape[1], sc_w_in.shape[2], sc_w_out.shape[1]]
    sends = [_pick_blocks(g, None, ax, ln, "cast", BF16, f"rs_cast{i}") for i, (g, ax, ln) in enumerate(zip(big, axes, lens))]
    recv_a = _rs_stage_a(sends, "reduce_scatter_d2d")
    part_a = [_pick_blocks(g, r, ax, ln, "mine", F32, f"pair_sum_mine{i}")
              for i, (g, r, ax, ln) in enumerate(zip(big, recv_a, axes, lens))]
    part_s = [_pick_blocks(g, r, ax, ln, "send", BF16, f"pair_sum_send{i}")
              for i, (g, r, ax, ln) in enumerate(zip(big, recv_a, axes, lens))]
    part_b = _rs_stage_b(part_s, "reduce_scatter_ici")

    res = {}
    res["na_ssd_w_in"] = _adam_sharded(na_ssd_w_in, m_na_ssd_w_in, v_na_ssd_w_in, part_a[0], part_b[0], transposed=True, name="adam_w_in")
    res["na_ssd_w_out"] = _adam_sharded(na_ssd_w_out, m_na_ssd_w_out, v_na_ssd_w_out, part_a[1], part_b[1], transposed=False, name="adam_w_out")
    res["sc_w_in"] = _adam_sharded(sc_w_in, m_sc_w_in, v_sc_w_in, part_a[2], part_b[2], transposed=False, name="adam_sc_w_in")
    res["sc_w_out"] = _adam_sharded(sc_w_out, m_sc_w_out, v_sc_w_out, part_a[3], part_b[3], transposed=False, name="adam_sc_w_out")

    grads = dict(g_rep)
    grads["ssd_conv_w"] = lax.dynamic_slice_in_dim(g_rep["ssd_conv_w"], me * ssd_conv_w.shape[-1], ssd_conv_w.shape[-1], axis=2)
    grads["sc_conv_w"] = lax.dynamic_slice_in_dim(g_rep["sc_conv_w"], me * sc_conv_w.shape[-1], sc_conv_w.shape[-1], axis=2)
    grads["c_ctx"] = grad_c_ctx
    res["ada_w"] = (grad_ada_w, *_adam_tiled(ada_w, grad_ada_w, m_ada_w, v_ada_w, "adam_ada_w"))
    grads["ada_b"] = grad_ada_b
    rest = [n for n in WEIGHT_ORDER if n not in res]
    shapes = [weights[n].shape for n in rest]
    d_flat, m_flat, v_flat = _adam_small(_flatten([weights[n] for n in rest]), _flatten([grads[n] for n in rest]),
                                         _flatten([mom_m[n] for n in rest]), _flatten([mom_v[n] for n in rest]), "adam_small")
    for n, dd, mm, vv in zip(rest, _unflatten(d_flat, shapes), _unflatten(m_flat, shapes), _unflatten(v_flat, shapes)):
        res[n] = (grads[n], dd, mm, vv)

    loss = lax.psum(loss_local, ("x", "y", "c"))
    return (loss, g_x[None], *[res[n][0] for n in WEIGHT_ORDER], *[res[n][1] for n in WEIGHT_ORDER],
            *[res[n][2] for n in WEIGHT_ORDER], *[res[n][3] for n in WEIGHT_ORDER])
```
